```python
import math
import jax
import jax.numpy as jnp
from jax import lax
import numpy as np

D_MODEL = 1024
BATCH = 8
SEQ = 2048
DEPTH = 2

GRID_W = 64
CTX_LEN = 256
N_EVEN = (DEPTH + 1) // 2
N_ODD = DEPTH // 2
N_MOD = 6
EPS = 1e-6
F32 = jnp.float32

A_HEADS = 8
A_KV_HEADS = 2
A_GROUP = A_HEADS // A_KV_HEADS
A_HEAD_DIM = 64
A_Q = A_HEADS * A_HEAD_DIM
A_KV = A_KV_HEADS * A_HEAD_DIM
A_IN = A_Q + 2 * A_KV
A_BLOCK = 128
ROPE_THETA = 10000.0

B_HEADS = 8
B_HEAD_DIM = 64
B_WIDTH = B_HEADS * B_HEAD_DIM
B_DECAY_LORA = 64
B_AAA_LORA = 64
B_GATE_LORA = 128
B_IN = 3 * B_WIDTH + B_DECAY_LORA + B_AAA_LORA + B_GATE_LORA
B_GN_EPS = 64e-5

EVEN_IN = A_IN + B_IN
EVEN_MIX = A_Q + B_WIDTH

C_INNER = 2 * D_MODEL
C_HEAD_DIM = 64
C_HEADS = C_INNER // C_HEAD_DIM
C_GROUPS = 4
C_STATE = 128
C_CONV = 3
C_CHUNK = 128
C_CONV_DIM = C_INNER + 2 * C_GROUPS * C_STATE
ODD_IN = C_INNER + C_CONV_DIM + 2 * C_HEADS

P_HEADS = 8
P_KEYS = 128
P_EXPERTS = P_KEYS * P_KEYS
P_KEY_DIM = 128
P_TOPK = 16
P_BLOCK = 128

kernel_name = 'hybrid_gqa_rwkv7_ssd_peer_dit'


def _rmsnorm(x, gain):
    xf = x.astype(F32)
    y = xf * lax.rsqrt(jnp.mean(xf * xf, axis=-1, keepdims=True) + EPS)
    return (y * gain.astype(F32)).astype(x.dtype)


def _axial_angles(rows):
    t = jnp.arange(rows * GRID_W)
    row = (t // GRID_W).astype(F32)
    col = (t % GRID_W).astype(F32)
    m = A_HEAD_DIM // 4
    inv = ROPE_THETA ** (-jnp.arange(m, dtype=F32) / m)
    return row[:, None] * inv, col[:, None] * inv


def _rope_half(x, ang):
    m = x.shape[-1] // 2
    x1, x2 = x[..., :m], x[..., m:]
    cos, sin = jnp.cos(ang), jnp.sin(ang)
    return jnp.concatenate([x1 * cos - x2 * sin, x2 * cos + x1 * sin], axis=-1)


def _rope_2d(x, ang_r, ang_c):
    shape = (ang_r.shape[0],) + (1,) * (x.ndim - 3) + (ang_r.shape[1],)
    xf = x.astype(F32)
    half = A_HEAD_DIM // 2
    out = jnp.concatenate([_rope_half(xf[..., :half], ang_r.reshape(shape)),
                           _rope_half(xf[..., half:], ang_c.reshape(shape))], axis=-1)
    return out.astype(x.dtype)


def _attend(q, k, v):
    s = jnp.einsum('bqhgd,bkhd->bhgqk', q, k, preferred_element_type=F32) * (A_HEAD_DIM ** -0.5)
    p = jax.nn.softmax(s, axis=-1).astype(v.dtype)
    return jnp.einsum('bhgqk,bkhd->bqhgd', p, v)


def _gqa_mixer(zc, zl, q_gain, k_gain, ang_r, ang_c, with_ctx):
    bsz, lc, _ = zc.shape
    seq = zl.shape[1]

    def qkv(z):
        n = z.shape[1]
        q = _rmsnorm(z[..., :A_Q].reshape(bsz, n, A_KV_HEADS, A_GROUP, A_HEAD_DIM), q_gain)
        k = _rmsnorm(z[..., A_Q:A_Q + A_KV].reshape(bsz, n, A_KV_HEADS, A_HEAD_DIM), k_gain)
        v = z[..., A_Q + A_KV:].reshape(bsz, n, A_KV_HEADS, A_HEAD_DIM)
        return q, k, v

    qc, kc, vc = qkv(zc)
    ql, kl, vl = qkv(zl)
    ql = _rope_2d(ql, ang_r, ang_c)
    kl = _rope_2d(kl, ang_r, ang_c)
    k_all = jnp.concatenate([kc, kl], axis=1)
    v_all = jnp.concatenate([vc, vl], axis=1)
    nb = seq // A_BLOCK
    qb = jnp.moveaxis(ql.reshape(bsz, nb, A_BLOCK, A_KV_HEADS, A_GROUP, A_HEAD_DIM), 1, 0)
    ol = lax.map(lambda qblk: _attend(qblk, k_all, v_all), qb)
    ol = jnp.moveaxis(ol, 0, 1).reshape(bsz, seq, A_Q)
    oc = _attend(qc, kc, vc).reshape(bsz, lc, A_Q) if with_ctx else None
    return oc, ol


def _centred_shift(z):
    zp = jnp.pad(z, ((0, 0), (1, 1), (0, 0)))
    return 0.5 * (zp[:, :-2] + zp[:, 2:])


def _seg_reverse(t, lc):
    return jnp.concatenate([jnp.flip(t[:, :lc], axis=1), jnp.flip(t[:, lc:], axis=1)], axis=1)


def _wkv7_scan(r, w, k, v, kk, a):
    def step(s, inp):
        r_t, w_t, k_t, v_t, kk_t, a_t = inp
        sa = jnp.einsum('bhvk,bhk->bhv', s, -kk_t)
        s = (s * w_t[:, :, None, :] + sa[..., None] * (kk_t * a_t)[:, :, None, :]
             + v_t[..., None] * k_t[:, :, None, :])
        return s, jnp.einsum('bhvk,bhk->bhv', s, r_t)
    s0 = jnp.zeros(r.shape[1:] + (r.shape[-1],), F32)
    _, out = lax.scan(step, s0, (r, w, k, v, kk, a))
    return out


def _head_groupnorm(y, w, b):
    mu = jnp.mean(y, axis=-1, keepdims=True)
    var = jnp.mean(jnp.square(y - mu), axis=-1, keepdims=True)
    return (y - mu) * lax.rsqrt(var + B_GN_EPS) * w + b


def _rwkv7_mixer(zc, zl, mu, w0, w2, a0, a2, g2, k_k, k_a, r_k, gn_w, gn_b):
    bsz, lc, _ = zc.shape
    z = jnp.concatenate([zc + mu * (_centred_shift(zc) - zc),
                         zl + mu * (_centred_shift(zl) - zl)], axis=1)
    ln = z.shape[1]
    o1, o2, o3 = B_WIDTH, 2 * B_WIDTH, 3 * B_WIDTH
    o4 = o3 + B_DECAY_LORA
    o5 = o4 + B_AAA_LORA
    r, k, v = z[..., :o1], z[..., o1:o2], z[..., o2:o3]
    wl, al, gl = z[..., o3:o4], z[..., o4:o5], z[..., o5:]

    def heads(t):
        return t.reshape(bsz, ln, B_HEADS, B_HEAD_DIM).astype(F32)

    rh, vh = heads(r), heads(v)
    kk = heads(k * k_k)
    kk = kk / jnp.maximum(jnp.sqrt(jnp.sum(kk * kk, axis=-1, keepdims=True)), 1e-12)
    rk = r_k.reshape(B_HEADS, B_HEAD_DIM).astype(F32)
    gw = gn_w.reshape(B_HEADS, B_HEAD_DIM).astype(F32)
    gb = gn_b.reshape(B_HEADS, B_HEAD_DIM).astype(F32)
    g = jax.nn.sigmoid(gl) @ g2
    wt = jnp.tanh(wl)
    y = jnp.zeros((bsz, ln, B_HEADS, B_HEAD_DIM), F32)
    for d in range(2):
        w = -jax.nn.softplus(-(w0[d] + wt @ w2[d])) - 0.5
        decay = heads(jnp.exp(-jnp.exp(w.astype(F32))))
        a_lin = jax.nn.sigmoid(a0[d] + al @ a2[d])
        kd = k * (1 + (a_lin - 1) * k_a)
        ah, kdh = heads(a_lin), heads(kd)
        if d == 0:
            order = lambda t: t
        else:
            order = lambda t: _seg_reverse(t, lc)
        seqs = [jnp.moveaxis(order(t), 1, 0) for t in (rh, decay, kdh, vh, kk, ah)]
        out = order(jnp.moveaxis(_wkv7_scan(*seqs), 0, 1))
        bonus = jnp.sum(rh * kdh * rk, axis=-1, keepdims=True) * vh
        y = y + _head_groupnorm(out, gw, gb) + bonus
    y = y.reshape(bsz, ln, B_WIDTH).astype(zc.dtype) * g
    return y[:, :lc], y[:, lc:]


def _even_mixer(zc, zl, q_gain, k_gain, mu, w0, w2, a0, a2, g2, k_k, k_a, r_k, gn_w, gn_b,
                ang_r, ang_c, with_ctx):
    oc, ol = _gqa_mixer(zc[..., :A_IN], zl[..., :A_IN], q_gain, k_gain, ang_r, ang_c, with_ctx)
    rc, rl = _rwkv7_mixer(zc[..., A_IN:], zl[..., A_IN:], mu, w0, w2, a0, a2, g2,
                          k_k, k_a, r_k, gn_w, gn_b)
    yl = jnp.concatenate([ol, rl], axis=-1)
    yc = jnp.concatenate([oc, rc], axis=-1) if with_ctx else None
    return yc, yl


def _dwconv_centred(z, w, b):
    k, ch = w.shape
    y = lax.conv_general_dilated(z, w[:, None, :].astype(z.dtype), window_strides=(1,),
                                 padding=[(k // 2, k // 2)],
                                 dimension_numbers=('NWC', 'WIO', 'NWC'),
                                 feature_group_count=ch)
    return y + b.astype(z.dtype)


def _ssd_chunked(x, dt, a, bm, cm):
    bsz, ln, nh, hp = x.shape
    ng, ns = bm.shape[2], bm.shape[3]
    nr = nh // ng
    q = C_CHUNK
    nc = ln // q
    xd = (x.astype(F32) * dt[..., None]).reshape(bsz, nc, q, ng, nr, hp)
    cum = jnp.cumsum((dt * a).reshape(bsz, nc, q, ng, nr), axis=2)
    bc = bm.astype(F32).reshape(bsz, nc, q, ng, ns)
    cc = cm.astype(F32).reshape(bsz, nc, q, ng, ns)
    lower = jnp.tril(jnp.ones((q, q), bool))[None, None, :, :, None, None]
    seg = cum[:, :, :, None] - cum[:, :, None, :]
    decay_ij = jnp.exp(jnp.where(lower, seg, -jnp.inf))
    cb = jnp.einsum('bcign,bcjgn->bcijg', cc, bc)
    y_diag = jnp.einsum('bcijgr,bcjgrp->bcigrp', cb[..., None] * decay_ij, xd)
    decay_end = jnp.exp(cum[:, :, -1:] - cum)
    states = jnp.einsum('bcjgn,bcjgr,bcjgrp->bcgrpn', bc, decay_end, xd)
    chunk_decay = jnp.exp(cum[:, :, -1])

    def step(s, inp):
        st, dec = inp
        return s * dec[..., None, None] + st, s

    s0 = jnp.zeros((bsz, ng, nr, hp, ns), F32)
    _, s_prev = lax.scan(step, s0, (jnp.moveaxis(states, 1, 0), jnp.moveaxis(chunk_decay, 1, 0)))
    s_prev = jnp.moveaxis(s_prev, 0, 1)
    y_off = jnp.einsum('bcign,bcgrpn,bcigr->bcigrp', cc, s_prev, jnp.exp(cum))
    return (y_diag + y_off).reshape(bsz, ln, nh, hp)


def _mamba2_mixer(zc, zl, conv_w, conv_b, dt_bias, a_log, d_skip, norm_w, with_ctx):
    bsz, lc, _ = zc.shape
    o1 = C_INNER
    o2 = C_INNER + C_CONV_DIM
    gn = C_GROUPS * C_STATE
    z_gate = jnp.concatenate([zc[..., :o1], zl[..., :o1]], axis=1)
    xbc = jnp.concatenate([jax.nn.silu(_dwconv_centred(zc[..., o1:o2], conv_w, conv_b)),
                           jax.nn.silu(_dwconv_centred(zl[..., o1:o2], conv_w, conv_b))], axis=1)
    dt_raw = jnp.concatenate([zc[..., o2:], zl[..., o2:]], axis=1).astype(F32)
    ln = xbc.shape[1]
    xs = xbc[..., :C_INNER].reshape(bsz, ln, C_HEADS, C_HEAD_DIM)
    bm = xbc[..., C_INNER:C_INNER + gn].reshape(bsz, ln, C_GROUPS, C_STATE)
    cm = xbc[..., C_INNER + gn:].reshape(bsz, ln, C_GROUPS, C_STATE)
    y = d_skip.astype(F32)[:, None] * xs.astype(F32)
    for d in range(2):
        dt = jax.nn.softplus(dt_raw[..., d * C_HEADS:(d + 1) * C_HEADS] + dt_bias[d].astype(F32))
        a = -jnp.exp(a_log[d].astype(F32))
        if d == 0:
            order = lambda t: t
        else:
            order = lambda t: _seg_reverse(t, lc)
        y = y + order(_ssd_chunked(order(xs), order(dt), a, order(bm), order(cm)))
    y = y.reshape(bsz, ln, C_INNER).astype(zc.dtype)
    if not with_ctx:
        y, z_gate = y[:, lc:], z_gate[:, lc:]
    n = y.shape[1]
    yg = (y * jax.nn.silu(z_gate)).reshape(bsz, n, C_GROUPS, C_INNER // C_GROUPS)
    yn = _rmsnorm(yg, norm_w.reshape(C_GROUPS, C_INNER // C_GROUPS)).reshape(bsz, n, C_INNER)
    if with_ctx:
        return yn[:, :lc], yn[:, lc:]
    return None, yn


def _peer(h, w_q, sub_keys, u_tab, v_tab):
    t_all, dm = h.shape

    def block(hb):
        n = hb.shape[0]
        q = (hb @ w_q).reshape(n, P_HEADS, 2, P_KEY_DIM)
        s = jnp.einsum('thpd,hpnd->thpn', q, sub_keys).astype(F32)
        sv, si = lax.top_k(s, P_TOPK)
        cand_s = (sv[:, :, 0, :, None] + sv[:, :, 1, None, :]).reshape(n, P_HEADS, P_TOPK * P_TOPK)
        cand_i = (si[:, :, 0, :, None] * P_KEYS + si[:, :, 1, None, :]).reshape(n, P_HEADS, P_TOPK * P_TOPK)
        best_s, pos = lax.top_k(cand_s, P_TOPK)
        idx = jnp.take_along_axis(cand_i, pos, axis=-1)
        gate = jax.nn.softmax(best_s, axis=-1)
        act = jax.nn.gelu(jnp.einsum('td,thkd->thk', hb, u_tab[idx]).astype(F32), approximate=False)
        coef = (gate * act).astype(hb.dtype)
        return jnp.einsum('thk,thkd->td', coef, v_tab[idx])

    out = lax.map(block, h.reshape(t_all // P_BLOCK, P_BLOCK, dm))
    return out.reshape(t_all, dm)


def setup_inputs(seed: int = 0) -> dict:
    key = jax.random.key(seed)
    ks = list(jax.random.split(key, 40))

    def nrm(shape, scale):
        return jax.random.normal(ks.pop(), shape, F32) * scale

    def unif(shape, lo, hi):
        return jax.random.uniform(ks.pop(), shape, F32, lo, hi)

    def gain(shape):
        return 1.0 + nrm(shape, 0.02)

    dt0 = jnp.exp(unif((N_ODD, 2, C_HEADS), math.log(1e-3), math.log(1e-1)))
    return {
        'x': nrm((BATCH, SEQ, D_MODEL), 1.0),
        'c': nrm((BATCH, D_MODEL), 1.0),
        'ctx': nrm((BATCH, CTX_LEN, D_MODEL), 1.0),
        'c_ctx': nrm((D_MODEL,), 1.0),
        'mod_w': nrm((DEPTH, D_MODEL, N_MOD * D_MODEL), 0.3 * D_MODEL ** -0.5),
        'mod_b': nrm((DEPTH, N_MOD * D_MODEL), 0.02),
        'norm1_g': gain((DEPTH, D_MODEL)),
        'norm2_g': gain((DEPTH, D_MODEL)),
        'ev_w_in': nrm((N_EVEN, D_MODEL, EVEN_IN), D_MODEL ** -0.5),
        'ev_w_out': nrm((N_EVEN, EVEN_MIX, D_MODEL), EVEN_MIX ** -0.5),
        'attn_q_gain': gain((N_EVEN, A_HEAD_DIM)),
        'attn_k_gain': gain((N_EVEN, A_HEAD_DIM)),
        'rw_mu': unif((N_EVEN, B_IN), 0.0, 1.0),
        'rw_w0': unif((N_EVEN, 2, B_WIDTH), -6.5, -1.0),
        'rw_w2': nrm((N_EVEN, 2, B_DECAY_LORA, B_WIDTH), 0.1),
        'rw_a0': nrm((N_EVEN, 2, B_WIDTH), 0.1),
        'rw_a2': nrm((N_EVEN, 2, B_AAA_LORA, B_WIDTH), 0.5 * B_AAA_LORA ** -0.5),
        'rw_g2': nrm((N_EVEN, B_GATE_LORA, B_WIDTH), B_GATE_LORA ** -0.5),
        'rw_k_k': 0.85 + nrm((N_EVEN, B_WIDTH), 0.02),
        'rw_k_a': gain((N_EVEN, B_WIDTH)),
        'rw_r_k': nrm((N_EVEN, B_WIDTH), 0.1),
        'rw_gn_w': gain((N_EVEN, B_WIDTH)),
        'rw_gn_b': nrm((N_EVEN, B_WIDTH), 0.02),
        'ssd_w_in': nrm((N_ODD, D_MODEL, ODD_IN), D_MODEL ** -0.5),
        'ssd_conv_w': nrm((N_ODD, C_CONV, C_CONV_DIM), C_CONV ** -0.5),
        'ssd_conv_b': nrm((N_ODD, C_CONV_DIM), 0.02),
        'ssd_dt_bias': dt0 + jnp.log(-jnp.expm1(-dt0)),
        'ssd_a_log': jnp.log(unif((N_ODD, 2, C_HEADS), 1.0, 16.0)),
        'ssd_d': gain((N_ODD, C_HEADS)),
        'ssd_norm_w': gain((N_ODD, C_INNER)),
        'ssd_w_out': nrm((N_ODD, C_INNER, D_MODEL), C_INNER ** -0.5),
        'peer_w_q': nrm((DEPTH, D_MODEL, P_HEADS * 2 * P_KEY_DIM), D_MODEL ** -0.5),
        'peer_keys': nrm((DEPTH, P_HEADS, 2, P_KEYS, P_KEY_DIM), P_KEY_DIM ** -0.5),
        'peer_u': nrm((DEPTH, P_EXPERTS, D_MODEL), D_MODEL ** -0.5),
        'peer_v': nrm((DEPTH, P_EXPERTS, D_MODEL), P_HEADS ** -0.5),
    }


def reference(x, c, ctx, c_ctx, mod_w, mod_b, norm1_g, norm2_g, ev_w_in, ev_w_out,
              attn_q_gain, attn_k_gain, rw_mu, rw_w0, rw_w2, rw_a0, rw_a2, rw_g2, rw_k_k,
              rw_k_a, rw_r_k, rw_gn_w, rw_gn_b, ssd_w_in, ssd_conv_w, ssd_conv_b, ssd_dt_bias,
              ssd_a_log, ssd_d, ssd_norm_w, ssd_w_out, peer_w_q, peer_keys, peer_u, peer_v):
    bsz, seq, dm = x.shape
    lc = ctx.shape[1]
    rows = seq // GRID_W
    ang_r, ang_c = _axial_angles(rows)
    xl, xc = x, ctx
    for i in range(DEPTH):
        last = i == DEPTH - 1
        j = i // 2
        ml = (jax.nn.silu(c) @ mod_w[i] + mod_b[i]).reshape(bsz, 1, N_MOD, dm)
        mc = (jax.nn.silu(c_ctx) @ mod_w[i] + mod_b[i]).reshape(1, 1, N_MOD, dm)
        hl = _rmsnorm(xl, norm1_g[i]) * (1 + ml[:, :, 1]) + ml[:, :, 0]
        hc = _rmsnorm(xc, norm1_g[i]) * (1 + mc[:, :, 1]) + mc[:, :, 0]
        if i % 2 == 0:
            yc, yl = _even_mixer(hc @ ev_w_in[j], hl @ ev_w_in[j], attn_q_gain[j], attn_k_gain[j],
                                 rw_mu[j], rw_w0[j], rw_w2[j], rw_a0[j], rw_a2[j], rw_g2[j],
                                 rw_k_k[j], rw_k_a[j], rw_r_k[j], rw_gn_w[j], rw_gn_b[j],
                                 ang_r, ang_c, not last)
            w_out = ev_w_out[j]
        else:
            yc, yl = _mamba2_mixer(hc @ ssd_w_in[j], hl @ ssd_w_in[j], ssd_conv_w[j], ssd_conv_b[j],
                                   ssd_dt_bias[j], ssd_a_log[j], ssd_d[j], ssd_norm_w[j], not last)
            w_out = ssd_w_out[j]
        xl = xl + ml[:, :, 2] * (yl @ w_out)
        hl2 = _rmsnorm(xl, norm2_g[i]) * (1 + ml[:, :, 4]) + ml[:, :, 3]
        if last:
            f = _peer(hl2.reshape(-1, dm), peer_w_q[i], peer_keys[i], peer_u[i], peer_v[i])
            xl = xl + ml[:, :, 5] * f.reshape(bsz, seq, dm)
        else:
            xc = xc + mc[:, :, 2] * (yc @ w_out)
            hc2 = _rmsnorm(xc, norm2_g[i]) * (1 + mc[:, :, 4]) + mc[:, :, 3]
            f = _peer(jnp.concatenate([hc2.reshape(-1, dm), hl2.reshape(-1, dm)], axis=0),
                      peer_w_q[i], peer_keys[i], peer_u[i], peer_v[i])
            n_ctx_tok = bsz * lc
            xc = xc + mc[:, :, 5] * f[:n_ctx_tok].reshape(bsz, lc, dm)
            xl = xl + ml[:, :, 5] * f[n_ctx_tok:].reshape(bsz, seq, dm)
    return xl
```

```python
import functools
import math

import numpy as np
import jax
import jax.numpy as jnp
from jax import lax
from jax.experimental import pallas as pl
from jax.experimental.pallas import tpu as pltpu

F32 = jnp.float32
BF16 = jnp.bfloat16
HI = lax.Precision.HIGHEST

GRID_W = 64
EPS = 1e-6
N_MOD = 6
A_HEADS, A_KV_HEADS, A_HEAD_DIM = 8, 2, 64
A_Q = A_HEADS * A_HEAD_DIM
A_KV = A_KV_HEADS * A_HEAD_DIM
A_IN = A_Q + 2 * A_KV
ROPE_THETA = 10000.0
B_HEADS, B_HEAD_DIM = 8, 64
B_WIDTH = B_HEADS * B_HEAD_DIM
B_DECAY_LORA, B_AAA_LORA, B_GATE_LORA = 64, 64, 128
B_IN = 3 * B_WIDTH + B_DECAY_LORA + B_AAA_LORA + B_GATE_LORA
B_GN_EPS = 64e-5
C_HEAD_DIM, C_GROUPS, C_STATE, C_CHUNK = 64, 4, 128, 128
P_HEADS, P_KEYS, P_KEY_DIM, P_TOPK = 8, 128, 128, 16

LANES = 128
TOK_TILE = 256
R_CHUNK = 64
PEER_TOK = 512
PEER_QUERY_TOK = 256
PEER_EXP = 512
VMEM_LIMIT = 56 * 1024 * 1024


def _mm(a, b, prec=None):
    return lax.dot_general(a, b, (((1,), (0,)), ((), ())), precision=prec,
                           preferred_element_type=F32)


def _mm_nt(a, b, prec=None):
    return lax.dot_general(a, b, (((1,), (1,)), ((), ())), precision=prec,
                           preferred_element_type=F32)


def _mm_tn(a, b, prec=None):
    return lax.dot_general(a, b, (((0,), (0,)), ((), ())), precision=prec,
                           preferred_element_type=F32)


def _params(*sem):
    return pltpu.CompilerParams(dimension_semantics=sem, vmem_limit_bytes=VMEM_LIMIT)


def _softplus(x):
    return jnp.maximum(x, 0.0) + jnp.log1p(jnp.exp(-jnp.abs(x)))


def _sigmoid(x):
    return 1.0 / (1.0 + jnp.exp(-x))


def _silu(x):
    return x * _sigmoid(x)


def _mod_kernel(c_ref, w_ref, b_ref, o_ref):
    o_ref[0] = _mm(_silu(c_ref[...]), w_ref[0], HI) + b_ref[0]


def _modulation(cc, mod_w, mod_b):
    depth, d, n6 = mod_w.shape
    rows = cc.shape[0]
    tn = 1024
    return pl.pallas_call(
        _mod_kernel,
        grid=(depth, n6 // tn),
        in_specs=[pl.BlockSpec((rows, d), lambda l, n: (0, 0)),
                  pl.BlockSpec((1, d, tn), lambda l, n: (l, 0, n)),
                  pl.BlockSpec((1, 1, tn), lambda l, n: (l, 0, n))],
        out_specs=pl.BlockSpec((1, rows, tn), lambda l, n: (l, 0, n)),
        out_shape=jax.ShapeDtypeStruct((depth, rows, n6), F32),
        compiler_params=_params("arbitrary", "arbitrary"),
        name="modulation",
    )(cc, mod_w, mod_b.reshape(depth, 1, n6))


def _norm_mod(x, gain, shift, scale):
    ms = jnp.mean(x * x, axis=-1, keepdims=True)
    return x * lax.rsqrt(ms + EPS) * gain * (1.0 + scale) + shift


def _in_proj_kernel(x_ref, mod_ref, g_ref, w_ref, o_ref):
    mod = mod_ref[0, 0]
    h = _norm_mod(x_ref[0], g_ref[...], mod[0:1], mod[1:2]).astype(BF16)
    n = o_ref.shape[-1]
    for c0 in range(0, n, 1024):
        c1 = min(n, c0 + 1024)
        o_ref[0, :, c0:c1] = _mm(h, w_ref[:, c0:c1])


def _in_proj(x, modx, gain, w, lc):
    bsz, ln, d = x.shape
    n = w.shape[1]
    tm = TOK_TILE
    lct = lc // tm
    return pl.pallas_call(
        _in_proj_kernel,
        grid=(bsz, ln // tm),
        in_specs=[pl.BlockSpec((1, tm, d), lambda b, j: (b, j, 0)),
                  pl.BlockSpec((1, 1, N_MOD, d), lambda b, j: (b, jnp.where(j < lct, 0, 1), 0, 0)),
                  pl.BlockSpec((1, d), lambda b, j: (0, 0)),
                  pl.BlockSpec((d, n), lambda b, j: (0, 0))],
        out_specs=pl.BlockSpec((1, tm, n), lambda b, j: (b, j, 0)),
        out_shape=jax.ShapeDtypeStruct((bsz, ln, n), F32),
        compiler_params=_params("arbitrary", "arbitrary"),
        name="in_proj",
    )(x, modx, gain.reshape(1, d), w)


def _rope_tables(lc, seq):
    t = jnp.arange(seq)
    row = (t // GRID_W).astype(F32)
    col = (t % GRID_W).astype(F32)
    m = A_HEAD_DIM // 4
    inv = ROPE_THETA ** (-jnp.arange(m, dtype=F32) / m)
    ar, ac = row[:, None] * inv, col[:, None] * inv
    cos = jnp.concatenate([jnp.cos(ar), jnp.cos(ar), jnp.cos(ac), jnp.cos(ac)], axis=-1)
    sin = jnp.concatenate([-jnp.sin(ar), jnp.sin(ar), -jnp.sin(ac), jnp.sin(ac)], axis=-1)
    reps = LANES // A_HEAD_DIM
    cos = jnp.concatenate([jnp.ones((lc, A_HEAD_DIM), F32), cos], axis=0)
    sin = jnp.concatenate([jnp.zeros((lc, A_HEAD_DIM), F32), sin], axis=0)
    return jnp.tile(cos, (1, reps)), jnp.tile(sin, (1, reps))


def _head_block_ones(width, scale):
    idx = np.arange(LANES) // width
    return jnp.asarray((idx[:, None] == idx[None, :]).astype(np.float32) * scale)


def _attn_pre_kernel(z_ref, qg_ref, kg_ref, cos_ref, sin_ref, bones_ref, q_ref, k_ref, v_ref):
    tm = z_ref.shape[1]
    cs, sn = cos_ref[...], sin_ref[...]
    bones = bones_ref[...]
    lane = lax.broadcasted_iota(jnp.int32, (tm, LANES), 1)
    first = (lane % 32) < 16
    low = lane < A_HEAD_DIM

    def norm_rope(x, gain):
        ms = _mm(x * x, bones, HI)
        y = x * lax.rsqrt(ms + EPS) * gain
        partner = jnp.where(first, pltpu.roll(y, LANES - 16, 1), pltpu.roll(y, 16, 1))
        return y * cs + partner * sn

    heads_per_slab = LANES // A_HEAD_DIM
    group = A_HEADS // A_KV_HEADS
    for m in range(A_Q // LANES):
        sl = slice(m * LANES, (m + 1) * LANES)
        y = norm_rope(z_ref[0, :, sl], qg_ref[:, sl]) * (A_HEAD_DIM ** -0.5)
        swapped = pltpu.roll(y, A_HEAD_DIM, 1)
        for hh in range(heads_per_slab):
            h = m * heads_per_slab + hh
            kvh = h // group
            src = y if kvh == hh else swapped
            keep = low if kvh == 0 else jnp.logical_not(low)
            q_ref[0, h] = jnp.where(keep, src, 0.0).astype(BF16)
    k_ref[0] = norm_rope(z_ref[0, :, A_Q:A_Q + A_KV], kg_ref[...]).astype(BF16)
    v_ref[0] = z_ref[0, :, A_Q + A_KV:A_IN].astype(BF16)


def _attn_pre(za, q_gain, k_gain, lc):
    bsz, ln, _ = za.shape
    tm = TOK_TILE
    cos, sin = _rope_tables(lc, ln - lc)
    qg = jnp.tile(q_gain, A_HEADS).reshape(1, A_Q)
    kg = jnp.tile(k_gain, A_KV_HEADS).reshape(1, A_KV)
    bones = _head_block_ones(A_HEAD_DIM, 1.0 / A_HEAD_DIM)
    return pl.pallas_call(
        _attn_pre_kernel,
        grid=(bsz, ln // tm),
        in_specs=[pl.BlockSpec((1, tm, A_IN), lambda b, j: (b, j, 0)),
                  pl.BlockSpec((1, A_Q), lambda b, j: (0, 0)),
                  pl.BlockSpec((1, A_KV), lambda b, j: (0, 0)),
                  pl.BlockSpec((tm, LANES), lambda b, j: (j, 0)),
                  pl.BlockSpec((tm, LANES), lambda b, j: (j, 0)),
                  pl.BlockSpec((LANES, LANES), lambda b, j: (0, 0))],
        out_specs=[pl.BlockSpec((1, A_HEADS, tm, LANES), lambda b, j: (b, 0, j, 0)),
                   pl.BlockSpec((1, tm, A_KV), lambda b, j: (b, j, 0)),
                   pl.BlockSpec((1, tm, A_KV), lambda b, j: (b, j, 0))],
        out_shape=[jax.ShapeDtypeStruct((bsz, A_HEADS, ln, LANES), BF16),
                   jax.ShapeDtypeStruct((bsz, ln, A_KV), BF16),
                   jax.ShapeDtypeStruct((bsz, ln, A_KV), BF16)],
        compiler_params=_params("arbitrary", "arbitrary"),
        name="attn_pre",
    )(za, qg, kg, cos, sin, bones)


def _attn_kernel(q_ref, k_ref, v_ref, o_ref, *, lc):
    tq = q_ref.shape[2]
    j = pl.program_id(1)
    lane = lax.broadcasted_iota(jnp.int32, (tq, LANES), 1)
    low = lane < A_HEAD_DIM
    heads_per_slab = LANES // A_HEAD_DIM
    group = A_HEADS // A_KV_HEADS

    def run(klen):
        k = k_ref[0, :klen, :]
        v = v_ref[0, :klen, :]
        for m in range(A_Q // LANES):
            parts = []
            for hh in range(heads_per_slab):
                h = m * heads_per_slab + hh
                kvh = h // group
                s = _mm_nt(q_ref[0, h], k)
                p = jnp.exp(s - jnp.max(s, axis=-1, keepdims=True))
                l = jnp.sum(p, axis=-1, keepdims=True)
                o = _mm(p.astype(BF16), v) / l
                parts.append(o if kvh == hh else pltpu.roll(o, A_HEAD_DIM, 1))
            o_ref[0, :, m * LANES:(m + 1) * LANES] = jnp.where(low, parts[0], parts[1])

    @pl.when(j * tq < lc)
    def _():
        run(lc)

    @pl.when(j * tq >= lc)
    def _():
        run(k_ref.shape[1])


def _attention(q, k, v, lc):
    bsz, _, ln, _ = q.shape
    tq = TOK_TILE
    return pl.pallas_call(
        functools.partial(_attn_kernel, lc=lc),
        grid=(bsz, ln // tq),
        in_specs=[pl.BlockSpec((1, A_HEADS, tq, LANES), lambda b, j: (b, 0, j, 0)),
                  pl.BlockSpec((1, ln, A_KV), lambda b, j: (b, 0, 0)),
                  pl.BlockSpec((1, ln, A_KV), lambda b, j: (b, 0, 0))],
        out_specs=pl.BlockSpec((1, tq, A_Q), lambda b, j: (b, j, 0)),
        out_shape=jax.ShapeDtypeStruct((bsz, ln, A_Q), F32),
        compiler_params=_params("arbitrary", "arbitrary"),
        name="attention",
    )(q, k, v)


def _shifted_rows(z, prev_row, next_row):
    tm = z.shape[0]
    row = lax.broadcasted_iota(jnp.int32, (tm, 1), 0)
    zprev = jnp.where(row == 0, prev_row, pltpu.roll(z, 1, 0))
    znext = jnp.where(row == tm - 1, next_row, pltpu.roll(z, tm - 1, 0))
    return zprev, znext


def _halo_rows(zp_ref, zn_ref, j, lct, nj):
    prev_ok = jnp.logical_and(j != 0, j != lct)
    next_ok = jnp.logical_and(j != lct - 1, j != nj - 1)
    prev_row = jnp.where(prev_ok, zp_ref[0, 7:8, :], 0.0)
    next_row = jnp.where(next_ok, zn_ref[0, 0:1, :], 0.0)
    return prev_row, next_row


def _halo_specs(tm, width, ln):
    nb8 = tm // 8
    last = ln // 8 - 1
    return [pl.BlockSpec((1, tm, width), lambda b, j: (b, j, 0)),
            pl.BlockSpec((1, 8, width), lambda b, j: (b, jnp.maximum(j * nb8 - 1, 0), 0)),
            pl.BlockSpec((1, 8, width), lambda b, j: (b, jnp.minimum((j + 1) * nb8, last), 0))]


def _per_head_sum(x, bones):
    return jnp.concatenate(
        [_mm(x[:, m * LANES:(m + 1) * LANES], bones, HI) for m in range(x.shape[1] // LANES)], axis=1)


def _rwkv_pre_kernel(z_ref, zp_ref, zn_ref, mu_ref, w0_ref, w2_ref, a0_ref, a2_ref, g2_ref,
                     kk_ref, ka_ref, rk_ref, bones_ref,
                     r_o, v_o, kkn_o, g_o, bonus_o, lw_o, a_o, kd_o, *, lct):
    j = pl.program_id(1)
    z = z_ref[0]
    prev_row, next_row = _halo_rows(zp_ref, zn_ref, j, lct, pl.num_programs(1))
    zprev, znext = _shifted_rows(z, prev_row, next_row)
    zm = z + mu_ref[...] * (0.5 * (zprev + znext) - z)
    w = B_WIDTH
    r, k, v = zm[:, 0:w], zm[:, w:2 * w], zm[:, 2 * w:3 * w]
    wa = zm[:, 3 * w:3 * w + B_DECAY_LORA + B_AAA_LORA]
    gl = zm[:, 3 * w + B_DECAY_LORA + B_AAA_LORA:]
    bones = bones_ref[...]
    kk = k * kk_ref[...]
    kkn = kk / jnp.maximum(jnp.sqrt(_per_head_sum(kk * kk, bones)), 1e-12)
    r_o[0] = r
    v_o[0] = v
    kkn_o[0] = kkn
    g_o[0] = _mm(_sigmoid(gl), g2_ref[...], HI)
    twa = jnp.tanh(wa)
    bonus = jnp.zeros_like(r)
    for d in range(2):
        wraw = w0_ref[d] + _mm(twa, w2_ref[d], HI)
        lw_o[d, 0] = -jnp.exp(-_softplus(-wraw) - 0.5)
        a = _sigmoid(a0_ref[d] + _mm(wa, a2_ref[d], HI))
        kd = k * (1.0 + (a - 1.0) * ka_ref[...])
        a_o[d, 0] = a
        kd_o[d, 0] = kd
        bonus = bonus + _per_head_sum(r * kd * rk_ref[...], bones) * v
    bonus_o[0] = bonus


def _rwkv_pre(zr, mu, w0, w2, a0, a2, g2, k_k, k_a, r_k, lc):
    bsz, ln, width = zr.shape
    tm = TOK_TILE
    w = B_WIDTH
    zeros = jnp.zeros((2, B_DECAY_LORA, w), F32)
    w2p = jnp.concatenate([w2, zeros], axis=1)
    a2p = jnp.concatenate([zeros, a2], axis=1)
    bones = _head_block_ones(B_HEAD_DIM, 1.0)
    full = lambda *shape: pl.BlockSpec(shape, lambda b, j: (0,) * len(shape))
    tok = pl.BlockSpec((1, tm, w), lambda b, j: (b, j, 0))
    tok2 = pl.BlockSpec((2, 1, tm, w), lambda b, j: (0, b, j, 0))
    sd = jax.ShapeDtypeStruct((bsz, ln, w), F32)
    sd2 = jax.ShapeDtypeStruct((2, bsz, ln, w), F32)
    return pl.pallas_call(
        functools.partial(_rwkv_pre_kernel, lct=lc // tm),
        grid=(bsz, ln // tm),
        in_specs=_halo_specs(tm, width, ln) + [
            full(1, width), full(2, 1, w), full(2, 2 * B_DECAY_LORA, w), full(2, 1, w),
            full(2, 2 * B_AAA_LORA, w), full(B_GATE_LORA, w), full(1, w), full(1, w), full(1, w),
            full(LANES, LANES)],
        out_specs=[tok, tok, tok, tok, tok, tok2, tok2, tok2],
        out_shape=[sd, sd, sd, sd, sd, sd2, sd2, sd2],
        compiler_params=_params("arbitrary", "arbitrary"),
        name="rwkv_pre",
    )(zr, zr, zr, mu.reshape(1, width), w0.reshape(2, 1, w), w2p, a0.reshape(2, 1, w), a2p, g2,
      k_k.reshape(1, w), k_a.reshape(1, w), r_k.reshape(1, w), bones)


def _scan_chunk(i, d, nctx, ntot):
    rev = jnp.where(i < nctx, nctx - 1 - i, ntot + nctx - 1 - i)
    return jnp.where(d == 0, i, rev)


def _order_masks(n, pair):
    t = np.arange(n * pair)
    same = (t[:, None] // n) == (t[None, :] // n)
    tt, ss = t[:, None] % n, t[None, :] % n
    incl = np.stack([same & (ss <= tt), same & (ss >= tt)]).astype(np.float32)
    strict = np.stack([same & (ss < tt), same & (ss > tt)]).astype(np.float32)
    return jnp.asarray(incl), jnp.asarray(strict)


def _pair_stack(x, low):
    return jnp.concatenate([jnp.where(low, x, 0.0), jnp.where(low, 0.0, x)], axis=0)


def _rwkv_chunk_kernel(r_ref, v_ref, kkn_ref, lw_ref, a_ref, kd_ref, tri_ref, incl_ref, strict_ref,
                       o_ref, state_ref):
    c = R_CHUNK
    i = pl.program_id(2)

    @pl.when(i == 0)
    def _():
        state_ref[...] = jnp.zeros_like(state_ref)

    lw = lw_ref[0, 0]
    cum = _mm(tri_ref[0], lw, HI)
    p_in = jnp.exp(cum)
    p_inv = jnp.exp(-cum)
    p_ex = jnp.exp(cum - lw)
    p_all = jnp.exp(jnp.sum(lw, axis=0, keepdims=True))
    kkn = kkn_ref[0]
    kt = kkn * p_ex
    bt = kkn * a_ref[0, 0] * p_inv
    kd = kd_ref[0, 0] * p_inv
    rt = r_ref[0] * p_in
    v = v_ref[0]
    incl = incl_ref[0] > 0.5
    strict = strict_ref[0] > 0.5
    low = lax.broadcasted_iota(jnp.int32, (c, LANES), 1) < B_HEAD_DIM
    eye = (lax.broadcasted_iota(jnp.int32, (LANES, LANES), 0)
           == lax.broadcasted_iota(jnp.int32, (LANES, LANES), 1)).astype(F32)
    n2 = 2 * c
    for m in range(B_WIDTH // LANES):
        sl = slice(m * LANES, (m + 1) * LANES)
        kts, bts = _pair_stack(kt[:, sl], low), _pair_stack(bt[:, sl], low)
        kds, rts = _pair_stack(kd[:, sl], low), _pair_stack(rt[:, sl], low)
        vs = _pair_stack(v[:, sl], low)
        gram = _mm_nt(jnp.concatenate([kts, rts], axis=0), jnp.concatenate([bts, kds], axis=0), HI)
        a_b = jnp.where(strict, gram[:n2, :n2], 0.0)
        a_k = jnp.where(strict, gram[:n2, n2:], 0.0)
        m_b = jnp.where(incl, gram[n2:, :n2], 0.0)
        m_k = jnp.where(incl, gram[n2:, n2:], 0.0)
        pw = -a_b
        tinv = eye + pw
        for _ in range(int(math.log2(c)) - 1):
            pw = _mm(pw, pw, HI)
            tinv = tinv + _mm(tinv, pw, HI)
        mv = _mm(jnp.concatenate([a_k, m_k], axis=0), vs, HI)
        wy = _mm(tinv, jnp.concatenate([kts, mv[:n2]], axis=1), HI)
        mbwy = _mm(m_b, wy, HI)
        q = rts - mbwy[:, :LANES]
        o_loc = mv[n2:] - mbwy[:, LANES:]
        wyb = _mm_tn(wy, bts, HI)
        vtk = _mm_tn(vs, kds, HI)
        pc = p_all[:, sl]
        g = (eye - wyb[:LANES]) * pc
        h = (vtk - wyb[LANES:]) * pc
        s0 = state_ref[m]
        o_st = _mm_nt(q, s0, HI) + o_loc
        o_ref[0, 0, :, sl] = o_st[:c] + o_st[c:]
        state_ref[m] = _mm(s0, g, HI) + h


def _rwkv_chunks(r, v, kkn, lw, a, kd, lc):
    bsz, ln, w = r.shape
    c = R_CHUNK
    nctx, ntot = lc // c, ln // c
    tri, _ = _order_masks(c, 1)
    incl, strict = _order_masks(c, 2)
    cm = lambda b, d, i: (b, _scan_chunk(i, d, nctx, ntot), 0)
    cm2 = lambda b, d, i: (d, b, _scan_chunk(i, d, nctx, ntot), 0)
    tok = pl.BlockSpec((1, c, w), cm)
    tok2 = pl.BlockSpec((1, 1, c, w), cm2)
    msk = lambda n: pl.BlockSpec((1, n, n), lambda b, d, i: (d, 0, 0))
    return pl.pallas_call(
        _rwkv_chunk_kernel,
        grid=(bsz, 2, ntot),
        in_specs=[tok, tok, tok, tok2, tok2, tok2, msk(c), msk(2 * c), msk(2 * c)],
        out_specs=tok2,
        out_shape=jax.ShapeDtypeStruct((2, bsz, ln, w), F32),
        scratch_shapes=[pltpu.VMEM((w // LANES, LANES, LANES), F32)],
        compiler_params=_params("arbitrary", "arbitrary", "arbitrary"),
        name="rwkv_chunks",
    )(r, v, kkn, lw, a, kd, tri, incl, strict)


def _even_out_kernel(x_ref, mod_ref, attn_ref, wkv_ref, bonus_ref, g_ref, gw_ref, gb_ref, bones_ref,
                     w_ref, o_ref):
    bones = bones_ref[...]
    y = bonus_ref[0]
    for d in range(2):
        o = wkv_ref[d, 0]
        dlt = o - _per_head_sum(o, bones)
        var = _per_head_sum(dlt * dlt, bones)
        y = y + dlt * lax.rsqrt(var + B_GN_EPS) * gw_ref[...] + gb_ref[...]
    y = y * g_ref[0]
    mix = _mm(attn_ref[0].astype(BF16), w_ref[:A_Q, :]) + _mm(y.astype(BF16), w_ref[A_Q:, :])
    o_ref[0] = x_ref[0] + mod_ref[0, 0][2:3] * mix


def _even_out(x, modx, attn, wkv, bonus, g, gn_w, gn_b, w_out, lc):
    bsz, ln, d = x.shape
    tm = TOK_TILE
    lct = lc // tm
    w = B_WIDTH
    bones = _head_block_ones(B_HEAD_DIM, 1.0 / B_HEAD_DIM)
    tokw = lambda n: pl.BlockSpec((1, tm, n), lambda b, j: (b, j, 0))
    full = lambda *shape: pl.BlockSpec(shape, lambda b, j: (0,) * len(shape))
    return pl.pallas_call(
        _even_out_kernel,
        grid=(bsz, ln // tm),
        in_specs=[tokw(d),
                  pl.BlockSpec((1, 1, N_MOD, d), lambda b, j: (b, jnp.where(j < lct, 0, 1), 0, 0)),
                  tokw(A_Q),
                  pl.BlockSpec((2, 1, tm, w), lambda b, j: (0, b, j, 0)),
                  tokw(w), tokw(w), full(1, w), full(1, w), full(LANES, LANES),
                  full(A_Q + w, d)],
        out_specs=tokw(d),
        out_shape=jax.ShapeDtypeStruct((bsz, ln, d), F32),
        compiler_params=_params("arbitrary", "arbitrary"),
        name="even_out",
    )(x, modx, attn, wkv, bonus, g, gn_w.reshape(1, w), gn_b.reshape(1, w), bones, w_out)


def _ssd_pre_kernel(z_ref, zp_ref, zn_ref, dt_ref, cw_ref, cb_ref, dtb_ref, x_o, dt_o, *, lct):
    j = pl.program_id(1)
    z = z_ref[0]
    prev_row, next_row = _halo_rows(zp_ref, zn_ref, j, lct, pl.num_programs(1))
    zprev, znext = _shifted_rows(z, prev_row, next_row)
    y = zprev * cw_ref[0:1, :] + z * cw_ref[1:2, :] + znext * cw_ref[2:3, :] + cb_ref[...]
    x_o[0] = _silu(y)
    dt_o[0] = _softplus(dt_ref[0] + dtb_ref[...])


def _ssd_pre(xbc, dtraw, conv_w, conv_b, dtb, lc):
    bsz, ln, width = xbc.shape
    tm = TOK_TILE
    nd = dtraw.shape[-1]
    full = lambda *shape: pl.BlockSpec(shape, lambda b, j: (0,) * len(shape))
    return pl.pallas_call(
        functools.partial(_ssd_pre_kernel, lct=lc // tm),
        grid=(bsz, ln // tm),
        in_specs=_halo_specs(tm, width, ln) + [
            pl.BlockSpec((1, tm, nd), lambda b, j: (b, j, 0)),
            full(conv_w.shape[0], width), full(1, width), full(1, nd)],
        out_specs=[pl.BlockSpec((1, tm, width), lambda b, j: (b, j, 0)),
                   pl.BlockSpec((1, tm, nd), lambda b, j: (b, j, 0))],
        out_shape=[jax.ShapeDtypeStruct((bsz, ln, width), F32),
                   jax.ShapeDtypeStruct((bsz, ln, nd), F32)],
        compiler_params=_params("arbitrary", "arbitrary"),
        name="ssd_pre",
    )(xbc, xbc, xbc, dtraw, conv_w, conv_b.reshape(1, width), dtb)


def _ssd_chunk_kernel(x_ref, b_ref, c_ref, dt_ref, alog_ref, incl_ref, expand_ref, o_ref, state_ref,
                      *, n_heads):
    q = C_CHUNK
    i = pl.program_id(2)

    @pl.when(i == 0)
    def _():
        state_ref[...] = jnp.zeros_like(state_ref)

    lane = lax.broadcasted_iota(jnp.int32, (1, LANES), 1)
    a_neg = jnp.where(lane < n_heads, -jnp.exp(alog_ref[0]), 0.0)
    dt = dt_ref[0]
    dta = dt * a_neg
    incl_f = incl_ref[0]
    cum = _mm(incl_f, dta, HI)
    cum_t = cum.T
    total = jnp.sum(dta, axis=0, keepdims=True)
    expand = expand_ref[...]
    dt_x = _mm(dt, expand, HI)
    ec_x = _mm(jnp.exp(cum), expand, HI)
    de_x = _mm(jnp.exp(total - cum), expand, HI)
    cd_x = _mm(jnp.broadcast_to(jnp.exp(total), (8, LANES)), expand, HI)[0:1]
    incl = incl_f > 0.5
    low = lax.broadcasted_iota(jnp.int32, (q, LANES), 1) < C_HEAD_DIM
    heads_per_group = n_heads // C_GROUPS
    pairs_per_group = heads_per_group // 2
    for g in range(C_GROUPS):
        bg = b_ref[0, :, g * C_STATE:(g + 1) * C_STATE]
        cg = c_ref[0, :, g * C_STATE:(g + 1) * C_STATE]
        cb = _mm_nt(cg, bg, HI)
        for pp in range(pairs_per_group):
            pr = g * pairs_per_group + pp
            sl = slice(pr * LANES, (pr + 1) * LANES)
            ms = []
            for hh in range(2):
                h = 2 * pr + hh
                seg = cum[:, h:h + 1] - cum_t[h:h + 1, :]
                ms.append(cb * jnp.exp(jnp.where(incl, seg, -1e30)))
            xd = x_ref[0, :, sl] * dt_x[:, sl]
            y_diag = _mm(jnp.concatenate(ms, axis=1), _pair_stack(xd, low), HI)
            s_prev = state_ref[pr]
            y_off = _mm(cg, s_prev, HI) * ec_x[:, sl]
            o_ref[0, 0, :, sl] = y_diag + y_off
            state_ref[pr] = s_prev * cd_x[:, sl] + _mm_tn(bg, xd * de_x[:, sl], HI)


def _ssd_chunks(xact, dtact, a_log, lc):
    bsz, ln, _ = xact.shape
    q = C_CHUNK
    n_heads = a_log.shape[1]
    inner = n_heads * C_HEAD_DIM
    gn = C_GROUPS * C_STATE
    nctx, ntot = lc // q, ln // q
    incl, _ = _order_masks(q, 1)
    expand = np.zeros((LANES, inner), np.float32)
    expand[np.arange(inner) // C_HEAD_DIM, np.arange(inner)] = 1.0
    alog = jnp.zeros((2, 1, LANES), F32).at[:, 0, :n_heads].set(a_log)
    cmap = lambda off: (lambda b, d, i: (b, _scan_chunk(i, d, nctx, ntot), off))
    return pl.pallas_call(
        functools.partial(_ssd_chunk_kernel, n_heads=n_heads),
        grid=(bsz, 2, ntot),
        in_specs=[pl.BlockSpec((1, q, inner), cmap(0)),
                  pl.BlockSpec((1, q, gn), cmap(inner // gn)),
                  pl.BlockSpec((1, q, gn), cmap(inner // gn + 1)),
                  pl.BlockSpec((1, q, LANES), lambda b, d, i: (b, _scan_chunk(i, d, nctx, ntot), d)),
                  pl.BlockSpec((1, 1, LANES), lambda b, d, i: (d, 0, 0)),
                  pl.BlockSpec((1, q, q), lambda b, d, i: (d, 0, 0)),
                  pl.BlockSpec((LANES, inner), lambda b, d, i: (0, 0))],
        out_specs=pl.BlockSpec((1, 1, q, inner), lambda b, d, i: (d, b, _scan_chunk(i, d, nctx, ntot), 0)),
        out_shape=jax.ShapeDtypeStruct((2, bsz, ln, inner), F32),
        scratch_shapes=[pltpu.VMEM((inner // LANES, C_STATE, LANES), F32)],
        compiler_params=_params("arbitrary", "arbitrary", "arbitrary"),
        name="ssd_chunks",
    )(xact, xact, xact, dtact, alog, incl, jnp.asarray(expand))


def _odd_out_kernel(x_ref, mod_ref, ys_ref, xs_ref, zg_ref, dsk_ref, nw_ref, w_ref, o_ref):
    y = ys_ref[0, 0] + ys_ref[1, 0] + dsk_ref[...] * xs_ref[0]
    yg = y * _silu(zg_ref[0])
    inner = yg.shape[1]
    gw = inner // C_GROUPS
    parts = []
    for g in range(C_GROUPS):
        t = yg[:, g * gw:(g + 1) * gw]
        ms = jnp.mean(t * t, axis=-1, keepdims=True)
        parts.append((t * lax.rsqrt(ms + EPS) * nw_ref[:, g * gw:(g + 1) * gw]).astype(BF16))
    yn = jnp.concatenate(parts, axis=1)
    o_ref[0] = x_ref[0] + mod_ref[0, 0][2:3] * _mm(yn, w_ref[...])


def _odd_out(x, modx, ys, xact, zg, d_skip, norm_w, w_out, lc):
    bsz, ln, d = x.shape
    tm = TOK_TILE
    lct = lc // tm
    inner = zg.shape[-1]
    dsk = jnp.repeat(d_skip, C_HEAD_DIM).reshape(1, inner)
    tokw = lambda n: pl.BlockSpec((1, tm, n), lambda b, j: (b, j + lct, 0))
    full = lambda *shape: pl.BlockSpec(shape, lambda b, j: (0,) * len(shape))
    return pl.pallas_call(
        _odd_out_kernel,
        grid=(bsz, (ln - lc) // tm),
        in_specs=[tokw(d),
                  pl.BlockSpec((1, 1, N_MOD, d), lambda b, j: (b, 1, 0, 0)),
                  pl.BlockSpec((2, 1, tm, inner), lambda b, j: (0, b, j + lct, 0)),
                  tokw(inner), tokw(inner), full(1, inner), full(1, inner), full(inner, d)],
        out_specs=pl.BlockSpec((1, tm, d), lambda b, j: (b, j, 0)),
        out_shape=jax.ShapeDtypeStruct((bsz, ln - lc, d), F32),
        compiler_params=_params("arbitrary", "arbitrary"),
        name="odd_out",
    )(x, modx, ys, xact, zg, dsk, norm_w.reshape(1, inner), w_out)


def _extract_top(x, count):
    rows = x.shape[0]
    idx = lax.broadcasted_iota(jnp.int32, x.shape, 0)
    vals = []
    for _ in range(count):
        m = jnp.max(x, axis=0, keepdims=True)
        first = jnp.min(jnp.where(x == m, idx, rows), axis=0, keepdims=True)
        x = jnp.where(idx == first, -jnp.inf, x)
        vals.append(m)
    return jnp.concatenate(vals, axis=0)


def _peer_query_kernel(x_ref, mod_ref, g_ref, wq_ref, keys_ref, ht_o, s1_o, s2_o, e2_o, c_o, tau_o):
    mod = mod_ref[0]
    h2 = _norm_mod(x_ref[...], g_ref[...], mod[3:4], mod[4:5])
    ht = h2.T
    ht_o[...] = ht.astype(BF16)
    qt = _mm(wq_ref[...], ht, HI)
    kd = P_KEY_DIM
    for h in range(P_HEADS):
        s1 = _mm(keys_ref[h, 0], qt[(2 * h) * kd:(2 * h + 1) * kd], HI)
        s2 = _mm(keys_ref[h, 1], qt[(2 * h + 1) * kd:(2 * h + 2) * kd], HI)
        t1 = _extract_top(s1, P_TOPK)
        t2 = _extract_top(s2, P_TOPK)
        cand = jnp.concatenate([t1[r:r + 1] + t2 for r in range(P_TOPK)], axis=0)
        best = _extract_top(cand, P_TOPK)
        z = jnp.sum(jnp.exp(best - best[0:1]), axis=0, keepdims=True)
        s1_o[h] = s1
        s2_o[h] = s2
        e2_o[h] = jnp.exp(s2 - t2[0:1])
        c_o[h] = jnp.exp(s1 - t1[0:1]) / z
        tau_o[h] = jnp.broadcast_to(best[P_TOPK - 1:P_TOPK], (8, s1.shape[1]))


def _peer_query(xt, mod, gain, wq_t, keys, blocks_per_mod, tt):
    t_all, d = xt.shape
    nq = wq_t.shape[0]
    nblk = t_all // tt
    head = lambda rows: pl.BlockSpec((P_HEADS, rows, tt), lambda i: (0, 0, i))
    hsd = lambda rows: jax.ShapeDtypeStruct((P_HEADS, rows, t_all), F32)
    return pl.pallas_call(
        _peer_query_kernel,
        grid=(nblk,),
        in_specs=[pl.BlockSpec((tt, d), lambda i: (i, 0)),
                  pl.BlockSpec((1, N_MOD, d), lambda i: (i // blocks_per_mod, 0, 0)),
                  pl.BlockSpec((1, d), lambda i: (0, 0)),
                  pl.BlockSpec((nq, d), lambda i: (0, 0)),
                  pl.BlockSpec((P_HEADS, 2, P_KEYS, P_KEY_DIM), lambda i: (0, 0, 0, 0))],
        out_specs=[pl.BlockSpec((d, tt), lambda i: (0, i)),
                   head(P_KEYS), head(P_KEYS), head(P_KEYS), head(P_KEYS), head(8)],
        out_shape=[jax.ShapeDtypeStruct((d, t_all), BF16),
                   hsd(P_KEYS), hsd(P_KEYS), hsd(P_KEYS), hsd(P_KEYS), hsd(8)],
        compiler_params=_params("arbitrary"),
        name="peer_query",
    )(xt, mod, gain.reshape(1, d), wq_t, keys)


def _gelu_exact(x):
    return 0.5 * x * (1.0 + lax.erf(x * (2.0 ** -0.5)))


def _peer_main_kernel(x_ref, mod_ref, ht_ref, u_ref, vt_ref, s1_ref, s2_ref, e2_ref, c_ref, tau_ref,
                      o_ref, acc_ref):
    step = pl.program_id(1)

    @pl.when(step == 0)
    def _():
        acc_ref[...] = jnp.zeros_like(acc_ref)

    act = _gelu_exact(_mm(u_ref[...], ht_ref[...]))
    rows_per_step = u_ref.shape[0] // P_KEYS
    parts = []
    for ii in range(rows_per_step):
        row = step * rows_per_step + ii
        coef = None
        for h in range(P_HEADS):
            s1 = s1_ref[h, pl.ds(row, 1), :]
            c = c_ref[h, pl.ds(row, 1), :]
            sel = (s1 + s2_ref[h]) >= tau_ref[h, 0:1, :]
            term = jnp.where(sel, e2_ref[h] * c, 0.0)
            coef = term if coef is None else coef + term
        parts.append((coef * act[ii * P_KEYS:(ii + 1) * P_KEYS]).astype(BF16))
    acc_ref[...] += _mm(vt_ref[...], jnp.concatenate(parts, axis=0))

    @pl.when(step == pl.num_programs(1) - 1)
    def _():
        o_ref[...] = x_ref[...] + mod_ref[0][5:6] * acc_ref[...].T


def _peer_main(xt, mod, ht, u_tab, vt_tab, s1, s2, e2, c, tau, blocks_per_mod, tt):
    t_all, d = xt.shape
    n_exp = u_tab.shape[0]
    eb = PEER_EXP
    head = lambda rows: pl.BlockSpec((P_HEADS, rows, tt), lambda i, s: (0, 0, i))
    return pl.pallas_call(
        _peer_main_kernel,
        grid=(t_all // tt, n_exp // eb),
        in_specs=[pl.BlockSpec((tt, d), lambda i, s: (i, 0)),
                  pl.BlockSpec((1, N_MOD, d), lambda i, s: (i // blocks_per_mod, 0, 0)),
                  pl.BlockSpec((d, tt), lambda i, s: (0, i)),
                  pl.BlockSpec((eb, d), lambda i, s: (s, 0)),
                  pl.BlockSpec((d, eb), lambda i, s: (0, s)),
                  head(P_KEYS), head(P_KEYS), head(P_KEYS), head(P_KEYS), head(8)],
        out_specs=pl.BlockSpec((tt, d), lambda i, s: (i, 0)),
        out_shape=jax.ShapeDtypeStruct((t_all, d), F32),
        scratch_shapes=[pltpu.VMEM((d, tt), F32)],
        compiler_params=_params("arbitrary", "arbitrary"),
        name="peer_main",
    )(xt, mod, ht, u_tab, vt_tab, s1, s2, e2, c, tau)


def _peer_layer(xt, mod, gain, w_q, keys, u_tab, v_tab, blocks_per_mod, tt):
    wq_t = w_q.T
    tq = PEER_QUERY_TOK
    ht, s1, s2, e2, c, tau = _peer_query(xt, mod, gain, wq_t, keys, blocks_per_mod * (tt // tq), tq)
    return _peer_main(xt, mod, ht, u_tab.astype(BF16), v_tab.T.astype(BF16), s1, s2, e2, c, tau,
                      blocks_per_mod, tt)


def kernel(x, c, ctx, c_ctx, mod_w, mod_b, norm1_g, norm2_g, ev_w_in, ev_w_out, attn_q_gain, attn_k_gain,
           rw_mu, rw_w0, rw_w2, rw_a0, rw_a2, rw_g2, rw_k_k, rw_k_a, rw_r_k, rw_gn_w, rw_gn_b, ssd_w_in,
           ssd_conv_w, ssd_conv_b, ssd_dt_bias, ssd_a_log, ssd_d, ssd_norm_w, ssd_w_out, peer_w_q,
           peer_keys, peer_u, peer_v):
    bsz, seq, dm = x.shape
    lc = ctx.shape[1]
    depth = mod_w.shape[0]
    assert lc == TOK_TILE and seq % PEER_TOK == 0 and (bsz * lc) % PEER_TOK == 0

    rows = -(-(bsz + 1) // 8) * 8
    cc = jnp.concatenate([c, c_ctx[None], jnp.zeros((rows - bsz - 1, dm), F32)], axis=0)
    mods = _modulation(cc, mod_w, mod_b)
    xall = jnp.concatenate([ctx, x], axis=1)

    for i in range(depth):
        last = i == depth - 1
        j = i // 2
        ml = mods[i, :bsz].reshape(bsz, N_MOD, dm)
        mc = mods[i, bsz].reshape(1, N_MOD, dm)
        modx = jnp.stack([jnp.broadcast_to(mc, (bsz, N_MOD, dm)), ml], axis=1)
        if i % 2 == 0:
            w_in = ev_w_in[j].astype(BF16)
            za = _in_proj(xall, modx, norm1_g[i], w_in[:, :A_IN], lc)
            zr = _in_proj(xall, modx, norm1_g[i], w_in[:, A_IN:], lc)
            q, k, v = _attn_pre(za, attn_q_gain[j], attn_k_gain[j], lc)
            attn = _attention(q, k, v, lc)
            r, vv, kkn, g, bonus, lw, a, kd = _rwkv_pre(
                zr, rw_mu[j], rw_w0[j], rw_w2[j], rw_a0[j], rw_a2[j], rw_g2[j], rw_k_k[j], rw_k_a[j],
                rw_r_k[j], lc)
            wkv = _rwkv_chunks(r, vv, kkn, lw, a, kd, lc)
            xall = _even_out(xall, modx, attn, wkv, bonus, g, rw_gn_w[j], rw_gn_b[j],
                             ev_w_out[j].astype(BF16), lc)
            xlat = xall[:, lc:]
        else:
            n_heads = ssd_a_log.shape[-1]
            inner = n_heads * C_HEAD_DIM
            conv_dim = inner + 2 * C_GROUPS * C_STATE
            w_in = ssd_w_in[j]
            w_dt = jnp.zeros((dm, 2 * LANES), F32)
            w_dt = w_dt.at[:, :n_heads].set(w_in[:, inner + conv_dim:inner + conv_dim + n_heads])
            w_dt = w_dt.at[:, LANES:LANES + n_heads].set(w_in[:, inner + conv_dim + n_heads:])
            dtb = jnp.zeros((1, 2 * LANES), F32)
            dtb = dtb.at[0, :n_heads].set(ssd_dt_bias[j, 0]).at[0, LANES:LANES + n_heads].set(ssd_dt_bias[j, 1])
            zg = _in_proj(xall, modx, norm1_g[i], w_in[:, :inner].astype(BF16), lc)
            xbc = _in_proj(xall, modx, norm1_g[i], w_in[:, inner:inner + conv_dim].astype(BF16), lc)
            dtraw = _in_proj(xall, modx, norm1_g[i], w_dt.astype(BF16), lc)
            xact, dtact = _ssd_pre(xbc, dtraw, ssd_conv_w[j], ssd_conv_b[j], dtb, lc)
            ys = _ssd_chunks(xact, dtact, ssd_a_log[j], lc)
            assert last, "the SSD layer keeps only the latent stream"
            xlat = _odd_out(xall, modx, ys, xact, zg, ssd_d[j], ssd_norm_w[j],
                            ssd_w_out[j].astype(BF16), lc)

        tt = PEER_TOK
        xl2 = _peer_layer(xlat.reshape(bsz * seq, dm), ml, norm2_g[i], peer_w_q[i], peer_keys[i],
                          peer_u[i], peer_v[i], seq // tt, tt).reshape(bsz, seq, dm)
        if last:
            return xl2
        xc2 = _peer_layer(xall[:, :lc].reshape(bsz * lc, dm), mc, norm2_g[i], peer_w_q[i], peer_keys[i],
                          peer_u[i], peer_v[i], (bsz * lc) // tt, tt).reshape(bsz, lc, dm)
        xall = jnp.concatenate([xc2, xl2], axis=1)
    return xall[:, lc:]
```

```python
import functools
import math

import numpy as np
import jax
import jax.numpy as jnp
from jax import lax
from jax.experimental import pallas as pl
from jax.experimental.pallas import tpu as pltpu

F32 = jnp.float32
BF16 = jnp.bfloat16
HI = lax.Precision.HIGHEST

GRID_W = 64
EPS = 1e-6
N_MOD = 6
A_HEADS, A_KV_HEADS, A_HEAD_DIM = 8, 2, 64
A_Q = A_HEADS * A_HEAD_DIM
A_KV = A_KV_HEADS * A_HEAD_DIM
A_IN = A_Q + 2 * A_KV
ROPE_THETA = 10000.0
B_HEADS, B_HEAD_DIM = 8, 64
B_WIDTH = B_HEADS * B_HEAD_DIM
B_DECAY_LORA, B_AAA_LORA, B_GATE_LORA = 64, 64, 128
B_IN = 3 * B_WIDTH + B_DECAY_LORA + B_AAA_LORA + B_GATE_LORA
B_GN_EPS = 64e-5
C_HEAD_DIM, C_GROUPS, C_STATE, C_CHUNK = 64, 4, 128, 128
P_HEADS, P_KEYS, P_KEY_DIM, P_TOPK = 8, 128, 128, 16

LANES = 128
TOK_TILE = 256
R_CHUNK = 64
PEER_TOK = 512
PEER_QUERY_TOK = 256
PEER_EXP = 512
VMEM_LIMIT = 56 * 1024 * 1024


def _mm(a, b, prec=None):
    return lax.dot_general(a, b, (((1,), (0,)), ((), ())), precision=prec,
                           preferred_element_type=F32)


def _mm_nt(a, b, prec=None):
    return lax.dot_general(a, b, (((1,), (1,)), ((), ())), precision=prec,
                           preferred_element_type=F32)


def _mm_tn(a, b, prec=None):
    return lax.dot_general(a, b, (((0,), (0,)), ((), ())), precision=prec,
                           preferred_element_type=F32)


def _params(*sem):
    return pltpu.CompilerParams(dimension_semantics=sem, vmem_limit_bytes=VMEM_LIMIT)


def _softplus(x):
    return jnp.maximum(x, 0.0) + jnp.log1p(jnp.exp(-jnp.abs(x)))


def _sigmoid(x):
    return 1.0 / (1.0 + jnp.exp(-x))


def _silu(x):
    return x * _sigmoid(x)


def _mod_kernel(c_ref, w_ref, b_ref, o_ref):
    o_ref[0] = _mm(_silu(c_ref[...]), w_ref[0], HI) + b_ref[0]


def _modulation(cc, mod_w, mod_b):
    depth, d, n6 = mod_w.shape
    rows = cc.shape[0]
    tn = 1024
    return pl.pallas_call(
        _mod_kernel,
        grid=(depth, n6 // tn),
        in_specs=[pl.BlockSpec((rows, d), lambda l, n: (0, 0)),
                  pl.BlockSpec((1, d, tn), lambda l, n: (l, 0, n)),
                  pl.BlockSpec((1, 1, tn), lambda l, n: (l, 0, n))],
        out_specs=pl.BlockSpec((1, rows, tn), lambda l, n: (l, 0, n)),
        out_shape=jax.ShapeDtypeStruct((depth, rows, n6), F32),
        compiler_params=_params("arbitrary", "arbitrary"),
        name="modulation",
    )(cc, mod_w, mod_b.reshape(depth, 1, n6))


def _norm_mod(x, gain, shift, scale):
    ms = jnp.mean(x * x, axis=-1, keepdims=True)
    return x * lax.rsqrt(ms + EPS) * gain * (1.0 + scale) + shift


def _in_proj_kernel(x_ref, mod_ref, g_ref, w_ref, o_ref):
    mod = mod_ref[0, 0]
    h = _norm_mod(x_ref[0], g_ref[...], mod[0:1], mod[1:2]).astype(BF16)
    n = o_ref.shape[-1]
    for c0 in range(0, n, 1024):
        c1 = min(n, c0 + 1024)
        o_ref[0, :, c0:c1] = _mm(h, w_ref[:, c0:c1])


def _in_proj(x, modx, gain, w, lc):
    bsz, ln, d = x.shape
    n = w.shape[1]
    tm = TOK_TILE
    lct = lc // tm
    return pl.pallas_call(
        _in_proj_kernel,
        grid=(bsz, ln // tm),
        in_specs=[pl.BlockSpec((1, tm, d), lambda b, j: (b, j, 0)),
                  pl.BlockSpec((1, 1, N_MOD, d), lambda b, j: (b, jnp.where(j < lct, 0, 1), 0, 0)),
                  pl.BlockSpec((1, d), lambda b, j: (0, 0)),
                  pl.BlockSpec((d, n), lambda b, j: (0, 0))],
        out_specs=pl.BlockSpec((1, tm, n), lambda b, j: (b, j, 0)),
        out_shape=jax.ShapeDtypeStruct((bsz, ln, n), F32),
        compiler_params=_params("arbitrary", "arbitrary"),
        name="in_proj",
    )(x, modx, gain.reshape(1, d), w)


def _rope_tables(lc, seq):
    t = jnp.arange(seq)
    row = (t // GRID_W).astype(F32)
    col = (t % GRID_W).astype(F32)
    m = A_HEAD_DIM // 4
    inv = ROPE_THETA ** (-jnp.arange(m, dtype=F32) / m)
    ar, ac = row[:, None] * inv, col[:, None] * inv
    cos = jnp.concatenate([jnp.cos(ar), jnp.cos(ar), jnp.cos(ac), jnp.cos(ac)], axis=-1)
    sin = jnp.concatenate([-jnp.sin(ar), jnp.sin(ar), -jnp.sin(ac), jnp.sin(ac)], axis=-1)
    reps = LANES // A_HEAD_DIM
    cos = jnp.concatenate([jnp.ones((lc, A_HEAD_DIM), F32), cos], axis=0)
    sin = jnp.concatenate([jnp.zeros((lc, A_HEAD_DIM), F32), sin], axis=0)
    return jnp.tile(cos, (1, reps)), jnp.tile(sin, (1, reps))


def _head_block_ones(width, scale):
    idx = np.arange(LANES) // width
    return jnp.asarray((idx[:, None] == idx[None, :]).astype(np.float32) * scale)


def _attn_pre_kernel(z_ref, qg_ref, kg_ref, cos_ref, sin_ref, bones_ref, q_ref, k_ref, v_ref):
    tm = z_ref.shape[1]
    cs, sn = cos_ref[...], sin_ref[...]
    bones = bones_ref[...]
    lane = lax.broadcasted_iota(jnp.int32, (tm, LANES), 1)
    first = (lane % 32) < 16
    low = lane < A_HEAD_DIM

    def norm_rope(x, gain):
        ms = _mm(x * x, bones, HI)
        y = x * lax.rsqrt(ms + EPS) * gain
        partner = jnp.where(first, pltpu.roll(y, LANES - 16, 1), pltpu.roll(y, 16, 1))
        return y * cs + partner * sn

    heads_per_slab = LANES // A_HEAD_DIM
    group = A_HEADS // A_KV_HEADS
    for m in range(A_Q // LANES):
        sl = slice(m * LANES, (m + 1) * LANES)
        y = norm_rope(z_ref[0, :, sl], qg_ref[:, sl]) * (A_HEAD_DIM ** -0.5)
        swapped = pltpu.roll(y, A_HEAD_DIM, 1)
        for hh in range(heads_per_slab):
            h = m * heads_per_slab + hh
            kvh = h // group
            src = y if kvh == hh else swapped
            keep = low if kvh == 0 else jnp.logical_not(low)
            q_ref[0, h] = jnp.where(keep, src, 0.0).astype(BF16)
    k_ref[0] = norm_rope(z_ref[0, :, A_Q:A_Q + A_KV], kg_ref[...]).astype(BF16)
    v_ref[0] = z_ref[0, :, A_Q + A_KV:A_IN].astype(BF16)


def _attn_pre(za, q_gain, k_gain, lc):
    bsz, ln, _ = za.shape
    tm = TOK_TILE
    cos, sin = _rope_tables(lc, ln - lc)
    qg = jnp.tile(q_gain, A_HEADS).reshape(1, A_Q)
    kg = jnp.tile(k_gain, A_KV_HEADS).reshape(1, A_KV)
    bones = _head_block_ones(A_HEAD_DIM, 1.0 / A_HEAD_DIM)
    return pl.pallas_call(
        _attn_pre_kernel,
        grid=(bsz, ln // tm),
        in_specs=[pl.BlockSpec((1, tm, A_IN), lambda b, j: (b, j, 0)),
                  pl.BlockSpec((1, A_Q), lambda b, j: (0, 0)),
                  pl.BlockSpec((1, A_KV), lambda b, j: (0, 0)),
                  pl.BlockSpec((tm, LANES), lambda b, j: (j, 0)),
                  pl.BlockSpec((tm, LANES), lambda b, j: (j, 0)),
                  pl.BlockSpec((LANES, LANES), lambda b, j: (0, 0))],
        out_specs=[pl.BlockSpec((1, A_HEADS, tm, LANES), lambda b, j: (b, 0, j, 0)),
                   pl.BlockSpec((1, tm, A_KV), lambda b, j: (b, j, 0)),
                   pl.BlockSpec((1, tm, A_KV), lambda b, j: (b, j, 0))],
        out_shape=[jax.ShapeDtypeStruct((bsz, A_HEADS, ln, LANES), BF16),
                   jax.ShapeDtypeStruct((bsz, ln, A_KV), BF16),
                   jax.ShapeDtypeStruct((bsz, ln, A_KV), BF16)],
        compiler_params=_params("arbitrary", "arbitrary"),
        name="attn_pre",
    )(za, qg, kg, cos, sin, bones)


def _attn_kernel(q_ref, k_ref, v_ref, o_ref, *, lc):
    tq = q_ref.shape[2]
    j = pl.program_id(1)
    lane = lax.broadcasted_iota(jnp.int32, (tq, LANES), 1)
    low = lane < A_HEAD_DIM
    heads_per_slab = LANES // A_HEAD_DIM
    group = A_HEADS // A_KV_HEADS

    def run(klen):
        k = k_ref[0, :klen, :]
        v = v_ref[0, :klen, :]
        for m in range(A_Q // LANES):
            parts = []
            for hh in range(heads_per_slab):
                h = m * heads_per_slab + hh
                kvh = h // group
                s = _mm_nt(q_ref[0, h], k)
                p = jnp.exp(s - jnp.max(s, axis=-1, keepdims=True))
                l = jnp.sum(p, axis=-1, keepdims=True)
                o = _mm(p.astype(BF16), v) / l
                parts.append(o if kvh == hh else pltpu.roll(o, A_HEAD_DIM, 1))
            o_ref[0, :, m * LANES:(m + 1) * LANES] = jnp.where(low, parts[0], parts[1])

    @pl.when(j * tq < lc)
    def _():
        run(lc)

    @pl.when(j * tq >= lc)
    def _():
        run(k_ref.shape[1])


def _attention(q, k, v, lc):
    bsz, _, ln, _ = q.shape
    tq = TOK_TILE
    return pl.pallas_call(
        functools.partial(_attn_kernel, lc=lc),
        grid=(bsz, ln // tq),
        in_specs=[pl.BlockSpec((1, A_HEADS, tq, LANES), lambda b, j: (b, 0, j, 0)),
                  pl.BlockSpec((1, ln, A_KV), lambda b, j: (b, 0, 0)),
                  pl.BlockSpec((1, ln, A_KV), lambda b, j: (b, 0, 0))],
        out_specs=pl.BlockSpec((1, tq, A_Q), lambda b, j: (b, j, 0)),
        out_shape=jax.ShapeDtypeStruct((bsz, ln, A_Q), F32),
        compiler_params=_params("arbitrary", "arbitrary"),
        name="attention",
    )(q, k, v)


def _shifted_rows(z, prev_row, next_row):
    tm = z.shape[0]
    row = lax.broadcasted_iota(jnp.int32, (tm, 1), 0)
    zprev = jnp.where(row == 0, prev_row, pltpu.roll(z, 1, 0))
    znext = jnp.where(row == tm - 1, next_row, pltpu.roll(z, tm - 1, 0))
    return zprev, znext


def _halo_rows(zp_ref, zn_ref, j, lct, nj):
    prev_ok = jnp.logical_and(j != 0, j != lct)
    next_ok = jnp.logical_and(j != lct - 1, j != nj - 1)
    prev_row = jnp.where(prev_ok, zp_ref[0, 7:8, :], 0.0)
    next_row = jnp.where(next_ok, zn_ref[0, 0:1, :], 0.0)
    return prev_row, next_row


def _halo_specs(tm, width, ln):
    nb8 = tm // 8
    last = ln // 8 - 1
    return [pl.BlockSpec((1, tm, width), lambda b, j: (b, j, 0)),
            pl.BlockSpec((1, 8, width), lambda b, j: (b, jnp.maximum(j * nb8 - 1, 0), 0)),
            pl.BlockSpec((1, 8, width), lambda b, j: (b, jnp.minimum((j + 1) * nb8, last), 0))]


def _per_head_sum(x, bones):
    return jnp.concatenate(
        [_mm(x[:, m * LANES:(m + 1) * LANES], bones, HI) for m in range(x.shape[1] // LANES)], axis=1)


def _rwkv_pre_kernel(z_ref, zp_ref, zn_ref, mu_ref, w0_ref, w2_ref, a0_ref, a2_ref, g2_ref,
                     kk_ref, ka_ref, rk_ref, bones_ref,
                     r_o, v_o, kkn_o, g_o, bonus_o, lw_o, a_o, kd_o, *, lct):
    j = pl.program_id(1)
    z = z_ref[0]
    prev_row, next_row = _halo_rows(zp_ref, zn_ref, j, lct, pl.num_programs(1))
    zprev, znext = _shifted_rows(z, prev_row, next_row)
    zm = z + mu_ref[...] * (0.5 * (zprev + znext) - z)
    w = B_WIDTH
    r, k, v = zm[:, 0:w], zm[:, w:2 * w], zm[:, 2 * w:3 * w]
    wa = zm[:, 3 * w:3 * w + B_DECAY_LORA + B_AAA_LORA]
    gl = zm[:, 3 * w + B_DECAY_LORA + B_AAA_LORA:]
    bones = bones_ref[...]
    kk = k * kk_ref[...]
    kkn = kk / jnp.maximum(jnp.sqrt(_per_head_sum(kk * kk, bones)), 1e-12)
    r_o[0] = r
    v_o[0] = v
    kkn_o[0] = kkn
    g_o[0] = _mm(_sigmoid(gl), g2_ref[...], HI)
    twa = jnp.tanh(wa)
    bonus = jnp.zeros_like(r)
    for d in range(2):
        wraw = w0_ref[d] + _mm(twa, w2_ref[d], HI)
        lw_o[d, 0] = -jnp.exp(-_softplus(-wraw) - 0.5)
        a = _sigmoid(a0_ref[d] + _mm(wa, a2_ref[d], HI))
        kd = k * (1.0 + (a - 1.0) * ka_ref[...])
        a_o[d, 0] = a
        kd_o[d, 0] = kd
        bonus = bonus + _per_head_sum(r * kd * rk_ref[...], bones) * v
    bonus_o[0] = bonus


def _rwkv_pre(zr, mu, w0, w2, a0, a2, g2, k_k, k_a, r_k, lc):
    bsz, ln, width = zr.shape
    tm = TOK_TILE
    w = B_WIDTH
    zeros = jnp.zeros((2, B_DECAY_LORA, w), F32)
    w2p = jnp.concatenate([w2, zeros], axis=1)
    a2p = jnp.concatenate([zeros, a2], axis=1)
    bones = _head_block_ones(B_HEAD_DIM, 1.0)
    full = lambda *shape: pl.BlockSpec(shape, lambda b, j: (0,) * len(shape))
    tok = pl.BlockSpec((1, tm, w), lambda b, j: (b, j, 0))
    tok2 = pl.BlockSpec((2, 1, tm, w), lambda b, j: (0, b, j, 0))
    sd = jax.ShapeDtypeStruct((bsz, ln, w), F32)
    sd2 = jax.ShapeDtypeStruct((2, bsz, ln, w), F32)
    return pl.pallas_call(
        functools.partial(_rwkv_pre_kernel, lct=lc // tm),
        grid=(bsz, ln // tm),
        in_specs=_halo_specs(tm, width, ln) + [
            full(1, width), full(2, 1, w), full(2, 2 * B_DECAY_LORA, w), full(2, 1, w),
            full(2, 2 * B_AAA_LORA, w), full(B_GATE_LORA, w), full(1, w), full(1, w), full(1, w),
            full(LANES, LANES)],
        out_specs=[tok, tok, tok, tok, tok, tok2, tok2, tok2],
        out_shape=[sd, sd, sd, sd, sd, sd2, sd2, sd2],
        compiler_params=_params("arbitrary", "arbitrary"),
        name="rwkv_pre",
    )(zr, zr, zr, mu.reshape(1, width), w0.reshape(2, 1, w), w2p, a0.reshape(2, 1, w), a2p, g2,
      k_k.reshape(1, w), k_a.reshape(1, w), r_k.reshape(1, w), bones)


def _scan_chunk(i, d, nctx, ntot):
    rev = jnp.where(i < nctx, nctx - 1 - i, ntot + nctx - 1 - i)
    return jnp.where(d == 0, i, rev)


def _order_masks(n, pair):
    t = np.arange(n * pair)
    same = (t[:, None] // n) == (t[None, :] // n)
    tt, ss = t[:, None] % n, t[None, :] % n
    incl = np.stack([same & (ss <= tt), same & (ss >= tt)]).astype(np.float32)
    strict = np.stack([same & (ss < tt), same & (ss > tt)]).astype(np.float32)
    return jnp.asarray(incl), jnp.asarray(strict)


def _pair_stack(x, low):
    return jnp.concatenate([jnp.where(low, x, 0.0), jnp.where(low, 0.0, x)], axis=0)


def _bmm(a, b):
    return _mm(a.astype(BF16), b.astype(BF16))


def _rwkv_chunk_kernel(rf_ref, vf_ref, kf_ref, lwf_ref, af_ref, kdf_ref,
                       rb_ref, vb_ref, kb_ref, lwb_ref, ab_ref, kdb_ref,
                       tri_ref, incl_ref, strict_ref, of_ref, ob_ref, state_ref):
    c = R_CHUNK
    n2 = 2 * c

    @pl.when(pl.program_id(1) == 0)
    def _():
        state_ref[...] = jnp.zeros_like(state_ref)

    low = lax.broadcasted_iota(jnp.int32, (c, LANES), 1) < B_HEAD_DIM
    eye = (lax.broadcasted_iota(jnp.int32, (LANES, LANES), 0)
           == lax.broadcasted_iota(jnp.int32, (LANES, LANES), 1)).astype(F32)
    stack = lambda x: _pair_stack(x, low)
    refs = ((rf_ref, vf_ref, kf_ref, lwf_ref, af_ref, kdf_ref, of_ref),
            (rb_ref, vb_ref, kb_ref, lwb_ref, ab_ref, kdb_ref, ob_ref))
    dirs, slabs, outs = [], [], []
    kts, bts, kds, rts, vs, pcs = [], [], [], [], [], []
    for d, (r_ref, v_ref, kkn_ref, lw_ref, a_ref, kd_ref, o_ref) in enumerate(refs):
        lw = lw_ref[0, 0]
        cum = _mm(tri_ref[d], lw, HI)
        p_in = jnp.exp(cum)
        p_inv = jnp.exp(-cum)
        p_ex = jnp.exp(cum - lw)
        p_all = jnp.exp(jnp.sum(lw, axis=0, keepdims=True))
        kkn = kkn_ref[0]
        kt = kkn * p_ex
        bt = kkn * a_ref[0, 0] * p_inv
        kd = kd_ref[0, 0] * p_inv
        rt = r_ref[0] * p_in
        v = v_ref[0]
        for m in range(B_WIDTH // LANES):
            sl = slice(m * LANES, (m + 1) * LANES)
            dirs.append(d)
            slabs.append(m)
            outs.append(o_ref)
            kts.append(stack(kt[:, sl]).astype(BF16))
            bts.append(stack(bt[:, sl]).astype(BF16))
            kds.append(stack(kd[:, sl]).astype(BF16))
            rts.append(stack(rt[:, sl]))
            vs.append(stack(v[:, sl]).astype(BF16))
            pcs.append(p_all[:, sl])
    n = len(dirs)
    incl = [incl_ref[d] > 0.5 for d in range(2)]
    strict = [strict_ref[d] > 0.5 for d in range(2)]
    gram = [_mm_nt(jnp.concatenate([kts[x], rts[x].astype(BF16)], axis=0),
                   jnp.concatenate([bts[x], kds[x]], axis=0)) for x in range(n)]
    a_b = [jnp.where(strict[dirs[x]], gram[x][:n2, :n2], 0.0) for x in range(n)]
    a_k = [jnp.where(strict[dirs[x]], gram[x][:n2, n2:], 0.0) for x in range(n)]
    m_b = [jnp.where(incl[dirs[x]], gram[x][n2:, :n2], 0.0) for x in range(n)]
    m_k = [jnp.where(incl[dirs[x]], gram[x][n2:, n2:], 0.0) for x in range(n)]
    pw = [-a for a in a_b]
    tinv = [eye + p for p in pw]
    for _ in range(int(math.log2(c)) - 1):
        pw = [_bmm(p, p) for p in pw]
        tinv = [t + _bmm(t, p) for t, p in zip(tinv, pw)]
    mv = [_bmm(jnp.concatenate([a_k[x], m_k[x]], axis=0), vs[x]) for x in range(n)]
    wy = [_bmm(tinv[x], jnp.concatenate([kts[x], mv[x][:n2].astype(BF16)], axis=1)) for x in range(n)]
    mbwy = [_bmm(m_b[x], wy[x]) for x in range(n)]
    wyb = [_mm_tn(wy[x].astype(BF16), bts[x]) for x in range(n)]
    vtk = [_mm_tn(vs[x], kds[x]) for x in range(n)]
    s0 = [state_ref[dirs[x], slabs[x]] for x in range(n)]
    o_st = [_mm_nt((rts[x] - mbwy[x][:, :LANES]).astype(BF16), s0[x].astype(BF16))
            + (mv[x][n2:] - mbwy[x][:, LANES:]) for x in range(n)]
    for x in range(n):
        g = (eye - wyb[x][:LANES]) * pcs[x]
        h = (vtk[x] - wyb[x][LANES:]) * pcs[x]
        outs[x][0, :, slabs[x] * LANES:(slabs[x] + 1) * LANES] = o_st[x][:c] + o_st[x][c:]
        state_ref[dirs[x], slabs[x]] = _bmm(s0[x], g) + h


def _rwkv_chunks(r, v, kkn, lw, a, kd, lc):
    bsz, ln, w = r.shape
    c = R_CHUNK
    nctx, ntot = lc // c, ln // c
    tri, _ = _order_masks(c, 1)
    incl, strict = _order_masks(c, 2)
    specs = []
    for d in range(2):
        tok = pl.BlockSpec((1, c, w), lambda b, i, d=d: (b, _scan_chunk(i, d, nctx, ntot), 0))
        tok2 = pl.BlockSpec((1, 1, c, w), lambda b, i, d=d: (d, b, _scan_chunk(i, d, nctx, ntot), 0))
        specs.append((tok, tok2))
    full = lambda *shape: pl.BlockSpec(shape, lambda b, i: (0,) * len(shape))
    sd = jax.ShapeDtypeStruct((bsz, ln, w), F32)
    return pl.pallas_call(
        _rwkv_chunk_kernel,
        grid=(bsz, ntot),
        in_specs=[specs[0][0]] * 3 + [specs[0][1]] * 3 + [specs[1][0]] * 3 + [specs[1][1]] * 3
        + [full(2, c, c), full(2, 2 * c, 2 * c), full(2, 2 * c, 2 * c)],
        out_specs=[specs[0][0], specs[1][0]],
        out_shape=[sd, sd],
        scratch_shapes=[pltpu.VMEM((2, w // LANES, LANES, LANES), F32)],
        compiler_params=_params("arbitrary", "arbitrary"),
        name="rwkv_chunks",
    )(r, v, kkn, lw, a, kd, r, v, kkn, lw, a, kd, tri, incl, strict)


def _even_out_kernel(x_ref, mod_ref, attn_ref, wkvf_ref, wkvb_ref, bonus_ref, g_ref, gw_ref, gb_ref,
                     bones_ref, w_ref, o_ref):
    bones = bones_ref[...]
    y = bonus_ref[0]
    for wkv_ref in (wkvf_ref, wkvb_ref):
        o = wkv_ref[0]
        dlt = o - _per_head_sum(o, bones)
        var = _per_head_sum(dlt * dlt, bones)
        y = y + dlt * lax.rsqrt(var + B_GN_EPS) * gw_ref[...] + gb_ref[...]
    y = y * g_ref[0]
    mix = _mm(attn_ref[0].astype(BF16), w_ref[:A_Q, :]) + _mm(y.astype(BF16), w_ref[A_Q:, :])
    o_ref[0] = x_ref[0] + mod_ref[0, 0][2:3] * mix


def _even_out(x, modx, attn, wkv, bonus, g, gn_w, gn_b, w_out, lc):
    bsz, ln, d = x.shape
    tm = TOK_TILE
    lct = lc // tm
    w = B_WIDTH
    bones = _head_block_ones(B_HEAD_DIM, 1.0 / B_HEAD_DIM)
    tokw = lambda n: pl.BlockSpec((1, tm, n), lambda b, j: (b, j, 0))
    full = lambda *shape: pl.BlockSpec(shape, lambda b, j: (0,) * len(shape))
    return pl.pallas_call(
        _even_out_kernel,
        grid=(bsz, ln // tm),
        in_specs=[tokw(d),
                  pl.BlockSpec((1, 1, N_MOD, d), lambda b, j: (b, jnp.where(j < lct, 0, 1), 0, 0)),
                  tokw(A_Q), tokw(w), tokw(w),
                  tokw(w), tokw(w), full(1, w), full(1, w), full(LANES, LANES),
                  full(A_Q + w, d)],
        out_specs=tokw(d),
        out_shape=jax.ShapeDtypeStruct((bsz, ln, d), F32),
        compiler_params=_params("arbitrary", "arbitrary"),
        name="even_out",
    )(x, modx, attn, wkv[0], wkv[1], bonus, g, gn_w.reshape(1, w), gn_b.reshape(1, w), bones, w_out)


def _ssd_pre_kernel(z_ref, zp_ref, zn_ref, dt_ref, cw_ref, cb_ref, dtb_ref, x_o, dt_o, *, lct):
    j = pl.program_id(1)
    z = z_ref[0]
    prev_row, next_row = _halo_rows(zp_ref, zn_ref, j, lct, pl.num_programs(1))
    zprev, znext = _shifted_rows(z, prev_row, next_row)
    y = zprev * cw_ref[0:1, :] + z * cw_ref[1:2, :] + znext * cw_ref[2:3, :] + cb_ref[...]
    x_o[0] = _silu(y)
    dt_o[0] = _softplus(dt_ref[0] + dtb_ref[...])


def _ssd_pre(xbc, dtraw, conv_w, conv_b, dtb, lc):
    bsz, ln, width = xbc.shape
    tm = TOK_TILE
    nd = dtraw.shape[-1]
    full = lambda *shape: pl.BlockSpec(shape, lambda b, j: (0,) * len(shape))
    return pl.pallas_call(
        functools.partial(_ssd_pre_kernel, lct=lc // tm),
        grid=(bsz, ln // tm),
        in_specs=_halo_specs(tm, width, ln) + [
            pl.BlockSpec((1, tm, nd), lambda b, j: (b, j, 0)),
            full(conv_w.shape[0], width), full(1, width), full(1, nd)],
        out_specs=[pl.BlockSpec((1, tm, width), lambda b, j: (b, j, 0)),
                   pl.BlockSpec((1, tm, nd), lambda b, j: (b, j, 0))],
        out_shape=[jax.ShapeDtypeStruct((bsz, ln, width), F32),
                   jax.ShapeDtypeStruct((bsz, ln, nd), F32)],
        compiler_params=_params("arbitrary", "arbitrary"),
        name="ssd_pre",
    )(xbc, xbc, xbc, dtraw, conv_w, conv_b.reshape(1, width), dtb)


def _ssd_chunk_kernel(x_ref, b_ref, c_ref, dt_ref, alog_ref, incl_ref, expand_ref, o_ref, state_ref,
                      *, n_heads):
    q = C_CHUNK
    i = pl.program_id(2)

    @pl.when(i == 0)
    def _():
        state_ref[...] = jnp.zeros_like(state_ref)

    lane = lax.broadcasted_iota(jnp.int32, (1, LANES), 1)
    a_neg = jnp.where(lane < n_heads, -jnp.exp(alog_ref[0]), 0.0)
    dt = dt_ref[0]
    dta = dt * a_neg
    incl_f = incl_ref[0]
    cum = _mm(incl_f, dta, HI)
    cum_t = cum.T
    total = jnp.sum(dta, axis=0, keepdims=True)
    expand = expand_ref[...]
    dt_x = _mm(dt, expand)
    ec_x = _mm(jnp.exp(cum), expand)
    de_x = _mm(jnp.exp(total - cum), expand)
    cd_x = _mm(jnp.broadcast_to(jnp.exp(total), (8, LANES)), expand)[0:1]
    incl = incl_f > 0.5
    low = lax.broadcasted_iota(jnp.int32, (q, LANES), 1) < C_HEAD_DIM
    heads_per_group = n_heads // C_GROUPS
    pairs_per_group = heads_per_group // 2
    for g in range(C_GROUPS):
        bg = b_ref[0, :, g * C_STATE:(g + 1) * C_STATE]
        cg = c_ref[0, :, g * C_STATE:(g + 1) * C_STATE]
        cb = _mm_nt(cg, bg)
        for pp in range(pairs_per_group):
            pr = g * pairs_per_group + pp
            sl = slice(pr * LANES, (pr + 1) * LANES)
            ms = []
            for hh in range(2):
                h = 2 * pr + hh
                seg = cum[:, h:h + 1] - cum_t[h:h + 1, :]
                ms.append(cb * jnp.exp(jnp.where(incl, seg, -1e30)))
            xd = x_ref[0, :, sl] * dt_x[:, sl]
            y_diag = _mm(jnp.concatenate(ms, axis=1), _pair_stack(xd, low))
            s_prev = state_ref[pr]
            y_off = _mm(cg, s_prev) * ec_x[:, sl]
            o_ref[0, 0, :, sl] = y_diag + y_off
            state_ref[pr] = s_prev * cd_x[:, sl] + _mm_tn(bg, xd * de_x[:, sl])


def _ssd_chunks(xact, dtact, a_log, lc):
    bsz, ln, _ = xact.shape
    q = C_CHUNK
    n_heads = a_log.shape[1]
    inner = n_heads * C_HEAD_DIM
    gn = C_GROUPS * C_STATE
    nctx, ntot = lc // q, ln // q
    incl, _ = _order_masks(q, 1)
    expand = np.zeros((LANES, inner), np.float32)
    expand[np.arange(inner) // C_HEAD_DIM, np.arange(inner)] = 1.0
    alog = jnp.zeros((2, 1, LANES), F32).at[:, 0, :n_heads].set(a_log)
    cmap = lambda off: (lambda b, d, i: (b, _scan_chunk(i, d, nctx, ntot), off))
    return pl.pallas_call(
        functools.partial(_ssd_chunk_kernel, n_heads=n_heads),
        grid=(bsz, 2, ntot),
        in_specs=[pl.BlockSpec((1, q, inner), cmap(0)),
                  pl.BlockSpec((1, q, gn), cmap(inner // gn)),
                  pl.BlockSpec((1, q, gn), cmap(inner // gn + 1)),
                  pl.BlockSpec((1, q, LANES), lambda b, d, i: (b, _scan_chunk(i, d, nctx, ntot), d)),
                  pl.BlockSpec((1, 1, LANES), lambda b, d, i: (d, 0, 0)),
                  pl.BlockSpec((1, q, q), lambda b, d, i: (d, 0, 0)),
                  pl.BlockSpec((LANES, inner), lambda b, d, i: (0, 0))],
        out_specs=pl.BlockSpec((1, 1, q, inner), lambda b, d, i: (d, b, _scan_chunk(i, d, nctx, ntot), 0)),
        out_shape=jax.ShapeDtypeStruct((2, bsz, ln, inner), F32),
        scratch_shapes=[pltpu.VMEM((inner // LANES, C_STATE, LANES), F32)],
        compiler_params=_params("arbitrary", "arbitrary", "arbitrary"),
        name="ssd_chunks",
    )(xact, xact, xact, dtact, alog, incl, jnp.asarray(expand))


def _odd_out_kernel(x_ref, mod_ref, ys_ref, xs_ref, zg_ref, dsk_ref, nw_ref, w_ref, o_ref):
    y = ys_ref[0, 0] + ys_ref[1, 0] + dsk_ref[...] * xs_ref[0]
    yg = y * _silu(zg_ref[0])
    inner = yg.shape[1]
    gw = inner // C_GROUPS
    parts = []
    for g in range(C_GROUPS):
        t = yg[:, g * gw:(g + 1) * gw]
        ms = jnp.mean(t * t, axis=-1, keepdims=True)
        parts.append((t * lax.rsqrt(ms + EPS) * nw_ref[:, g * gw:(g + 1) * gw]).astype(BF16))
    yn = jnp.concatenate(parts, axis=1)
    o_ref[0] = x_ref[0] + mod_ref[0, 0][2:3] * _mm(yn, w_ref[...])


def _odd_out(x, modx, ys, xact, zg, d_skip, norm_w, w_out, lc):
    bsz, ln, d = x.shape
    tm = TOK_TILE
    lct = lc // tm
    inner = zg.shape[-1]
    dsk = jnp.repeat(d_skip, C_HEAD_DIM).reshape(1, inner)
    tokw = lambda n: pl.BlockSpec((1, tm, n), lambda b, j: (b, j + lct, 0))
    full = lambda *shape: pl.BlockSpec(shape, lambda b, j: (0,) * len(shape))
    return pl.pallas_call(
        _odd_out_kernel,
        grid=(bsz, (ln - lc) // tm),
        in_specs=[tokw(d),
                  pl.BlockSpec((1, 1, N_MOD, d), lambda b, j: (b, 1, 0, 0)),
                  pl.BlockSpec((2, 1, tm, inner), lambda b, j: (0, b, j + lct, 0)),
                  tokw(inner), tokw(inner), full(1, inner), full(1, inner), full(inner, d)],
        out_specs=pl.BlockSpec((1, tm, d), lambda b, j: (b, j, 0)),
        out_shape=jax.ShapeDtypeStruct((bsz, ln - lc, d), F32),
        compiler_params=_params("arbitrary", "arbitrary"),
        name="odd_out",
    )(x, modx, ys, xact, zg, dsk, norm_w.reshape(1, inner), w_out)


def _exchange(xs, i, l, descending=True):
    hi, lo = jnp.maximum(xs[i], xs[l]), jnp.minimum(xs[i], xs[l])
    xs[i], xs[l] = (hi, lo) if descending else (lo, hi)


def _bitonic_merge(xs):
    xs = list(xs)
    j = len(xs) // 2
    while j >= 1:
        for i in range(len(xs)):
            if i ^ j > i:
                _exchange(xs, i, i ^ j)
        j //= 2
    return xs


def _bitonic_sort(xs):
    xs = list(xs)
    k = 2
    while k <= len(xs):
        j = k // 2
        while j >= 1:
            for i in range(len(xs)):
                if i ^ j > i:
                    _exchange(xs, i, i ^ j, descending=(i & k) == 0)
            j //= 2
        k *= 2
    return xs


def _merge_sublanes(xs):
    n = len(xs)
    for shift in (4, 2, 1):
        other = [pltpu.roll(x, shift, 0) for x in xs]
        xs = _bitonic_merge([jnp.maximum(xs[k], other[n - 1 - k]) for k in range(n)])
    return xs


def _top_values(s):
    return _merge_sublanes(_bitonic_sort([s[8 * k:8 * k + 8] for k in range(s.shape[0] // 8)]))


def _top_pair_sums(t1, t2):
    sub = lax.broadcasted_iota(jnp.int32, t2[0].shape, 0)
    lo, hi = t2[0], t2[8]
    for b in range(1, 8):
        lo = jnp.where(sub == b, t2[b], lo)
        hi = jnp.where(sub == b, t2[8 + b], hi)
    xs = [t1[a] + lo for a in range(P_TOPK)]
    carry = t1[0] + hi
    for k in range(P_TOPK):
        xs[k], carry = jnp.maximum(xs[k], carry), jnp.minimum(xs[k], carry)
    return _merge_sublanes(xs)


def _split_bf16(x):
    hi = x.astype(BF16)
    return hi, (x - hi.astype(F32)).astype(BF16)


def _mm_split(a_hi, a_lo, b):
    b_hi, b_lo = _split_bf16(b)
    return _mm(a_hi, b_hi) + (_mm(a_hi, b_lo) + _mm(a_lo, b_hi))


def _peer_query_kernel(x_ref, mod_ref, g_ref, wqh_ref, wql_ref, kh_ref, kl_ref,
                       ht_o, s1_o, s2_o, e2_o, c_o, tau_o):
    assert P_TOPK == 16 and P_KEYS == 128
    mod = mod_ref[0]
    h2 = _norm_mod(x_ref[...], g_ref[...], mod[3:4], mod[4:5])
    ht = h2.T
    ht_o[...] = ht.astype(BF16)
    qt = _mm_split(wqh_ref[...], wql_ref[...], ht)
    kd = P_KEY_DIM
    for h in range(P_HEADS):
        s1 = _mm_split(kh_ref[h, 0], kl_ref[h, 0], qt[(2 * h) * kd:(2 * h + 1) * kd])
        s2 = _mm_split(kh_ref[h, 1], kl_ref[h, 1], qt[(2 * h + 1) * kd:(2 * h + 2) * kd])
        t1 = _top_values(s1)
        t2 = _top_values(s2)
        best = _top_pair_sums(t1, t2)
        z = jnp.ones_like(best[0])
        for k in range(1, P_TOPK):
            z = z + jnp.exp(best[k] - best[0])
        inv_z = 1.0 / z
        s1_o[h] = s1
        s2_o[h] = s2
        for k in range(P_KEYS // 8):
            rows = slice(8 * k, 8 * k + 8)
            e2_o[h, rows] = jnp.exp(s2[rows] - t2[0])
            c_o[h, rows] = jnp.exp(s1[rows] - t1[0]) * inv_z
        tau_o[h] = best[P_TOPK - 1]


def _peer_query(xt, mod, gain, wq_t, keys, blocks_per_mod, tt):
    t_all, d = xt.shape
    nq = wq_t.shape[0]
    nblk = t_all // tt
    head = lambda rows: pl.BlockSpec((P_HEADS, rows, tt), lambda i: (0, 0, i))
    hsd = lambda rows: jax.ShapeDtypeStruct((P_HEADS, rows, t_all), F32)
    wq_hi, wq_lo = _split_bf16(wq_t)
    k_hi, k_lo = _split_bf16(keys)
    wspec = pl.BlockSpec((nq, d), lambda i: (0, 0))
    kspec = pl.BlockSpec((P_HEADS, 2, P_KEYS, P_KEY_DIM), lambda i: (0, 0, 0, 0))
    return pl.pallas_call(
        _peer_query_kernel,
        grid=(nblk,),
        in_specs=[pl.BlockSpec((tt, d), lambda i: (i, 0)),
                  pl.BlockSpec((1, N_MOD, d), lambda i: (i // blocks_per_mod, 0, 0)),
                  pl.BlockSpec((1, d), lambda i: (0, 0)),
                  wspec, wspec, kspec, kspec],
        out_specs=[pl.BlockSpec((d, tt), lambda i: (0, i)),
                   head(P_KEYS), head(P_KEYS), head(P_KEYS), head(P_KEYS), head(8)],
        out_shape=[jax.ShapeDtypeStruct((d, t_all), BF16),
                   hsd(P_KEYS), hsd(P_KEYS), hsd(P_KEYS), hsd(P_KEYS), hsd(8)],
        compiler_params=_params("arbitrary"),
        name="peer_query",
    )(xt, mod, gain.reshape(1, d), wq_hi, wq_lo, k_hi, k_lo)


def _gelu_exact(x):
    return 0.5 * x * (1.0 + lax.erf(x * (2.0 ** -0.5)))


def _peer_main_kernel(x_ref, mod_ref, ht_ref, u_ref, vt_ref, s1_ref, s2_ref, e2_ref, c_ref, tau_ref,
                      o_ref, acc_ref):
    step = pl.program_id(1)

    @pl.when(step == 0)
    def _():
        acc_ref[...] = jnp.zeros_like(acc_ref)

    act = _gelu_exact(_mm(u_ref[...], ht_ref[...]))
    rows_per_step = u_ref.shape[0] // P_KEYS
    parts = []
    for ii in range(rows_per_step):
        row = step * rows_per_step + ii
        coef = None
        for h in range(P_HEADS):
            s1 = s1_ref[h, pl.ds(row, 1), :]
            c = c_ref[h, pl.ds(row, 1), :]
            sel = (s1 + s2_ref[h]) >= tau_ref[h, 0:1, :]
            term = jnp.where(sel, e2_ref[h] * c, 0.0)
            coef = term if coef is None else coef + term
        parts.append((coef * act[ii * P_KEYS:(ii + 1) * P_KEYS]).astype(BF16))
    acc_ref[...] += _mm(vt_ref[...], jnp.concatenate(parts, axis=0))

    @pl.when(step == pl.num_programs(1) - 1)
    def _():
        o_ref[...] = x_ref[...] + mod_ref[0][5:6] * acc_ref[...].T


def _peer_main(xt, mod, ht, u_tab, vt_tab, s1, s2, e2, c, tau, blocks_per_mod, tt):
    t_all, d = xt.shape
    n_exp = u_tab.shape[0]
    eb = PEER_EXP
    head = lambda rows: pl.BlockSpec((P_HEADS, rows, tt), lambda i, s: (0, 0, i))
    return pl.pallas_call(
        _peer_main_kernel,
        grid=(t_all // tt, n_exp // eb),
        in_specs=[pl.BlockSpec((tt, d), lambda i, s: (i, 0)),
                  pl.BlockSpec((1, N_MOD, d), lambda i, s: (i // blocks_per_mod, 0, 0)),
                  pl.BlockSpec((d, tt), lambda i, s: (0, i)),
                  pl.BlockSpec((eb, d), lambda i, s: (s, 0)),
                  pl.BlockSpec((d, eb), lambda i, s: (0, s)),
                  head(P_KEYS), head(P_KEYS), head(P_KEYS), head(P_KEYS), head(8)],
        out_specs=pl.BlockSpec((tt, d), lambda i, s: (i, 0)),
        out_shape=jax.ShapeDtypeStruct((t_all, d), F32),
        scratch_shapes=[pltpu.VMEM((d, tt), F32)],
        compiler_params=_params("arbitrary", "arbitrary"),
        name="peer_main",
    )(xt, mod, ht, u_tab, vt_tab, s1, s2, e2, c, tau)


def _peer_layer(xt, mod, gain, w_q, keys, u_tab, v_tab, blocks_per_mod, tt):
    wq_t = w_q.T
    tq = PEER_QUERY_TOK
    ht, s1, s2, e2, c, tau = _peer_query(xt, mod, gain, wq_t, keys, blocks_per_mod * (tt // tq), tq)
    return _peer_main(xt, mod, ht, u_tab.astype(BF16), v_tab.T.astype(BF16), s1, s2, e2, c, tau,
                      blocks_per_mod, tt)


def kernel(x, c, ctx, c_ctx, mod_w, mod_b, norm1_g, norm2_g, ev_w_in, ev_w_out, attn_q_gain, attn_k_gain,
           rw_mu, rw_w0, rw_w2, rw_a0, rw_a2, rw_g2, rw_k_k, rw_k_a, rw_r_k, rw_gn_w, rw_gn_b, ssd_w_in,
           ssd_conv_w, ssd_conv_b, ssd_dt_bias, ssd_a_log, ssd_d, ssd_norm_w, ssd_w_out, peer_w_q,
           peer_keys, peer_u, peer_v):
    bsz, seq, dm = x.shape
    lc = ctx.shape[1]
    depth = mod_w.shape[0]
    assert lc == TOK_TILE and seq % PEER_TOK == 0 and (bsz * lc) % PEER_TOK == 0

    rows = -(-(bsz + 1) // 8) * 8
    cc = jnp.concatenate([c, c_ctx[None], jnp.zeros((rows - bsz - 1, dm), F32)], axis=0)
    mods = _modulation(cc, mod_w, mod_b)
    xall = jnp.concatenate([ctx, x], axis=1)

    for i in range(depth):
        last = i == depth - 1
        j = i // 2
        ml = mods[i, :bsz].reshape(bsz, N_MOD, dm)
        mc = mods[i, bsz].reshape(1, N_MOD, dm)
        modx = jnp.stack([jnp.broadcast_to(mc, (bsz, N_MOD, dm)), ml], axis=1)
        if i % 2 == 0:
            w_in = ev_w_in[j].astype(BF16)
            za = _in_proj(xall, modx, norm1_g[i], w_in[:, :A_IN], lc)
            zr = _in_proj(xall, modx, norm1_g[i], w_in[:, A_IN:], lc)
            q, k, v = _attn_pre(za, attn_q_gain[j], attn_k_gain[j], lc)
            attn = _attention(q, k, v, lc)
            r, vv, kkn, g, bonus, lw, a, kd = _rwkv_pre(
                zr, rw_mu[j], rw_w0[j], rw_w2[j], rw_a0[j], rw_a2[j], rw_g2[j], rw_k_k[j], rw_k_a[j],
                rw_r_k[j], lc)
            wkv = _rwkv_chunks(r, vv, kkn, lw, a, kd, lc)
            xall = _even_out(xall, modx, attn, wkv, bonus, g, rw_gn_w[j], rw_gn_b[j],
                             ev_w_out[j].astype(BF16), lc)
            xlat = xall[:, lc:]
        else:
            n_heads = ssd_a_log.shape[-1]
            inner = n_heads * C_HEAD_DIM
            conv_dim = inner + 2 * C_GROUPS * C_STATE
            w_in = ssd_w_in[j]
            w_dt = jnp.zeros((dm, 2 * LANES), F32)
            w_dt = w_dt.at[:, :n_heads].set(w_in[:, inner + conv_dim:inner + conv_dim + n_heads])
            w_dt = w_dt.at[:, LANES:LANES + n_heads].set(w_in[:, inner + conv_dim + n_heads:])
            dtb = jnp.zeros((1, 2 * LANES), F32)
            dtb = dtb.at[0, :n_heads].set(ssd_dt_bias[j, 0]).at[0, LANES:LANES + n_heads].set(ssd_dt_bias[j, 1])
            zg = _in_proj(xall, modx, norm1_g[i], w_in[:, :inner].astype(BF16), lc)
            xbc = _in_proj(xall, modx, norm1_g[i], w_in[:, inner:inner + conv_dim].astype(BF16), lc)
            dtraw = _in_proj(xall, modx, norm1_g[i], w_dt.astype(BF16), lc)
            xact, dtact = _ssd_pre(xbc, dtraw, ssd_conv_w[j], ssd_conv_b[j], dtb, lc)
            ys = _ssd_chunks(xact, dtact, ssd_a_log[j], lc)
            assert last, "the SSD layer keeps only the latent stream"
            xlat = _odd_out(xall, modx, ys, xact, zg, ssd_d[j], ssd_norm_w[j],
                            ssd_w_out[j].astype(BF16), lc)

        tt = PEER_TOK
        xl2 = _peer_layer(xlat.reshape(bsz * seq, dm), ml, norm2_g[i], peer_w_q[i], peer_keys[i],
                          peer_u[i], peer_v[i], seq // tt, tt).reshape(bsz, seq, dm)
        if last:
            return xl2
        xc2 = _peer_layer(xall[:, :lc].reshape(bsz * lc, dm), mc, norm2_g[i], peer_w_q[i], peer_keys[i],
                          peer_u[i], peer_v[i], (bsz * lc) // tt, tt).reshape(bsz, lc, dm)
        xall = jnp.concatenate([xc2, xl2], axis=1)
    return xall[:, lc:]
```

```python
import functools
import math

import numpy as np
import jax
import jax.numpy as jnp
from jax import lax
from jax.experimental import pallas as pl
from jax.experimental.pallas import tpu as pltpu

F32 = jnp.float32
BF16 = jnp.bfloat16
HI = lax.Precision.HIGHEST

GRID_W = 64
EPS = 1e-6
N_MOD = 6
A_HEADS, A_KV_HEADS, A_HEAD_DIM = 8, 2, 64
A_Q = A_HEADS * A_HEAD_DIM
A_KV = A_KV_HEADS * A_HEAD_DIM
A_IN = A_Q + 2 * A_KV
ROPE_THETA = 10000.0
B_HEADS, B_HEAD_DIM = 8, 64
B_WIDTH = B_HEADS * B_HEAD_DIM
B_DECAY_LORA, B_AAA_LORA, B_GATE_LORA = 64, 64, 128
B_IN = 3 * B_WIDTH + B_DECAY_LORA + B_AAA_LORA + B_GATE_LORA
B_GN_EPS = 64e-5
C_HEAD_DIM, C_GROUPS, C_STATE, C_CHUNK = 64, 4, 128, 128
P_HEADS, P_KEYS, P_KEY_DIM, P_TOPK = 8, 128, 128, 16

LANES = 128
TOK_TILE = 256
R_CHUNK = 64
PEER_TOK = 512
PEER_QUERY_TOK = 256
PEER_EXP = 1024
VMEM_LIMIT = 56 * 1024 * 1024


def _mm(a, b, prec=None):
    return lax.dot_general(a, b, (((1,), (0,)), ((), ())), precision=prec,
                           preferred_element_type=F32)


def _mm_nt(a, b, prec=None):
    return lax.dot_general(a, b, (((1,), (1,)), ((), ())), precision=prec,
                           preferred_element_type=F32)


def _mm_tn(a, b, prec=None):
    return lax.dot_general(a, b, (((0,), (0,)), ((), ())), precision=prec,
                           preferred_element_type=F32)


def _params(*sem):
    return pltpu.CompilerParams(dimension_semantics=sem, vmem_limit_bytes=VMEM_LIMIT)


def _softplus(x):
    return jnp.maximum(x, 0.0) + jnp.log1p(jnp.exp(-jnp.abs(x)))


def _sigmoid(x):
    return 1.0 / (1.0 + jnp.exp(-x))


def _silu(x):
    return x * _sigmoid(x)


def _mod_kernel(c_ref, w_ref, b_ref, o_ref):
    o_ref[0] = _mm(_silu(c_ref[...]), w_ref[0], HI) + b_ref[0]


def _modulation(cc, mod_w, mod_b):
    depth, d, n6 = mod_w.shape
    rows = cc.shape[0]
    tn = 1024
    return pl.pallas_call(
        _mod_kernel,
        grid=(depth, n6 // tn),
        in_specs=[pl.BlockSpec((rows, d), lambda l, n: (0, 0)),
                  pl.BlockSpec((1, d, tn), lambda l, n: (l, 0, n)),
                  pl.BlockSpec((1, 1, tn), lambda l, n: (l, 0, n))],
        out_specs=pl.BlockSpec((1, rows, tn), lambda l, n: (l, 0, n)),
        out_shape=jax.ShapeDtypeStruct((depth, rows, n6), F32),
        compiler_params=_params("arbitrary", "arbitrary"),
        name="modulation",
    )(cc, mod_w, mod_b.reshape(depth, 1, n6))


def _norm_mod(x, gain, shift, scale):
    ms = jnp.mean(x * x, axis=-1, keepdims=True)
    return x * lax.rsqrt(ms + EPS) * gain * (1.0 + scale) + shift


def _in_proj_kernel(x_ref, mod_ref, g_ref, w_ref, o_ref):
    mod = mod_ref[0, 0]
    h = _norm_mod(x_ref[0], g_ref[...], mod[0:1], mod[1:2]).astype(BF16)
    n = o_ref.shape[-1]
    for c0 in range(0, n, 1024):
        c1 = min(n, c0 + 1024)
        o_ref[0, :, c0:c1] = _mm(h, w_ref[:, c0:c1])


def _in_proj(x, modx, gain, w, lc):
    bsz, ln, d = x.shape
    n = w.shape[1]
    tm = TOK_TILE
    lct = lc // tm
    return pl.pallas_call(
        _in_proj_kernel,
        grid=(bsz, ln // tm),
        in_specs=[pl.BlockSpec((1, tm, d), lambda b, j: (b, j, 0)),
                  pl.BlockSpec((1, 1, N_MOD, d), lambda b, j: (b, jnp.where(j < lct, 0, 1), 0, 0)),
                  pl.BlockSpec((1, d), lambda b, j: (0, 0)),
                  pl.BlockSpec((d, n), lambda b, j: (0, 0))],
        out_specs=pl.BlockSpec((1, tm, n), lambda b, j: (b, j, 0)),
        out_shape=jax.ShapeDtypeStruct((bsz, ln, n), F32),
        compiler_params=_params("arbitrary", "arbitrary"),
        name="in_proj",
    )(x, modx, gain.reshape(1, d), w)


def _rope_tables(lc, seq):
    t = jnp.arange(seq)
    row = (t // GRID_W).astype(F32)
    col = (t % GRID_W).astype(F32)
    m = A_HEAD_DIM // 4
    inv = ROPE_THETA ** (-jnp.arange(m, dtype=F32) / m)
    ar, ac = row[:, None] * inv, col[:, None] * inv
    cos = jnp.concatenate([jnp.cos(ar), jnp.cos(ar), jnp.cos(ac), jnp.cos(ac)], axis=-1)
    sin = jnp.concatenate([-jnp.sin(ar), jnp.sin(ar), -jnp.sin(ac), jnp.sin(ac)], axis=-1)
    reps = LANES // A_HEAD_DIM
    cos = jnp.concatenate([jnp.ones((lc, A_HEAD_DIM), F32), cos], axis=0)
    sin = jnp.concatenate([jnp.zeros((lc, A_HEAD_DIM), F32), sin], axis=0)
    return jnp.tile(cos, (1, reps)), jnp.tile(sin, (1, reps))


def _head_block_ones(width, scale):
    idx = np.arange(LANES) // width
    return jnp.asarray((idx[:, None] == idx[None, :]).astype(np.float32) * scale)


def _attn_pre_kernel(z_ref, qg_ref, kg_ref, cos_ref, sin_ref, bones_ref, q_ref, k_ref, v_ref):
    tm = z_ref.shape[1]
    cs, sn = cos_ref[...], sin_ref[...]
    bones = bones_ref[...]
    lane = lax.broadcasted_iota(jnp.int32, (tm, LANES), 1)
    first = (lane % 32) < 16
    low = lane < A_HEAD_DIM

    def norm_rope(x, gain):
        ms = _mm(x * x, bones, HI)
        y = x * lax.rsqrt(ms + EPS) * gain
        partner = jnp.where(first, pltpu.roll(y, LANES - 16, 1), pltpu.roll(y, 16, 1))
        return y * cs + partner * sn

    heads_per_slab = LANES // A_HEAD_DIM
    group = A_HEADS // A_KV_HEADS
    for m in range(A_Q // LANES):
        sl = slice(m * LANES, (m + 1) * LANES)
        y = norm_rope(z_ref[0, :, sl], qg_ref[:, sl]) * (A_HEAD_DIM ** -0.5)
        swapped = pltpu.roll(y, A_HEAD_DIM, 1)
        for hh in range(heads_per_slab):
            h = m * heads_per_slab + hh
            kvh = h // group
            src = y if kvh == hh else swapped
            keep = low if kvh == 0 else jnp.logical_not(low)
            q_ref[0, h] = jnp.where(keep, src, 0.0).astype(BF16)
    k_ref[0] = norm_rope(z_ref[0, :, A_Q:A_Q + A_KV], kg_ref[...]).astype(BF16)
    v_ref[0] = z_ref[0, :, A_Q + A_KV:A_IN].astype(BF16)


def _attn_pre(za, q_gain, k_gain, lc):
    bsz, ln, _ = za.shape
    tm = TOK_TILE
    cos, sin = _rope_tables(lc, ln - lc)
    qg = jnp.tile(q_gain, A_HEADS).reshape(1, A_Q)
    kg = jnp.tile(k_gain, A_KV_HEADS).reshape(1, A_KV)
    bones = _head_block_ones(A_HEAD_DIM, 1.0 / A_HEAD_DIM)
    return pl.pallas_call(
        _attn_pre_kernel,
        grid=(bsz, ln // tm),
        in_specs=[pl.BlockSpec((1, tm, A_IN), lambda b, j: (b, j, 0)),
                  pl.BlockSpec((1, A_Q), lambda b, j: (0, 0)),
                  pl.BlockSpec((1, A_KV), lambda b, j: (0, 0)),
                  pl.BlockSpec((tm, LANES), lambda b, j: (j, 0)),
                  pl.BlockSpec((tm, LANES), lambda b, j: (j, 0)),
                  pl.BlockSpec((LANES, LANES), lambda b, j: (0, 0))],
        out_specs=[pl.BlockSpec((1, A_HEADS, tm, LANES), lambda b, j: (b, 0, j, 0)),
                   pl.BlockSpec((1, tm, A_KV), lambda b, j: (b, j, 0)),
                   pl.BlockSpec((1, tm, A_KV), lambda b, j: (b, j, 0))],
        out_shape=[jax.ShapeDtypeStruct((bsz, A_HEADS, ln, LANES), BF16),
                   jax.ShapeDtypeStruct((bsz, ln, A_KV), BF16),
                   jax.ShapeDtypeStruct((bsz, ln, A_KV), BF16)],
        compiler_params=_params("arbitrary", "arbitrary"),
        name="attn_pre",
    )(za, qg, kg, cos, sin, bones)


def _attn_kernel(q_ref, k_ref, v_ref, o_ref, *, lc):
    tq = q_ref.shape[2]
    j = pl.program_id(1)
    lane = lax.broadcasted_iota(jnp.int32, (tq, LANES), 1)
    low = lane < A_HEAD_DIM
    heads_per_slab = LANES // A_HEAD_DIM
    group = A_HEADS // A_KV_HEADS

    def run(klen):
        k = k_ref[0, :klen, :]
        v = v_ref[0, :klen, :]
        for m in range(A_Q // LANES):
            parts = []
            for hh in range(heads_per_slab):
                h = m * heads_per_slab + hh
                kvh = h // group
                s = _mm_nt(q_ref[0, h], k)
                p = jnp.exp(s - jnp.max(s, axis=-1, keepdims=True))
                l = jnp.sum(p, axis=-1, keepdims=True)
                o = _mm(p.astype(BF16), v) / l
                parts.append(o if kvh == hh else pltpu.roll(o, A_HEAD_DIM, 1))
            o_ref[0, :, m * LANES:(m + 1) * LANES] = jnp.where(low, parts[0], parts[1])

    @pl.when(j * tq < lc)
    def _():
        run(lc)

    @pl.when(j * tq >= lc)
    def _():
        run(k_ref.shape[1])


def _attention(q, k, v, lc):
    bsz, _, ln, _ = q.shape
    tq = TOK_TILE
    return pl.pallas_call(
        functools.partial(_attn_kernel, lc=lc),
        grid=(bsz, ln // tq),
        in_specs=[pl.BlockSpec((1, A_HEADS, tq, LANES), lambda b, j: (b, 0, j, 0)),
                  pl.BlockSpec((1, ln, A_KV), lambda b, j: (b, 0, 0)),
                  pl.BlockSpec((1, ln, A_KV), lambda b, j: (b, 0, 0))],
        out_specs=pl.BlockSpec((1, tq, A_Q), lambda b, j: (b, j, 0)),
        out_shape=jax.ShapeDtypeStruct((bsz, ln, A_Q), F32),
        compiler_params=_params("arbitrary", "arbitrary"),
        name="attention",
    )(q, k, v)


def _shifted_rows(z, prev_row, next_row):
    tm = z.shape[0]
    row = lax.broadcasted_iota(jnp.int32, (tm, 1), 0)
    zprev = jnp.where(row == 0, prev_row, pltpu.roll(z, 1, 0))
    znext = jnp.where(row == tm - 1, next_row, pltpu.roll(z, tm - 1, 0))
    return zprev, znext


def _halo_rows(zp_ref, zn_ref, j, lct, nj):
    prev_ok = jnp.logical_and(j != 0, j != lct)
    next_ok = jnp.logical_and(j != lct - 1, j != nj - 1)
    prev_row = jnp.where(prev_ok, zp_ref[0, 7:8, :], 0.0)
    next_row = jnp.where(next_ok, zn_ref[0, 0:1, :], 0.0)
    return prev_row, next_row


def _halo_specs(tm, width, ln):
    nb8 = tm // 8
    last = ln // 8 - 1
    return [pl.BlockSpec((1, tm, width), lambda b, j: (b, j, 0)),
            pl.BlockSpec((1, 8, width), lambda b, j: (b, jnp.maximum(j * nb8 - 1, 0), 0)),
            pl.BlockSpec((1, 8, width), lambda b, j: (b, jnp.minimum((j + 1) * nb8, last), 0))]


def _per_head_sum(x, bones):
    return jnp.concatenate(
        [_mm(x[:, m * LANES:(m + 1) * LANES], bones, HI) for m in range(x.shape[1] // LANES)], axis=1)


def _rwkv_pre_kernel(z_ref, zp_ref, zn_ref, mu_ref, w0_ref, w2_ref, a0_ref, a2_ref, g2_ref,
                     kk_ref, ka_ref, rk_ref, bones_ref,
                     r_o, v_o, kkn_o, g_o, bonus_o, lw_o, a_o, kd_o, *, lct):
    j = pl.program_id(1)
    z = z_ref[0]
    prev_row, next_row = _halo_rows(zp_ref, zn_ref, j, lct, pl.num_programs(1))
    zprev, znext = _shifted_rows(z, prev_row, next_row)
    zm = z + mu_ref[...] * (0.5 * (zprev + znext) - z)
    w = B_WIDTH
    r, k, v = zm[:, 0:w], zm[:, w:2 * w], zm[:, 2 * w:3 * w]
    wa = zm[:, 3 * w:3 * w + B_DECAY_LORA + B_AAA_LORA]
    gl = zm[:, 3 * w + B_DECAY_LORA + B_AAA_LORA:]
    bones = bones_ref[...]
    kk = k * kk_ref[...]
    kkn = kk / jnp.maximum(jnp.sqrt(_per_head_sum(kk * kk, bones)), 1e-12)
    r_o[0] = r
    v_o[0] = v
    kkn_o[0] = kkn
    g_o[0] = _mm(_sigmoid(gl), g2_ref[...], HI)
    twa = jnp.tanh(wa)
    bonus = jnp.zeros_like(r)
    for d in range(2):
        wraw = w0_ref[d] + _mm(twa, w2_ref[d], HI)
        lw_o[d, 0] = -jnp.exp(-_softplus(-wraw) - 0.5)
        a = _sigmoid(a0_ref[d] + _mm(wa, a2_ref[d], HI))
        kd = k * (1.0 + (a - 1.0) * ka_ref[...])
        a_o[d, 0] = a
        kd_o[d, 0] = kd
        bonus = bonus + _per_head_sum(r * kd * rk_ref[...], bones) * v
    bonus_o[0] = bonus


def _rwkv_pre(zr, mu, w0, w2, a0, a2, g2, k_k, k_a, r_k, lc):
    bsz, ln, width = zr.shape
    tm = TOK_TILE
    w = B_WIDTH
    zeros = jnp.zeros((2, B_DECAY_LORA, w), F32)
    w2p = jnp.concatenate([w2, zeros], axis=1)
    a2p = jnp.concatenate([zeros, a2], axis=1)
    bones = _head_block_ones(B_HEAD_DIM, 1.0)
    full = lambda *shape: pl.BlockSpec(shape, lambda b, j: (0,) * len(shape))
    tok = pl.BlockSpec((1, tm, w), lambda b, j: (b, j, 0))
    tok2 = pl.BlockSpec((2, 1, tm, w), lambda b, j: (0, b, j, 0))
    sd = jax.ShapeDtypeStruct((bsz, ln, w), F32)
    sd2 = jax.ShapeDtypeStruct((2, bsz, ln, w), F32)
    return pl.pallas_call(
        functools.partial(_rwkv_pre_kernel, lct=lc // tm),
        grid=(bsz, ln // tm),
        in_specs=_halo_specs(tm, width, ln) + [
            full(1, width), full(2, 1, w), full(2, 2 * B_DECAY_LORA, w), full(2, 1, w),
            full(2, 2 * B_AAA_LORA, w), full(B_GATE_LORA, w), full(1, w), full(1, w), full(1, w),
            full(LANES, LANES)],
        out_specs=[tok, tok, tok, tok, tok, tok2, tok2, tok2],
        out_shape=[sd, sd, sd, sd, sd, sd2, sd2, sd2],
        compiler_params=_params("arbitrary", "arbitrary"),
        name="rwkv_pre",
    )(zr, zr, zr, mu.reshape(1, width), w0.reshape(2, 1, w), w2p, a0.reshape(2, 1, w), a2p, g2,
      k_k.reshape(1, w), k_a.reshape(1, w), r_k.reshape(1, w), bones)


def _scan_chunk(i, d, nctx, ntot):
    rev = jnp.where(i < nctx, nctx - 1 - i, ntot + nctx - 1 - i)
    return jnp.where(d == 0, i, rev)


def _order_masks(n, pair):
    t = np.arange(n * pair)
    same = (t[:, None] // n) == (t[None, :] // n)
    tt, ss = t[:, None] % n, t[None, :] % n
    incl = np.stack([same & (ss <= tt), same & (ss >= tt)]).astype(np.float32)
    strict = np.stack([same & (ss < tt), same & (ss > tt)]).astype(np.float32)
    return jnp.asarray(incl), jnp.asarray(strict)


def _pair_stack(x, low):
    return jnp.concatenate([jnp.where(low, x, 0.0), jnp.where(low, 0.0, x)], axis=0)


def _bmm(a, b):
    return _mm(a.astype(BF16), b.astype(BF16))


def _rwkv_chunk_kernel(rf_ref, vf_ref, kf_ref, lwf_ref, af_ref, kdf_ref,
                       rb_ref, vb_ref, kb_ref, lwb_ref, ab_ref, kdb_ref,
                       tri_ref, incl_ref, strict_ref, of_ref, ob_ref, state_ref):
    c = R_CHUNK
    n2 = 2 * c

    @pl.when(pl.program_id(1) == 0)
    def _():
        state_ref[...] = jnp.zeros_like(state_ref)

    low = lax.broadcasted_iota(jnp.int32, (c, LANES), 1) < B_HEAD_DIM
    eye = (lax.broadcasted_iota(jnp.int32, (LANES, LANES), 0)
           == lax.broadcasted_iota(jnp.int32, (LANES, LANES), 1)).astype(F32)
    stack = lambda x: _pair_stack(x, low)
    refs = ((rf_ref, vf_ref, kf_ref, lwf_ref, af_ref, kdf_ref, of_ref),
            (rb_ref, vb_ref, kb_ref, lwb_ref, ab_ref, kdb_ref, ob_ref))
    dirs, slabs, outs = [], [], []
    kts, bts, kds, rts, vs, pcs = [], [], [], [], [], []
    for d, (r_ref, v_ref, kkn_ref, lw_ref, a_ref, kd_ref, o_ref) in enumerate(refs):
        lw = lw_ref[0, 0]
        cum = _mm(tri_ref[d], lw, HI)
        p_in = jnp.exp(cum)
        p_inv = jnp.exp(-cum)
        p_ex = jnp.exp(cum - lw)
        p_all = jnp.exp(jnp.sum(lw, axis=0, keepdims=True))
        kkn = kkn_ref[0]
        kt = kkn * p_ex
        bt = kkn * a_ref[0, 0] * p_inv
        kd = kd_ref[0, 0] * p_inv
        rt = r_ref[0] * p_in
        v = v_ref[0]
        for m in range(B_WIDTH // LANES):
            sl = slice(m * LANES, (m + 1) * LANES)
            dirs.append(d)
            slabs.append(m)
            outs.append(o_ref)
            kts.append(stack(kt[:, sl]).astype(BF16))
            bts.append(stack(bt[:, sl]).astype(BF16))
            kds.append(stack(kd[:, sl]).astype(BF16))
            rts.append(stack(rt[:, sl]))
            vs.append(stack(v[:, sl]).astype(BF16))
            pcs.append(p_all[:, sl])
    n = len(dirs)
    incl = [incl_ref[d] > 0.5 for d in range(2)]
    strict = [strict_ref[d] > 0.5 for d in range(2)]
    gram = [_mm_nt(jnp.concatenate([kts[x], rts[x].astype(BF16)], axis=0),
                   jnp.concatenate([bts[x], kds[x]], axis=0)) for x in range(n)]
    a_b = [jnp.where(strict[dirs[x]], gram[x][:n2, :n2], 0.0) for x in range(n)]
    a_k = [jnp.where(strict[dirs[x]], gram[x][:n2, n2:], 0.0) for x in range(n)]
    m_b = [jnp.where(incl[dirs[x]], gram[x][n2:, :n2], 0.0) for x in range(n)]
    m_k = [jnp.where(incl[dirs[x]], gram[x][n2:, n2:], 0.0) for x in range(n)]
    pw = [-a for a in a_b]
    tinv = [eye + p for p in pw]
    for _ in range(int(math.log2(c)) - 1):
        pw = [_bmm(p, p) for p in pw]
        tinv = [t + _bmm(t, p) for t, p in zip(tinv, pw)]
    mv = [_bmm(jnp.concatenate([a_k[x], m_k[x]], axis=0), vs[x]) for x in range(n)]
    wy = [_bmm(tinv[x], jnp.concatenate([kts[x], mv[x][:n2].astype(BF16)], axis=1)) for x in range(n)]
    mbwy = [_bmm(m_b[x], wy[x]) for x in range(n)]
    wyb = [_mm_tn(wy[x].astype(BF16), bts[x]) for x in range(n)]
    vtk = [_mm_tn(vs[x], kds[x]) for x in range(n)]
    s0 = [state_ref[dirs[x], slabs[x]] for x in range(n)]
    o_st = [_mm_nt((rts[x] - mbwy[x][:, :LANES]).astype(BF16), s0[x].astype(BF16))
            + (mv[x][n2:] - mbwy[x][:, LANES:]) for x in range(n)]
    for x in range(n):
        g = (eye - wyb[x][:LANES]) * pcs[x]
        h = (vtk[x] - wyb[x][LANES:]) * pcs[x]
        outs[x][0, :, slabs[x] * LANES:(slabs[x] + 1) * LANES] = o_st[x][:c] + o_st[x][c:]
        state_ref[dirs[x], slabs[x]] = _bmm(s0[x], g) + h


def _rwkv_chunks(r, v, kkn, lw, a, kd, lc):
    bsz, ln, w = r.shape
    c = R_CHUNK
    nctx, ntot = lc // c, ln // c
    tri, _ = _order_masks(c, 1)
    incl, strict = _order_masks(c, 2)
    specs = []
    for d in range(2):
        tok = pl.BlockSpec((1, c, w), lambda b, i, d=d: (b, _scan_chunk(i, d, nctx, ntot), 0))
        tok2 = pl.BlockSpec((1, 1, c, w), lambda b, i, d=d: (d, b, _scan_chunk(i, d, nctx, ntot), 0))
        specs.append((tok, tok2))
    full = lambda *shape: pl.BlockSpec(shape, lambda b, i: (0,) * len(shape))
    sd = jax.ShapeDtypeStruct((bsz, ln, w), F32)
    return pl.pallas_call(
        _rwkv_chunk_kernel,
        grid=(bsz, ntot),
        in_specs=[specs[0][0]] * 3 + [specs[0][1]] * 3 + [specs[1][0]] * 3 + [specs[1][1]] * 3
        + [full(2, c, c), full(2, 2 * c, 2 * c), full(2, 2 * c, 2 * c)],
        out_specs=[specs[0][0], specs[1][0]],
        out_shape=[sd, sd],
        scratch_shapes=[pltpu.VMEM((2, w // LANES, LANES, LANES), F32)],
        compiler_params=_params("arbitrary", "arbitrary"),
        name="rwkv_chunks",
    )(r, v, kkn, lw, a, kd, r, v, kkn, lw, a, kd, tri, incl, strict)


def _even_out_kernel(x_ref, mod_ref, attn_ref, wkvf_ref, wkvb_ref, bonus_ref, g_ref, gw_ref, gb_ref,
                     bones_ref, w_ref, o_ref):
    bones = bones_ref[...]
    y = bonus_ref[0]
    for wkv_ref in (wkvf_ref, wkvb_ref):
        o = wkv_ref[0]
        dlt = o - _per_head_sum(o, bones)
        var = _per_head_sum(dlt * dlt, bones)
        y = y + dlt * lax.rsqrt(var + B_GN_EPS) * gw_ref[...] + gb_ref[...]
    y = y * g_ref[0]
    mix = _mm(attn_ref[0].astype(BF16), w_ref[:A_Q, :]) + _mm(y.astype(BF16), w_ref[A_Q:, :])
    o_ref[0] = x_ref[0] + mod_ref[0, 0][2:3] * mix


def _even_out(x, modx, attn, wkv, bonus, g, gn_w, gn_b, w_out, lc):
    bsz, ln, d = x.shape
    tm = TOK_TILE
    lct = lc // tm
    w = B_WIDTH
    bones = _head_block_ones(B_HEAD_DIM, 1.0 / B_HEAD_DIM)
    tokw = lambda n: pl.BlockSpec((1, tm, n), lambda b, j: (b, j, 0))
    full = lambda *shape: pl.BlockSpec(shape, lambda b, j: (0,) * len(shape))
    return pl.pallas_call(
        _even_out_kernel,
        grid=(bsz, ln // tm),
        in_specs=[tokw(d),
                  pl.BlockSpec((1, 1, N_MOD, d), lambda b, j: (b, jnp.where(j < lct, 0, 1), 0, 0)),
                  tokw(A_Q), tokw(w), tokw(w),
                  tokw(w), tokw(w), full(1, w), full(1, w), full(LANES, LANES),
                  full(A_Q + w, d)],
        out_specs=tokw(d),
        out_shape=jax.ShapeDtypeStruct((bsz, ln, d), F32),
        compiler_params=_params("arbitrary", "arbitrary"),
        name="even_out",
    )(x, modx, attn, wkv[0], wkv[1], bonus, g, gn_w.reshape(1, w), gn_b.reshape(1, w), bones, w_out)


def _ssd_pre_kernel(z_ref, zp_ref, zn_ref, dt_ref, cw_ref, cb_ref, dtb_ref, x_o, dt_o, *, lct):
    j = pl.program_id(1)
    z = z_ref[0]
    prev_row, next_row = _halo_rows(zp_ref, zn_ref, j, lct, pl.num_programs(1))
    zprev, znext = _shifted_rows(z, prev_row, next_row)
    y = zprev * cw_ref[0:1, :] + z * cw_ref[1:2, :] + znext * cw_ref[2:3, :] + cb_ref[...]
    x_o[0] = _silu(y)
    dt_o[0] = _softplus(dt_ref[0] + dtb_ref[...])


def _ssd_pre(xbc, dtraw, conv_w, conv_b, dtb, lc):
    bsz, ln, width = xbc.shape
    tm = TOK_TILE
    nd = dtraw.shape[-1]
    full = lambda *shape: pl.BlockSpec(shape, lambda b, j: (0,) * len(shape))
    return pl.pallas_call(
        functools.partial(_ssd_pre_kernel, lct=lc // tm),
        grid=(bsz, ln // tm),
        in_specs=_halo_specs(tm, width, ln) + [
            pl.BlockSpec((1, tm, nd), lambda b, j: (b, j, 0)),
            full(conv_w.shape[0], width), full(1, width), full(1, nd)],
        out_specs=[pl.BlockSpec((1, tm, width), lambda b, j: (b, j, 0)),
                   pl.BlockSpec((1, tm, nd), lambda b, j: (b, j, 0))],
        out_shape=[jax.ShapeDtypeStruct((bsz, ln, width), F32),
                   jax.ShapeDtypeStruct((bsz, ln, nd), F32)],
        compiler_params=_params("arbitrary", "arbitrary"),
        name="ssd_pre",
    )(xbc, xbc, xbc, dtraw, conv_w, conv_b.reshape(1, width), dtb)


def _ssd_chunk_kernel(x_ref, b_ref, c_ref, dt_ref, alog_ref, incl_ref, expand_ref, o_ref, state_ref,
                      *, n_heads):
    q = C_CHUNK
    i = pl.program_id(2)

    @pl.when(i == 0)
    def _():
        state_ref[...] = jnp.zeros_like(state_ref)

    lane = lax.broadcasted_iota(jnp.int32, (1, LANES), 1)
    a_neg = jnp.where(lane < n_heads, -jnp.exp(alog_ref[0]), 0.0)
    dt = dt_ref[0]
    dta = dt * a_neg
    incl_f = incl_ref[0]
    cum = _mm(incl_f, dta, HI)
    cum_t = cum.T
    total = jnp.sum(dta, axis=0, keepdims=True)
    expand = expand_ref[...]
    dt_x = _mm(dt, expand)
    ec_x = _mm(jnp.exp(cum), expand)
    de_x = _mm(jnp.exp(total - cum), expand)
    cd_x = _mm(jnp.broadcast_to(jnp.exp(total), (8, LANES)), expand)[0:1]
    incl = incl_f > 0.5
    low = lax.broadcasted_iota(jnp.int32, (q, LANES), 1) < C_HEAD_DIM
    heads_per_group = n_heads // C_GROUPS
    pairs_per_group = heads_per_group // 2
    for g in range(C_GROUPS):
        bg = b_ref[0, :, g * C_STATE:(g + 1) * C_STATE]
        cg = c_ref[0, :, g * C_STATE:(g + 1) * C_STATE]
        cb = _mm_nt(cg, bg)
        for pp in range(pairs_per_group):
            pr = g * pairs_per_group + pp
            sl = slice(pr * LANES, (pr + 1) * LANES)
            ms = []
            for hh in range(2):
                h = 2 * pr + hh
                seg = cum[:, h:h + 1] - cum_t[h:h + 1, :]
                ms.append(cb * jnp.exp(jnp.where(incl, seg, -1e30)))
            xd = x_ref[0, :, sl] * dt_x[:, sl]
            y_diag = _mm(jnp.concatenate(ms, axis=1), _pair_stack(xd, low))
            s_prev = state_ref[pr]
            y_off = _mm(cg, s_prev) * ec_x[:, sl]
            o_ref[0, 0, :, sl] = y_diag + y_off
            state_ref[pr] = s_prev * cd_x[:, sl] + _mm_tn(bg, xd * de_x[:, sl])


def _ssd_chunks(xact, dtact, a_log, lc):
    bsz, ln, _ = xact.shape
    q = C_CHUNK
    n_heads = a_log.shape[1]
    inner = n_heads * C_HEAD_DIM
    gn = C_GROUPS * C_STATE
    nctx, ntot = lc // q, ln // q
    incl, _ = _order_masks(q, 1)
    expand = np.zeros((LANES, inner), np.float32)
    expand[np.arange(inner) // C_HEAD_DIM, np.arange(inner)] = 1.0
    alog = jnp.zeros((2, 1, LANES), F32).at[:, 0, :n_heads].set(a_log)
    cmap = lambda off: (lambda b, d, i: (b, _scan_chunk(i, d, nctx, ntot), off))
    return pl.pallas_call(
        functools.partial(_ssd_chunk_kernel, n_heads=n_heads),
        grid=(bsz, 2, ntot),
        in_specs=[pl.BlockSpec((1, q, inner), cmap(0)),
                  pl.BlockSpec((1, q, gn), cmap(inner // gn)),
                  pl.BlockSpec((1, q, gn), cmap(inner // gn + 1)),
                  pl.BlockSpec((1, q, LANES), lambda b, d, i: (b, _scan_chunk(i, d, nctx, ntot), d)),
                  pl.BlockSpec((1, 1, LANES), lambda b, d, i: (d, 0, 0)),
                  pl.BlockSpec((1, q, q), lambda b, d, i: (d, 0, 0)),
                  pl.BlockSpec((LANES, inner), lambda b, d, i: (0, 0))],
        out_specs=pl.BlockSpec((1, 1, q, inner), lambda b, d, i: (d, b, _scan_chunk(i, d, nctx, ntot), 0)),
        out_shape=jax.ShapeDtypeStruct((2, bsz, ln, inner), F32),
        scratch_shapes=[pltpu.VMEM((inner // LANES, C_STATE, LANES), F32)],
        compiler_params=_params("arbitrary", "arbitrary", "arbitrary"),
        name="ssd_chunks",
    )(xact, xact, xact, dtact, alog, incl, jnp.asarray(expand))


def _odd_out_kernel(x_ref, mod_ref, ys_ref, xs_ref, zg_ref, dsk_ref, nw_ref, w_ref, o_ref):
    y = ys_ref[0, 0] + ys_ref[1, 0] + dsk_ref[...] * xs_ref[0]
    yg = y * _silu(zg_ref[0])
    inner = yg.shape[1]
    gw = inner // C_GROUPS
    parts = []
    for g in range(C_GROUPS):
        t = yg[:, g * gw:(g + 1) * gw]
        ms = jnp.mean(t * t, axis=-1, keepdims=True)
        parts.append((t * lax.rsqrt(ms + EPS) * nw_ref[:, g * gw:(g + 1) * gw]).astype(BF16))
    yn = jnp.concatenate(parts, axis=1)
    o_ref[0] = x_ref[0] + mod_ref[0, 0][2:3] * _mm(yn, w_ref[...])


def _odd_out(x, modx, ys, xact, zg, d_skip, norm_w, w_out, lc):
    bsz, ln, d = x.shape
    tm = TOK_TILE
    lct = lc // tm
    inner = zg.shape[-1]
    dsk = jnp.repeat(d_skip, C_HEAD_DIM).reshape(1, inner)
    tokw = lambda n: pl.BlockSpec((1, tm, n), lambda b, j: (b, j + lct, 0))
    full = lambda *shape: pl.BlockSpec(shape, lambda b, j: (0,) * len(shape))
    return pl.pallas_call(
        _odd_out_kernel,
        grid=(bsz, (ln - lc) // tm),
        in_specs=[tokw(d),
                  pl.BlockSpec((1, 1, N_MOD, d), lambda b, j: (b, 1, 0, 0)),
                  pl.BlockSpec((2, 1, tm, inner), lambda b, j: (0, b, j + lct, 0)),
                  tokw(inner), tokw(inner), full(1, inner), full(1, inner), full(inner, d)],
        out_specs=pl.BlockSpec((1, tm, d), lambda b, j: (b, j, 0)),
        out_shape=jax.ShapeDtypeStruct((bsz, ln - lc, d), F32),
        compiler_params=_params("arbitrary", "arbitrary"),
        name="odd_out",
    )(x, modx, ys, xact, zg, dsk, norm_w.reshape(1, inner), w_out)


def _exchange(xs, i, l, descending=True):
    hi, lo = jnp.maximum(xs[i], xs[l]), jnp.minimum(xs[i], xs[l])
    xs[i], xs[l] = (hi, lo) if descending else (lo, hi)


def _bitonic_merge(xs):
    xs = list(xs)
    j = len(xs) // 2
    while j >= 1:
        for i in range(len(xs)):
            if i ^ j > i:
                _exchange(xs, i, i ^ j)
        j //= 2
    return xs


def _bitonic_sort(xs):
    xs = list(xs)
    k = 2
    while k <= len(xs):
        j = k // 2
        while j >= 1:
            for i in range(len(xs)):
                if i ^ j > i:
                    _exchange(xs, i, i ^ j, descending=(i & k) == 0)
            j //= 2
        k *= 2
    return xs


def _merge_sublanes(xs):
    n = len(xs)
    for shift in (4, 2, 1):
        other = [pltpu.roll(x, shift, 0) for x in xs]
        xs = _bitonic_merge([jnp.maximum(xs[k], other[n - 1 - k]) for k in range(n)])
    return xs


def _top_values(s):
    return _merge_sublanes(_bitonic_sort([s[8 * k:8 * k + 8] for k in range(s.shape[0] // 8)]))


def _top_pair_sums(t1, t2):
    sub = lax.broadcasted_iota(jnp.int32, t2[0].shape, 0)
    lo, hi = t2[0], t2[8]
    for b in range(1, 8):
        lo = jnp.where(sub == b, t2[b], lo)
        hi = jnp.where(sub == b, t2[8 + b], hi)
    xs = [t1[a] + lo for a in range(P_TOPK)]
    carry = t1[0] + hi
    for k in range(P_TOPK):
        xs[k], carry = jnp.maximum(xs[k], carry), jnp.minimum(xs[k], carry)
    return _merge_sublanes(xs)


def _split_bf16(x):
    hi = x.astype(BF16)
    return hi, (x - hi.astype(F32)).astype(BF16)


def _mm_split(a_hi, a_lo, b):
    b_hi, b_lo = _split_bf16(b)
    return _mm(a_hi, b_hi) + (_mm(a_hi, b_lo) + _mm(a_lo, b_hi))


def _count(flags):
    total = None
    for f in flags:
        one = jnp.where(f, 1.0, 0.0)
        total = one if total is None else total + one
    return total


def _peer_query_kernel(x_ref, mod_ref, g_ref, wqh_ref, wql_ref, kh_ref, kl_ref,
                       ht_o, n_o, c_o, rank_o, e2_o):
    assert P_TOPK == 16 and P_KEYS == 128
    mod = mod_ref[0]
    h2 = _norm_mod(x_ref[...], g_ref[...], mod[3:4], mod[4:5])
    ht = h2.T
    ht_o[...] = ht.astype(BF16)
    qt = _mm_split(wqh_ref[...], wql_ref[...], ht)
    kd = P_KEY_DIM
    for h in range(P_HEADS):
        s1 = _mm_split(kh_ref[h, 0], kl_ref[h, 0], qt[(2 * h) * kd:(2 * h + 1) * kd])
        s2 = _mm_split(kh_ref[h, 1], kl_ref[h, 1], qt[(2 * h + 1) * kd:(2 * h + 2) * kd])
        t1 = _top_values(s1)
        t2 = _top_values(s2)
        best = _top_pair_sums(t1, t2)
        z = jnp.ones_like(best[0])
        for k in range(1, P_TOPK):
            z = z + jnp.exp(best[k] - best[0])
        inv_z = 1.0 / z
        tau = best[P_TOPK - 1]
        ranks, e2s = [], []
        for k in range(P_KEYS // 8):
            rows = slice(8 * k, 8 * k + 8)
            n_o[h, rows] = _count([(s1[rows] + t2[b]) >= tau for b in range(P_TOPK)])
            c_o[h, rows] = jnp.exp(s1[rows] - t1[0]) * inv_z
            ranks.append(_count([t2[b] > s2[rows] for b in range(P_TOPK)]))
            e2s.append(jnp.exp(s2[rows] - t2[0]))
        rank_o[h] = jnp.concatenate(ranks, axis=0).astype(BF16)
        e2_o[h] = jnp.concatenate(e2s, axis=0).astype(BF16)


def _peer_query(xt, mod, gain, wq_t, keys, blocks_per_mod, tt):
    t_all, d = xt.shape
    nq = wq_t.shape[0]
    nblk = t_all // tt
    head = pl.BlockSpec((P_HEADS, P_KEYS, tt), lambda i: (0, 0, i))
    hsd = lambda dt: jax.ShapeDtypeStruct((P_HEADS, P_KEYS, t_all), dt)
    wq_hi, wq_lo = _split_bf16(wq_t)
    k_hi, k_lo = _split_bf16(keys)
    wspec = pl.BlockSpec((nq, d), lambda i: (0, 0))
    kspec = pl.BlockSpec((P_HEADS, 2, P_KEYS, P_KEY_DIM), lambda i: (0, 0, 0, 0))
    return pl.pallas_call(
        _peer_query_kernel,
        grid=(nblk,),
        in_specs=[pl.BlockSpec((tt, d), lambda i: (i, 0)),
                  pl.BlockSpec((1, N_MOD, d), lambda i: (i // blocks_per_mod, 0, 0)),
                  pl.BlockSpec((1, d), lambda i: (0, 0)),
                  wspec, wspec, kspec, kspec],
        out_specs=[pl.BlockSpec((d, tt), lambda i: (0, i)), head, head, head, head],
        out_shape=[jax.ShapeDtypeStruct((d, t_all), BF16), hsd(F32), hsd(F32), hsd(BF16), hsd(BF16)],
        compiler_params=_params("arbitrary"),
        name="peer_query",
    )(xt, mod, gain.reshape(1, d), wq_hi, wq_lo, k_hi, k_lo)


def _gelu_exact(x):
    return 0.5 * x * (1.0 + lax.erf(x * (2.0 ** -0.5)))


BF16_ROWS = 16


def _peer_main_kernel(x_ref, mod_ref, ht_ref, u_ref, vt_ref, n_ref, c_ref, rank_ref, e2_ref,
                      o_ref, acc_ref, act_ref, coef_ref):
    g = pl.program_id(1)
    last = pl.num_programs(1) - 1
    tt = ht_ref.shape[1]

    @pl.when(g == 0)
    def _():
        acc_ref[...] = jnp.zeros_like(acc_ref)
        act_ref[...] = jnp.zeros_like(act_ref)
        coef_ref[...] = jnp.zeros_like(coef_ref)

    acc_ref[...] += _mm(vt_ref[...], coef_ref[...] * act_ref[...])

    blk = jnp.minimum(g, last - 1)
    rows_per_step = u_ref.shape[0] // P_KEYS
    tiles = P_KEYS // BF16_ROWS
    zero = jnp.zeros((BF16_ROWS, tt), BF16)
    for ii in range(rows_per_step):
        row = blk * rows_per_step + ii
        coef = [None] * tiles
        for h in range(P_HEADS):
            nb = jnp.broadcast_to(n_ref[h, pl.ds(row, 1), :], (BF16_ROWS, tt)).astype(BF16)
            cb = jnp.broadcast_to(c_ref[h, pl.ds(row, 1), :], (BF16_ROWS, tt)).astype(BF16)
            for k in range(tiles):
                rows = slice(k * BF16_ROWS, (k + 1) * BF16_ROWS)
                term = jnp.where(rank_ref[h, rows, :] < nb, e2_ref[h, rows, :] * cb, zero)
                coef[k] = term if coef[k] is None else coef[k] + term
        for k in range(tiles):
            rows = slice(ii * P_KEYS + k * BF16_ROWS, ii * P_KEYS + (k + 1) * BF16_ROWS)
            coef_ref[rows, :] = coef[k]
    act_ref[...] = _gelu_exact(_mm(u_ref[...], ht_ref[...])).astype(BF16)

    @pl.when(g == last)
    def _():
        o_ref[...] = x_ref[...] + mod_ref[0][5:6] * acc_ref[...].T


def _peer_main(xt, mod, ht, u_tab, vt_tab, n, c, rank, e2, blocks_per_mod, tt):
    t_all, d = xt.shape
    n_exp = u_tab.shape[0]
    eb = PEER_EXP
    nsteps = n_exp // eb
    head = pl.BlockSpec((P_HEADS, P_KEYS, tt), lambda i, s: (0, 0, i))
    return pl.pallas_call(
        _peer_main_kernel,
        grid=(t_all // tt, nsteps + 1),
        in_specs=[pl.BlockSpec((tt, d), lambda i, s: (i, 0)),
                  pl.BlockSpec((1, N_MOD, d), lambda i, s: (i // blocks_per_mod, 0, 0)),
                  pl.BlockSpec((d, tt), lambda i, s: (0, i)),
                  pl.BlockSpec((eb, d), lambda i, s: (jnp.minimum(s, nsteps - 1), 0)),
                  pl.BlockSpec((d, eb), lambda i, s: (0, jnp.maximum(s - 1, 0))),
                  head, head, head, head],
        out_specs=pl.BlockSpec((tt, d), lambda i, s: (i, 0)),
        out_shape=jax.ShapeDtypeStruct((t_all, d), F32),
        scratch_shapes=[pltpu.VMEM((d, tt), F32), pltpu.VMEM((eb, tt), BF16), pltpu.VMEM((eb, tt), BF16)],
        compiler_params=_params("arbitrary", "arbitrary"),
        name="peer_main",
    )(xt, mod, ht, u_tab, vt_tab, n, c, rank, e2)


def _peer_layer(xt, mod, gain, w_q, keys, u_tab, v_tab, blocks_per_mod, tt):
    wq_t = w_q.T
    tq = PEER_QUERY_TOK
    ht, n, c, rank, e2 = _peer_query(xt, mod, gain, wq_t, keys, blocks_per_mod * (tt // tq), tq)
    return _peer_main(xt, mod, ht, u_tab.astype(BF16), v_tab.T.astype(BF16), n, c, rank, e2,
                      blocks_per_mod, tt)


def kernel(x, c, ctx, c_ctx, mod_w, mod_b, norm1_g, norm2_g, ev_w_in, ev_w_out, attn_q_gain, attn_k_gain,
           rw_mu, rw_w0, rw_w2, rw_a0, rw_a2, rw_g2, rw_k_k, rw_k_a, rw_r_k, rw_gn_w, rw_gn_b, ssd_w_in,
           ssd_conv_w, ssd_conv_b, ssd_dt_bias, ssd_a_log, ssd_d, ssd_norm_w, ssd_w_out, peer_w_q,
           peer_keys, peer_u, peer_v):
    bsz, seq, dm = x.shape
    lc = ctx.shape[1]
    depth = mod_w.shape[0]
    assert lc == TOK_TILE and seq % PEER_TOK == 0 and (bsz * lc) % PEER_TOK == 0

    rows = -(-(bsz + 1) // 8) * 8
    cc = jnp.concatenate([c, c_ctx[None], jnp.zeros((rows - bsz - 1, dm), F32)], axis=0)
    mods = _modulation(cc, mod_w, mod_b)
    xall = jnp.concatenate([ctx, x], axis=1)

    for i in range(depth):
        last = i == depth - 1
        j = i // 2
        ml = mods[i, :bsz].reshape(bsz, N_MOD, dm)
        mc = mods[i, bsz].reshape(1, N_MOD, dm)
        modx = jnp.stack([jnp.broadcast_to(mc, (bsz, N_MOD, dm)), ml], axis=1)
        if i % 2 == 0:
            w_in = ev_w_in[j].astype(BF16)
            za = _in_proj(xall, modx, norm1_g[i], w_in[:, :A_IN], lc)
            zr = _in_proj(xall, modx, norm1_g[i], w_in[:, A_IN:], lc)
            q, k, v = _attn_pre(za, attn_q_gain[j], attn_k_gain[j], lc)
            attn = _attention(q, k, v, lc)
            r, vv, kkn, g, bonus, lw, a, kd = _rwkv_pre(
                zr, rw_mu[j], rw_w0[j], rw_w2[j], rw_a0[j], rw_a2[j], rw_g2[j], rw_k_k[j], rw_k_a[j],
                rw_r_k[j], lc)
            wkv = _rwkv_chunks(r, vv, kkn, lw, a, kd, lc)
            xall = _even_out(xall, modx, attn, wkv, bonus, g, rw_gn_w[j], rw_gn_b[j],
                             ev_w_out[j].astype(BF16), lc)
            xlat = xall[:, lc:]
        else:
            n_heads = ssd_a_log.shape[-1]
            inner = n_heads * C_HEAD_DIM
            conv_dim = inner + 2 * C_GROUPS * C_STATE
            w_in = ssd_w_in[j]
            w_dt = jnp.zeros((dm, 2 * LANES), F32)
            w_dt = w_dt.at[:, :n_heads].set(w_in[:, inner + conv_dim:inner + conv_dim + n_heads])
            w_dt = w_dt.at[:, LANES:LANES + n_heads].set(w_in[:, inner + conv_dim + n_heads:])
            dtb = jnp.zeros((1, 2 * LANES), F32)
            dtb = dtb.at[0, :n_heads].set(ssd_dt_bias[j, 0]).at[0, LANES:LANES + n_heads].set(ssd_dt_bias[j, 1])
            zg = _in_proj(xall, modx, norm1_g[i], w_in[:, :inner].astype(BF16), lc)
            xbc = _in_proj(xall, modx, norm1_g[i], w_in[:, inner:inner + conv_dim].astype(BF16), lc)
            dtraw = _in_proj(xall, modx, norm1_g[i], w_dt.astype(BF16), lc)
            xact, dtact = _ssd_pre(xbc, dtraw, ssd_conv_w[j], ssd_conv_b[j], dtb, lc)
            ys = _ssd_chunks(xact, dtact, ssd_a_log[j], lc)
            assert last, "the SSD layer keeps only the latent stream"
            xlat = _odd_out(xall, modx, ys, xact, zg, ssd_d[j], ssd_norm_w[j],
                            ssd_w_out[j].astype(BF16), lc)

        tt = PEER_TOK
        xl2 = _peer_layer(xlat.reshape(bsz * seq, dm), ml, norm2_g[i], peer_w_q[i], peer_keys[i],
                          peer_u[i], peer_v[i], seq // tt, tt).reshape(bsz, seq, dm)
        if last:
            return xl2
        xc2 = _peer_layer(xall[:, :lc].reshape(bsz * lc, dm), mc, norm2_g[i], peer_w_q[i], peer_keys[i],
                          peer_u[i], peer_v[i], (bsz * lc) // tt, tt).reshape(bsz, lc, dm)
        xall = jnp.concatenate([xc2, xl2], axis=1)
    return xall[:, lc:]
```

```python
import functools
import math

import numpy as np
import jax
import jax.numpy as jnp
from jax import lax
from jax.experimental import pallas as pl
from jax.experimental.pallas import tpu as pltpu

F32 = jnp.float32
BF16 = jnp.bfloat16
HI = lax.Precision.HIGHEST

GRID_W = 64
EPS = 1e-6
N_MOD = 6
A_HEADS, A_KV_HEADS, A_HEAD_DIM = 8, 2, 64
A_Q = A_HEADS * A_HEAD_DIM
A_KV = A_KV_HEADS * A_HEAD_DIM
A_IN = A_Q + 2 * A_KV
ROPE_THETA = 10000.0
B_HEADS, B_HEAD_DIM = 8, 64
B_WIDTH = B_HEADS * B_HEAD_DIM
B_DECAY_LORA, B_AAA_LORA, B_GATE_LORA = 64, 64, 128
B_IN = 3 * B_WIDTH + B_DECAY_LORA + B_AAA_LORA + B_GATE_LORA
B_GN_EPS = 64e-5
C_HEAD_DIM, C_GROUPS, C_STATE, C_CHUNK = 64, 4, 128, 128
P_HEADS, P_KEYS, P_KEY_DIM, P_TOPK = 8, 128, 128, 16

LANES = 128
BF16_ROWS = 16
TOK_TILE = 256
R_CHUNK = 64
PEER_TOK = 512
PEER_QUERY_TOK = 256
PEER_EXP = 1024
VMEM_LIMIT = 56 * 1024 * 1024


def _mm(a, b, prec=None):
    return lax.dot_general(a, b, (((1,), (0,)), ((), ())), precision=prec,
                           preferred_element_type=F32)


def _mm_nt(a, b, prec=None):
    return lax.dot_general(a, b, (((1,), (1,)), ((), ())), precision=prec,
                           preferred_element_type=F32)


def _mm_tn(a, b, prec=None):
    return lax.dot_general(a, b, (((0,), (0,)), ((), ())), precision=prec,
                           preferred_element_type=F32)


def _params(*sem):
    return pltpu.CompilerParams(dimension_semantics=sem, vmem_limit_bytes=VMEM_LIMIT)


def _softplus(x):
    return jnp.maximum(x, 0.0) + jnp.log1p(jnp.exp(-jnp.abs(x)))


def _sigmoid(x):
    return 1.0 / (1.0 + jnp.exp(-x))


def _silu(x):
    return x * _sigmoid(x)


def _mod_kernel(c_ref, w_ref, b_ref, o_ref):
    o_ref[0] = _mm(_silu(c_ref[...]), w_ref[0], HI) + b_ref[0]


def _modulation(cc, mod_w, mod_b):
    depth, d, n6 = mod_w.shape
    rows = cc.shape[0]
    tn = 1024
    return pl.pallas_call(
        _mod_kernel,
        grid=(depth, n6 // tn),
        in_specs=[pl.BlockSpec((rows, d), lambda l, n: (0, 0)),
                  pl.BlockSpec((1, d, tn), lambda l, n: (l, 0, n)),
                  pl.BlockSpec((1, 1, tn), lambda l, n: (l, 0, n))],
        out_specs=pl.BlockSpec((1, rows, tn), lambda l, n: (l, 0, n)),
        out_shape=jax.ShapeDtypeStruct((depth, rows, n6), F32),
        compiler_params=_params("arbitrary", "arbitrary"),
        name="modulation",
    )(cc, mod_w, mod_b.reshape(depth, 1, n6))


def _norm_mod(x, gain, shift, scale):
    ms = jnp.mean(x * x, axis=-1, keepdims=True)
    return x * lax.rsqrt(ms + EPS) * gain * (1.0 + scale) + shift


def _stream_specs(tm, d, lct):
    return [pl.BlockSpec((1, tm, d), lambda b, j: (b, jnp.minimum(j, lct - 1), 0)),
            pl.BlockSpec((1, tm, d), lambda b, j: (b, jnp.maximum(j - lct, 0), 0))]


def _stream_tile(xc_ref, xl_ref, lct):
    return jnp.where(pl.program_id(1) < lct, xc_ref[0], xl_ref[0])


def _in_proj_kernel(xc_ref, xl_ref, mod_ref, g_ref, w_ref, o_ref, *, lct):
    mod = mod_ref[0, 0]
    h = _norm_mod(_stream_tile(xc_ref, xl_ref, lct), g_ref[...], mod[0:1], mod[1:2]).astype(BF16)
    n = o_ref.shape[-1]
    for c0 in range(0, n, 1024):
        c1 = min(n, c0 + 1024)
        o_ref[0, :, c0:c1] = _mm(h, w_ref[:, c0:c1])


def _in_proj(xc, xl, modx, gain, w):
    bsz, lc, d = xc.shape
    ln = lc + xl.shape[1]
    n = w.shape[1]
    tm = TOK_TILE
    lct = lc // tm
    return pl.pallas_call(
        functools.partial(_in_proj_kernel, lct=lct),
        grid=(bsz, ln // tm),
        in_specs=_stream_specs(tm, d, lct) + [
            pl.BlockSpec((1, 1, N_MOD, d), lambda b, j: (b, jnp.where(j < lct, 0, 1), 0, 0)),
            pl.BlockSpec((1, d), lambda b, j: (0, 0)),
            pl.BlockSpec((d, n), lambda b, j: (0, 0))],
        out_specs=pl.BlockSpec((1, tm, n), lambda b, j: (b, j, 0)),
        out_shape=jax.ShapeDtypeStruct((bsz, ln, n), F32),
        compiler_params=_params("arbitrary", "arbitrary"),
        name="in_proj",
    )(xc, xl, modx, gain.reshape(1, d), w)


def _rope_tables(lc, seq):
    t = jnp.arange(seq)
    row = (t // GRID_W).astype(F32)
    col = (t % GRID_W).astype(F32)
    m = A_HEAD_DIM // 4
    inv = ROPE_THETA ** (-jnp.arange(m, dtype=F32) / m)
    ar, ac = row[:, None] * inv, col[:, None] * inv
    cos = jnp.concatenate([jnp.cos(ar), jnp.cos(ar), jnp.cos(ac), jnp.cos(ac)], axis=-1)
    sin = jnp.concatenate([-jnp.sin(ar), jnp.sin(ar), -jnp.sin(ac), jnp.sin(ac)], axis=-1)
    reps = LANES // A_HEAD_DIM
    cos = jnp.concatenate([jnp.ones((lc, A_HEAD_DIM), F32), cos], axis=0)
    sin = jnp.concatenate([jnp.zeros((lc, A_HEAD_DIM), F32), sin], axis=0)
    return jnp.tile(cos, (1, reps)), jnp.tile(sin, (1, reps))


def _head_block_ones(width, scale):
    idx = np.arange(LANES) // width
    return jnp.asarray((idx[:, None] == idx[None, :]).astype(np.float32) * scale)


def _attn_pre_kernel(z_ref, qg_ref, kg_ref, cos_ref, sin_ref, bones_ref, q_ref, k_ref, v_ref):
    tm = z_ref.shape[1]
    cs, sn = cos_ref[...], sin_ref[...]
    bones = bones_ref[...]
    lane = lax.broadcasted_iota(jnp.int32, (tm, LANES), 1)
    first = (lane % 32) < 16
    low = lane < A_HEAD_DIM

    def norm_rope(x, gain):
        ms = _bmm(x * x, bones)
        y = x * lax.rsqrt(ms + EPS) * gain
        partner = jnp.where(first, pltpu.roll(y, LANES - 16, 1), pltpu.roll(y, 16, 1))
        return y * cs + partner * sn

    heads_per_slab = LANES // A_HEAD_DIM
    group = A_HEADS // A_KV_HEADS
    for m in range(A_Q // LANES):
        sl = slice(m * LANES, (m + 1) * LANES)
        y = norm_rope(z_ref[0, :, sl], qg_ref[:, sl]) * (A_HEAD_DIM ** -0.5)
        swapped = pltpu.roll(y, A_HEAD_DIM, 1)
        for hh in range(heads_per_slab):
            h = m * heads_per_slab + hh
            kvh = h // group
            src = y if kvh == hh else swapped
            keep = low if kvh == 0 else jnp.logical_not(low)
            q_ref[0, h] = jnp.where(keep, src, 0.0).astype(BF16)
    k_ref[0] = norm_rope(z_ref[0, :, A_Q:A_Q + A_KV], kg_ref[...]).astype(BF16)
    v_ref[0] = z_ref[0, :, A_Q + A_KV:A_IN].astype(BF16)


def _attn_pre(za, q_gain, k_gain, lc):
    bsz, ln, _ = za.shape
    tm = TOK_TILE
    cos, sin = _rope_tables(lc, ln - lc)
    qg = jnp.tile(q_gain, A_HEADS).reshape(1, A_Q)
    kg = jnp.tile(k_gain, A_KV_HEADS).reshape(1, A_KV)
    bones = _head_block_ones(A_HEAD_DIM, 1.0 / A_HEAD_DIM)
    return pl.pallas_call(
        _attn_pre_kernel,
        grid=(bsz, ln // tm),
        in_specs=[pl.BlockSpec((1, tm, A_IN), lambda b, j: (b, j, 0)),
                  pl.BlockSpec((1, A_Q), lambda b, j: (0, 0)),
                  pl.BlockSpec((1, A_KV), lambda b, j: (0, 0)),
                  pl.BlockSpec((tm, LANES), lambda b, j: (j, 0)),
                  pl.BlockSpec((tm, LANES), lambda b, j: (j, 0)),
                  pl.BlockSpec((LANES, LANES), lambda b, j: (0, 0))],
        out_specs=[pl.BlockSpec((1, A_HEADS, tm, LANES), lambda b, j: (b, 0, j, 0)),
                   pl.BlockSpec((1, tm, A_KV), lambda b, j: (b, j, 0)),
                   pl.BlockSpec((1, tm, A_KV), lambda b, j: (b, j, 0))],
        out_shape=[jax.ShapeDtypeStruct((bsz, A_HEADS, ln, LANES), BF16),
                   jax.ShapeDtypeStruct((bsz, ln, A_KV), BF16),
                   jax.ShapeDtypeStruct((bsz, ln, A_KV), BF16)],
        compiler_params=_params("arbitrary", "arbitrary"),
        name="attn_pre",
    )(za, qg, kg, cos, sin, bones)


def _attn_kernel(q_ref, k_ref, v_ref, o_ref, *, lc):
    tq = q_ref.shape[2]
    j = pl.program_id(1)
    lane = lax.broadcasted_iota(jnp.int32, (tq, LANES), 1)
    low = lane < A_HEAD_DIM
    heads_per_slab = LANES // A_HEAD_DIM
    group = A_HEADS // A_KV_HEADS

    def run(klen):
        k = k_ref[0, :klen, :]
        v = v_ref[0, :klen, :]
        for m in range(A_Q // LANES):
            parts = []
            for hh in range(heads_per_slab):
                h = m * heads_per_slab + hh
                kvh = h // group
                s = _mm_nt(q_ref[0, h], k)
                p = jnp.exp(s - jnp.max(s, axis=-1, keepdims=True))
                l = jnp.sum(p, axis=-1, keepdims=True)
                o = _mm(p.astype(BF16), v) / l
                parts.append(o if kvh == hh else pltpu.roll(o, A_HEAD_DIM, 1))
            o_ref[0, :, m * LANES:(m + 1) * LANES] = jnp.where(low, parts[0], parts[1])

    @pl.when(j * tq < lc)
    def _():
        run(lc)

    @pl.when(j * tq >= lc)
    def _():
        run(k_ref.shape[1])


def _attention(q, k, v, lc):
    bsz, _, ln, _ = q.shape
    tq = TOK_TILE
    return pl.pallas_call(
        functools.partial(_attn_kernel, lc=lc),
        grid=(bsz, ln // tq),
        in_specs=[pl.BlockSpec((1, A_HEADS, tq, LANES), lambda b, j: (b, 0, j, 0)),
                  pl.BlockSpec((1, ln, A_KV), lambda b, j: (b, 0, 0)),
                  pl.BlockSpec((1, ln, A_KV), lambda b, j: (b, 0, 0))],
        out_specs=pl.BlockSpec((1, tq, A_Q), lambda b, j: (b, j, 0)),
        out_shape=jax.ShapeDtypeStruct((bsz, ln, A_Q), F32),
        compiler_params=_params("arbitrary", "arbitrary"),
        name="attention",
    )(q, k, v)


def _shifted_rows(z, prev_row, next_row):
    tm = z.shape[0]
    row = lax.broadcasted_iota(jnp.int32, (tm, 1), 0)
    zprev = jnp.where(row == 0, prev_row, pltpu.roll(z, 1, 0))
    znext = jnp.where(row == tm - 1, next_row, pltpu.roll(z, tm - 1, 0))
    return zprev, znext


def _halo_rows(zp_ref, zn_ref, j, lct, nj):
    prev_ok = jnp.logical_and(j != 0, j != lct)
    next_ok = jnp.logical_and(j != lct - 1, j != nj - 1)
    prev_row = jnp.where(prev_ok, zp_ref[0, 7:8, :], 0.0)
    next_row = jnp.where(next_ok, zn_ref[0, 0:1, :], 0.0)
    return prev_row, next_row


def _halo_specs(tm, width, ln):
    nb8 = tm // 8
    last = ln // 8 - 1
    return [pl.BlockSpec((1, tm, width), lambda b, j: (b, j, 0)),
            pl.BlockSpec((1, 8, width), lambda b, j: (b, jnp.maximum(j * nb8 - 1, 0), 0)),
            pl.BlockSpec((1, 8, width), lambda b, j: (b, jnp.minimum((j + 1) * nb8, last), 0))]


def _per_head_sum(x, bones):
    return jnp.concatenate(
        [_bmm(x[:, m * LANES:(m + 1) * LANES], bones) for m in range(x.shape[1] // LANES)], axis=1)


def _rwkv_pre_kernel(z_ref, zp_ref, zn_ref, mu_ref, w0_ref, w2_ref, a0_ref, a2_ref, g2_ref,
                     kk_ref, ka_ref, rk_ref, bones_ref,
                     r_o, v_o, kkn_o, g_o, bonus_o, lw_o, a_o, kd_o, *, lct):
    j = pl.program_id(1)
    z = z_ref[0]
    prev_row, next_row = _halo_rows(zp_ref, zn_ref, j, lct, pl.num_programs(1))
    zprev, znext = _shifted_rows(z, prev_row, next_row)
    zm = z + mu_ref[...] * (0.5 * (zprev + znext) - z)
    w = B_WIDTH
    r, k, v = zm[:, 0:w], zm[:, w:2 * w], zm[:, 2 * w:3 * w]
    wa = zm[:, 3 * w:3 * w + B_DECAY_LORA + B_AAA_LORA]
    gl = zm[:, 3 * w + B_DECAY_LORA + B_AAA_LORA:]
    bones = bones_ref[...]
    kk = k * kk_ref[...]
    kkn = kk / jnp.maximum(jnp.sqrt(_per_head_sum(kk * kk, bones)), 1e-12)
    r_o[0] = r
    v_o[0] = v
    kkn_o[0] = kkn
    g_o[0] = _bmm(_sigmoid(gl), g2_ref[...])
    twa = jnp.tanh(wa)
    bonus = jnp.zeros_like(r)
    for d in range(2):
        wraw = w0_ref[d] + _bmm(twa, w2_ref[d])
        lw_o[d, 0] = -jnp.exp(-_softplus(-wraw) - 0.5)
        a = _sigmoid(a0_ref[d] + _bmm(wa, a2_ref[d]))
        kd = k * (1.0 + (a - 1.0) * ka_ref[...])
        a_o[d, 0] = a
        kd_o[d, 0] = kd
        bonus = bonus + _per_head_sum(r * kd * rk_ref[...], bones) * v
    bonus_o[0] = bonus


def _rwkv_pre(zr, mu, w0, w2, a0, a2, g2, k_k, k_a, r_k, lc):
    bsz, ln, width = zr.shape
    tm = TOK_TILE
    w = B_WIDTH
    zeros = jnp.zeros((2, B_DECAY_LORA, w), F32)
    w2p = jnp.concatenate([w2, zeros], axis=1)
    a2p = jnp.concatenate([zeros, a2], axis=1)
    bones = _head_block_ones(B_HEAD_DIM, 1.0)
    full = lambda *shape: pl.BlockSpec(shape, lambda b, j: (0,) * len(shape))
    tok = pl.BlockSpec((1, tm, w), lambda b, j: (b, j, 0))
    tok2 = pl.BlockSpec((2, 1, tm, w), lambda b, j: (0, b, j, 0))
    sd = jax.ShapeDtypeStruct((bsz, ln, w), F32)
    sd2 = jax.ShapeDtypeStruct((2, bsz, ln, w), F32)
    return pl.pallas_call(
        functools.partial(_rwkv_pre_kernel, lct=lc // tm),
        grid=(bsz, ln // tm),
        in_specs=_halo_specs(tm, width, ln) + [
            full(1, width), full(2, 1, w), full(2, 2 * B_DECAY_LORA, w), full(2, 1, w),
            full(2, 2 * B_AAA_LORA, w), full(B_GATE_LORA, w), full(1, w), full(1, w), full(1, w),
            full(LANES, LANES)],
        out_specs=[tok, tok, tok, tok, tok, tok2, tok2, tok2],
        out_shape=[sd, sd, sd, sd, sd, sd2, sd2, sd2],
        compiler_params=_params("arbitrary", "arbitrary"),
        name="rwkv_pre",
    )(zr, zr, zr, mu.reshape(1, width), w0.reshape(2, 1, w), w2p, a0.reshape(2, 1, w), a2p, g2,
      k_k.reshape(1, w), k_a.reshape(1, w), r_k.reshape(1, w), bones)


def _scan_chunk(i, d, nctx, ntot):
    rev = jnp.where(i < nctx, nctx - 1 - i, ntot + nctx - 1 - i)
    return jnp.where(d == 0, i, rev)


def _order_masks(n, pair):
    t = np.arange(n * pair)
    same = (t[:, None] // n) == (t[None, :] // n)
    tt, ss = t[:, None] % n, t[None, :] % n
    incl = np.stack([same & (ss <= tt), same & (ss >= tt)]).astype(np.float32)
    strict = np.stack([same & (ss < tt), same & (ss > tt)]).astype(np.float32)
    return jnp.asarray(incl), jnp.asarray(strict)


def _pair_stack(x, low):
    return jnp.concatenate([jnp.where(low, x, 0.0), jnp.where(low, 0.0, x)], axis=0)


def _bmm(a, b):
    return _mm(a.astype(BF16), b.astype(BF16))


def _rwkv_chunk_kernel(rf_ref, vf_ref, kf_ref, lwf_ref, af_ref, kdf_ref,
                       rb_ref, vb_ref, kb_ref, lwb_ref, ab_ref, kdb_ref,
                       tri_ref, incl_ref, strict_ref, of_ref, ob_ref, state_ref):
    c = R_CHUNK
    n2 = 2 * c

    @pl.when(pl.program_id(1) == 0)
    def _():
        state_ref[...] = jnp.zeros_like(state_ref)

    low = lax.broadcasted_iota(jnp.int32, (c, LANES), 1) < B_HEAD_DIM
    eye = (lax.broadcasted_iota(jnp.int32, (LANES, LANES), 0)
           == lax.broadcasted_iota(jnp.int32, (LANES, LANES), 1)).astype(F32)
    stack = lambda x: _pair_stack(x, low)
    refs = ((rf_ref, vf_ref, kf_ref, lwf_ref, af_ref, kdf_ref, of_ref),
            (rb_ref, vb_ref, kb_ref, lwb_ref, ab_ref, kdb_ref, ob_ref))
    dirs, slabs, outs = [], [], []
    kts, bts, kds, rts, vs, pcs = [], [], [], [], [], []
    for d, (r_ref, v_ref, kkn_ref, lw_ref, a_ref, kd_ref, o_ref) in enumerate(refs):
        lw = lw_ref[0, 0]
        cum = _mm(tri_ref[d], lw, HI)
        p_in = jnp.exp(cum)
        p_inv = jnp.exp(-cum)
        p_ex = jnp.exp(cum - lw)
        p_all = jnp.exp(jnp.sum(lw, axis=0, keepdims=True))
        kkn = kkn_ref[0]
        kt = kkn * p_ex
        bt = kkn * a_ref[0, 0] * p_inv
        kd = kd_ref[0, 0] * p_inv
        rt = r_ref[0] * p_in
        v = v_ref[0]
        for m in range(B_WIDTH // LANES):
            sl = slice(m * LANES, (m + 1) * LANES)
            dirs.append(d)
            slabs.append(m)
            outs.append(o_ref)
            kts.append(stack(kt[:, sl]).astype(BF16))
            bts.append(stack(bt[:, sl]).astype(BF16))
            kds.append(stack(kd[:, sl]).astype(BF16))
            rts.append(stack(rt[:, sl]))
            vs.append(stack(v[:, sl]).astype(BF16))
            pcs.append(p_all[:, sl])
    n = len(dirs)
    incl = [incl_ref[d] > 0.5 for d in range(2)]
    strict = [strict_ref[d] > 0.5 for d in range(2)]
    gram = [_mm_nt(jnp.concatenate([kts[x], rts[x].astype(BF16)], axis=0),
                   jnp.concatenate([bts[x], kds[x]], axis=0)) for x in range(n)]
    a_b = [jnp.where(strict[dirs[x]], gram[x][:n2, :n2], 0.0) for x in range(n)]
    a_k = [jnp.where(strict[dirs[x]], gram[x][:n2, n2:], 0.0) for x in range(n)]
    m_b = [jnp.where(incl[dirs[x]], gram[x][n2:, :n2], 0.0) for x in range(n)]
    m_k = [jnp.where(incl[dirs[x]], gram[x][n2:, n2:], 0.0) for x in range(n)]
    pw = [-a for a in a_b]
    tinv = [eye + p for p in pw]
    for _ in range(int(math.log2(c)) - 1):
        pw = [_bmm(p, p) for p in pw]
        tinv = [t + _bmm(t, p) for t, p in zip(tinv, pw)]
    mv = [_bmm(jnp.concatenate([a_k[x], m_k[x]], axis=0), vs[x]) for x in range(n)]
    wy = [_bmm(tinv[x], jnp.concatenate([kts[x], mv[x][:n2].astype(BF16)], axis=1)) for x in range(n)]
    mbwy = [_bmm(m_b[x], wy[x]) for x in range(n)]
    wyb = [_mm_tn(wy[x].astype(BF16), bts[x]) for x in range(n)]
    vtk = [_mm_tn(vs[x], kds[x]) for x in range(n)]
    s0 = [state_ref[dirs[x], slabs[x]] for x in range(n)]
    o_st = [_mm_nt((rts[x] - mbwy[x][:, :LANES]).astype(BF16), s0[x].astype(BF16))
            + (mv[x][n2:] - mbwy[x][:, LANES:]) for x in range(n)]
    for x in range(n):
        g = (eye - wyb[x][:LANES]) * pcs[x]
        h = (vtk[x] - wyb[x][LANES:]) * pcs[x]
        outs[x][0, :, slabs[x] * LANES:(slabs[x] + 1) * LANES] = o_st[x][:c] + o_st[x][c:]
        state_ref[dirs[x], slabs[x]] = _bmm(s0[x], g) + h


def _rwkv_chunks(r, v, kkn, lw, a, kd, lc):
    bsz, ln, w = r.shape
    c = R_CHUNK
    nctx, ntot = lc // c, ln // c
    tri, _ = _order_masks(c, 1)
    incl, strict = _order_masks(c, 2)
    specs = []
    for d in range(2):
        tok = pl.BlockSpec((1, c, w), lambda b, i, d=d: (b, _scan_chunk(i, d, nctx, ntot), 0))
        tok2 = pl.BlockSpec((1, 1, c, w), lambda b, i, d=d: (d, b, _scan_chunk(i, d, nctx, ntot), 0))
        specs.append((tok, tok2))
    full = lambda *shape: pl.BlockSpec(shape, lambda b, i: (0,) * len(shape))
    sd = jax.ShapeDtypeStruct((bsz, ln, w), F32)
    return pl.pallas_call(
        _rwkv_chunk_kernel,
        grid=(bsz, ntot),
        in_specs=[specs[0][0]] * 3 + [specs[0][1]] * 3 + [specs[1][0]] * 3 + [specs[1][1]] * 3
        + [full(2, c, c), full(2, 2 * c, 2 * c), full(2, 2 * c, 2 * c)],
        out_specs=[specs[0][0], specs[1][0]],
        out_shape=[sd, sd],
        scratch_shapes=[pltpu.VMEM((2, w // LANES, LANES, LANES), F32)],
        compiler_params=_params("arbitrary", "arbitrary"),
        name="rwkv_chunks",
    )(r, v, kkn, lw, a, kd, r, v, kkn, lw, a, kd, tri, incl, strict)


def _even_out_kernel(xc_ref, xl_ref, mod_ref, attn_ref, wkvf_ref, wkvb_ref, bonus_ref, g_ref, gw_ref,
                     gb_ref, bones_ref, w_ref, oc_ref, ol_ref, *, lct):
    bones = bones_ref[...]
    y = bonus_ref[0]
    for wkv_ref in (wkvf_ref, wkvb_ref):
        o = wkv_ref[0]
        dlt = o - _per_head_sum(o, bones)
        var = _per_head_sum(dlt * dlt, bones)
        y = y + dlt * lax.rsqrt(var + B_GN_EPS) * gw_ref[...] + gb_ref[...]
    y = y * g_ref[0]
    mix = _mm(attn_ref[0].astype(BF16), w_ref[:A_Q, :]) + _mm(y.astype(BF16), w_ref[A_Q:, :])
    res = _stream_tile(xc_ref, xl_ref, lct) + mod_ref[0, 0][2:3] * mix
    is_ctx = pl.program_id(1) < lct

    @pl.when(is_ctx)
    def _():
        oc_ref[0] = res

    @pl.when(jnp.logical_not(is_ctx))
    def _():
        ol_ref[0] = res


def _even_out(xc, xl, modx, attn, wkv, bonus, g, gn_w, gn_b, w_out):
    bsz, lc, d = xc.shape
    seq = xl.shape[1]
    tm = TOK_TILE
    lct = lc // tm
    w = B_WIDTH
    bones = _head_block_ones(B_HEAD_DIM, 1.0 / B_HEAD_DIM)
    tokw = lambda n: pl.BlockSpec((1, tm, n), lambda b, j: (b, j, 0))
    full = lambda *shape: pl.BlockSpec(shape, lambda b, j: (0,) * len(shape))
    streams = _stream_specs(tm, d, lct)
    return pl.pallas_call(
        functools.partial(_even_out_kernel, lct=lct),
        grid=(bsz, (lc + seq) // tm),
        in_specs=streams + [
            pl.BlockSpec((1, 1, N_MOD, d), lambda b, j: (b, jnp.where(j < lct, 0, 1), 0, 0)),
            tokw(A_Q), tokw(w), tokw(w),
            tokw(w), tokw(w), full(1, w), full(1, w), full(LANES, LANES),
            full(A_Q + w, d)],
        out_specs=streams,
        out_shape=[jax.ShapeDtypeStruct((bsz, lc, d), F32), jax.ShapeDtypeStruct((bsz, seq, d), F32)],
        compiler_params=_params("arbitrary", "arbitrary"),
        name="even_out",
    )(xc, xl, modx, attn, wkv[0], wkv[1], bonus, g, gn_w.reshape(1, w), gn_b.reshape(1, w), bones, w_out)


def _ssd_pre_kernel(z_ref, zp_ref, zn_ref, dt_ref, cw_ref, cb_ref, dtb_ref, x_o, dt_o, *, lct):
    j = pl.program_id(1)
    z = z_ref[0]
    prev_row, next_row = _halo_rows(zp_ref, zn_ref, j, lct, pl.num_programs(1))
    zprev, znext = _shifted_rows(z, prev_row, next_row)
    y = zprev * cw_ref[0:1, :] + z * cw_ref[1:2, :] + znext * cw_ref[2:3, :] + cb_ref[...]
    x_o[0] = _silu(y)
    dt_o[0] = _softplus(dt_ref[0] + dtb_ref[...])


def _ssd_pre(xbc, dtraw, conv_w, conv_b, dtb, lc):
    bsz, ln, width = xbc.shape
    tm = TOK_TILE
    nd = dtraw.shape[-1]
    full = lambda *shape: pl.BlockSpec(shape, lambda b, j: (0,) * len(shape))
    return pl.pallas_call(
        functools.partial(_ssd_pre_kernel, lct=lc // tm),
        grid=(bsz, ln // tm),
        in_specs=_halo_specs(tm, width, ln) + [
            pl.BlockSpec((1, tm, nd), lambda b, j: (b, j, 0)),
            full(conv_w.shape[0], width), full(1, width), full(1, nd)],
        out_specs=[pl.BlockSpec((1, tm, width), lambda b, j: (b, j, 0)),
                   pl.BlockSpec((1, tm, nd), lambda b, j: (b, j, 0))],
        out_shape=[jax.ShapeDtypeStruct((bsz, ln, width), F32),
                   jax.ShapeDtypeStruct((bsz, ln, nd), F32)],
        compiler_params=_params("arbitrary", "arbitrary"),
        name="ssd_pre",
    )(xbc, xbc, xbc, dtraw, conv_w, conv_b.reshape(1, width), dtb)


def _ssd_chunk_kernel(x_ref, b_ref, c_ref, dt_ref, alog_ref, incl_ref, expand_ref, o_ref, state_ref,
                      *, n_heads):
    q = C_CHUNK
    i = pl.program_id(2)

    @pl.when(i == 0)
    def _():
        state_ref[...] = jnp.zeros_like(state_ref)

    lane = lax.broadcasted_iota(jnp.int32, (1, LANES), 1)
    a_neg = jnp.where(lane < n_heads, -jnp.exp(alog_ref[0]), 0.0)
    dt = dt_ref[0]
    dta = dt * a_neg
    incl_f = incl_ref[0]
    cum = _mm(incl_f, dta, HI)
    cum_t = cum.T
    total = jnp.sum(dta, axis=0, keepdims=True)
    expand = expand_ref[...]
    dt_x = _bmm(dt, expand)
    ec_x = _bmm(jnp.exp(cum), expand)
    de_x = _bmm(jnp.exp(total - cum), expand)
    cd_x = _bmm(jnp.broadcast_to(jnp.exp(total), (BF16_ROWS, LANES)), expand)[0:1]
    incl = incl_f > 0.5
    low = lax.broadcasted_iota(jnp.int32, (q, LANES), 1) < C_HEAD_DIM
    heads_per_group = n_heads // C_GROUPS
    pairs_per_group = heads_per_group // 2
    gw = heads_per_group * C_HEAD_DIM
    for g in range(C_GROUPS):
        gsl = slice(g * gw, (g + 1) * gw)
        bg = b_ref[0, :, g * C_STATE:(g + 1) * C_STATE].astype(BF16)
        cg = c_ref[0, :, g * C_STATE:(g + 1) * C_STATE].astype(BF16)
        cb = _mm_nt(cg, bg)
        xd = x_ref[0, :, gsl] * dt_x[:, gsl]
        s_prev = state_ref[g]
        y_off = _mm(cg, s_prev.astype(BF16)) * ec_x[:, gsl]
        state_ref[g] = s_prev * cd_x[:, gsl] + _mm_tn(bg, (xd * de_x[:, gsl]).astype(BF16))
        for pp in range(pairs_per_group):
            pr = g * pairs_per_group + pp
            sl = slice(pr * LANES, (pr + 1) * LANES)
            ms = []
            for hh in range(2):
                h = 2 * pr + hh
                seg = cum[:, h:h + 1] - cum_t[h:h + 1, :]
                ms.append((cb * jnp.exp(jnp.where(incl, seg, -1e30))).astype(BF16))
            xd_pair = _pair_stack(xd[:, pp * LANES:(pp + 1) * LANES], low).astype(BF16)
            y_diag = _mm(jnp.concatenate(ms, axis=1), xd_pair)
            o_ref[0, 0, :, sl] = y_diag + y_off[:, pp * LANES:(pp + 1) * LANES]


def _ssd_chunks(xact, dtact, a_log, lc):
    bsz, ln, _ = xact.shape
    q = C_CHUNK
    n_heads = a_log.shape[1]
    inner = n_heads * C_HEAD_DIM
    gn = C_GROUPS * C_STATE
    nctx, ntot = lc // q, ln // q
    incl, _ = _order_masks(q, 1)
    expand = np.zeros((LANES, inner), np.float32)
    expand[np.arange(inner) // C_HEAD_DIM, np.arange(inner)] = 1.0
    alog = jnp.zeros((2, 1, LANES), F32).at[:, 0, :n_heads].set(a_log)
    cmap = lambda off: (lambda b, d, i: (b, _scan_chunk(i, d, nctx, ntot), off))
    return pl.pallas_call(
        functools.partial(_ssd_chunk_kernel, n_heads=n_heads),
        grid=(bsz, 2, ntot),
        in_specs=[pl.BlockSpec((1, q, inner), cmap(0)),
                  pl.BlockSpec((1, q, gn), cmap(inner // gn)),
                  pl.BlockSpec((1, q, gn), cmap(inner // gn + 1)),
                  pl.BlockSpec((1, q, LANES), lambda b, d, i: (b, _scan_chunk(i, d, nctx, ntot), d)),
                  pl.BlockSpec((1, 1, LANES), lambda b, d, i: (d, 0, 0)),
                  pl.BlockSpec((1, q, q), lambda b, d, i: (d, 0, 0)),
                  pl.BlockSpec((LANES, inner), lambda b, d, i: (0, 0))],
        out_specs=pl.BlockSpec((1, 1, q, inner), lambda b, d, i: (d, b, _scan_chunk(i, d, nctx, ntot), 0)),
        out_shape=jax.ShapeDtypeStruct((2, bsz, ln, inner), F32),
        scratch_shapes=[pltpu.VMEM((C_GROUPS, C_STATE, inner // C_GROUPS), F32)],
        compiler_params=_params("arbitrary", "arbitrary", "arbitrary"),
        name="ssd_chunks",
    )(xact, xact, xact, dtact, alog, incl, jnp.asarray(expand, dtype=BF16))


def _odd_out_kernel(x_ref, mod_ref, ys_ref, xs_ref, zg_ref, dsk_ref, nw_ref, w_ref, o_ref):
    y = ys_ref[0, 0] + ys_ref[1, 0] + dsk_ref[...] * xs_ref[0]
    yg = y * _silu(zg_ref[0])
    inner = yg.shape[1]
    gw = inner // C_GROUPS
    parts = []
    for g in range(C_GROUPS):
        t = yg[:, g * gw:(g + 1) * gw]
        ms = jnp.mean(t * t, axis=-1, keepdims=True)
        parts.append((t * lax.rsqrt(ms + EPS) * nw_ref[:, g * gw:(g + 1) * gw]).astype(BF16))
    yn = jnp.concatenate(parts, axis=1)
    o_ref[0] = x_ref[0] + mod_ref[0, 0][2:3] * _mm(yn, w_ref[...])


def _odd_out(x, modx, ys, xact, zg, d_skip, norm_w, w_out, lc):
    bsz, seq, d = x.shape
    ln = lc + seq
    tm = TOK_TILE
    lct = lc // tm
    inner = zg.shape[-1]
    dsk = jnp.repeat(d_skip, C_HEAD_DIM).reshape(1, inner)
    tokw = lambda n: pl.BlockSpec((1, tm, n), lambda b, j: (b, j + lct, 0))
    full = lambda *shape: pl.BlockSpec(shape, lambda b, j: (0,) * len(shape))
    return pl.pallas_call(
        _odd_out_kernel,
        grid=(bsz, (ln - lc) // tm),
        in_specs=[pl.BlockSpec((1, tm, d), lambda b, j: (b, j, 0)),
                  pl.BlockSpec((1, 1, N_MOD, d), lambda b, j: (b, 1, 0, 0)),
                  pl.BlockSpec((2, 1, tm, inner), lambda b, j: (0, b, j + lct, 0)),
                  tokw(inner), tokw(inner), full(1, inner), full(1, inner), full(inner, d)],
        out_specs=pl.BlockSpec((1, tm, d), lambda b, j: (b, j, 0)),
        out_shape=jax.ShapeDtypeStruct((bsz, ln - lc, d), F32),
        compiler_params=_params("arbitrary", "arbitrary"),
        name="odd_out",
    )(x, modx, ys, xact, zg, dsk, norm_w.reshape(1, inner), w_out)


def _exchange(xs, i, l, descending=True):
    hi, lo = jnp.maximum(xs[i], xs[l]), jnp.minimum(xs[i], xs[l])
    xs[i], xs[l] = (hi, lo) if descending else (lo, hi)


def _bitonic_merge(xs):
    xs = list(xs)
    j = len(xs) // 2
    while j >= 1:
        for i in range(len(xs)):
            if i ^ j > i:
                _exchange(xs, i, i ^ j)
        j //= 2
    return xs


def _bitonic_sort(xs):
    xs = list(xs)
    k = 2
    while k <= len(xs):
        j = k // 2
        while j >= 1:
            for i in range(len(xs)):
                if i ^ j > i:
                    _exchange(xs, i, i ^ j, descending=(i & k) == 0)
            j //= 2
        k *= 2
    return xs


def _merge_sublanes(xs):
    n = len(xs)
    for shift in (4, 2, 1):
        other = [pltpu.roll(x, shift, 0) for x in xs]
        xs = _bitonic_merge([jnp.maximum(xs[k], other[n - 1 - k]) for k in range(n)])
    return xs


def _top_values(s):
    return _merge_sublanes(_bitonic_sort([s[8 * k:8 * k + 8] for k in range(s.shape[0] // 8)]))


def _top_pair_sums(t1, t2):
    sub = lax.broadcasted_iota(jnp.int32, t2[0].shape, 0)
    lo, hi = t2[0], t2[8]
    for b in range(1, 8):
        lo = jnp.where(sub == b, t2[b], lo)
        hi = jnp.where(sub == b, t2[8 + b], hi)
    xs = [t1[a] + lo for a in range(P_TOPK)]
    carry = t1[0] + hi
    for k in range(P_TOPK):
        xs[k], carry = jnp.maximum(xs[k], carry), jnp.minimum(xs[k], carry)
    return _merge_sublanes(xs)


def _split_bf16(x):
    hi = x.astype(BF16)
    return hi, (x - hi.astype(F32)).astype(BF16)


def _mm_split(a_hi, a_lo, b):
    b_hi, b_lo = _split_bf16(b)
    return _mm(a_hi, b_hi) + (_mm(a_hi, b_lo) + _mm(a_lo, b_hi))


def _count(flags):
    total = None
    for f in flags:
        one = jnp.where(f, 1.0, 0.0)
        total = one if total is None else total + one
    return total


def _peer_query_kernel(x_ref, mod_ref, g_ref, wqh_ref, wql_ref, kh_ref, kl_ref,
                       ht_o, n_o, c_o, rank_o, e2_o):
    assert P_TOPK == 16 and P_KEYS == 128
    mod = mod_ref[0]
    h2 = _norm_mod(x_ref[...], g_ref[...], mod[3:4], mod[4:5])
    ht = h2.T
    ht_o[...] = ht.astype(BF16)
    qt = _mm_split(wqh_ref[...], wql_ref[...], ht)
    kd = P_KEY_DIM
    for h in range(P_HEADS):
        s1 = _mm_split(kh_ref[h, 0], kl_ref[h, 0], qt[(2 * h) * kd:(2 * h + 1) * kd])
        s2 = _mm_split(kh_ref[h, 1], kl_ref[h, 1], qt[(2 * h + 1) * kd:(2 * h + 2) * kd])
        t1 = _top_values(s1)
        t2 = _top_values(s2)
        best = _top_pair_sums(t1, t2)
        z = jnp.ones_like(best[0])
        for k in range(1, P_TOPK):
            z = z + jnp.exp(best[k] - best[0])
        inv_z = 1.0 / z
        tau = best[P_TOPK - 1]
        ranks, e2s = [], []
        for k in range(P_KEYS // 8):
            rows = slice(8 * k, 8 * k + 8)
            n_o[h, rows] = _count([(s1[rows] + t2[b]) >= tau for b in range(P_TOPK)])
            c_o[h, rows] = jnp.exp(s1[rows] - t1[0]) * inv_z
            ranks.append(_count([t2[b] > s2[rows] for b in range(P_TOPK)]))
            e2s.append(jnp.exp(s2[rows] - t2[0]))
        rank_o[h] = jnp.concatenate(ranks, axis=0).astype(BF16)
        e2_o[h] = jnp.concatenate(e2s, axis=0).astype(BF16)


def _peer_query(xt, mod, gain, wq_t, keys, blocks_per_mod, tt):
    t_all, d = xt.shape
    nq = wq_t.shape[0]
    nblk = t_all // tt
    head = pl.BlockSpec((P_HEADS, P_KEYS, tt), lambda i: (0, 0, i))
    hsd = lambda dt: jax.ShapeDtypeStruct((P_HEADS, P_KEYS, t_all), dt)
    wq_hi, wq_lo = _split_bf16(wq_t)
    k_hi, k_lo = _split_bf16(keys)
    wspec = pl.BlockSpec((nq, d), lambda i: (0, 0))
    kspec = pl.BlockSpec((P_HEADS, 2, P_KEYS, P_KEY_DIM), lambda i: (0, 0, 0, 0))
    return pl.pallas_call(
        _peer_query_kernel,
        grid=(nblk,),
        in_specs=[pl.BlockSpec((tt, d), lambda i: (i, 0)),
                  pl.BlockSpec((1, N_MOD, d), lambda i: (i // blocks_per_mod, 0, 0)),
                  pl.BlockSpec((1, d), lambda i: (0, 0)),
                  wspec, wspec, kspec, kspec],
        out_specs=[pl.BlockSpec((d, tt), lambda i: (0, i)), head, head, head, head],
        out_shape=[jax.ShapeDtypeStruct((d, t_all), BF16), hsd(F32), hsd(F32), hsd(BF16), hsd(BF16)],
        compiler_params=_params("arbitrary"),
        name="peer_query",
    )(xt, mod, gain.reshape(1, d), wq_hi, wq_lo, k_hi, k_lo)


def _gelu_exact(x):
    return 0.5 * x * (1.0 + lax.erf(x * (2.0 ** -0.5)))


def _peer_main_kernel(x_ref, mod_ref, ht_ref, u_ref, vt_ref, n_ref, c_ref, rank_ref, e2_ref,
                      o_ref, acc_ref, act_ref, coef_ref):
    g = pl.program_id(1)
    last = pl.num_programs(1) - 1
    tt = ht_ref.shape[1]

    @pl.when(g == 0)
    def _():
        acc_ref[...] = jnp.zeros_like(acc_ref)
        act_ref[...] = jnp.zeros_like(act_ref)
        coef_ref[...] = jnp.zeros_like(coef_ref)

    acc_ref[...] += _mm(vt_ref[...], coef_ref[...] * act_ref[...])

    blk = jnp.minimum(g, last - 1)
    rows_per_step = u_ref.shape[0] // P_KEYS
    tiles = P_KEYS // BF16_ROWS
    zero = jnp.zeros((BF16_ROWS, tt), BF16)
    for ii in range(rows_per_step):
        row = blk * rows_per_step + ii
        coef = [None] * tiles
        for h in range(P_HEADS):
            nb = jnp.broadcast_to(n_ref[h, pl.ds(row, 1), :], (BF16_ROWS, tt)).astype(BF16)
            cb = jnp.broadcast_to(c_ref[h, pl.ds(row, 1), :], (BF16_ROWS, tt)).astype(BF16)
            for k in range(tiles):
                rows = slice(k * BF16_ROWS, (k + 1) * BF16_ROWS)
                term = jnp.where(rank_ref[h, rows, :] < nb, e2_ref[h, rows, :] * cb, zero)
                coef[k] = term if coef[k] is None else coef[k] + term
        for k in range(tiles):
            rows = slice(ii * P_KEYS + k * BF16_ROWS, ii * P_KEYS + (k + 1) * BF16_ROWS)
            coef_ref[rows, :] = coef[k]
    act_ref[...] = _gelu_exact(_mm(u_ref[...], ht_ref[...])).astype(BF16)

    @pl.when(g == last)
    def _():
        o_ref[...] = x_ref[...] + mod_ref[0][5:6] * acc_ref[...].T


def _peer_main(xt, mod, ht, u_tab, vt_tab, n, c, rank, e2, blocks_per_mod, tt):
    t_all, d = xt.shape
    n_exp = u_tab.shape[0]
    eb = PEER_EXP
    nsteps = n_exp // eb
    head = pl.BlockSpec((P_HEADS, P_KEYS, tt), lambda i, s: (0, 0, i))
    return pl.pallas_call(
        _peer_main_kernel,
        grid=(t_all // tt, nsteps + 1),
        in_specs=[pl.BlockSpec((tt, d), lambda i, s: (i, 0)),
                  pl.BlockSpec((1, N_MOD, d), lambda i, s: (i // blocks_per_mod, 0, 0)),
                  pl.BlockSpec((d, tt), lambda i, s: (0, i)),
                  pl.BlockSpec((eb, d), lambda i, s: (jnp.minimum(s, nsteps - 1), 0)),
                  pl.BlockSpec((d, eb), lambda i, s: (0, jnp.maximum(s - 1, 0))),
                  head, head, head, head],
        out_specs=pl.BlockSpec((tt, d), lambda i, s: (i, 0)),
        out_shape=jax.ShapeDtypeStruct((t_all, d), F32),
        scratch_shapes=[pltpu.VMEM((d, tt), F32), pltpu.VMEM((eb, tt), BF16), pltpu.VMEM((eb, tt), BF16)],
        compiler_params=_params("arbitrary", "arbitrary"),
        name="peer_main",
    )(xt, mod, ht, u_tab, vt_tab, n, c, rank, e2)


def _peer_layer(xt, mod, gain, w_q, keys, u_tab, v_tab, blocks_per_mod, tt):
    wq_t = w_q.T
    tq = PEER_QUERY_TOK
    ht, n, c, rank, e2 = _peer_query(xt, mod, gain, wq_t, keys, blocks_per_mod * (tt // tq), tq)
    return _peer_main(xt, mod, ht, u_tab.astype(BF16), v_tab.T.astype(BF16), n, c, rank, e2,
                      blocks_per_mod, tt)


def kernel(x, c, ctx, c_ctx, mod_w, mod_b, norm1_g, norm2_g, ev_w_in, ev_w_out, attn_q_gain, attn_k_gain,
           rw_mu, rw_w0, rw_w2, rw_a0, rw_a2, rw_g2, rw_k_k, rw_k_a, rw_r_k, rw_gn_w, rw_gn_b, ssd_w_in,
           ssd_conv_w, ssd_conv_b, ssd_dt_bias, ssd_a_log, ssd_d, ssd_norm_w, ssd_w_out, peer_w_q,
           peer_keys, peer_u, peer_v):
    bsz, seq, dm = x.shape
    lc = ctx.shape[1]
    depth = mod_w.shape[0]
    assert lc == TOK_TILE and seq % PEER_TOK == 0 and (bsz * lc) % PEER_TOK == 0

    rows = -(-(bsz + 1) // 8) * 8
    cc = jnp.concatenate([c, c_ctx[None], jnp.zeros((rows - bsz - 1, dm), F32)], axis=0)
    mods = _modulation(cc, mod_w, mod_b)
    xc, xl = ctx, x

    for i in range(depth):
        last = i == depth - 1
        j = i // 2
        ml = mods[i, :bsz].reshape(bsz, N_MOD, dm)
        mc = mods[i, bsz].reshape(1, N_MOD, dm)
        modx = jnp.stack([jnp.broadcast_to(mc, (bsz, N_MOD, dm)), ml], axis=1)
        if i % 2 == 0:
            w_in = ev_w_in[j].astype(BF16)
            za = _in_proj(xc, xl, modx, norm1_g[i], w_in[:, :A_IN])
            zr = _in_proj(xc, xl, modx, norm1_g[i], w_in[:, A_IN:])
            q, k, v = _attn_pre(za, attn_q_gain[j], attn_k_gain[j], lc)
            attn = _attention(q, k, v, lc)
            r, vv, kkn, g, bonus, lw, a, kd = _rwkv_pre(
                zr, rw_mu[j], rw_w0[j], rw_w2[j], rw_a0[j], rw_a2[j], rw_g2[j], rw_k_k[j], rw_k_a[j],
                rw_r_k[j], lc)
            wkv = _rwkv_chunks(r, vv, kkn, lw, a, kd, lc)
            xc, xl = _even_out(xc, xl, modx, attn, wkv, bonus, g, rw_gn_w[j], rw_gn_b[j],
                               ev_w_out[j].astype(BF16))
        else:
            n_heads = ssd_a_log.shape[-1]
            inner = n_heads * C_HEAD_DIM
            conv_dim = inner + 2 * C_GROUPS * C_STATE
            w_in = ssd_w_in[j]
            w_dt = jnp.zeros((dm, 2 * LANES), F32)
            w_dt = w_dt.at[:, :n_heads].set(w_in[:, inner + conv_dim:inner + conv_dim + n_heads])
            w_dt = w_dt.at[:, LANES:LANES + n_heads].set(w_in[:, inner + conv_dim + n_heads:])
            dtb = jnp.zeros((1, 2 * LANES), F32)
            dtb = dtb.at[0, :n_heads].set(ssd_dt_bias[j, 0]).at[0, LANES:LANES + n_heads].set(ssd_dt_bias[j, 1])
            zg = _in_proj(xc, xl, modx, norm1_g[i], w_in[:, :inner].astype(BF16))
            xbc = _in_proj(xc, xl, modx, norm1_g[i], w_in[:, inner:inner + conv_dim].astype(BF16))
            dtraw = _in_proj(xc, xl, modx, norm1_g[i], w_dt.astype(BF16))
            xact, dtact = _ssd_pre(xbc, dtraw, ssd_conv_w[j], ssd_conv_b[j], dtb, lc)
            ys = _ssd_chunks(xact, dtact, ssd_a_log[j], lc)
            assert last, "the SSD layer keeps only the latent stream"
            xl = _odd_out(xl, modx, ys, xact, zg, ssd_d[j], ssd_norm_w[j], ssd_w_out[j].astype(BF16), lc)

        tt = PEER_TOK
        xl = _peer_layer(xl.reshape(bsz * seq, dm), ml, norm2_g[i], peer_w_q[i], peer_keys[i],
                         peer_u[i], peer_v[i], seq // tt, tt).reshape(bsz, seq, dm)
        if not last:
            xc = _peer_layer(xc.reshape(bsz * lc, dm), mc, norm2_g[i], peer_w_q[i], peer_keys[i],
                             peer_u[i], peer_v[i], (bsz * lc) // tt, tt).reshape(bsz, lc, dm)
    return xl
```

```python
import functools
import math

import numpy as np
import jax
import jax.numpy as jnp
from jax import lax
from jax.experimental import pallas as pl
from jax.experimental.pallas import tpu as pltpu

F32 = jnp.float32
BF16 = jnp.bfloat16
HI = lax.Precision.HIGHEST

GRID_W = 64
EPS = 1e-6
N_MOD = 6
A_HEADS, A_KV_HEADS, A_HEAD_DIM = 8, 2, 64
A_Q = A_HEADS * A_HEAD_DIM
A_KV = A_KV_HEADS * A_HEAD_DIM
A_IN = A_Q + 2 * A_KV
ROPE_THETA = 10000.0
B_HEADS, B_HEAD_DIM = 8, 64
B_WIDTH = B_HEADS * B_HEAD_DIM
B_DECAY_LORA, B_AAA_LORA, B_GATE_LORA = 64, 64, 128
B_IN = 3 * B_WIDTH + B_DECAY_LORA + B_AAA_LORA + B_GATE_LORA
B_GN_EPS = 64e-5
C_HEAD_DIM, C_GROUPS, C_STATE, C_CHUNK = 64, 4, 128, 128
P_HEADS, P_KEYS, P_KEY_DIM, P_TOPK = 8, 128, 128, 16

LANES = 128
BF16_ROWS = 16
TOK_TILE = 256
R_CHUNK = 64
PEER_TOK = 512
PEER_QUERY_TOK = 256
PEER_EXP = 1024
VMEM_LIMIT = 56 * 1024 * 1024


def _mm(a, b, prec=None):
    return lax.dot_general(a, b, (((1,), (0,)), ((), ())), precision=prec,
                           preferred_element_type=F32)


def _mm_nt(a, b, prec=None):
    return lax.dot_general(a, b, (((1,), (1,)), ((), ())), precision=prec,
                           preferred_element_type=F32)


def _mm_tn(a, b, prec=None):
    return lax.dot_general(a, b, (((0,), (0,)), ((), ())), precision=prec,
                           preferred_element_type=F32)


def _params(*sem):
    return pltpu.CompilerParams(dimension_semantics=sem, vmem_limit_bytes=VMEM_LIMIT)


def _softplus(x):
    return jnp.maximum(x, 0.0) + jnp.log1p(jnp.exp(-jnp.abs(x)))


def _sigmoid(x):
    return 1.0 / (1.0 + jnp.exp(-x))


def _silu(x):
    return x * _sigmoid(x)


def _mod_kernel(c_ref, w_ref, b_ref, o_ref):
    o_ref[0] = _mm(_silu(c_ref[...]), w_ref[0], HI) + b_ref[0]


def _modulation(cc, mod_w, mod_b):
    depth, d, n6 = mod_w.shape
    rows = cc.shape[0]
    tn = 1024
    return pl.pallas_call(
        _mod_kernel,
        grid=(depth, n6 // tn),
        in_specs=[pl.BlockSpec((rows, d), lambda l, n: (0, 0)),
                  pl.BlockSpec((1, d, tn), lambda l, n: (l, 0, n)),
                  pl.BlockSpec((1, 1, tn), lambda l, n: (l, 0, n))],
        out_specs=pl.BlockSpec((1, rows, tn), lambda l, n: (l, 0, n)),
        out_shape=jax.ShapeDtypeStruct((depth, rows, n6), F32),
        compiler_params=_params("arbitrary", "arbitrary"),
        name="modulation",
    )(cc, mod_w, mod_b.reshape(depth, 1, n6))


def _norm_mod(x, gain, shift, scale):
    ms = jnp.mean(x * x, axis=-1, keepdims=True)
    return x * lax.rsqrt(ms + EPS) * gain * (1.0 + scale) + shift


def _stream_specs(tm, d, lct):
    return [pl.BlockSpec((1, tm, d), lambda b, j: (b, jnp.minimum(j, lct - 1), 0)),
            pl.BlockSpec((1, tm, d), lambda b, j: (b, jnp.maximum(j - lct, 0), 0))]


def _stream_tile(xc_ref, xl_ref, lct):
    return jnp.where(pl.program_id(1) < lct, xc_ref[0], xl_ref[0])


def _in_proj_kernel(xc_ref, xl_ref, mod_ref, g_ref, w_ref, o_ref, *, lct):
    mod = mod_ref[0, 0]
    h = _norm_mod(_stream_tile(xc_ref, xl_ref, lct), g_ref[...], mod[0:1], mod[1:2]).astype(BF16)
    n = o_ref.shape[-1]
    for c0 in range(0, n, 1024):
        c1 = min(n, c0 + 1024)
        o_ref[0, :, c0:c1] = _mm(h, w_ref[:, c0:c1]).astype(o_ref.dtype)


def _in_proj(xc, xl, modx, gain, w, out_dtype=F32):
    bsz, lc, d = xc.shape
    ln = lc + xl.shape[1]
    n = w.shape[1]
    tm = TOK_TILE
    lct = lc // tm
    return pl.pallas_call(
        functools.partial(_in_proj_kernel, lct=lct),
        grid=(bsz, ln // tm),
        in_specs=_stream_specs(tm, d, lct) + [
            pl.BlockSpec((1, 1, N_MOD, d), lambda b, j: (b, jnp.where(j < lct, 0, 1), 0, 0)),
            pl.BlockSpec((1, d), lambda b, j: (0, 0)),
            pl.BlockSpec((d, n), lambda b, j: (0, 0))],
        out_specs=pl.BlockSpec((1, tm, n), lambda b, j: (b, j, 0)),
        out_shape=jax.ShapeDtypeStruct((bsz, ln, n), out_dtype),
        compiler_params=_params("arbitrary", "arbitrary"),
        name="in_proj",
    )(xc, xl, modx, gain.reshape(1, d), w)


def _rope_tables(lc, seq):
    t = jnp.arange(seq)
    row = (t // GRID_W).astype(F32)
    col = (t % GRID_W).astype(F32)
    m = A_HEAD_DIM // 4
    inv = ROPE_THETA ** (-jnp.arange(m, dtype=F32) / m)
    ar, ac = row[:, None] * inv, col[:, None] * inv
    cos = jnp.concatenate([jnp.cos(ar), jnp.cos(ar), jnp.cos(ac), jnp.cos(ac)], axis=-1)
    sin = jnp.concatenate([-jnp.sin(ar), jnp.sin(ar), -jnp.sin(ac), jnp.sin(ac)], axis=-1)
    reps = LANES // A_HEAD_DIM
    cos = jnp.concatenate([jnp.ones((lc, A_HEAD_DIM), F32), cos], axis=0)
    sin = jnp.concatenate([jnp.zeros((lc, A_HEAD_DIM), F32), sin], axis=0)
    return jnp.tile(cos, (1, reps)), jnp.tile(sin, (1, reps))


def _head_block_ones(width, scale):
    idx = np.arange(LANES) // width
    return jnp.asarray((idx[:, None] == idx[None, :]).astype(np.float32) * scale)


def _attn_pre_kernel(z_ref, qg_ref, kg_ref, cos_ref, sin_ref, bones_ref, q_ref, k_ref, v_ref):
    tm = z_ref.shape[1]
    cs, sn = cos_ref[...], sin_ref[...]
    bones = bones_ref[...]
    lane = lax.broadcasted_iota(jnp.int32, (tm, LANES), 1)
    first = (lane % 32) < 16
    low = lane < A_HEAD_DIM

    def norm_rope(x, gain):
        ms = _bmm(x * x, bones)
        y = x * lax.rsqrt(ms + EPS) * gain
        partner = jnp.where(first, pltpu.roll(y, LANES - 16, 1), pltpu.roll(y, 16, 1))
        return y * cs + partner * sn

    heads_per_slab = LANES // A_HEAD_DIM
    group = A_HEADS // A_KV_HEADS
    for m in range(A_Q // LANES):
        sl = slice(m * LANES, (m + 1) * LANES)
        y = norm_rope(z_ref[0, :, sl], qg_ref[:, sl]) * (A_HEAD_DIM ** -0.5)
        swapped = pltpu.roll(y, A_HEAD_DIM, 1)
        for hh in range(heads_per_slab):
            h = m * heads_per_slab + hh
            kvh = h // group
            src = y if kvh == hh else swapped
            keep = low if kvh == 0 else jnp.logical_not(low)
            q_ref[0, h] = jnp.where(keep, src, 0.0).astype(BF16)
    k_ref[0] = norm_rope(z_ref[0, :, A_Q:A_Q + A_KV], kg_ref[...]).astype(BF16)
    v_ref[0] = z_ref[0, :, A_Q + A_KV:A_IN].astype(BF16)


def _attn_pre(za, q_gain, k_gain, lc):
    bsz, ln, _ = za.shape
    tm = TOK_TILE
    cos, sin = _rope_tables(lc, ln - lc)
    qg = jnp.tile(q_gain, A_HEADS).reshape(1, A_Q)
    kg = jnp.tile(k_gain, A_KV_HEADS).reshape(1, A_KV)
    bones = _head_block_ones(A_HEAD_DIM, 1.0 / A_HEAD_DIM)
    return pl.pallas_call(
        _attn_pre_kernel,
        grid=(bsz, ln // tm),
        in_specs=[pl.BlockSpec((1, tm, A_IN), lambda b, j: (b, j, 0)),
                  pl.BlockSpec((1, A_Q), lambda b, j: (0, 0)),
                  pl.BlockSpec((1, A_KV), lambda b, j: (0, 0)),
                  pl.BlockSpec((tm, LANES), lambda b, j: (j, 0)),
                  pl.BlockSpec((tm, LANES), lambda b, j: (j, 0)),
                  pl.BlockSpec((LANES, LANES), lambda b, j: (0, 0))],
        out_specs=[pl.BlockSpec((1, A_HEADS, tm, LANES), lambda b, j: (b, 0, j, 0)),
                   pl.BlockSpec((1, tm, A_KV), lambda b, j: (b, j, 0)),
                   pl.BlockSpec((1, tm, A_KV), lambda b, j: (b, j, 0))],
        out_shape=[jax.ShapeDtypeStruct((bsz, A_HEADS, ln, LANES), BF16),
                   jax.ShapeDtypeStruct((bsz, ln, A_KV), BF16),
                   jax.ShapeDtypeStruct((bsz, ln, A_KV), BF16)],
        compiler_params=_params("arbitrary", "arbitrary"),
        name="attn_pre",
    )(za, qg, kg, cos, sin, bones)


def _attn_kernel(q_ref, k_ref, v_ref, o_ref, *, lc):
    tq = q_ref.shape[2]
    j = pl.program_id(1)
    lane = lax.broadcasted_iota(jnp.int32, (tq, LANES), 1)
    low = lane < A_HEAD_DIM
    heads_per_slab = LANES // A_HEAD_DIM
    group = A_HEADS // A_KV_HEADS

    def run(klen):
        k = k_ref[0, :klen, :]
        v = v_ref[0, :klen, :]
        for m in range(A_Q // LANES):
            parts = []
            for hh in range(heads_per_slab):
                h = m * heads_per_slab + hh
                kvh = h // group
                s = _mm_nt(q_ref[0, h], k)
                p = jnp.exp(s - jnp.max(s, axis=-1, keepdims=True))
                l = jnp.sum(p, axis=-1, keepdims=True)
                o = _mm(p.astype(BF16), v) / l
                parts.append(o if kvh == hh else pltpu.roll(o, A_HEAD_DIM, 1))
            o_ref[0, :, m * LANES:(m + 1) * LANES] = jnp.where(low, parts[0], parts[1])

    @pl.when(j * tq < lc)
    def _():
        run(lc)

    @pl.when(j * tq >= lc)
    def _():
        run(k_ref.shape[1])


def _attention(q, k, v, lc):
    bsz, _, ln, _ = q.shape
    tq = TOK_TILE
    return pl.pallas_call(
        functools.partial(_attn_kernel, lc=lc),
        grid=(bsz, ln // tq),
        in_specs=[pl.BlockSpec((1, A_HEADS, tq, LANES), lambda b, j: (b, 0, j, 0)),
                  pl.BlockSpec((1, ln, A_KV), lambda b, j: (b, 0, 0)),
                  pl.BlockSpec((1, ln, A_KV), lambda b, j: (b, 0, 0))],
        out_specs=pl.BlockSpec((1, tq, A_Q), lambda b, j: (b, j, 0)),
        out_shape=jax.ShapeDtypeStruct((bsz, ln, A_Q), F32),
        compiler_params=_params("arbitrary", "arbitrary"),
        name="attention",
    )(q, k, v)


def _shifted_rows(z, prev_row, next_row):
    tm = z.shape[0]
    row = lax.broadcasted_iota(jnp.int32, (tm, 1), 0)
    zprev = jnp.where(row == 0, prev_row, pltpu.roll(z, 1, 0))
    znext = jnp.where(row == tm - 1, next_row, pltpu.roll(z, tm - 1, 0))
    return zprev, znext


def _halo_rows(zp_ref, zn_ref, j, lct, nj):
    prev_ok = jnp.logical_and(j != 0, j != lct)
    next_ok = jnp.logical_and(j != lct - 1, j != nj - 1)
    prev_row = jnp.where(prev_ok, zp_ref[0, 7:8, :], 0.0)
    next_row = jnp.where(next_ok, zn_ref[0, 0:1, :], 0.0)
    return prev_row, next_row


def _halo_specs(tm, width, ln):
    nb8 = tm // 8
    last = ln // 8 - 1
    return [pl.BlockSpec((1, tm, width), lambda b, j: (b, j, 0)),
            pl.BlockSpec((1, 8, width), lambda b, j: (b, jnp.maximum(j * nb8 - 1, 0), 0)),
            pl.BlockSpec((1, 8, width), lambda b, j: (b, jnp.minimum((j + 1) * nb8, last), 0))]


def _per_head_sum(x, bones):
    return jnp.concatenate(
        [_bmm(x[:, m * LANES:(m + 1) * LANES], bones) for m in range(x.shape[1] // LANES)], axis=1)


def _rwkv_pre_kernel(z_ref, zp_ref, zn_ref, mu_ref, w0_ref, w2_ref, a0_ref, a2_ref, g2_ref,
                     kk_ref, ka_ref, rk_ref, bones_ref,
                     r_o, v_o, kkn_o, g_o, bonus_o, lw_o, a_o, kd_o, *, lct):
    j = pl.program_id(1)
    z = z_ref[0]
    prev_row, next_row = _halo_rows(zp_ref, zn_ref, j, lct, pl.num_programs(1))
    zprev, znext = _shifted_rows(z, prev_row, next_row)
    zm = z + mu_ref[...] * (0.5 * (zprev + znext) - z)
    w = B_WIDTH
    r, k, v = zm[:, 0:w], zm[:, w:2 * w], zm[:, 2 * w:3 * w]
    wa = zm[:, 3 * w:3 * w + B_DECAY_LORA + B_AAA_LORA]
    gl = zm[:, 3 * w + B_DECAY_LORA + B_AAA_LORA:]
    bones = bones_ref[...]
    kk = k * kk_ref[...]
    kkn = kk / jnp.maximum(jnp.sqrt(_per_head_sum(kk * kk, bones)), 1e-12)
    r_o[0] = r
    v_o[0] = v
    kkn_o[0] = kkn
    g_o[0] = _bmm(_sigmoid(gl), g2_ref[...])
    twa = jnp.tanh(wa)
    bonus = jnp.zeros_like(r)
    for d in range(2):
        wraw = w0_ref[d] + _bmm(twa, w2_ref[d])
        lw_o[d, 0] = -jnp.exp(-_softplus(-wraw) - 0.5)
        a = _sigmoid(a0_ref[d] + _bmm(wa, a2_ref[d]))
        kd = k * (1.0 + (a - 1.0) * ka_ref[...])
        a_o[d, 0] = a
        kd_o[d, 0] = kd
        bonus = bonus + _per_head_sum(r * kd * rk_ref[...], bones) * v
    bonus_o[0] = bonus


def _rwkv_pre(zr, mu, w0, w2, a0, a2, g2, k_k, k_a, r_k, lc):
    bsz, ln, width = zr.shape
    tm = TOK_TILE
    w = B_WIDTH
    zeros = jnp.zeros((2, B_DECAY_LORA, w), F32)
    w2p = jnp.concatenate([w2, zeros], axis=1)
    a2p = jnp.concatenate([zeros, a2], axis=1)
    bones = _head_block_ones(B_HEAD_DIM, 1.0)
    full = lambda *shape: pl.BlockSpec(shape, lambda b, j: (0,) * len(shape))
    tok = pl.BlockSpec((1, tm, w), lambda b, j: (b, j, 0))
    tok2 = pl.BlockSpec((2, 1, tm, w), lambda b, j: (0, b, j, 0))
    sd = jax.ShapeDtypeStruct((bsz, ln, w), F32)
    sd2 = jax.ShapeDtypeStruct((2, bsz, ln, w), F32)
    return pl.pallas_call(
        functools.partial(_rwkv_pre_kernel, lct=lc // tm),
        grid=(bsz, ln // tm),
        in_specs=_halo_specs(tm, width, ln) + [
            full(1, width), full(2, 1, w), full(2, 2 * B_DECAY_LORA, w), full(2, 1, w),
            full(2, 2 * B_AAA_LORA, w), full(B_GATE_LORA, w), full(1, w), full(1, w), full(1, w),
            full(LANES, LANES)],
        out_specs=[tok, tok, tok, tok, tok, tok2, tok2, tok2],
        out_shape=[sd, sd, sd, sd, sd, sd2, sd2, sd2],
        compiler_params=_params("arbitrary", "arbitrary"),
        name="rwkv_pre",
    )(zr, zr, zr, mu.reshape(1, width), w0.reshape(2, 1, w), w2p, a0.reshape(2, 1, w), a2p, g2,
      k_k.reshape(1, w), k_a.reshape(1, w), r_k.reshape(1, w), bones)


def _scan_chunk(i, d, nctx, ntot):
    rev = jnp.where(i < nctx, nctx - 1 - i, ntot + nctx - 1 - i)
    return jnp.where(d == 0, i, rev)


def _order_masks(n, pair):
    t = np.arange(n * pair)
    same = (t[:, None] // n) == (t[None, :] // n)
    tt, ss = t[:, None] % n, t[None, :] % n
    incl = np.stack([same & (ss <= tt), same & (ss >= tt)]).astype(np.float32)
    strict = np.stack([same & (ss < tt), same & (ss > tt)]).astype(np.float32)
    return jnp.asarray(incl), jnp.asarray(strict)


def _pair_stack(x, low):
    return jnp.concatenate([jnp.where(low, x, 0.0), jnp.where(low, 0.0, x)], axis=0)


def _bmm(a, b):
    return _mm(a.astype(BF16), b.astype(BF16))


def _rwkv_chunk_kernel(rf_ref, vf_ref, kf_ref, lwf_ref, af_ref, kdf_ref,
                       rb_ref, vb_ref, kb_ref, lwb_ref, ab_ref, kdb_ref,
                       tri_ref, incl_ref, strict_ref, of_ref, ob_ref, state_ref):
    c = R_CHUNK
    n2 = 2 * c

    @pl.when(pl.program_id(1) == 0)
    def _():
        state_ref[...] = jnp.zeros_like(state_ref)

    low = lax.broadcasted_iota(jnp.int32, (c, LANES), 1) < B_HEAD_DIM
    eye = (lax.broadcasted_iota(jnp.int32, (LANES, LANES), 0)
           == lax.broadcasted_iota(jnp.int32, (LANES, LANES), 1)).astype(F32)
    stack = lambda x: _pair_stack(x, low)
    refs = ((rf_ref, vf_ref, kf_ref, lwf_ref, af_ref, kdf_ref, of_ref),
            (rb_ref, vb_ref, kb_ref, lwb_ref, ab_ref, kdb_ref, ob_ref))
    dirs, slabs, outs = [], [], []
    kts, bts, kds, rts, vs, pcs = [], [], [], [], [], []
    for d, (r_ref, v_ref, kkn_ref, lw_ref, a_ref, kd_ref, o_ref) in enumerate(refs):
        lw = lw_ref[0, 0]
        cum = _mm(tri_ref[d], lw, HI)
        p_in = jnp.exp(cum)
        p_inv = jnp.exp(-cum)
        p_ex = jnp.exp(cum - lw)
        p_all = jnp.exp(jnp.sum(lw, axis=0, keepdims=True))
        kkn = kkn_ref[0]
        kt = kkn * p_ex
        bt = kkn * a_ref[0, 0] * p_inv
        kd = kd_ref[0, 0] * p_inv
        rt = r_ref[0] * p_in
        v = v_ref[0]
        for m in range(B_WIDTH // LANES):
            sl = slice(m * LANES, (m + 1) * LANES)
            dirs.append(d)
            slabs.append(m)
            outs.append(o_ref)
            kts.append(stack(kt[:, sl]).astype(BF16))
            bts.append(stack(bt[:, sl]).astype(BF16))
            kds.append(stack(kd[:, sl]).astype(BF16))
            rts.append(stack(rt[:, sl]))
            vs.append(stack(v[:, sl]).astype(BF16))
            pcs.append(p_all[:, sl])
    n = len(dirs)
    incl = [incl_ref[d] > 0.5 for d in range(2)]
    strict = [strict_ref[d] > 0.5 for d in range(2)]
    gram = [_mm_nt(jnp.concatenate([kts[x], rts[x].astype(BF16)], axis=0),
                   jnp.concatenate([bts[x], kds[x]], axis=0)) for x in range(n)]
    a_b = [jnp.where(strict[dirs[x]], gram[x][:n2, :n2], 0.0) for x in range(n)]
    a_k = [jnp.where(strict[dirs[x]], gram[x][:n2, n2:], 0.0) for x in range(n)]
    m_b = [jnp.where(incl[dirs[x]], gram[x][n2:, :n2], 0.0) for x in range(n)]
    m_k = [jnp.where(incl[dirs[x]], gram[x][n2:, n2:], 0.0) for x in range(n)]
    pw = [-a for a in a_b]
    tinv = [eye + p for p in pw]
    for _ in range(int(math.log2(c)) - 1):
        pw = [_bmm(p, p) for p in pw]
        tinv = [t + _bmm(t, p) for t, p in zip(tinv, pw)]
    mv = [_bmm(jnp.concatenate([a_k[x], m_k[x]], axis=0), vs[x]) for x in range(n)]
    wy = [_bmm(tinv[x], jnp.concatenate([kts[x], mv[x][:n2].astype(BF16)], axis=1)) for x in range(n)]
    mbwy = [_bmm(m_b[x], wy[x]) for x in range(n)]
    wyb = [_mm_tn(wy[x].astype(BF16), bts[x]) for x in range(n)]
    vtk = [_mm_tn(vs[x], kds[x]) for x in range(n)]
    s0 = [state_ref[dirs[x], slabs[x]] for x in range(n)]
    o_st = [_mm_nt((rts[x] - mbwy[x][:, :LANES]).astype(BF16), s0[x].astype(BF16))
            + (mv[x][n2:] - mbwy[x][:, LANES:]) for x in range(n)]
    for x in range(n):
        g = (eye - wyb[x][:LANES]) * pcs[x]
        h = (vtk[x] - wyb[x][LANES:]) * pcs[x]
        outs[x][0, :, slabs[x] * LANES:(slabs[x] + 1) * LANES] = o_st[x][:c] + o_st[x][c:]
        state_ref[dirs[x], slabs[x]] = _bmm(s0[x], g) + h


def _rwkv_chunks(r, v, kkn, lw, a, kd, lc):
    bsz, ln, w = r.shape
    c = R_CHUNK
    nctx, ntot = lc // c, ln // c
    tri, _ = _order_masks(c, 1)
    incl, strict = _order_masks(c, 2)
    specs = []
    for d in range(2):
        tok = pl.BlockSpec((1, c, w), lambda b, i, d=d: (b, _scan_chunk(i, d, nctx, ntot), 0))
        tok2 = pl.BlockSpec((1, 1, c, w), lambda b, i, d=d: (d, b, _scan_chunk(i, d, nctx, ntot), 0))
        specs.append((tok, tok2))
    full = lambda *shape: pl.BlockSpec(shape, lambda b, i: (0,) * len(shape))
    sd = jax.ShapeDtypeStruct((bsz, ln, w), F32)
    return pl.pallas_call(
        _rwkv_chunk_kernel,
        grid=(bsz, ntot),
        in_specs=[specs[0][0]] * 3 + [specs[0][1]] * 3 + [specs[1][0]] * 3 + [specs[1][1]] * 3
        + [full(2, c, c), full(2, 2 * c, 2 * c), full(2, 2 * c, 2 * c)],
        out_specs=[specs[0][0], specs[1][0]],
        out_shape=[sd, sd],
        scratch_shapes=[pltpu.VMEM((2, w // LANES, LANES, LANES), F32)],
        compiler_params=_params("arbitrary", "arbitrary"),
        name="rwkv_chunks",
    )(r, v, kkn, lw, a, kd, r, v, kkn, lw, a, kd, tri, incl, strict)


def _even_out_kernel(xc_ref, xl_ref, mod_ref, attn_ref, wkvf_ref, wkvb_ref, bonus_ref, g_ref, gw_ref,
                     gb_ref, bones_ref, w_ref, oc_ref, ol_ref, *, lct):
    bones = bones_ref[...]
    y = bonus_ref[0]
    for wkv_ref in (wkvf_ref, wkvb_ref):
        o = wkv_ref[0]
        dlt = o - _per_head_sum(o, bones)
        var = _per_head_sum(dlt * dlt, bones)
        y = y + dlt * lax.rsqrt(var + B_GN_EPS) * gw_ref[...] + gb_ref[...]
    y = y * g_ref[0]
    mix = _mm(attn_ref[0].astype(BF16), w_ref[:A_Q, :]) + _mm(y.astype(BF16), w_ref[A_Q:, :])
    res = _stream_tile(xc_ref, xl_ref, lct) + mod_ref[0, 0][2:3] * mix
    is_ctx = pl.program_id(1) < lct

    @pl.when(is_ctx)
    def _():
        oc_ref[0] = res

    @pl.when(jnp.logical_not(is_ctx))
    def _():
        ol_ref[0] = res


def _even_out(xc, xl, modx, attn, wkv, bonus, g, gn_w, gn_b, w_out):
    bsz, lc, d = xc.shape
    seq = xl.shape[1]
    tm = TOK_TILE
    lct = lc // tm
    w = B_WIDTH
    bones = _head_block_ones(B_HEAD_DIM, 1.0 / B_HEAD_DIM)
    tokw = lambda n: pl.BlockSpec((1, tm, n), lambda b, j: (b, j, 0))
    full = lambda *shape: pl.BlockSpec(shape, lambda b, j: (0,) * len(shape))
    streams = _stream_specs(tm, d, lct)
    return pl.pallas_call(
        functools.partial(_even_out_kernel, lct=lct),
        grid=(bsz, (lc + seq) // tm),
        in_specs=streams + [
            pl.BlockSpec((1, 1, N_MOD, d), lambda b, j: (b, jnp.where(j < lct, 0, 1), 0, 0)),
            tokw(A_Q), tokw(w), tokw(w),
            tokw(w), tokw(w), full(1, w), full(1, w), full(LANES, LANES),
            full(A_Q + w, d)],
        out_specs=streams,
        out_shape=[jax.ShapeDtypeStruct((bsz, lc, d), F32), jax.ShapeDtypeStruct((bsz, seq, d), F32)],
        compiler_params=_params("arbitrary", "arbitrary"),
        name="even_out",
    )(xc, xl, modx, attn, wkv[0], wkv[1], bonus, g, gn_w.reshape(1, w), gn_b.reshape(1, w), bones, w_out)


def _ssd_pre_kernel(z_ref, zp_ref, zn_ref, dt_ref, cw_ref, cb_ref, dtb_ref, x_o, dt_o, *, lct):
    j = pl.program_id(1)
    z = z_ref[0]
    prev_row, next_row = _halo_rows(zp_ref, zn_ref, j, lct, pl.num_programs(1))
    zprev, znext = _shifted_rows(z, prev_row, next_row)
    y = zprev * cw_ref[0:1, :] + z * cw_ref[1:2, :] + znext * cw_ref[2:3, :] + cb_ref[...]
    x_o[0] = _silu(y).astype(x_o.dtype)
    dt_o[0] = _softplus(dt_ref[0] + dtb_ref[...])


def _ssd_pre(xbc, dtraw, conv_w, conv_b, dtb, lc):
    bsz, ln, width = xbc.shape
    tm = TOK_TILE
    nd = dtraw.shape[-1]
    full = lambda *shape: pl.BlockSpec(shape, lambda b, j: (0,) * len(shape))
    return pl.pallas_call(
        functools.partial(_ssd_pre_kernel, lct=lc // tm),
        grid=(bsz, ln // tm),
        in_specs=_halo_specs(tm, width, ln) + [
            pl.BlockSpec((1, tm, nd), lambda b, j: (b, j, 0)),
            full(conv_w.shape[0], width), full(1, width), full(1, nd)],
        out_specs=[pl.BlockSpec((1, tm, width), lambda b, j: (b, j, 0)),
                   pl.BlockSpec((1, tm, nd), lambda b, j: (b, j, 0))],
        out_shape=[jax.ShapeDtypeStruct((bsz, ln, width), BF16),
                   jax.ShapeDtypeStruct((bsz, ln, nd), F32)],
        compiler_params=_params("arbitrary", "arbitrary"),
        name="ssd_pre",
    )(xbc, xbc, xbc, dtraw, conv_w, conv_b.reshape(1, width), dtb)


def _ssd_chunk_kernel(x_ref, b_ref, c_ref, dt_ref, alog_ref, incl_ref, expand_ref, o_ref, state_ref,
                      *, n_heads):
    q = C_CHUNK
    i = pl.program_id(2)

    @pl.when(i == 0)
    def _():
        state_ref[...] = jnp.zeros_like(state_ref)

    lane = lax.broadcasted_iota(jnp.int32, (1, LANES), 1)
    a_neg = jnp.where(lane < n_heads, -jnp.exp(alog_ref[0]), 0.0)
    dt = dt_ref[0]
    dta = dt * a_neg
    incl_f = incl_ref[0]
    cum = _mm(incl_f, dta, HI)
    cum_t = cum.T
    total = jnp.sum(dta, axis=0, keepdims=True)
    expand = expand_ref[...]
    dt_x = _bmm(dt, expand)
    ec_x = _bmm(jnp.exp(cum), expand)
    de_x = _bmm(jnp.exp(total - cum), expand)
    cd_x = _bmm(jnp.broadcast_to(jnp.exp(total), (BF16_ROWS, LANES)), expand)[0:1]
    incl = incl_f > 0.5
    low = lax.broadcasted_iota(jnp.int32, (q, LANES), 1) < C_HEAD_DIM
    heads_per_group = n_heads // C_GROUPS
    pairs_per_group = heads_per_group // 2
    gw = heads_per_group * C_HEAD_DIM
    for g in range(C_GROUPS):
        gsl = slice(g * gw, (g + 1) * gw)
        bg = b_ref[0, :, g * C_STATE:(g + 1) * C_STATE].astype(BF16)
        cg = c_ref[0, :, g * C_STATE:(g + 1) * C_STATE].astype(BF16)
        cb = _mm_nt(cg, bg)
        xd = x_ref[0, :, gsl].astype(F32) * dt_x[:, gsl]
        s_prev = state_ref[g]
        y_off = _mm(cg, s_prev.astype(BF16)) * ec_x[:, gsl]
        state_ref[g] = s_prev * cd_x[:, gsl] + _mm_tn(bg, (xd * de_x[:, gsl]).astype(BF16))
        for pp in range(pairs_per_group):
            pr = g * pairs_per_group + pp
            sl = slice(pr * LANES, (pr + 1) * LANES)
            ms = []
            for hh in range(2):
                h = 2 * pr + hh
                seg = cum[:, h:h + 1] - cum_t[h:h + 1, :]
                ms.append((cb * jnp.exp(jnp.where(incl, seg, -1e30))).astype(BF16))
            xd_pair = _pair_stack(xd[:, pp * LANES:(pp + 1) * LANES], low).astype(BF16)
            y_diag = _mm(jnp.concatenate(ms, axis=1), xd_pair)
            o_ref[0, 0, :, sl] = (y_diag + y_off[:, pp * LANES:(pp + 1) * LANES]).astype(o_ref.dtype)


def _ssd_chunks(xact, dtact, a_log, lc):
    bsz, ln, _ = xact.shape
    q = C_CHUNK
    n_heads = a_log.shape[1]
    inner = n_heads * C_HEAD_DIM
    gn = C_GROUPS * C_STATE
    nctx, ntot = lc // q, ln // q
    incl, _ = _order_masks(q, 1)
    expand = np.zeros((LANES, inner), np.float32)
    expand[np.arange(inner) // C_HEAD_DIM, np.arange(inner)] = 1.0
    alog = jnp.zeros((2, 1, LANES), F32).at[:, 0, :n_heads].set(a_log)
    cmap = lambda off: (lambda b, d, i: (b, _scan_chunk(i, d, nctx, ntot), off))
    return pl.pallas_call(
        functools.partial(_ssd_chunk_kernel, n_heads=n_heads),
        grid=(bsz, 2, ntot),
        in_specs=[pl.BlockSpec((1, q, inner), cmap(0)),
                  pl.BlockSpec((1, q, gn), cmap(inner // gn)),
                  pl.BlockSpec((1, q, gn), cmap(inner // gn + 1)),
                  pl.BlockSpec((1, q, LANES), lambda b, d, i: (b, _scan_chunk(i, d, nctx, ntot), d)),
                  pl.BlockSpec((1, 1, LANES), lambda b, d, i: (d, 0, 0)),
                  pl.BlockSpec((1, q, q), lambda b, d, i: (d, 0, 0)),
                  pl.BlockSpec((LANES, inner), lambda b, d, i: (0, 0))],
        out_specs=pl.BlockSpec((1, 1, q, inner), lambda b, d, i: (d, b, _scan_chunk(i, d, nctx, ntot), 0)),
        out_shape=jax.ShapeDtypeStruct((2, bsz, ln, inner), BF16),
        scratch_shapes=[pltpu.VMEM((C_GROUPS, C_STATE, inner // C_GROUPS), F32)],
        compiler_params=_params("arbitrary", "arbitrary", "arbitrary"),
        name="ssd_chunks",
    )(xact, xact, xact, dtact, alog, incl, jnp.asarray(expand, dtype=BF16))


def _odd_out_kernel(x_ref, mod_ref, ys_ref, xs_ref, zg_ref, dsk_ref, nw_ref, w_ref, o_ref):
    y = (ys_ref[0, 0].astype(F32) + ys_ref[1, 0].astype(F32)) + dsk_ref[...] * xs_ref[0].astype(F32)
    yg = y * _silu(zg_ref[0].astype(F32))
    inner = yg.shape[1]
    gw = inner // C_GROUPS
    parts = []
    for g in range(C_GROUPS):
        t = yg[:, g * gw:(g + 1) * gw]
        ms = jnp.mean(t * t, axis=-1, keepdims=True)
        parts.append((t * lax.rsqrt(ms + EPS) * nw_ref[:, g * gw:(g + 1) * gw]).astype(BF16))
    yn = jnp.concatenate(parts, axis=1)
    o_ref[0] = x_ref[0] + mod_ref[0, 0][2:3] * _mm(yn, w_ref[...])


def _odd_out(x, modx, ys, xact, zg, d_skip, norm_w, w_out, lc):
    bsz, seq, d = x.shape
    ln = lc + seq
    tm = TOK_TILE
    lct = lc // tm
    inner = zg.shape[-1]
    dsk = jnp.repeat(d_skip, C_HEAD_DIM).reshape(1, inner)
    tokw = lambda n: pl.BlockSpec((1, tm, n), lambda b, j: (b, j + lct, 0))
    full = lambda *shape: pl.BlockSpec(shape, lambda b, j: (0,) * len(shape))
    return pl.pallas_call(
        _odd_out_kernel,
        grid=(bsz, (ln - lc) // tm),
        in_specs=[pl.BlockSpec((1, tm, d), lambda b, j: (b, j, 0)),
                  pl.BlockSpec((1, 1, N_MOD, d), lambda b, j: (b, 1, 0, 0)),
                  pl.BlockSpec((2, 1, tm, inner), lambda b, j: (0, b, j + lct, 0)),
                  tokw(inner), tokw(inner), full(1, inner), full(1, inner), full(inner, d)],
        out_specs=pl.BlockSpec((1, tm, d), lambda b, j: (b, j, 0)),
        out_shape=jax.ShapeDtypeStruct((bsz, ln - lc, d), F32),
        compiler_params=_params("arbitrary", "arbitrary"),
        name="odd_out",
    )(x, modx, ys, xact, zg, dsk, norm_w.reshape(1, inner), w_out)


def _exchange(xs, i, l, descending=True):
    hi, lo = jnp.maximum(xs[i], xs[l]), jnp.minimum(xs[i], xs[l])
    xs[i], xs[l] = (hi, lo) if descending else (lo, hi)


def _bitonic_merge(xs):
    xs = list(xs)
    j = len(xs) // 2
    while j >= 1:
        for i in range(len(xs)):
            if i ^ j > i:
                _exchange(xs, i, i ^ j)
        j //= 2
    return xs


def _bitonic_sort(xs):
    xs = list(xs)
    k = 2
    while k <= len(xs):
        j = k // 2
        while j >= 1:
            for i in range(len(xs)):
                if i ^ j > i:
                    _exchange(xs, i, i ^ j, descending=(i & k) == 0)
            j //= 2
        k *= 2
    return xs


def _merge_sublanes(xs):
    n = len(xs)
    for shift in (4, 2, 1):
        other = [pltpu.roll(x, shift, 0) for x in xs]
        xs = _bitonic_merge([jnp.maximum(xs[k], other[n - 1 - k]) for k in range(n)])
    return xs


def _top_values(s):
    return _merge_sublanes(_bitonic_sort([s[8 * k:8 * k + 8] for k in range(s.shape[0] // 8)]))


def _top_pair_sums(t1, t2):
    sub = lax.broadcasted_iota(jnp.int32, t2[0].shape, 0)
    lo, hi = t2[0], t2[8]
    for b in range(1, 8):
        lo = jnp.where(sub == b, t2[b], lo)
        hi = jnp.where(sub == b, t2[8 + b], hi)
    xs = [t1[a] + lo for a in range(P_TOPK)]
    carry = t1[0] + hi
    for k in range(P_TOPK):
        xs[k], carry = jnp.maximum(xs[k], carry), jnp.minimum(xs[k], carry)
    return _merge_sublanes(xs)


def _split_bf16(x):
    hi = x.astype(BF16)
    return hi, (x - hi.astype(F32)).astype(BF16)


def _mm_split(a_hi, a_lo, b):
    b_hi, b_lo = _split_bf16(b)
    return _mm(a_hi, b_hi) + (_mm(a_hi, b_lo) + _mm(a_lo, b_hi))


def _prefix_count(test, vals):
    sel = jnp.where
    b3 = test(vals[7])
    b2 = test(sel(b3, vals[11], vals[3]))
    b1 = test(sel(b3, sel(b2, vals[13], vals[9]), sel(b2, vals[5], vals[1])))
    b0 = test(sel(b3, sel(b2, sel(b1, vals[14], vals[12]), sel(b1, vals[10], vals[8])),
                  sel(b2, sel(b1, vals[6], vals[4]), sel(b1, vals[2], vals[0]))))
    n = (sel(b3, 8.0, 0.0) + sel(b2, 4.0, 0.0)) + (sel(b1, 2.0, 0.0) + sel(b0, 1.0, 0.0))
    return sel(test(vals[15]), 16.0, n)


def _peer_query_kernel(x_ref, mod_ref, g_ref, wqh_ref, wql_ref, kh_ref, kl_ref,
                       ht_o, n_o, c_o, rank_o, e2_o):
    assert P_TOPK == 16 and P_KEYS == 128
    mod = mod_ref[0]
    h2 = _norm_mod(x_ref[...], g_ref[...], mod[3:4], mod[4:5])
    ht = h2.T
    ht_o[...] = ht.astype(BF16)
    qt = _mm_split(wqh_ref[...], wql_ref[...], ht)
    kd = P_KEY_DIM
    for h in range(P_HEADS):
        s1 = _mm_split(kh_ref[h, 0], kl_ref[h, 0], qt[(2 * h) * kd:(2 * h + 1) * kd])
        s2 = _mm_split(kh_ref[h, 1], kl_ref[h, 1], qt[(2 * h + 1) * kd:(2 * h + 2) * kd])
        t1 = _top_values(s1)
        t2 = _top_values(s2)
        best = _top_pair_sums(t1, t2)
        z = jnp.ones_like(best[0])
        for k in range(1, P_TOPK):
            z = z + jnp.exp(best[k] - best[0])
        inv_z = 0.5 / z
        tau = best[P_TOPK - 1]
        ranks, e2s = [], []
        for k in range(P_KEYS // 8):
            rows = slice(8 * k, 8 * k + 8)
            n_o[h, rows] = _prefix_count(lambda v, s=s1[rows]: (s + v) >= tau, t2)
            c_o[h, rows] = jnp.exp(s1[rows] - t1[0]) * inv_z
            ranks.append(_prefix_count(lambda v, s=s2[rows]: v > s, t2))
            e2s.append(jnp.exp(s2[rows] - t2[0]))
        rank_o[h] = jnp.concatenate(ranks, axis=0).astype(BF16)
        e2_o[h] = jnp.concatenate(e2s, axis=0).astype(BF16)


def _peer_query(xt, mod, gain, wq_t, keys, blocks_per_mod, tt):
    t_all, d = xt.shape
    nq = wq_t.shape[0]
    nblk = t_all // tt
    head = pl.BlockSpec((P_HEADS, P_KEYS, tt), lambda i: (0, 0, i))
    hsd = lambda dt: jax.ShapeDtypeStruct((P_HEADS, P_KEYS, t_all), dt)
    wq_hi, wq_lo = _split_bf16(wq_t)
    k_hi, k_lo = _split_bf16(keys)
    wspec = pl.BlockSpec((nq, d), lambda i: (0, 0))
    kspec = pl.BlockSpec((P_HEADS, 2, P_KEYS, P_KEY_DIM), lambda i: (0, 0, 0, 0))
    return pl.pallas_call(
        _peer_query_kernel,
        grid=(nblk,),
        in_specs=[pl.BlockSpec((tt, d), lambda i: (i, 0)),
                  pl.BlockSpec((1, N_MOD, d), lambda i: (i // blocks_per_mod, 0, 0)),
                  pl.BlockSpec((1, d), lambda i: (0, 0)),
                  wspec, wspec, kspec, kspec],
        out_specs=[pl.BlockSpec((d, tt), lambda i: (0, i)), head, head, head, head],
        out_shape=[jax.ShapeDtypeStruct((d, t_all), BF16), hsd(F32), hsd(F32), hsd(BF16), hsd(BF16)],
        compiler_params=_params("arbitrary"),
        name="peer_query",
    )(xt, mod, gain.reshape(1, d), wq_hi, wq_lo, k_hi, k_lo)


def _twice_gelu(x):
    return x * (1.0 + lax.erf(x * (2.0 ** -0.5)))


def _peer_main_kernel(x_ref, mod_ref, ht_ref, u_ref, vt_ref, n_ref, c_ref, rank_ref, e2_ref,
                      o_ref, acc_ref, act_ref, coef_ref):
    g = pl.program_id(1)
    last = pl.num_programs(1) - 1
    tt = ht_ref.shape[1]

    @pl.when(g == 0)
    def _():
        acc_ref[...] = jnp.zeros_like(acc_ref)
        act_ref[...] = jnp.zeros_like(act_ref)
        coef_ref[...] = jnp.zeros_like(coef_ref)

    acc_ref[...] += _mm(vt_ref[...], coef_ref[...] * act_ref[...])

    blk = jnp.minimum(g, last - 1)
    rows_per_step = u_ref.shape[0] // P_KEYS
    tiles = P_KEYS // BF16_ROWS
    zero = jnp.zeros((BF16_ROWS, tt), BF16)
    for ii in range(rows_per_step):
        row = blk * rows_per_step + ii
        coef = [None] * tiles
        for h in range(P_HEADS):
            nb = jnp.broadcast_to(n_ref[h, pl.ds(row, 1), :], (BF16_ROWS, tt)).astype(BF16)
            cb = jnp.broadcast_to(c_ref[h, pl.ds(row, 1), :], (BF16_ROWS, tt)).astype(BF16)
            for k in range(tiles):
                rows = slice(k * BF16_ROWS, (k + 1) * BF16_ROWS)
                term = jnp.where(rank_ref[h, rows, :] < nb, e2_ref[h, rows, :] * cb, zero)
                coef[k] = term if coef[k] is None else coef[k] + term
        for k in range(tiles):
            rows = slice(ii * P_KEYS + k * BF16_ROWS, ii * P_KEYS + (k + 1) * BF16_ROWS)
            coef_ref[rows, :] = coef[k]
    act_ref[...] = _twice_gelu(_mm(u_ref[...], ht_ref[...])).astype(BF16)

    @pl.when(g == last)
    def _():
        o_ref[...] = x_ref[...] + mod_ref[0][5:6] * acc_ref[...].T


def _peer_main(xt, mod, ht, u_tab, vt_tab, n, c, rank, e2, blocks_per_mod, tt):
    t_all, d = xt.shape
    n_exp = u_tab.shape[0]
    eb = PEER_EXP
    nsteps = n_exp // eb
    head = pl.BlockSpec((P_HEADS, P_KEYS, tt), lambda i, s: (0, 0, i))
    return pl.pallas_call(
        _peer_main_kernel,
        grid=(t_all // tt, nsteps + 1),
        in_specs=[pl.BlockSpec((tt, d), lambda i, s: (i, 0)),
                  pl.BlockSpec((1, N_MOD, d), lambda i, s: (i // blocks_per_mod, 0, 0)),
                  pl.BlockSpec((d, tt), lambda i, s: (0, i)),
                  pl.BlockSpec((eb, d), lambda i, s: (jnp.minimum(s, nsteps - 1), 0)),
                  pl.BlockSpec((d, eb), lambda i, s: (0, jnp.maximum(s - 1, 0))),
                  head, head, head, head],
        out_specs=pl.BlockSpec((tt, d), lambda i, s: (i, 0)),
        out_shape=jax.ShapeDtypeStruct((t_all, d), F32),
        scratch_shapes=[pltpu.VMEM((d, tt), F32), pltpu.VMEM((eb, tt), BF16), pltpu.VMEM((eb, tt), BF16)],
        compiler_params=_params("arbitrary", "arbitrary"),
        name="peer_main",
    )(xt, mod, ht, u_tab, vt_tab, n, c, rank, e2)


def _peer_layer(xt, mod, gain, w_q, keys, u_tab, v_tab, blocks_per_mod, tt):
    wq_t = w_q.T
    tq = PEER_QUERY_TOK
    ht, n, c, rank, e2 = _peer_query(xt, mod, gain, wq_t, keys, blocks_per_mod * (tt // tq), tq)
    return _peer_main(xt, mod, ht, u_tab.astype(BF16), v_tab.T.astype(BF16), n, c, rank, e2,
                      blocks_per_mod, tt)


def kernel(x, c, ctx, c_ctx, mod_w, mod_b, norm1_g, norm2_g, ev_w_in, ev_w_out, attn_q_gain, attn_k_gain,
           rw_mu, rw_w0, rw_w2, rw_a0, rw_a2, rw_g2, rw_k_k, rw_k_a, rw_r_k, rw_gn_w, rw_gn_b, ssd_w_in,
           ssd_conv_w, ssd_conv_b, ssd_dt_bias, ssd_a_log, ssd_d, ssd_norm_w, ssd_w_out, peer_w_q,
           peer_keys, peer_u, peer_v):
    bsz, seq, dm = x.shape
    lc = ctx.shape[1]
    depth = mod_w.shape[0]
    assert lc == TOK_TILE and seq % PEER_TOK == 0 and (bsz * lc) % PEER_TOK == 0

    rows = -(-(bsz + 1) // 8) * 8
    cc = jnp.concatenate([c, c_ctx[None], jnp.zeros((rows - bsz - 1, dm), F32)], axis=0)
    mods = _modulation(cc, mod_w, mod_b)
    xc, xl = ctx, x

    for i in range(depth):
        last = i == depth - 1
        j = i // 2
        ml = mods[i, :bsz].reshape(bsz, N_MOD, dm)
        mc = mods[i, bsz].reshape(1, N_MOD, dm)
        modx = jnp.stack([jnp.broadcast_to(mc, (bsz, N_MOD, dm)), ml], axis=1)
        if i % 2 == 0:
            w_in = ev_w_in[j].astype(BF16)
            za = _in_proj(xc, xl, modx, norm1_g[i], w_in[:, :A_IN])
            zr = _in_proj(xc, xl, modx, norm1_g[i], w_in[:, A_IN:])
            q, k, v = _attn_pre(za, attn_q_gain[j], attn_k_gain[j], lc)
            attn = _attention(q, k, v, lc)
            r, vv, kkn, g, bonus, lw, a, kd = _rwkv_pre(
                zr, rw_mu[j], rw_w0[j], rw_w2[j], rw_a0[j], rw_a2[j], rw_g2[j], rw_k_k[j], rw_k_a[j],
                rw_r_k[j], lc)
            wkv = _rwkv_chunks(r, vv, kkn, lw, a, kd, lc)
            xc, xl = _even_out(xc, xl, modx, attn, wkv, bonus, g, rw_gn_w[j], rw_gn_b[j],
                               ev_w_out[j].astype(BF16))
        else:
            n_heads = ssd_a_log.shape[-1]
            inner = n_heads * C_HEAD_DIM
            conv_dim = inner + 2 * C_GROUPS * C_STATE
            w_in = ssd_w_in[j]
            w_dt = jnp.zeros((dm, 2 * LANES), F32)
            w_dt = w_dt.at[:, :n_heads].set(w_in[:, inner + conv_dim:inner + conv_dim + n_heads])
            w_dt = w_dt.at[:, LANES:LANES + n_heads].set(w_in[:, inner + conv_dim + n_heads:])
            dtb = jnp.zeros((1, 2 * LANES), F32)
            dtb = dtb.at[0, :n_heads].set(ssd_dt_bias[j, 0]).at[0, LANES:LANES + n_heads].set(ssd_dt_bias[j, 1])
            zg = _in_proj(xc, xl, modx, norm1_g[i], w_in[:, :inner].astype(BF16), out_dtype=BF16)
            xbc = _in_proj(xc, xl, modx, norm1_g[i], w_in[:, inner:inner + conv_dim].astype(BF16))
            dtraw = _in_proj(xc, xl, modx, norm1_g[i], w_dt.astype(BF16))
            xact, dtact = _ssd_pre(xbc, dtraw, ssd_conv_w[j], ssd_conv_b[j], dtb, lc)
            ys = _ssd_chunks(xact, dtact, ssd_a_log[j], lc)
            assert last, "the SSD layer keeps only the latent stream"
            xl = _odd_out(xl, modx, ys, xact, zg, ssd_d[j], ssd_norm_w[j], ssd_w_out[j].astype(BF16), lc)

        tt = PEER_TOK
        xl = _peer_layer(xl.reshape(bsz * seq, dm), ml, norm2_g[i], peer_w_q[i], peer_keys[i],
                         peer_u[i], peer_v[i], seq // tt, tt).reshape(bsz, seq, dm)
        if not last:
            xc = _peer_layer(xc.reshape(bsz * lc, dm), mc, norm2_g[i], peer_w_q[i], peer_keys[i],
                             peer_u[i], peer_v[i], (bsz * lc) // tt, tt).reshape(bsz, lc, dm)
    return xl
```

```python
import functools
import math

import numpy as np
import jax
import jax.numpy as jnp
from jax import lax
from jax.experimental import pallas as pl
from jax.experimental.pallas import tpu as pltpu

F32 = jnp.float32
BF16 = jnp.bfloat16
HI = lax.Precision.HIGHEST

GRID_W = 64
EPS = 1e-6
N_MOD = 6
A_HEADS, A_KV_HEADS, A_HEAD_DIM = 8, 2, 64
A_Q = A_HEADS * A_HEAD_DIM
A_KV = A_KV_HEADS * A_HEAD_DIM
A_IN = A_Q + 2 * A_KV
ROPE_THETA = 10000.0
B_HEADS, B_HEAD_DIM = 8, 64
B_WIDTH = B_HEADS * B_HEAD_DIM
B_DECAY_LORA, B_AAA_LORA, B_GATE_LORA = 64, 64, 128
B_IN = 3 * B_WIDTH + B_DECAY_LORA + B_AAA_LORA + B_GATE_LORA
B_GN_EPS = 64e-5
C_HEAD_DIM, C_GROUPS, C_STATE, C_CHUNK = 64, 4, 128, 128
P_HEADS, P_KEYS, P_KEY_DIM, P_TOPK = 8, 128, 128, 16

LANES = 128
BF16_ROWS = 16
TOK_TILE = 256
R_CHUNK = 64
PEER_TOK = 512
PEER_QUERY_TOK = 256
PEER_EXP = 1024
VMEM_LIMIT = 56 * 1024 * 1024


def _mm(a, b, prec=None):
    return lax.dot_general(a, b, (((1,), (0,)), ((), ())), precision=prec,
                           preferred_element_type=F32)


def _mm_nt(a, b, prec=None):
    return lax.dot_general(a, b, (((1,), (1,)), ((), ())), precision=prec,
                           preferred_element_type=F32)


def _mm_tn(a, b, prec=None):
    return lax.dot_general(a, b, (((0,), (0,)), ((), ())), precision=prec,
                           preferred_element_type=F32)


def _params(*sem):
    return pltpu.CompilerParams(dimension_semantics=sem, vmem_limit_bytes=VMEM_LIMIT)


def _softplus(x):
    return jnp.maximum(x, 0.0) + jnp.log1p(jnp.exp(-jnp.abs(x)))


def _sigmoid(x):
    return 1.0 / (1.0 + jnp.exp(-x))


def _silu(x):
    return x * _sigmoid(x)


def _mod_kernel(c_ref, w_ref, b_ref, o_ref):
    o_ref[0] = _mm(_silu(c_ref[...]), w_ref[0], HI) + b_ref[0]


def _modulation(cc, mod_w, mod_b):
    depth, d, n6 = mod_w.shape
    rows = cc.shape[0]
    tn = 1024
    return pl.pallas_call(
        _mod_kernel,
        grid=(depth, n6 // tn),
        in_specs=[pl.BlockSpec((rows, d), lambda l, n: (0, 0)),
                  pl.BlockSpec((1, d, tn), lambda l, n: (l, 0, n)),
                  pl.BlockSpec((1, 1, tn), lambda l, n: (l, 0, n))],
        out_specs=pl.BlockSpec((1, rows, tn), lambda l, n: (l, 0, n)),
        out_shape=jax.ShapeDtypeStruct((depth, rows, n6), F32),
        compiler_params=_params("arbitrary", "arbitrary"),
        name="modulation",
    )(cc, mod_w, mod_b.reshape(depth, 1, n6))


def _norm_mod(x, gain, shift, scale):
    ms = jnp.mean(x * x, axis=-1, keepdims=True)
    return x * lax.rsqrt(ms + EPS) * gain * (1.0 + scale) + shift


def _stream_specs(tm, d, lct):
    return [pl.BlockSpec((1, tm, d), lambda b, j: (b, jnp.minimum(j, lct - 1), 0)),
            pl.BlockSpec((1, tm, d), lambda b, j: (b, jnp.maximum(j - lct, 0), 0))]


def _stream_tile(xc_ref, xl_ref, lct):
    return jnp.where(pl.program_id(1) < lct, xc_ref[0], xl_ref[0])


def _in_proj_kernel(xc_ref, xl_ref, mod_ref, g_ref, w_ref, *o_refs, lct):
    mod = mod_ref[0, 0]
    h = _norm_mod(_stream_tile(xc_ref, xl_ref, lct), g_ref[...], mod[0:1], mod[1:2]).astype(BF16)
    off = 0
    for o_ref in o_refs:
        n = o_ref.shape[-1]
        for c0 in range(0, n, 1024):
            c1 = min(n, c0 + 1024)
            o_ref[0, :, c0:c1] = _mm(h, w_ref[:, off + c0:off + c1]).astype(o_ref.dtype)
        off += n


def _in_proj(xc, xl, modx, gain, w, splits):
    bsz, lc, d = xc.shape
    ln = lc + xl.shape[1]
    assert w.shape[1] == sum(n for n, _ in splits)
    tm = TOK_TILE
    lct = lc // tm
    return pl.pallas_call(
        functools.partial(_in_proj_kernel, lct=lct),
        grid=(bsz, ln // tm),
        in_specs=_stream_specs(tm, d, lct) + [
            pl.BlockSpec((1, 1, N_MOD, d), lambda b, j: (b, jnp.where(j < lct, 0, 1), 0, 0)),
            pl.BlockSpec((1, d), lambda b, j: (0, 0)),
            pl.BlockSpec(w.shape, lambda b, j: (0, 0))],
        out_specs=[pl.BlockSpec((1, tm, n), lambda b, j: (b, j, 0)) for n, _ in splits],
        out_shape=[jax.ShapeDtypeStruct((bsz, ln, n), dt) for n, dt in splits],
        compiler_params=_params("arbitrary", "arbitrary"),
        name="in_proj",
    )(xc, xl, modx, gain.reshape(1, d), w)


def _rope_tables(lc, seq):
    t = jnp.arange(seq)
    row = (t // GRID_W).astype(F32)
    col = (t % GRID_W).astype(F32)
    m = A_HEAD_DIM // 4
    inv = ROPE_THETA ** (-jnp.arange(m, dtype=F32) / m)
    ar, ac = row[:, None] * inv, col[:, None] * inv
    cos = jnp.concatenate([jnp.cos(ar), jnp.cos(ar), jnp.cos(ac), jnp.cos(ac)], axis=-1)
    sin = jnp.concatenate([-jnp.sin(ar), jnp.sin(ar), -jnp.sin(ac), jnp.sin(ac)], axis=-1)
    reps = LANES // A_HEAD_DIM
    cos = jnp.concatenate([jnp.ones((lc, A_HEAD_DIM), F32), cos], axis=0)
    sin = jnp.concatenate([jnp.zeros((lc, A_HEAD_DIM), F32), sin], axis=0)
    return jnp.tile(cos, (1, reps)), jnp.tile(sin, (1, reps))


def _head_block_ones(width, scale):
    idx = np.arange(LANES) // width
    return jnp.asarray((idx[:, None] == idx[None, :]).astype(np.float32) * scale)


def _attn_pre_kernel(z_ref, qg_ref, kg_ref, cos_ref, sin_ref, bones_ref, q_ref, k_ref, v_ref):
    tm = z_ref.shape[1]
    cs, sn = cos_ref[...], sin_ref[...]
    bones = bones_ref[...]
    lane = lax.broadcasted_iota(jnp.int32, (tm, LANES), 1)
    first = (lane % 32) < 16
    low = lane < A_HEAD_DIM

    def norm_rope(x, gain):
        ms = _bmm(x * x, bones)
        y = x * lax.rsqrt(ms + EPS) * gain
        partner = jnp.where(first, pltpu.roll(y, LANES - 16, 1), pltpu.roll(y, 16, 1))
        return y * cs + partner * sn

    heads_per_slab = LANES // A_HEAD_DIM
    group = A_HEADS // A_KV_HEADS
    for m in range(A_Q // LANES):
        sl = slice(m * LANES, (m + 1) * LANES)
        y = norm_rope(z_ref[0, :, sl], qg_ref[:, sl]) * (A_HEAD_DIM ** -0.5)
        swapped = pltpu.roll(y, A_HEAD_DIM, 1)
        for hh in range(heads_per_slab):
            h = m * heads_per_slab + hh
            kvh = h // group
            src = y if kvh == hh else swapped
            keep = low if kvh == 0 else jnp.logical_not(low)
            q_ref[0, h] = jnp.where(keep, src, 0.0).astype(BF16)
    k_ref[0] = norm_rope(z_ref[0, :, A_Q:A_Q + A_KV], kg_ref[...]).astype(BF16)
    v_ref[0] = z_ref[0, :, A_Q + A_KV:A_IN].astype(BF16)


def _attn_pre(za, q_gain, k_gain, lc):
    bsz, ln, _ = za.shape
    tm = TOK_TILE
    cos, sin = _rope_tables(lc, ln - lc)
    qg = jnp.tile(q_gain, A_HEADS).reshape(1, A_Q)
    kg = jnp.tile(k_gain, A_KV_HEADS).reshape(1, A_KV)
    bones = _head_block_ones(A_HEAD_DIM, 1.0 / A_HEAD_DIM)
    return pl.pallas_call(
        _attn_pre_kernel,
        grid=(bsz, ln // tm),
        in_specs=[pl.BlockSpec((1, tm, A_IN), lambda b, j: (b, j, 0)),
                  pl.BlockSpec((1, A_Q), lambda b, j: (0, 0)),
                  pl.BlockSpec((1, A_KV), lambda b, j: (0, 0)),
                  pl.BlockSpec((tm, LANES), lambda b, j: (j, 0)),
                  pl.BlockSpec((tm, LANES), lambda b, j: (j, 0)),
                  pl.BlockSpec((LANES, LANES), lambda b, j: (0, 0))],
        out_specs=[pl.BlockSpec((1, A_HEADS, tm, LANES), lambda b, j: (b, 0, j, 0)),
                   pl.BlockSpec((1, tm, A_KV), lambda b, j: (b, j, 0)),
                   pl.BlockSpec((1, tm, A_KV), lambda b, j: (b, j, 0))],
        out_shape=[jax.ShapeDtypeStruct((bsz, A_HEADS, ln, LANES), BF16),
                   jax.ShapeDtypeStruct((bsz, ln, A_KV), BF16),
                   jax.ShapeDtypeStruct((bsz, ln, A_KV), BF16)],
        compiler_params=_params("arbitrary", "arbitrary"),
        name="attn_pre",
    )(za, qg, kg, cos, sin, bones)


def _attn_kernel(q_ref, k_ref, v_ref, o_ref, *, lc):
    tq = q_ref.shape[2]
    j = pl.program_id(1)
    lane = lax.broadcasted_iota(jnp.int32, (tq, LANES), 1)
    low = lane < A_HEAD_DIM
    heads_per_slab = LANES // A_HEAD_DIM
    group = A_HEADS // A_KV_HEADS

    def run(klen):
        k = k_ref[0, :klen, :]
        v = v_ref[0, :klen, :]
        for m in range(A_Q // LANES):
            parts = []
            for hh in range(heads_per_slab):
                h = m * heads_per_slab + hh
                kvh = h // group
                s = _mm_nt(q_ref[0, h], k)
                p = jnp.exp(s - jnp.max(s, axis=-1, keepdims=True))
                l = jnp.sum(p, axis=-1, keepdims=True)
                o = _mm(p.astype(BF16), v) / l
                parts.append(o if kvh == hh else pltpu.roll(o, A_HEAD_DIM, 1))
            o_ref[0, :, m * LANES:(m + 1) * LANES] = jnp.where(low, parts[0], parts[1])

    @pl.when(j * tq < lc)
    def _():
        run(lc)

    @pl.when(j * tq >= lc)
    def _():
        run(k_ref.shape[1])


def _attention(q, k, v, lc):
    bsz, _, ln, _ = q.shape
    tq = TOK_TILE
    return pl.pallas_call(
        functools.partial(_attn_kernel, lc=lc),
        grid=(bsz, ln // tq),
        in_specs=[pl.BlockSpec((1, A_HEADS, tq, LANES), lambda b, j: (b, 0, j, 0)),
                  pl.BlockSpec((1, ln, A_KV), lambda b, j: (b, 0, 0)),
                  pl.BlockSpec((1, ln, A_KV), lambda b, j: (b, 0, 0))],
        out_specs=pl.BlockSpec((1, tq, A_Q), lambda b, j: (b, j, 0)),
        out_shape=jax.ShapeDtypeStruct((bsz, ln, A_Q), F32),
        compiler_params=_params("arbitrary", "arbitrary"),
        name="attention",
    )(q, k, v)


def _shifted_rows(z, prev_row, next_row):
    tm = z.shape[0]
    row = lax.broadcasted_iota(jnp.int32, (tm, 1), 0)
    zprev = jnp.where(row == 0, prev_row, pltpu.roll(z, 1, 0))
    znext = jnp.where(row == tm - 1, next_row, pltpu.roll(z, tm - 1, 0))
    return zprev, znext


def _halo_rows(zp_ref, zn_ref, j, lct, nj):
    prev_ok = jnp.logical_and(j != 0, j != lct)
    next_ok = jnp.logical_and(j != lct - 1, j != nj - 1)
    prev_row = jnp.where(prev_ok, zp_ref[0, 7:8, :], 0.0)
    next_row = jnp.where(next_ok, zn_ref[0, 0:1, :], 0.0)
    return prev_row, next_row


def _halo_specs(tm, width, ln):
    nb8 = tm // 8
    last = ln // 8 - 1
    return [pl.BlockSpec((1, tm, width), lambda b, j: (b, j, 0)),
            pl.BlockSpec((1, 8, width), lambda b, j: (b, jnp.maximum(j * nb8 - 1, 0), 0)),
            pl.BlockSpec((1, 8, width), lambda b, j: (b, jnp.minimum((j + 1) * nb8, last), 0))]


def _per_head_sum(x, bones):
    return jnp.concatenate(
        [_bmm(x[:, m * LANES:(m + 1) * LANES], bones) for m in range(x.shape[1] // LANES)], axis=1)


def _rwkv_pre_kernel(z_ref, zp_ref, zn_ref, mu_ref, w0_ref, w2_ref, a0_ref, a2_ref, g2_ref,
                     kk_ref, ka_ref, rk_ref, bones_ref,
                     r_o, v_o, kkn_o, g_o, bonus_o, lw_o, a_o, kd_o, *, lct):
    j = pl.program_id(1)
    z = z_ref[0]
    prev_row, next_row = _halo_rows(zp_ref, zn_ref, j, lct, pl.num_programs(1))
    zprev, znext = _shifted_rows(z, prev_row, next_row)
    zm = z + mu_ref[...] * (0.5 * (zprev + znext) - z)
    w = B_WIDTH
    r, k, v = zm[:, 0:w], zm[:, w:2 * w], zm[:, 2 * w:3 * w]
    wa = zm[:, 3 * w:3 * w + B_DECAY_LORA + B_AAA_LORA]
    gl = zm[:, 3 * w + B_DECAY_LORA + B_AAA_LORA:]
    bones = bones_ref[...]
    kk = k * kk_ref[...]
    kkn = kk / jnp.maximum(jnp.sqrt(_per_head_sum(kk * kk, bones)), 1e-12)
    r_o[0] = r
    v_o[0] = v
    kkn_o[0] = kkn
    g_o[0] = _bmm(_sigmoid(gl), g2_ref[...])
    twa = jnp.tanh(wa)
    bonus = jnp.zeros_like(r)
    for d in range(2):
        wraw = w0_ref[d] + _bmm(twa, w2_ref[d])
        lw_o[d, 0] = -jnp.exp(-_softplus(-wraw) - 0.5)
        a = _sigmoid(a0_ref[d] + _bmm(wa, a2_ref[d]))
        kd = k * (1.0 + (a - 1.0) * ka_ref[...])
        a_o[d, 0] = a
        kd_o[d, 0] = kd
        bonus = bonus + _per_head_sum(r * kd * rk_ref[...], bones) * v
    bonus_o[0] = bonus


def _rwkv_pre(zr, mu, w0, w2, a0, a2, g2, k_k, k_a, r_k, lc):
    bsz, ln, width = zr.shape
    tm = TOK_TILE
    w = B_WIDTH
    zeros = jnp.zeros((2, B_DECAY_LORA, w), F32)
    w2p = jnp.concatenate([w2, zeros], axis=1)
    a2p = jnp.concatenate([zeros, a2], axis=1)
    bones = _head_block_ones(B_HEAD_DIM, 1.0)
    full = lambda *shape: pl.BlockSpec(shape, lambda b, j: (0,) * len(shape))
    tok = pl.BlockSpec((1, tm, w), lambda b, j: (b, j, 0))
    tok2 = pl.BlockSpec((2, 1, tm, w), lambda b, j: (0, b, j, 0))
    sd = jax.ShapeDtypeStruct((bsz, ln, w), F32)
    sd2 = jax.ShapeDtypeStruct((2, bsz, ln, w), F32)
    return pl.pallas_call(
        functools.partial(_rwkv_pre_kernel, lct=lc // tm),
        grid=(bsz, ln // tm),
        in_specs=_halo_specs(tm, width, ln) + [
            full(1, width), full(2, 1, w), full(2, 2 * B_DECAY_LORA, w), full(2, 1, w),
            full(2, 2 * B_AAA_LORA, w), full(B_GATE_LORA, w), full(1, w), full(1, w), full(1, w),
            full(LANES, LANES)],
        out_specs=[tok, tok, tok, tok, tok, tok2, tok2, tok2],
        out_shape=[sd, sd, sd, sd, sd, sd2, sd2, sd2],
        compiler_params=_params("arbitrary", "arbitrary"),
        name="rwkv_pre",
    )(zr, zr, zr, mu.reshape(1, width), w0.reshape(2, 1, w), w2p, a0.reshape(2, 1, w), a2p, g2,
      k_k.reshape(1, w), k_a.reshape(1, w), r_k.reshape(1, w), bones)


def _scan_chunk(i, d, nctx, ntot):
    rev = jnp.where(i < nctx, nctx - 1 - i, ntot + nctx - 1 - i)
    return jnp.where(d == 0, i, rev)


def _order_masks(n, pair):
    t = np.arange(n * pair)
    same = (t[:, None] // n) == (t[None, :] // n)
    tt, ss = t[:, None] % n, t[None, :] % n
    incl = np.stack([same & (ss <= tt), same & (ss >= tt)]).astype(np.float32)
    strict = np.stack([same & (ss < tt), same & (ss > tt)]).astype(np.float32)
    return jnp.asarray(incl), jnp.asarray(strict)


def _pair_stack(x, low):
    return jnp.concatenate([jnp.where(low, x, 0.0), jnp.where(low, 0.0, x)], axis=0)


def _bmm(a, b):
    return _mm(a.astype(BF16), b.astype(BF16))


def _rwkv_chunk_kernel(rf_ref, vf_ref, kf_ref, lwf_ref, af_ref, kdf_ref,
                       rb_ref, vb_ref, kb_ref, lwb_ref, ab_ref, kdb_ref,
                       tri_ref, incl_ref, strict_ref, of_ref, ob_ref, state_ref):
    c = R_CHUNK
    n2 = 2 * c

    @pl.when(pl.program_id(1) == 0)
    def _():
        state_ref[...] = jnp.zeros_like(state_ref)

    low = lax.broadcasted_iota(jnp.int32, (c, LANES), 1) < B_HEAD_DIM
    eye = (lax.broadcasted_iota(jnp.int32, (LANES, LANES), 0)
           == lax.broadcasted_iota(jnp.int32, (LANES, LANES), 1)).astype(F32)
    stack = lambda x: _pair_stack(x, low)
    refs = ((rf_ref, vf_ref, kf_ref, lwf_ref, af_ref, kdf_ref, of_ref),
            (rb_ref, vb_ref, kb_ref, lwb_ref, ab_ref, kdb_ref, ob_ref))
    dirs, slabs, outs = [], [], []
    kts, bts, kds, rts, vs, pcs = [], [], [], [], [], []
    for d, (r_ref, v_ref, kkn_ref, lw_ref, a_ref, kd_ref, o_ref) in enumerate(refs):
        lw = lw_ref[0, 0]
        cum = _mm(tri_ref[d], lw, HI)
        p_in = jnp.exp(cum)
        p_inv = jnp.exp(-cum)
        p_ex = jnp.exp(cum - lw)
        p_all = jnp.exp(jnp.sum(lw, axis=0, keepdims=True))
        kkn = kkn_ref[0]
        kt = kkn * p_ex
        bt = kkn * a_ref[0, 0] * p_inv
        kd = kd_ref[0, 0] * p_inv
        rt = r_ref[0] * p_in
        v = v_ref[0]
        for m in range(B_WIDTH // LANES):
            sl = slice(m * LANES, (m + 1) * LANES)
            dirs.append(d)
            slabs.append(m)
            outs.append(o_ref)
            kts.append(stack(kt[:, sl]).astype(BF16))
            bts.append(stack(bt[:, sl]).astype(BF16))
            kds.append(stack(kd[:, sl]).astype(BF16))
            rts.append(stack(rt[:, sl]))
            vs.append(stack(v[:, sl]).astype(BF16))
            pcs.append(p_all[:, sl])
    n = len(dirs)
    incl = [incl_ref[d] > 0.5 for d in range(2)]
    strict = [strict_ref[d] > 0.5 for d in range(2)]
    gram = [_mm_nt(jnp.concatenate([kts[x], rts[x].astype(BF16)], axis=0),
                   jnp.concatenate([bts[x], kds[x]], axis=0)) for x in range(n)]
    a_b = [jnp.where(strict[dirs[x]], gram[x][:n2, :n2], 0.0) for x in range(n)]
    a_k = [jnp.where(strict[dirs[x]], gram[x][:n2, n2:], 0.0) for x in range(n)]
    m_b = [jnp.where(incl[dirs[x]], gram[x][n2:, :n2], 0.0) for x in range(n)]
    m_k = [jnp.where(incl[dirs[x]], gram[x][n2:, n2:], 0.0) for x in range(n)]
    pw = [-a for a in a_b]
    tinv = [eye + p for p in pw]
    for _ in range(int(math.log2(c)) - 1):
        pw = [_bmm(p, p) for p in pw]
        tinv = [t + _bmm(t, p) for t, p in zip(tinv, pw)]
    mv = [_bmm(jnp.concatenate([a_k[x], m_k[x]], axis=0), vs[x]) for x in range(n)]
    wy = [_bmm(tinv[x], jnp.concatenate([kts[x], mv[x][:n2].astype(BF16)], axis=1)) for x in range(n)]
    mbwy = [_bmm(m_b[x], wy[x]) for x in range(n)]
    wyb = [_mm_tn(wy[x].astype(BF16), bts[x]) for x in range(n)]
    vtk = [_mm_tn(vs[x], kds[x]) for x in range(n)]
    s0 = [state_ref[dirs[x], slabs[x]] for x in range(n)]
    o_st = [_mm_nt((rts[x] - mbwy[x][:, :LANES]).astype(BF16), s0[x].astype(BF16))
            + (mv[x][n2:] - mbwy[x][:, LANES:]) for x in range(n)]
    for x in range(n):
        g = (eye - wyb[x][:LANES]) * pcs[x]
        h = (vtk[x] - wyb[x][LANES:]) * pcs[x]
        outs[x][0, :, slabs[x] * LANES:(slabs[x] + 1) * LANES] = o_st[x][:c] + o_st[x][c:]
        state_ref[dirs[x], slabs[x]] = _bmm(s0[x], g) + h


def _rwkv_chunks(r, v, kkn, lw, a, kd, lc):
    bsz, ln, w = r.shape
    c = R_CHUNK
    nctx, ntot = lc // c, ln // c
    tri, _ = _order_masks(c, 1)
    incl, strict = _order_masks(c, 2)
    specs = []
    for d in range(2):
        tok = pl.BlockSpec((1, c, w), lambda b, i, d=d: (b, _scan_chunk(i, d, nctx, ntot), 0))
        tok2 = pl.BlockSpec((1, 1, c, w), lambda b, i, d=d: (d, b, _scan_chunk(i, d, nctx, ntot), 0))
        specs.append((tok, tok2))
    full = lambda *shape: pl.BlockSpec(shape, lambda b, i: (0,) * len(shape))
    sd = jax.ShapeDtypeStruct((bsz, ln, w), F32)
    return pl.pallas_call(
        _rwkv_chunk_kernel,
        grid=(bsz, ntot),
        in_specs=[specs[0][0]] * 3 + [specs[0][1]] * 3 + [specs[1][0]] * 3 + [specs[1][1]] * 3
        + [full(2, c, c), full(2, 2 * c, 2 * c), full(2, 2 * c, 2 * c)],
        out_specs=[specs[0][0], specs[1][0]],
        out_shape=[sd, sd],
        scratch_shapes=[pltpu.VMEM((2, w // LANES, LANES, LANES), F32)],
        compiler_params=_params("arbitrary", "arbitrary"),
        name="rwkv_chunks",
    )(r, v, kkn, lw, a, kd, r, v, kkn, lw, a, kd, tri, incl, strict)


def _even_out_kernel(xc_ref, xl_ref, mod_ref, attn_ref, wkvf_ref, wkvb_ref, bonus_ref, g_ref, gw_ref,
                     gb_ref, bones_ref, w_ref, oc_ref, ol_ref, *, lct):
    bones = bones_ref[...]
    y = bonus_ref[0]
    for wkv_ref in (wkvf_ref, wkvb_ref):
        o = wkv_ref[0]
        dlt = o - _per_head_sum(o, bones)
        var = _per_head_sum(dlt * dlt, bones)
        y = y + dlt * lax.rsqrt(var + B_GN_EPS) * gw_ref[...] + gb_ref[...]
    y = y * g_ref[0]
    mix = _mm(attn_ref[0].astype(BF16), w_ref[:A_Q, :]) + _mm(y.astype(BF16), w_ref[A_Q:, :])
    res = _stream_tile(xc_ref, xl_ref, lct) + mod_ref[0, 0][2:3] * mix
    is_ctx = pl.program_id(1) < lct

    @pl.when(is_ctx)
    def _():
        oc_ref[0] = res

    @pl.when(jnp.logical_not(is_ctx))
    def _():
        ol_ref[0] = res


def _even_out(xc, xl, modx, attn, wkv, bonus, g, gn_w, gn_b, w_out):
    bsz, lc, d = xc.shape
    seq = xl.shape[1]
    tm = TOK_TILE
    lct = lc // tm
    w = B_WIDTH
    bones = _head_block_ones(B_HEAD_DIM, 1.0 / B_HEAD_DIM)
    tokw = lambda n: pl.BlockSpec((1, tm, n), lambda b, j: (b, j, 0))
    full = lambda *shape: pl.BlockSpec(shape, lambda b, j: (0,) * len(shape))
    streams = _stream_specs(tm, d, lct)
    return pl.pallas_call(
        functools.partial(_even_out_kernel, lct=lct),
        grid=(bsz, (lc + seq) // tm),
        in_specs=streams + [
            pl.BlockSpec((1, 1, N_MOD, d), lambda b, j: (b, jnp.where(j < lct, 0, 1), 0, 0)),
            tokw(A_Q), tokw(w), tokw(w),
            tokw(w), tokw(w), full(1, w), full(1, w), full(LANES, LANES),
            full(A_Q + w, d)],
        out_specs=streams,
        out_shape=[jax.ShapeDtypeStruct((bsz, lc, d), F32), jax.ShapeDtypeStruct((bsz, seq, d), F32)],
        compiler_params=_params("arbitrary", "arbitrary"),
        name="even_out",
    )(xc, xl, modx, attn, wkv[0], wkv[1], bonus, g, gn_w.reshape(1, w), gn_b.reshape(1, w), bones, w_out)


def _ssd_pre_kernel(z_ref, zp_ref, zn_ref, dt_ref, cw_ref, cb_ref, dtb_ref, x_o, dt_o, *, lct):
    j = pl.program_id(1)
    z = z_ref[0]
    prev_row, next_row = _halo_rows(zp_ref, zn_ref, j, lct, pl.num_programs(1))
    zprev, znext = _shifted_rows(z, prev_row, next_row)
    y = zprev * cw_ref[0:1, :] + z * cw_ref[1:2, :] + znext * cw_ref[2:3, :] + cb_ref[...]
    x_o[0] = _silu(y).astype(x_o.dtype)
    dt_o[0] = _softplus(dt_ref[0] + dtb_ref[...])


def _ssd_pre(xbc, dtraw, conv_w, conv_b, dtb, lc):
    bsz, ln, width = xbc.shape
    tm = TOK_TILE
    nd = dtraw.shape[-1]
    full = lambda *shape: pl.BlockSpec(shape, lambda b, j: (0,) * len(shape))
    return pl.pallas_call(
        functools.partial(_ssd_pre_kernel, lct=lc // tm),
        grid=(bsz, ln // tm),
        in_specs=_halo_specs(tm, width, ln) + [
            pl.BlockSpec((1, tm, nd), lambda b, j: (b, j, 0)),
            full(conv_w.shape[0], width), full(1, width), full(1, nd)],
        out_specs=[pl.BlockSpec((1, tm, width), lambda b, j: (b, j, 0)),
                   pl.BlockSpec((1, tm, nd), lambda b, j: (b, j, 0))],
        out_shape=[jax.ShapeDtypeStruct((bsz, ln, width), BF16),
                   jax.ShapeDtypeStruct((bsz, ln, nd), F32)],
        compiler_params=_params("arbitrary", "arbitrary"),
        name="ssd_pre",
    )(xbc, xbc, xbc, dtraw, conv_w, conv_b.reshape(1, width), dtb)


def _ssd_chunk_kernel(x_ref, b_ref, c_ref, dt_ref, alog_ref, incl_ref, expand_ref, o_ref, state_ref,
                      *, n_heads):
    q = C_CHUNK
    i = pl.program_id(2)

    @pl.when(i == 0)
    def _():
        state_ref[...] = jnp.zeros_like(state_ref)

    lane = lax.broadcasted_iota(jnp.int32, (1, LANES), 1)
    a_neg = jnp.where(lane < n_heads, -jnp.exp(alog_ref[0]), 0.0)
    dt = dt_ref[0]
    dta = dt * a_neg
    incl_f = incl_ref[0]
    cum = _mm(incl_f, dta, HI)
    cum_t = cum.T
    total = jnp.sum(dta, axis=0, keepdims=True)
    expand = expand_ref[...]
    dt_x = _bmm(dt, expand)
    ec_x = _bmm(jnp.exp(cum), expand)
    de_x = _bmm(jnp.exp(total - cum), expand)
    cd_x = _bmm(jnp.broadcast_to(jnp.exp(total), (BF16_ROWS, LANES)), expand)[0:1]
    incl = incl_f > 0.5
    low = lax.broadcasted_iota(jnp.int32, (q, LANES), 1) < C_HEAD_DIM
    heads_per_group = n_heads // C_GROUPS
    pairs_per_group = heads_per_group // 2
    gw = heads_per_group * C_HEAD_DIM
    for g in range(C_GROUPS):
        gsl = slice(g * gw, (g + 1) * gw)
        bg = b_ref[0, :, g * C_STATE:(g + 1) * C_STATE].astype(BF16)
        cg = c_ref[0, :, g * C_STATE:(g + 1) * C_STATE].astype(BF16)
        cb = _mm_nt(cg, bg)
        xd = x_ref[0, :, gsl].astype(F32) * dt_x[:, gsl]
        s_prev = state_ref[g]
        y_off = _mm(cg, s_prev.astype(BF16)) * ec_x[:, gsl]
        state_ref[g] = s_prev * cd_x[:, gsl] + _mm_tn(bg, (xd * de_x[:, gsl]).astype(BF16))
        for pp in range(pairs_per_group):
            pr = g * pairs_per_group + pp
            sl = slice(pr * LANES, (pr + 1) * LANES)
            ms = []
            for hh in range(2):
                h = 2 * pr + hh
                seg = cum[:, h:h + 1] - cum_t[h:h + 1, :]
                ms.append((cb * jnp.exp(jnp.where(incl, seg, -1e30))).astype(BF16))
            xd_pair = _pair_stack(xd[:, pp * LANES:(pp + 1) * LANES], low).astype(BF16)
            y_diag = _mm(jnp.concatenate(ms, axis=1), xd_pair)
            o_ref[0, 0, :, sl] = (y_diag + y_off[:, pp * LANES:(pp + 1) * LANES]).astype(o_ref.dtype)


def _ssd_chunks(xact, dtact, a_log, lc):
    bsz, ln, _ = xact.shape
    q = C_CHUNK
    n_heads = a_log.shape[1]
    inner = n_heads * C_HEAD_DIM
    gn = C_GROUPS * C_STATE
    nctx, ntot = lc // q, ln // q
    incl, _ = _order_masks(q, 1)
    expand = np.zeros((LANES, inner), np.float32)
    expand[np.arange(inner) // C_HEAD_DIM, np.arange(inner)] = 1.0
    alog = jnp.zeros((2, 1, LANES), F32).at[:, 0, :n_heads].set(a_log)
    cmap = lambda off: (lambda b, d, i: (b, _scan_chunk(i, d, nctx, ntot), off))
    return pl.pallas_call(
        functools.partial(_ssd_chunk_kernel, n_heads=n_heads),
        grid=(bsz, 2, ntot),
        in_specs=[pl.BlockSpec((1, q, inner), cmap(0)),
                  pl.BlockSpec((1, q, gn), cmap(inner // gn)),
                  pl.BlockSpec((1, q, gn), cmap(inner // gn + 1)),
                  pl.BlockSpec((1, q, LANES), lambda b, d, i: (b, _scan_chunk(i, d, nctx, ntot), d)),
                  pl.BlockSpec((1, 1, LANES), lambda b, d, i: (d, 0, 0)),
                  pl.BlockSpec((1, q, q), lambda b, d, i: (d, 0, 0)),
                  pl.BlockSpec((LANES, inner), lambda b, d, i: (0, 0))],
        out_specs=pl.BlockSpec((1, 1, q, inner), lambda b, d, i: (d, b, _scan_chunk(i, d, nctx, ntot), 0)),
        out_shape=jax.ShapeDtypeStruct((2, bsz, ln, inner), BF16),
        scratch_shapes=[pltpu.VMEM((C_GROUPS, C_STATE, inner // C_GROUPS), F32)],
        compiler_params=_params("arbitrary", "arbitrary", "arbitrary"),
        name="ssd_chunks",
    )(xact, xact, xact, dtact, alog, incl, jnp.asarray(expand, dtype=BF16))


def _odd_out_kernel(x_ref, mod_ref, ys_ref, xs_ref, zg_ref, dsk_ref, nw_ref, w_ref, o_ref):
    y = (ys_ref[0, 0].astype(F32) + ys_ref[1, 0].astype(F32)) + dsk_ref[...] * xs_ref[0].astype(F32)
    yg = y * _silu(zg_ref[0].astype(F32))
    inner = yg.shape[1]
    gw = inner // C_GROUPS
    parts = []
    for g in range(C_GROUPS):
        t = yg[:, g * gw:(g + 1) * gw]
        ms = jnp.mean(t * t, axis=-1, keepdims=True)
        parts.append((t * lax.rsqrt(ms + EPS) * nw_ref[:, g * gw:(g + 1) * gw]).astype(BF16))
    yn = jnp.concatenate(parts, axis=1)
    o_ref[0] = x_ref[0] + mod_ref[0, 0][2:3] * _mm(yn, w_ref[...])


def _odd_out(x, modx, ys, xact, zg, d_skip, norm_w, w_out, lc):
    bsz, seq, d = x.shape
    ln = lc + seq
    tm = TOK_TILE
    lct = lc // tm
    inner = zg.shape[-1]
    dsk = jnp.repeat(d_skip, C_HEAD_DIM).reshape(1, inner)
    tokw = lambda n: pl.BlockSpec((1, tm, n), lambda b, j: (b, j + lct, 0))
    full = lambda *shape: pl.BlockSpec(shape, lambda b, j: (0,) * len(shape))
    return pl.pallas_call(
        _odd_out_kernel,
        grid=(bsz, (ln - lc) // tm),
        in_specs=[pl.BlockSpec((1, tm, d), lambda b, j: (b, j, 0)),
                  pl.BlockSpec((1, 1, N_MOD, d), lambda b, j: (b, 1, 0, 0)),
                  pl.BlockSpec((2, 1, tm, inner), lambda b, j: (0, b, j + lct, 0)),
                  tokw(inner), tokw(inner), full(1, inner), full(1, inner), full(inner, d)],
        out_specs=pl.BlockSpec((1, tm, d), lambda b, j: (b, j, 0)),
        out_shape=jax.ShapeDtypeStruct((bsz, ln - lc, d), F32),
        compiler_params=_params("arbitrary", "arbitrary"),
        name="odd_out",
    )(x, modx, ys, xact, zg, dsk, norm_w.reshape(1, inner), w_out)


def _exchange(xs, i, l, descending=True):
    hi, lo = jnp.maximum(xs[i], xs[l]), jnp.minimum(xs[i], xs[l])
    xs[i], xs[l] = (hi, lo) if descending else (lo, hi)


def _bitonic_merge(xs):
    xs = list(xs)
    j = len(xs) // 2
    while j >= 1:
        for i in range(len(xs)):
            if i ^ j > i:
                _exchange(xs, i, i ^ j)
        j //= 2
    return xs


def _bitonic_sort(xs):
    xs = list(xs)
    k = 2
    while k <= len(xs):
        j = k // 2
        while j >= 1:
            for i in range(len(xs)):
                if i ^ j > i:
                    _exchange(xs, i, i ^ j, descending=(i & k) == 0)
            j //= 2
        k *= 2
    return xs


def _merge_sublanes(xs):
    n = len(xs)
    for shift in (4, 2, 1):
        other = [pltpu.roll(x, shift, 0) for x in xs]
        xs = _bitonic_merge([jnp.maximum(xs[k], other[n - 1 - k]) for k in range(n)])
    return xs


def _top_values(s):
    return _merge_sublanes(_bitonic_sort([s[8 * k:8 * k + 8] for k in range(s.shape[0] // 8)]))


def _top_pair_sums(t1, t2):
    sub = lax.broadcasted_iota(jnp.int32, t2[0].shape, 0)
    lo, hi = t2[0], t2[8]
    for b in range(1, 8):
        lo = jnp.where(sub == b, t2[b], lo)
        hi = jnp.where(sub == b, t2[8 + b], hi)
    xs = [t1[a] + lo for a in range(P_TOPK)]
    carry = t1[0] + hi
    for k in range(P_TOPK):
        xs[k], carry = jnp.maximum(xs[k], carry), jnp.minimum(xs[k], carry)
    return _merge_sublanes(xs)


def _split_bf16(x):
    hi = x.astype(BF16)
    return hi, (x - hi.astype(F32)).astype(BF16)


def _mm_split(a_hi, a_lo, b):
    b_hi, b_lo = _split_bf16(b)
    return _mm(a_hi, b_hi) + (_mm(a_hi, b_lo) + _mm(a_lo, b_hi))


def _prefix_count(test, vals):
    sel = jnp.where
    b3 = test(vals[7])
    b2 = test(sel(b3, vals[11], vals[3]))
    b1 = test(sel(b3, sel(b2, vals[13], vals[9]), sel(b2, vals[5], vals[1])))
    b0 = test(sel(b3, sel(b2, sel(b1, vals[14], vals[12]), sel(b1, vals[10], vals[8])),
                  sel(b2, sel(b1, vals[6], vals[4]), sel(b1, vals[2], vals[0]))))
    n = (sel(b3, 8.0, 0.0) + sel(b2, 4.0, 0.0)) + (sel(b1, 2.0, 0.0) + sel(b0, 1.0, 0.0))
    return sel(test(vals[15]), 16.0, n)


def _peer_query_kernel(x_ref, mod_ref, g_ref, wqh_ref, wql_ref, kh_ref, kl_ref,
                       ht_o, n_o, c_o, rank_o, e2_o):
    assert P_TOPK == 16 and P_KEYS == 128
    mod = mod_ref[0]
    h2 = _norm_mod(x_ref[...], g_ref[...], mod[3:4], mod[4:5])
    ht = h2.T
    ht_o[...] = ht.astype(BF16)
    qt = _mm_split(wqh_ref[...], wql_ref[...], ht)
    kd = P_KEY_DIM
    for h in range(P_HEADS):
        s1 = _mm_split(kh_ref[h, 0], kl_ref[h, 0], qt[(2 * h) * kd:(2 * h + 1) * kd])
        s2 = _mm_split(kh_ref[h, 1], kl_ref[h, 1], qt[(2 * h + 1) * kd:(2 * h + 2) * kd])
        t1 = _top_values(s1)
        t2 = _top_values(s2)
        best = _top_pair_sums(t1, t2)
        z = jnp.ones_like(best[0])
        for k in range(1, P_TOPK):
            z = z + jnp.exp(best[k] - best[0])
        inv_z = 0.5 / z
        tau = best[P_TOPK - 1]
        ranks, e2s = [], []
        for k in range(P_KEYS // 8):
            rows = slice(8 * k, 8 * k + 8)
            n_o[h, rows] = _prefix_count(lambda v, s=s1[rows]: (s + v) >= tau, t2)
            c_o[h, rows] = jnp.exp(s1[rows] - t1[0]) * inv_z
            ranks.append(_prefix_count(lambda v, s=s2[rows]: v > s, t2))
            e2s.append(jnp.exp(s2[rows] - t2[0]))
        rank_o[h] = jnp.concatenate(ranks, axis=0).astype(BF16)
        e2_o[h] = jnp.concatenate(e2s, axis=0).astype(BF16)


def _peer_query(xt, mod, gain, wq_t, keys, blocks_per_mod, tt):
    t_all, d = xt.shape
    nq = wq_t.shape[0]
    nblk = t_all // tt
    head = pl.BlockSpec((P_HEADS, P_KEYS, tt), lambda i: (0, 0, i))
    hsd = lambda dt: jax.ShapeDtypeStruct((P_HEADS, P_KEYS, t_all), dt)
    wq_hi, wq_lo = _split_bf16(wq_t)
    k_hi, k_lo = _split_bf16(keys)
    wspec = pl.BlockSpec((nq, d), lambda i: (0, 0))
    kspec = pl.BlockSpec((P_HEADS, 2, P_KEYS, P_KEY_DIM), lambda i: (0, 0, 0, 0))
    return pl.pallas_call(
        _peer_query_kernel,
        grid=(nblk,),
        in_specs=[pl.BlockSpec((tt, d), lambda i: (i, 0)),
                  pl.BlockSpec((1, N_MOD, d), lambda i: (i // blocks_per_mod, 0, 0)),
                  pl.BlockSpec((1, d), lambda i: (0, 0)),
                  wspec, wspec, kspec, kspec],
        out_specs=[pl.BlockSpec((d, tt), lambda i: (0, i)), head, head, head, head],
        out_shape=[jax.ShapeDtypeStruct((d, t_all), BF16), hsd(F32), hsd(F32), hsd(BF16), hsd(BF16)],
        compiler_params=_params("arbitrary"),
        name="peer_query",
    )(xt, mod, gain.reshape(1, d), wq_hi, wq_lo, k_hi, k_lo)


def _twice_gelu(x):
    return x * (1.0 + lax.erf(x * (2.0 ** -0.5)))


def _peer_main_kernel(x_ref, mod_ref, ht_ref, u_ref, vt_ref, n_ref, c_ref, rank_ref, e2_ref,
                      o_ref, acc_ref, act_ref, coef_ref):
    g = pl.program_id(1)
    last = pl.num_programs(1) - 1
    tt = ht_ref.shape[1]

    @pl.when(g == 0)
    def _():
        acc_ref[...] = jnp.zeros_like(acc_ref)
        act_ref[...] = jnp.zeros_like(act_ref)
        coef_ref[...] = jnp.zeros_like(coef_ref)

    acc_ref[...] += _mm(vt_ref[...], coef_ref[...] * act_ref[...])

    blk = jnp.minimum(g, last - 1)
    rows_per_step = u_ref.shape[0] // P_KEYS
    tiles = P_KEYS // BF16_ROWS
    zero = jnp.zeros((BF16_ROWS, tt), BF16)
    for ii in range(rows_per_step):
        row = blk * rows_per_step + ii
        coef = [None] * tiles
        for h in range(P_HEADS):
            nb = jnp.broadcast_to(n_ref[h, pl.ds(row, 1), :], (BF16_ROWS, tt)).astype(BF16)
            cb = jnp.broadcast_to(c_ref[h, pl.ds(row, 1), :], (BF16_ROWS, tt)).astype(BF16)
            for k in range(tiles):
                rows = slice(k * BF16_ROWS, (k + 1) * BF16_ROWS)
                term = jnp.where(rank_ref[h, rows, :] < nb, e2_ref[h, rows, :] * cb, zero)
                coef[k] = term if coef[k] is None else coef[k] + term
        for k in range(tiles):
            rows = slice(ii * P_KEYS + k * BF16_ROWS, ii * P_KEYS + (k + 1) * BF16_ROWS)
            coef_ref[rows, :] = coef[k]
    act_ref[...] = _twice_gelu(_mm(u_ref[...], ht_ref[...])).astype(BF16)

    @pl.when(g == last)
    def _():
        o_ref[...] = x_ref[...] + mod_ref[0][5:6] * acc_ref[...].T


def _peer_main(xt, mod, ht, u_all, vt_all, layer, n, c, rank, e2, blocks_per_mod, tt):
    t_all, d = xt.shape
    n_exp = u_all.shape[1]
    eb = PEER_EXP
    nsteps = n_exp // eb
    head = pl.BlockSpec((P_HEADS, P_KEYS, tt), lambda i, s: (0, 0, i))
    return pl.pallas_call(
        _peer_main_kernel,
        grid=(t_all // tt, nsteps + 1),
        in_specs=[pl.BlockSpec((tt, d), lambda i, s: (i, 0)),
                  pl.BlockSpec((1, N_MOD, d), lambda i, s: (i // blocks_per_mod, 0, 0)),
                  pl.BlockSpec((d, tt), lambda i, s: (0, i)),
                  pl.BlockSpec((None, eb, d), lambda i, s: (layer, jnp.minimum(s, nsteps - 1), 0)),
                  pl.BlockSpec((None, d, eb), lambda i, s: (layer, 0, jnp.maximum(s - 1, 0))),
                  head, head, head, head],
        out_specs=pl.BlockSpec((tt, d), lambda i, s: (i, 0)),
        out_shape=jax.ShapeDtypeStruct((t_all, d), F32),
        scratch_shapes=[pltpu.VMEM((d, tt), F32), pltpu.VMEM((eb, tt), BF16), pltpu.VMEM((eb, tt), BF16)],
        compiler_params=_params("arbitrary", "arbitrary"),
        name="peer_main",
    )(xt, mod, ht, u_all, vt_all, n, c, rank, e2)


def _peer_layer(xt, mod, gain, w_q, keys, u_all, vt_all, layer, blocks_per_mod, tt):
    wq_t = w_q.T
    tq = PEER_QUERY_TOK
    ht, n, c, rank, e2 = _peer_query(xt, mod, gain, wq_t, keys, blocks_per_mod * (tt // tq), tq)
    return _peer_main(xt, mod, ht, u_all, vt_all, layer, n, c, rank, e2, blocks_per_mod, tt)


def kernel(x, c, ctx, c_ctx, mod_w, mod_b, norm1_g, norm2_g, ev_w_in, ev_w_out, attn_q_gain, attn_k_gain,
           rw_mu, rw_w0, rw_w2, rw_a0, rw_a2, rw_g2, rw_k_k, rw_k_a, rw_r_k, rw_gn_w, rw_gn_b, ssd_w_in,
           ssd_conv_w, ssd_conv_b, ssd_dt_bias, ssd_a_log, ssd_d, ssd_norm_w, ssd_w_out, peer_w_q,
           peer_keys, peer_u, peer_v):
    bsz, seq, dm = x.shape
    lc = ctx.shape[1]
    depth = mod_w.shape[0]
    assert lc == TOK_TILE and seq % PEER_TOK == 0 and (bsz * lc) % PEER_TOK == 0

    rows = -(-(bsz + 1) // 8) * 8
    cc = jnp.concatenate([c, c_ctx[None], jnp.zeros((rows - bsz - 1, dm), F32)], axis=0)
    mods = _modulation(cc, mod_w, mod_b)
    xc, xl = ctx, x
    u_all = peer_u.astype(BF16)
    vt_all = jnp.swapaxes(peer_v, 1, 2).astype(BF16)

    for i in range(depth):
        last = i == depth - 1
        j = i // 2
        ml = mods[i, :bsz].reshape(bsz, N_MOD, dm)
        mc = mods[i, bsz].reshape(1, N_MOD, dm)
        modx = jnp.stack([jnp.broadcast_to(mc, (bsz, N_MOD, dm)), ml], axis=1)
        if i % 2 == 0:
            za, zr = _in_proj(xc, xl, modx, norm1_g[i], ev_w_in[j].astype(BF16),
                              ((A_IN, F32), (B_IN, F32)))
            q, k, v = _attn_pre(za, attn_q_gain[j], attn_k_gain[j], lc)
            attn = _attention(q, k, v, lc)
            r, vv, kkn, g, bonus, lw, a, kd = _rwkv_pre(
                zr, rw_mu[j], rw_w0[j], rw_w2[j], rw_a0[j], rw_a2[j], rw_g2[j], rw_k_k[j], rw_k_a[j],
                rw_r_k[j], lc)
            wkv = _rwkv_chunks(r, vv, kkn, lw, a, kd, lc)
            xc, xl = _even_out(xc, xl, modx, attn, wkv, bonus, g, rw_gn_w[j], rw_gn_b[j],
                               ev_w_out[j].astype(BF16))
        else:
            n_heads = ssd_a_log.shape[-1]
            inner = n_heads * C_HEAD_DIM
            conv_dim = inner + 2 * C_GROUPS * C_STATE
            w_in = ssd_w_in[j]
            w_dt = jnp.zeros((dm, 2 * LANES), F32)
            w_dt = w_dt.at[:, :n_heads].set(w_in[:, inner + conv_dim:inner + conv_dim + n_heads])
            w_dt = w_dt.at[:, LANES:LANES + n_heads].set(w_in[:, inner + conv_dim + n_heads:])
            dtb = jnp.zeros((1, 2 * LANES), F32)
            dtb = dtb.at[0, :n_heads].set(ssd_dt_bias[j, 0]).at[0, LANES:LANES + n_heads].set(ssd_dt_bias[j, 1])
            w_all = jnp.concatenate([w_in[:, :inner + conv_dim], w_dt], axis=1).astype(BF16)
            zg, xbc, dtraw = _in_proj(xc, xl, modx, norm1_g[i], w_all,
                                      ((inner, BF16), (conv_dim, F32), (2 * LANES, F32)))
            xact, dtact = _ssd_pre(xbc, dtraw, ssd_conv_w[j], ssd_conv_b[j], dtb, lc)
            ys = _ssd_chunks(xact, dtact, ssd_a_log[j], lc)
            assert last, "the SSD layer keeps only the latent stream"
            xl = _odd_out(xl, modx, ys, xact, zg, ssd_d[j], ssd_norm_w[j], ssd_w_out[j].astype(BF16), lc)

        tt = PEER_TOK
        xl = _peer_layer(xl.reshape(bsz * seq, dm), ml, norm2_g[i], peer_w_q[i], peer_keys[i],
                         u_all, vt_all, i, seq // tt, tt).reshape(bsz, seq, dm)
        if not last:
            xc = _peer_layer(xc.reshape(bsz * lc, dm), mc, norm2_g[i], peer_w_q[i], peer_keys[i],
                             u_all, vt_all, i, (bsz * lc) // tt, tt).reshape(bsz, lc, dm)
    return xl
```

```python
import functools
import math

import numpy as np
import jax
import jax.numpy as jnp
from jax import lax
from jax.experimental import pallas as pl
from jax.experimental.pallas import tpu as pltpu

F32 = jnp.float32
BF16 = jnp.bfloat16
HI = lax.Precision.HIGHEST

GRID_W = 64
EPS = 1e-6
N_MOD = 6
A_HEADS, A_KV_HEADS, A_HEAD_DIM = 8, 2, 64
A_Q = A_HEADS * A_HEAD_DIM
A_KV = A_KV_HEADS * A_HEAD_DIM
A_IN = A_Q + 2 * A_KV
ROPE_THETA = 10000.0
B_HEADS, B_HEAD_DIM = 8, 64
B_WIDTH = B_HEADS * B_HEAD_DIM
B_DECAY_LORA, B_AAA_LORA, B_GATE_LORA = 64, 64, 128
B_IN = 3 * B_WIDTH + B_DECAY_LORA + B_AAA_LORA + B_GATE_LORA
B_GN_EPS = 64e-5
C_HEAD_DIM, C_GROUPS, C_STATE, C_CHUNK = 64, 4, 128, 128
P_HEADS, P_KEYS, P_KEY_DIM, P_TOPK = 8, 128, 128, 16

LANES = 128
BF16_ROWS = 16
TOK_TILE = 256
R_CHUNK = 64
PEER_TOK = 512
PEER_QUERY_TOK = 256
PEER_HEAD_GROUP = 2
PEER_EXP = 1024
PEER_COEF_TOK = 256
VMEM_LIMIT = 56 * 1024 * 1024


def _mm(a, b, prec=None):
    return lax.dot_general(a, b, (((1,), (0,)), ((), ())), precision=prec,
                           preferred_element_type=F32)


def _mm_nt(a, b, prec=None):
    return lax.dot_general(a, b, (((1,), (1,)), ((), ())), precision=prec,
                           preferred_element_type=F32)


def _mm_tn(a, b, prec=None):
    return lax.dot_general(a, b, (((0,), (0,)), ((), ())), precision=prec,
                           preferred_element_type=F32)


def _params(*sem):
    return pltpu.CompilerParams(dimension_semantics=sem, vmem_limit_bytes=VMEM_LIMIT)


def _softplus(x):
    return jnp.maximum(x, 0.0) + jnp.log1p(jnp.exp(-jnp.abs(x)))


def _sigmoid(x):
    return 1.0 / (1.0 + jnp.exp(-x))


def _silu(x):
    return x * _sigmoid(x)


def _mod_kernel(c_ref, w_ref, b_ref, o_ref):
    o_ref[0] = _mm(_silu(c_ref[...]), w_ref[0], HI) + b_ref[0]


def _modulation(cc, mod_w, mod_b):
    depth, d, n6 = mod_w.shape
    rows = cc.shape[0]
    tn = 1024
    return pl.pallas_call(
        _mod_kernel,
        grid=(depth, n6 // tn),
        in_specs=[pl.BlockSpec((rows, d), lambda l, n: (0, 0)),
                  pl.BlockSpec((1, d, tn), lambda l, n: (l, 0, n)),
                  pl.BlockSpec((1, 1, tn), lambda l, n: (l, 0, n))],
        out_specs=pl.BlockSpec((1, rows, tn), lambda l, n: (l, 0, n)),
        out_shape=jax.ShapeDtypeStruct((depth, rows, n6), F32),
        compiler_params=_params("arbitrary", "arbitrary"),
        name="modulation",
    )(cc, mod_w, mod_b.reshape(depth, 1, n6))


def _norm_mod(x, gain, shift, scale):
    ms = jnp.mean(x * x, axis=-1, keepdims=True)
    return x * lax.rsqrt(ms + EPS) * gain * (1.0 + scale) + shift


def _stream_specs(tm, d, lct):
    return [pl.BlockSpec((1, tm, d), lambda b, j: (b, jnp.minimum(j, lct - 1), 0)),
            pl.BlockSpec((1, tm, d), lambda b, j: (b, jnp.maximum(j - lct, 0), 0))]


def _stream_tile(xc_ref, xl_ref, lct):
    return jnp.where(pl.program_id(1) < lct, xc_ref[0], xl_ref[0])


def _in_proj_kernel(xc_ref, xl_ref, mod_ref, g_ref, w_ref, *o_refs, lct):
    mod = mod_ref[0, 0]
    h = _norm_mod(_stream_tile(xc_ref, xl_ref, lct), g_ref[...], mod[0:1], mod[1:2]).astype(BF16)
    off = 0
    for o_ref in o_refs:
        n = o_ref.shape[-1]
        for c0 in range(0, n, 1024):
            c1 = min(n, c0 + 1024)
            o_ref[0, :, c0:c1] = _mm(h, w_ref[:, off + c0:off + c1]).astype(o_ref.dtype)
        off += n


def _in_proj(xc, xl, modx, gain, w, splits):
    bsz, lc, d = xc.shape
    ln = lc + xl.shape[1]
    assert w.shape[1] == sum(n for n, _ in splits)
    tm = TOK_TILE
    lct = lc // tm
    return pl.pallas_call(
        functools.partial(_in_proj_kernel, lct=lct),
        grid=(bsz, ln // tm),
        in_specs=_stream_specs(tm, d, lct) + [
            pl.BlockSpec((1, 1, N_MOD, d), lambda b, j: (b, jnp.where(j < lct, 0, 1), 0, 0)),
            pl.BlockSpec((1, d), lambda b, j: (0, 0)),
            pl.BlockSpec(w.shape, lambda b, j: (0, 0))],
        out_specs=[pl.BlockSpec((1, tm, n), lambda b, j: (b, j, 0)) for n, _ in splits],
        out_shape=[jax.ShapeDtypeStruct((bsz, ln, n), dt) for n, dt in splits],
        compiler_params=_params("arbitrary", "arbitrary"),
        name="in_proj",
    )(xc, xl, modx, gain.reshape(1, d), w)


def _rope_tables(lc, seq):
    t = jnp.arange(seq)
    row = (t // GRID_W).astype(F32)
    col = (t % GRID_W).astype(F32)
    m = A_HEAD_DIM // 4
    inv = ROPE_THETA ** (-jnp.arange(m, dtype=F32) / m)
    ar, ac = row[:, None] * inv, col[:, None] * inv
    cos = jnp.concatenate([jnp.cos(ar), jnp.cos(ar), jnp.cos(ac), jnp.cos(ac)], axis=-1)
    sin = jnp.concatenate([-jnp.sin(ar), jnp.sin(ar), -jnp.sin(ac), jnp.sin(ac)], axis=-1)
    reps = LANES // A_HEAD_DIM
    cos = jnp.concatenate([jnp.ones((lc, A_HEAD_DIM), F32), cos], axis=0)
    sin = jnp.concatenate([jnp.zeros((lc, A_HEAD_DIM), F32), sin], axis=0)
    return jnp.tile(cos, (1, reps)), jnp.tile(sin, (1, reps))


def _head_block_ones(width, scale):
    idx = np.arange(LANES) // width
    return jnp.asarray((idx[:, None] == idx[None, :]).astype(np.float32) * scale)


def _attn_pre_kernel(z_ref, qg_ref, kg_ref, cos_ref, sin_ref, bones_ref, q_ref, k_ref, v_ref):
    tm = z_ref.shape[1]
    cs, sn = cos_ref[...], sin_ref[...]
    bones = bones_ref[...]
    lane = lax.broadcasted_iota(jnp.int32, (tm, LANES), 1)
    first = (lane % 32) < 16
    low = lane < A_HEAD_DIM

    def norm_rope(x, gain):
        ms = _bmm(x * x, bones)
        y = x * lax.rsqrt(ms + EPS) * gain
        partner = jnp.where(first, pltpu.roll(y, LANES - 16, 1), pltpu.roll(y, 16, 1))
        return y * cs + partner * sn

    heads_per_slab = LANES // A_HEAD_DIM
    group = A_HEADS // A_KV_HEADS
    for m in range(A_Q // LANES):
        sl = slice(m * LANES, (m + 1) * LANES)
        y = norm_rope(z_ref[0, :, sl], qg_ref[:, sl]) * (A_HEAD_DIM ** -0.5)
        swapped = pltpu.roll(y, A_HEAD_DIM, 1)
        for hh in range(heads_per_slab):
            h = m * heads_per_slab + hh
            kvh = h // group
            src = y if kvh == hh else swapped
            keep = low if kvh == 0 else jnp.logical_not(low)
            q_ref[0, h] = jnp.where(keep, src, 0.0).astype(BF16)
    k_ref[0] = norm_rope(z_ref[0, :, A_Q:A_Q + A_KV], kg_ref[...]).astype(BF16)
    v_ref[0] = z_ref[0, :, A_Q + A_KV:A_IN].astype(BF16)


def _attn_pre(za, q_gain, k_gain, lc):
    bsz, ln, _ = za.shape
    tm = TOK_TILE
    cos, sin = _rope_tables(lc, ln - lc)
    qg = jnp.tile(q_gain, A_HEADS).reshape(1, A_Q)
    kg = jnp.tile(k_gain, A_KV_HEADS).reshape(1, A_KV)
    bones = _head_block_ones(A_HEAD_DIM, 1.0 / A_HEAD_DIM)
    return pl.pallas_call(
        _attn_pre_kernel,
        grid=(bsz, ln // tm),
        in_specs=[pl.BlockSpec((1, tm, A_IN), lambda b, j: (b, j, 0)),
                  pl.BlockSpec((1, A_Q), lambda b, j: (0, 0)),
                  pl.BlockSpec((1, A_KV), lambda b, j: (0, 0)),
                  pl.BlockSpec((tm, LANES), lambda b, j: (j, 0)),
                  pl.BlockSpec((tm, LANES), lambda b, j: (j, 0)),
                  pl.BlockSpec((LANES, LANES), lambda b, j: (0, 0))],
        out_specs=[pl.BlockSpec((1, A_HEADS, tm, LANES), lambda b, j: (b, 0, j, 0)),
                   pl.BlockSpec((1, tm, A_KV), lambda b, j: (b, j, 0)),
                   pl.BlockSpec((1, tm, A_KV), lambda b, j: (b, j, 0))],
        out_shape=[jax.ShapeDtypeStruct((bsz, A_HEADS, ln, LANES), BF16),
                   jax.ShapeDtypeStruct((bsz, ln, A_KV), BF16),
                   jax.ShapeDtypeStruct((bsz, ln, A_KV), BF16)],
        compiler_params=_params("arbitrary", "arbitrary"),
        name="attn_pre",
    )(za, qg, kg, cos, sin, bones)


def _attn_kernel(q_ref, k_ref, v_ref, o_ref, *, lc):
    tq = q_ref.shape[2]
    j = pl.program_id(1)
    lane = lax.broadcasted_iota(jnp.int32, (tq, LANES), 1)
    low = lane < A_HEAD_DIM
    heads_per_slab = LANES // A_HEAD_DIM
    group = A_HEADS // A_KV_HEADS

    def run(klen):
        k = k_ref[0, :klen, :]
        v = v_ref[0, :klen, :]
        for m in range(A_Q // LANES):
            parts = []
            for hh in range(heads_per_slab):
                h = m * heads_per_slab + hh
                kvh = h // group
                s = _mm_nt(q_ref[0, h], k)
                p = jnp.exp(s - jnp.max(s, axis=-1, keepdims=True))
                l = jnp.sum(p, axis=-1, keepdims=True)
                o = _mm(p.astype(BF16), v) / l
                parts.append(o if kvh == hh else pltpu.roll(o, A_HEAD_DIM, 1))
            o_ref[0, :, m * LANES:(m + 1) * LANES] = jnp.where(low, parts[0], parts[1])

    @pl.when(j * tq < lc)
    def _():
        run(lc)

    @pl.when(j * tq >= lc)
    def _():
        run(k_ref.shape[1])


def _attention(q, k, v, lc):
    bsz, _, ln, _ = q.shape
    tq = TOK_TILE
    return pl.pallas_call(
        functools.partial(_attn_kernel, lc=lc),
        grid=(bsz, ln // tq),
        in_specs=[pl.BlockSpec((1, A_HEADS, tq, LANES), lambda b, j: (b, 0, j, 0)),
                  pl.BlockSpec((1, ln, A_KV), lambda b, j: (b, 0, 0)),
                  pl.BlockSpec((1, ln, A_KV), lambda b, j: (b, 0, 0))],
        out_specs=pl.BlockSpec((1, tq, A_Q), lambda b, j: (b, j, 0)),
        out_shape=jax.ShapeDtypeStruct((bsz, ln, A_Q), F32),
        compiler_params=_params("arbitrary", "arbitrary"),
        name="attention",
    )(q, k, v)


def _shifted_rows(z, prev_row, next_row):
    tm = z.shape[0]
    row = lax.broadcasted_iota(jnp.int32, (tm, 1), 0)
    zprev = jnp.where(row == 0, prev_row, pltpu.roll(z, 1, 0))
    znext = jnp.where(row == tm - 1, next_row, pltpu.roll(z, tm - 1, 0))
    return zprev, znext


def _halo_rows(zp_ref, zn_ref, j, lct, nj):
    prev_ok = jnp.logical_and(j != 0, j != lct)
    next_ok = jnp.logical_and(j != lct - 1, j != nj - 1)
    prev_row = jnp.where(prev_ok, zp_ref[0, 7:8, :], 0.0)
    next_row = jnp.where(next_ok, zn_ref[0, 0:1, :], 0.0)
    return prev_row, next_row


def _halo_specs(tm, width, ln):
    nb8 = tm // 8
    last = ln // 8 - 1
    return [pl.BlockSpec((1, tm, width), lambda b, j: (b, j, 0)),
            pl.BlockSpec((1, 8, width), lambda b, j: (b, jnp.maximum(j * nb8 - 1, 0), 0)),
            pl.BlockSpec((1, 8, width), lambda b, j: (b, jnp.minimum((j + 1) * nb8, last), 0))]


def _per_head_sum(x, bones):
    return jnp.concatenate(
        [_bmm(x[:, m * LANES:(m + 1) * LANES], bones) for m in range(x.shape[1] // LANES)], axis=1)


def _rwkv_pre_kernel(z_ref, zp_ref, zn_ref, mu_ref, w0_ref, w2_ref, a0_ref, a2_ref, g2_ref,
                     kk_ref, ka_ref, rk_ref, bones_ref,
                     r_o, v_o, kkn_o, g_o, bonus_o, lw_o, a_o, kd_o, *, lct):
    j = pl.program_id(1)
    z = z_ref[0]
    prev_row, next_row = _halo_rows(zp_ref, zn_ref, j, lct, pl.num_programs(1))
    zprev, znext = _shifted_rows(z, prev_row, next_row)
    zm = z + mu_ref[...] * (0.5 * (zprev + znext) - z)
    w = B_WIDTH
    r, k, v = zm[:, 0:w], zm[:, w:2 * w], zm[:, 2 * w:3 * w]
    wa = zm[:, 3 * w:3 * w + B_DECAY_LORA + B_AAA_LORA]
    gl = zm[:, 3 * w + B_DECAY_LORA + B_AAA_LORA:]
    bones = bones_ref[...]
    kk = k * kk_ref[...]
    kkn = kk / jnp.maximum(jnp.sqrt(_per_head_sum(kk * kk, bones)), 1e-12)
    r_o[0] = r
    v_o[0] = v
    kkn_o[0] = kkn
    g_o[0] = _bmm(_sigmoid(gl), g2_ref[...])
    twa = jnp.tanh(wa)
    bonus = jnp.zeros_like(r)
    for d in range(2):
        wraw = w0_ref[d] + _bmm(twa, w2_ref[d])
        lw_o[d, 0] = -jnp.exp(-_softplus(-wraw) - 0.5)
        a = _sigmoid(a0_ref[d] + _bmm(wa, a2_ref[d]))
        kd = k * (1.0 + (a - 1.0) * ka_ref[...])
        a_o[d, 0] = a
        kd_o[d, 0] = kd
        bonus = bonus + _per_head_sum(r * kd * rk_ref[...], bones) * v
    bonus_o[0] = bonus


def _rwkv_pre(zr, mu, w0, w2, a0, a2, g2, k_k, k_a, r_k, lc):
    bsz, ln, width = zr.shape
    tm = TOK_TILE
    w = B_WIDTH
    zeros = jnp.zeros((2, B_DECAY_LORA, w), F32)
    w2p = jnp.concatenate([w2, zeros], axis=1)
    a2p = jnp.concatenate([zeros, a2], axis=1)
    bones = _head_block_ones(B_HEAD_DIM, 1.0)
    full = lambda *shape: pl.BlockSpec(shape, lambda b, j: (0,) * len(shape))
    tok = pl.BlockSpec((1, tm, w), lambda b, j: (b, j, 0))
    tok2 = pl.BlockSpec((2, 1, tm, w), lambda b, j: (0, b, j, 0))
    sd = jax.ShapeDtypeStruct((bsz, ln, w), F32)
    sd2 = jax.ShapeDtypeStruct((2, bsz, ln, w), F32)
    return pl.pallas_call(
        functools.partial(_rwkv_pre_kernel, lct=lc // tm),
        grid=(bsz, ln // tm),
        in_specs=_halo_specs(tm, width, ln) + [
            full(1, width), full(2, 1, w), full(2, 2 * B_DECAY_LORA, w), full(2, 1, w),
            full(2, 2 * B_AAA_LORA, w), full(B_GATE_LORA, w), full(1, w), full(1, w), full(1, w),
            full(LANES, LANES)],
        out_specs=[tok, tok, tok, tok, tok, tok2, tok2, tok2],
        out_shape=[sd, sd, sd, sd, sd, sd2, sd2, sd2],
        compiler_params=_params("arbitrary", "arbitrary"),
        name="rwkv_pre",
    )(zr, zr, zr, mu.reshape(1, width), w0.reshape(2, 1, w), w2p, a0.reshape(2, 1, w), a2p, g2,
      k_k.reshape(1, w), k_a.reshape(1, w), r_k.reshape(1, w), bones)


def _scan_chunk(i, d, nctx, ntot):
    rev = jnp.where(i < nctx, nctx - 1 - i, ntot + nctx - 1 - i)
    return jnp.where(d == 0, i, rev)


def _order_masks(n, pair):
    t = np.arange(n * pair)
    same = (t[:, None] // n) == (t[None, :] // n)
    tt, ss = t[:, None] % n, t[None, :] % n
    incl = np.stack([same & (ss <= tt), same & (ss >= tt)]).astype(np.float32)
    strict = np.stack([same & (ss < tt), same & (ss > tt)]).astype(np.float32)
    return jnp.asarray(incl), jnp.asarray(strict)


def _pair_stack(x, low):
    return jnp.concatenate([jnp.where(low, x, 0.0), jnp.where(low, 0.0, x)], axis=0)


def _bmm(a, b):
    return _mm(a.astype(BF16), b.astype(BF16))


def _rwkv_chunk_kernel(rf_ref, vf_ref, kf_ref, lwf_ref, af_ref, kdf_ref,
                       rb_ref, vb_ref, kb_ref, lwb_ref, ab_ref, kdb_ref,
                       tri_ref, incl_ref, strict_ref, of_ref, ob_ref, state_ref):
    c = R_CHUNK
    n2 = 2 * c

    @pl.when(pl.program_id(1) == 0)
    def _():
        state_ref[...] = jnp.zeros_like(state_ref)

    low = lax.broadcasted_iota(jnp.int32, (c, LANES), 1) < B_HEAD_DIM
    eye = (lax.broadcasted_iota(jnp.int32, (LANES, LANES), 0)
           == lax.broadcasted_iota(jnp.int32, (LANES, LANES), 1)).astype(F32)
    stack = lambda x: _pair_stack(x, low)
    refs = ((rf_ref, vf_ref, kf_ref, lwf_ref, af_ref, kdf_ref, of_ref),
            (rb_ref, vb_ref, kb_ref, lwb_ref, ab_ref, kdb_ref, ob_ref))
    dirs, slabs, outs = [], [], []
    kts, bts, kds, rts, vs, pcs = [], [], [], [], [], []
    for d, (r_ref, v_ref, kkn_ref, lw_ref, a_ref, kd_ref, o_ref) in enumerate(refs):
        lw = lw_ref[0, 0]
        cum = _mm(tri_ref[d], lw, HI)
        p_in = jnp.exp(cum)
        p_inv = jnp.exp(-cum)
        p_ex = jnp.exp(cum - lw)
        p_all = jnp.exp(jnp.sum(lw, axis=0, keepdims=True))
        kkn = kkn_ref[0]
        kt = kkn * p_ex
        bt = kkn * a_ref[0, 0] * p_inv
        kd = kd_ref[0, 0] * p_inv
        rt = r_ref[0] * p_in
        v = v_ref[0]
        for m in range(B_WIDTH // LANES):
            sl = slice(m * LANES, (m + 1) * LANES)
            dirs.append(d)
            slabs.append(m)
            outs.append(o_ref)
            kts.append(stack(kt[:, sl]).astype(BF16))
            bts.append(stack(bt[:, sl]).astype(BF16))
            kds.append(stack(kd[:, sl]).astype(BF16))
            rts.append(stack(rt[:, sl]))
            vs.append(stack(v[:, sl]).astype(BF16))
            pcs.append(p_all[:, sl])
    n = len(dirs)
    incl = [incl_ref[d] > 0.5 for d in range(2)]
    strict = [strict_ref[d] > 0.5 for d in range(2)]
    gram = [_mm_nt(jnp.concatenate([kts[x], rts[x].astype(BF16)], axis=0),
                   jnp.concatenate([bts[x], kds[x]], axis=0)) for x in range(n)]
    a_b = [jnp.where(strict[dirs[x]], gram[x][:n2, :n2], 0.0) for x in range(n)]
    a_k = [jnp.where(strict[dirs[x]], gram[x][:n2, n2:], 0.0) for x in range(n)]
    m_b = [jnp.where(incl[dirs[x]], gram[x][n2:, :n2], 0.0) for x in range(n)]
    m_k = [jnp.where(incl[dirs[x]], gram[x][n2:, n2:], 0.0) for x in range(n)]
    pw = [-a for a in a_b]
    tinv = [eye + p for p in pw]
    for _ in range(int(math.log2(c)) - 1):
        pw = [_bmm(p, p) for p in pw]
        tinv = [t + _bmm(t, p) for t, p in zip(tinv, pw)]
    mv = [_bmm(jnp.concatenate([a_k[x], m_k[x]], axis=0), vs[x]) for x in range(n)]
    wy = [_bmm(tinv[x], jnp.concatenate([kts[x], mv[x][:n2].astype(BF16)], axis=1)) for x in range(n)]
    mbwy = [_bmm(m_b[x], wy[x]) for x in range(n)]
    wyb = [_mm_tn(wy[x].astype(BF16), bts[x]) for x in range(n)]
    vtk = [_mm_tn(vs[x], kds[x]) for x in range(n)]
    s0 = [state_ref[dirs[x], slabs[x]] for x in range(n)]
    o_st = [_mm_nt((rts[x] - mbwy[x][:, :LANES]).astype(BF16), s0[x].astype(BF16))
            + (mv[x][n2:] - mbwy[x][:, LANES:]) for x in range(n)]
    for x in range(n):
        g = (eye - wyb[x][:LANES]) * pcs[x]
        h = (vtk[x] - wyb[x][LANES:]) * pcs[x]
        outs[x][0, :, slabs[x] * LANES:(slabs[x] + 1) * LANES] = o_st[x][:c] + o_st[x][c:]
        state_ref[dirs[x], slabs[x]] = _bmm(s0[x], g) + h


def _rwkv_chunks(r, v, kkn, lw, a, kd, lc):
    bsz, ln, w = r.shape
    c = R_CHUNK
    nctx, ntot = lc // c, ln // c
    tri, _ = _order_masks(c, 1)
    incl, strict = _order_masks(c, 2)
    specs = []
    for d in range(2):
        tok = pl.BlockSpec((1, c, w), lambda b, i, d=d: (b, _scan_chunk(i, d, nctx, ntot), 0))
        tok2 = pl.BlockSpec((1, 1, c, w), lambda b, i, d=d: (d, b, _scan_chunk(i, d, nctx, ntot), 0))
        specs.append((tok, tok2))
    full = lambda *shape: pl.BlockSpec(shape, lambda b, i: (0,) * len(shape))
    sd = jax.ShapeDtypeStruct((bsz, ln, w), F32)
    return pl.pallas_call(
        _rwkv_chunk_kernel,
        grid=(bsz, ntot),
        in_specs=[specs[0][0]] * 3 + [specs[0][1]] * 3 + [specs[1][0]] * 3 + [specs[1][1]] * 3
        + [full(2, c, c), full(2, 2 * c, 2 * c), full(2, 2 * c, 2 * c)],
        out_specs=[specs[0][0], specs[1][0]],
        out_shape=[sd, sd],
        scratch_shapes=[pltpu.VMEM((2, w // LANES, LANES, LANES), F32)],
        compiler_params=_params("arbitrary", "arbitrary"),
        name="rwkv_chunks",
    )(r, v, kkn, lw, a, kd, r, v, kkn, lw, a, kd, tri, incl, strict)


def _even_out_kernel(xc_ref, xl_ref, mod_ref, attn_ref, wkvf_ref, wkvb_ref, bonus_ref, g_ref, gw_ref,
                     gb_ref, bones_ref, w_ref, oc_ref, ol_ref, *, lct):
    bones = bones_ref[...]
    y = bonus_ref[0]
    for wkv_ref in (wkvf_ref, wkvb_ref):
        o = wkv_ref[0]
        dlt = o - _per_head_sum(o, bones)
        var = _per_head_sum(dlt * dlt, bones)
        y = y + dlt * lax.rsqrt(var + B_GN_EPS) * gw_ref[...] + gb_ref[...]
    y = y * g_ref[0]
    mix = _mm(attn_ref[0].astype(BF16), w_ref[:A_Q, :]) + _mm(y.astype(BF16), w_ref[A_Q:, :])
    res = _stream_tile(xc_ref, xl_ref, lct) + mod_ref[0, 0][2:3] * mix
    is_ctx = pl.program_id(1) < lct

    @pl.when(is_ctx)
    def _():
        oc_ref[0] = res

    @pl.when(jnp.logical_not(is_ctx))
    def _():
        ol_ref[0] = res


def _even_out(xc, xl, modx, attn, wkv, bonus, g, gn_w, gn_b, w_out):
    bsz, lc, d = xc.shape
    seq = xl.shape[1]
    tm = TOK_TILE
    lct = lc // tm
    w = B_WIDTH
    bones = _head_block_ones(B_HEAD_DIM, 1.0 / B_HEAD_DIM)
    tokw = lambda n: pl.BlockSpec((1, tm, n), lambda b, j: (b, j, 0))
    full = lambda *shape: pl.BlockSpec(shape, lambda b, j: (0,) * len(shape))
    streams = _stream_specs(tm, d, lct)
    return pl.pallas_call(
        functools.partial(_even_out_kernel, lct=lct),
        grid=(bsz, (lc + seq) // tm),
        in_specs=streams + [
            pl.BlockSpec((1, 1, N_MOD, d), lambda b, j: (b, jnp.where(j < lct, 0, 1), 0, 0)),
            tokw(A_Q), tokw(w), tokw(w),
            tokw(w), tokw(w), full(1, w), full(1, w), full(LANES, LANES),
            full(A_Q + w, d)],
        out_specs=streams,
        out_shape=[jax.ShapeDtypeStruct((bsz, lc, d), F32), jax.ShapeDtypeStruct((bsz, seq, d), F32)],
        compiler_params=_params("arbitrary", "arbitrary"),
        name="even_out",
    )(xc, xl, modx, attn, wkv[0], wkv[1], bonus, g, gn_w.reshape(1, w), gn_b.reshape(1, w), bones, w_out)


def _ssd_pre_kernel(z_ref, zp_ref, zn_ref, dt_ref, cw_ref, cb_ref, dtb_ref, x_o, dt_o, *, lct):
    j = pl.program_id(1)
    z = z_ref[0]
    prev_row, next_row = _halo_rows(zp_ref, zn_ref, j, lct, pl.num_programs(1))
    zprev, znext = _shifted_rows(z, prev_row, next_row)
    y = zprev * cw_ref[0:1, :] + z * cw_ref[1:2, :] + znext * cw_ref[2:3, :] + cb_ref[...]
    x_o[0] = _silu(y).astype(x_o.dtype)
    dt_o[0] = _softplus(dt_ref[0] + dtb_ref[...])


def _ssd_pre(xbc, dtraw, conv_w, conv_b, dtb, lc):
    bsz, ln, width = xbc.shape
    tm = TOK_TILE
    nd = dtraw.shape[-1]
    full = lambda *shape: pl.BlockSpec(shape, lambda b, j: (0,) * len(shape))
    return pl.pallas_call(
        functools.partial(_ssd_pre_kernel, lct=lc // tm),
        grid=(bsz, ln // tm),
        in_specs=_halo_specs(tm, width, ln) + [
            pl.BlockSpec((1, tm, nd), lambda b, j: (b, j, 0)),
            full(conv_w.shape[0], width), full(1, width), full(1, nd)],
        out_specs=[pl.BlockSpec((1, tm, width), lambda b, j: (b, j, 0)),
                   pl.BlockSpec((1, tm, nd), lambda b, j: (b, j, 0))],
        out_shape=[jax.ShapeDtypeStruct((bsz, ln, width), BF16),
                   jax.ShapeDtypeStruct((bsz, ln, nd), F32)],
        compiler_params=_params("arbitrary", "arbitrary"),
        name="ssd_pre",
    )(xbc, xbc, xbc, dtraw, conv_w, conv_b.reshape(1, width), dtb)


def _ssd_chunk_kernel(x_ref, b_ref, c_ref, dt_ref, alog_ref, incl_ref, expand_ref, o_ref, state_ref,
                      *, n_heads):
    q = C_CHUNK
    i = pl.program_id(2)

    @pl.when(i == 0)
    def _():
        state_ref[...] = jnp.zeros_like(state_ref)

    lane = lax.broadcasted_iota(jnp.int32, (1, LANES), 1)
    a_neg = jnp.where(lane < n_heads, -jnp.exp(alog_ref[0]), 0.0)
    dt = dt_ref[0]
    dta = dt * a_neg
    incl_f = incl_ref[0]
    cum = _mm(incl_f, dta, HI)
    cum_t = cum.T
    total = jnp.sum(dta, axis=0, keepdims=True)
    expand = expand_ref[...]
    dt_x = _bmm(dt, expand)
    ec_x = _bmm(jnp.exp(cum), expand)
    de_x = _bmm(jnp.exp(total - cum), expand)
    cd_x = _bmm(jnp.broadcast_to(jnp.exp(total), (BF16_ROWS, LANES)), expand)[0:1]
    incl = incl_f > 0.5
    low = lax.broadcasted_iota(jnp.int32, (q, LANES), 1) < C_HEAD_DIM
    heads_per_group = n_heads // C_GROUPS
    pairs_per_group = heads_per_group // 2
    gw = heads_per_group * C_HEAD_DIM
    for g in range(C_GROUPS):
        gsl = slice(g * gw, (g + 1) * gw)
        bg = b_ref[0, :, g * C_STATE:(g + 1) * C_STATE].astype(BF16)
        cg = c_ref[0, :, g * C_STATE:(g + 1) * C_STATE].astype(BF16)
        cb = _mm_nt(cg, bg)
        xd = x_ref[0, :, gsl].astype(F32) * dt_x[:, gsl]
        s_prev = state_ref[g]
        y_off = _mm(cg, s_prev.astype(BF16)) * ec_x[:, gsl]
        state_ref[g] = s_prev * cd_x[:, gsl] + _mm_tn(bg, (xd * de_x[:, gsl]).astype(BF16))
        for pp in range(pairs_per_group):
            pr = g * pairs_per_group + pp
            sl = slice(pr * LANES, (pr + 1) * LANES)
            ms = []
            for hh in range(2):
                h = 2 * pr + hh
                seg = cum[:, h:h + 1] - cum_t[h:h + 1, :]
                ms.append((cb * jnp.exp(jnp.where(incl, seg, -1e30))).astype(BF16))
            xd_pair = _pair_stack(xd[:, pp * LANES:(pp + 1) * LANES], low).astype(BF16)
            y_diag = _mm(jnp.concatenate(ms, axis=1), xd_pair)
            o_ref[0, 0, :, sl] = (y_diag + y_off[:, pp * LANES:(pp + 1) * LANES]).astype(o_ref.dtype)


def _ssd_chunks(xact, dtact, a_log, lc):
    bsz, ln, _ = xact.shape
    q = C_CHUNK
    n_heads = a_log.shape[1]
    inner = n_heads * C_HEAD_DIM
    gn = C_GROUPS * C_STATE
    nctx, ntot = lc // q, ln // q
    incl, _ = _order_masks(q, 1)
    expand = np.zeros((LANES, inner), np.float32)
    expand[np.arange(inner) // C_HEAD_DIM, np.arange(inner)] = 1.0
    alog = jnp.zeros((2, 1, LANES), F32).at[:, 0, :n_heads].set(a_log)
    cmap = lambda off: (lambda b, d, i: (b, _scan_chunk(i, d, nctx, ntot), off))
    return pl.pallas_call(
        functools.partial(_ssd_chunk_kernel, n_heads=n_heads),
        grid=(bsz, 2, ntot),
        in_specs=[pl.BlockSpec((1, q, inner), cmap(0)),
                  pl.BlockSpec((1, q, gn), cmap(inner // gn)),
                  pl.BlockSpec((1, q, gn), cmap(inner // gn + 1)),
                  pl.BlockSpec((1, q, LANES), lambda b, d, i: (b, _scan_chunk(i, d, nctx, ntot), d)),
                  pl.BlockSpec((1, 1, LANES), lambda b, d, i: (d, 0, 0)),
                  pl.BlockSpec((1, q, q), lambda b, d, i: (d, 0, 0)),
                  pl.BlockSpec((LANES, inner), lambda b, d, i: (0, 0))],
        out_specs=pl.BlockSpec((1, 1, q, inner), lambda b, d, i: (d, b, _scan_chunk(i, d, nctx, ntot), 0)),
        out_shape=jax.ShapeDtypeStruct((2, bsz, ln, inner), BF16),
        scratch_shapes=[pltpu.VMEM((C_GROUPS, C_STATE, inner // C_GROUPS), F32)],
        compiler_params=_params("arbitrary", "arbitrary", "arbitrary"),
        name="ssd_chunks",
    )(xact, xact, xact, dtact, alog, incl, jnp.asarray(expand, dtype=BF16))


def _odd_out_kernel(x_ref, mod_ref, ys_ref, xs_ref, zg_ref, dsk_ref, nw_ref, w_ref, o_ref):
    y = (ys_ref[0, 0].astype(F32) + ys_ref[1, 0].astype(F32)) + dsk_ref[...] * xs_ref[0].astype(F32)
    yg = y * _silu(zg_ref[0].astype(F32))
    inner = yg.shape[1]
    gw = inner // C_GROUPS
    parts = []
    for g in range(C_GROUPS):
        t = yg[:, g * gw:(g + 1) * gw]
        ms = jnp.mean(t * t, axis=-1, keepdims=True)
        parts.append((t * lax.rsqrt(ms + EPS) * nw_ref[:, g * gw:(g + 1) * gw]).astype(BF16))
    yn = jnp.concatenate(parts, axis=1)
    o_ref[0] = x_ref[0] + mod_ref[0, 0][2:3] * _mm(yn, w_ref[...])


def _odd_out(x, modx, ys, xact, zg, d_skip, norm_w, w_out, lc):
    bsz, seq, d = x.shape
    ln = lc + seq
    tm = TOK_TILE
    lct = lc // tm
    inner = zg.shape[-1]
    dsk = jnp.repeat(d_skip, C_HEAD_DIM).reshape(1, inner)
    tokw = lambda n: pl.BlockSpec((1, tm, n), lambda b, j: (b, j + lct, 0))
    full = lambda *shape: pl.BlockSpec(shape, lambda b, j: (0,) * len(shape))
    return pl.pallas_call(
        _odd_out_kernel,
        grid=(bsz, (ln - lc) // tm),
        in_specs=[pl.BlockSpec((1, tm, d), lambda b, j: (b, j, 0)),
                  pl.BlockSpec((1, 1, N_MOD, d), lambda b, j: (b, 1, 0, 0)),
                  pl.BlockSpec((2, 1, tm, inner), lambda b, j: (0, b, j + lct, 0)),
                  tokw(inner), tokw(inner), full(1, inner), full(1, inner), full(inner, d)],
        out_specs=pl.BlockSpec((1, tm, d), lambda b, j: (b, j, 0)),
        out_shape=jax.ShapeDtypeStruct((bsz, ln - lc, d), F32),
        compiler_params=_params("arbitrary", "arbitrary"),
        name="odd_out",
    )(x, modx, ys, xact, zg, dsk, norm_w.reshape(1, inner), w_out)


class _Group:
    def __init__(self, parts):
        self.parts = tuple(parts)


def _ew(f, *args):
    groups = [a for a in args if isinstance(a, _Group)]
    if not groups:
        return f(*args)
    return _Group(f(*[a.parts[m] if isinstance(a, _Group) else a for a in args])
                  for m in range(len(groups[0].parts)))


def _exchange(xs, i, l, descending=True):
    hi, lo = _ew(jnp.maximum, xs[i], xs[l]), _ew(jnp.minimum, xs[i], xs[l])
    xs[i], xs[l] = (hi, lo) if descending else (lo, hi)


def _bitonic_merge(xs):
    xs = list(xs)
    j = len(xs) // 2
    while j >= 1:
        for i in range(len(xs)):
            if i ^ j > i:
                _exchange(xs, i, i ^ j)
        j //= 2
    return xs


def _bitonic_sort(xs):
    xs = list(xs)
    k = 2
    while k <= len(xs):
        j = k // 2
        while j >= 1:
            for i in range(len(xs)):
                if i ^ j > i:
                    _exchange(xs, i, i ^ j, descending=(i & k) == 0)
            j //= 2
        k *= 2
    return xs


def _merge_sublanes(xs):
    n = len(xs)
    for shift in (4, 2, 1):
        other = [_ew(lambda v: pltpu.roll(v, shift, 0), x) for x in xs]
        xs = _bitonic_merge([_ew(jnp.maximum, xs[k], other[n - 1 - k]) for k in range(n)])
    return xs


def _row_tile(s, k):
    return _ew(lambda v: v[8 * k:8 * k + 8], s)


def _top_values(s):
    return _merge_sublanes(_bitonic_sort([_row_tile(s, k) for k in range(P_KEYS // 8)]))


def _top_pair_sums(t1, t2, shape):
    sub = lax.broadcasted_iota(jnp.int32, shape, 0)
    lo, hi = t2[0], t2[8]
    for b in range(1, 8):
        lo = _ew(lambda x, y: jnp.where(sub == b, x, y), t2[b], lo)
        hi = _ew(lambda x, y: jnp.where(sub == b, x, y), t2[8 + b], hi)
    xs = [_ew(jnp.add, t1[a], lo) for a in range(P_TOPK)]
    carry = _ew(jnp.add, t1[0], hi)
    for k in range(P_TOPK):
        xs[k], carry = _ew(jnp.maximum, xs[k], carry), _ew(jnp.minimum, xs[k], carry)
    return _merge_sublanes(xs)


def _prefix_count(test, vals):
    sel = functools.partial(_ew, jnp.where)
    b3 = test(vals[7])
    b2 = test(sel(b3, vals[11], vals[3]))
    b1 = test(sel(b3, sel(b2, vals[13], vals[9]), sel(b2, vals[5], vals[1])))
    b0 = test(sel(b3, sel(b2, sel(b1, vals[14], vals[12]), sel(b1, vals[10], vals[8])),
                  sel(b2, sel(b1, vals[6], vals[4]), sel(b1, vals[2], vals[0]))))
    n = _ew(lambda p3, p2, p1, p0: (jnp.where(p3, 8.0, 0.0) + jnp.where(p2, 4.0, 0.0))
            + (jnp.where(p1, 2.0, 0.0) + jnp.where(p0, 1.0, 0.0)), b3, b2, b1, b0)
    return _ew(lambda full, m: jnp.where(full, 16.0, m), test(vals[15]), n)


def _peer_query_kernel(x_ref, mod_ref, g_ref, wq_ref, keys_ref, ht_o, n_o, c_o, rank_o, e2_o):
    assert P_TOPK == 16 and P_KEYS == 128
    mod = mod_ref[0]
    h2 = _norm_mod(x_ref[...], g_ref[...], mod[3:4], mod[4:5])
    ht = h2.T.astype(BF16)
    ht_o[...] = ht
    qt = _mm(wq_ref[...], ht).astype(BF16)
    kd = P_KEY_DIM
    tile_shape = (8, ht.shape[1])
    for h0 in range(0, P_HEADS, PEER_HEAD_GROUP):
        heads = range(h0, h0 + PEER_HEAD_GROUP)
        s1 = _Group(_mm(keys_ref[h, 0], qt[(2 * h) * kd:(2 * h + 1) * kd]) for h in heads)
        s2 = _Group(_mm(keys_ref[h, 1], qt[(2 * h + 1) * kd:(2 * h + 2) * kd]) for h in heads)
        t1 = _top_values(s1)
        t2 = _top_values(s2)
        best = _top_pair_sums(t1, t2, tile_shape)
        z = _ew(jnp.ones_like, best[0])
        for k in range(1, P_TOPK):
            z = _ew(lambda acc, v, top: acc + jnp.exp(v - top), z, best[k], best[0])
        inv_z = _ew(lambda v: 0.5 / v, z)
        tau = best[P_TOPK - 1]
        ranks, e2s = [], []
        for k in range(P_KEYS // 8):
            rows = slice(8 * k, 8 * k + 8)
            s1k, s2k = _row_tile(s1, k), _row_tile(s2, k)
            nk = _prefix_count(lambda v: _ew(lambda s, x, t: (s + x) >= t, s1k, v, tau), t2)
            ck = _ew(lambda s, top, iz: jnp.exp(s - top) * iz, s1k, t1[0], inv_z)
            ranks.append(_prefix_count(lambda v: _ew(lambda s, x: x > s, s2k, v), t2))
            e2s.append(_ew(lambda s, top: jnp.exp(s - top), s2k, t2[0]))
            for m, h in enumerate(heads):
                n_o[h, rows] = nk.parts[m]
                c_o[h, rows] = ck.parts[m]
        for m, h in enumerate(heads):
            rank_o[h] = jnp.concatenate([r.parts[m] for r in ranks], axis=0).astype(BF16)
            e2_o[h] = jnp.concatenate([e.parts[m] for e in e2s], axis=0).astype(BF16)


def _peer_query(xt, mod, gain, wq_t, keys, blocks_per_mod, tt):
    t_all, d = xt.shape
    nq = wq_t.shape[0]
    nblk = t_all // tt
    head = pl.BlockSpec((P_HEADS, P_KEYS, tt), lambda i: (0, 0, i))
    hsd = lambda dt: jax.ShapeDtypeStruct((P_HEADS, P_KEYS, t_all), dt)
    wspec = pl.BlockSpec((nq, d), lambda i: (0, 0))
    kspec = pl.BlockSpec((P_HEADS, 2, P_KEYS, P_KEY_DIM), lambda i: (0, 0, 0, 0))
    return pl.pallas_call(
        _peer_query_kernel,
        grid=(nblk,),
        in_specs=[pl.BlockSpec((tt, d), lambda i: (i, 0)),
                  pl.BlockSpec((1, N_MOD, d), lambda i: (i // blocks_per_mod, 0, 0)),
                  pl.BlockSpec((1, d), lambda i: (0, 0)),
                  wspec, kspec],
        out_specs=[pl.BlockSpec((d, tt), lambda i: (0, i)), head, head, head, head],
        out_shape=[jax.ShapeDtypeStruct((d, t_all), BF16), hsd(F32), hsd(F32), hsd(BF16), hsd(BF16)],
        compiler_params=_params("arbitrary"),
        name="peer_query",
    )(xt, mod, gain.reshape(1, d), wq_t.astype(BF16), keys.astype(BF16))


def _twice_gelu(x):
    return x * (1.0 + lax.erf(x * (2.0 ** -0.5)))


def _peer_main_kernel(x_ref, mod_ref, ht_ref, u_ref, vt_ref, n_ref, c_ref, rank_ref, e2_ref,
                      o_ref, acc_ref, act_ref, coef_ref):
    g = pl.program_id(1)
    last = pl.num_programs(1) - 1
    tt = ht_ref.shape[1]

    @pl.when(g == 0)
    def _():
        acc_ref[...] = jnp.zeros_like(acc_ref)
        act_ref[...] = jnp.zeros_like(act_ref)
        coef_ref[...] = jnp.zeros_like(coef_ref)

    acc_ref[...] += _mm(vt_ref[...], coef_ref[...] * act_ref[...])

    blk = jnp.minimum(g, last - 1)
    rows_per_step = u_ref.shape[0] // P_KEYS
    tiles = P_KEYS // BF16_ROWS
    tw = PEER_COEF_TOK
    zero = jnp.zeros((BF16_ROWS, tw), BF16)
    for ii in range(rows_per_step):
        row = blk * rows_per_step + ii
        for t0 in range(0, tt, tw):
            tok = slice(t0, t0 + tw)
            coef = [None] * tiles
            for h in range(P_HEADS):
                nb = jnp.broadcast_to(n_ref[h, pl.ds(row, 1), tok], (BF16_ROWS, tw)).astype(BF16)
                cb = jnp.broadcast_to(c_ref[h, pl.ds(row, 1), tok], (BF16_ROWS, tw)).astype(BF16)
                for k in range(tiles):
                    rows = slice(k * BF16_ROWS, (k + 1) * BF16_ROWS)
                    term = jnp.where(rank_ref[h, rows, tok] < nb, e2_ref[h, rows, tok] * cb, zero)
                    coef[k] = term if coef[k] is None else coef[k] + term
            for k in range(tiles):
                rows = slice(ii * P_KEYS + k * BF16_ROWS, ii * P_KEYS + (k + 1) * BF16_ROWS)
                coef_ref[rows, tok] = coef[k]
    act_ref[...] = _twice_gelu(_mm(u_ref[...], ht_ref[...])).astype(BF16)

    @pl.when(g == last)
    def _():
        o_ref[...] = x_ref[...] + mod_ref[0][5:6] * acc_ref[...].T


def _peer_main(xt, mod, ht, u_all, vt_all, layer, n, c, rank, e2, blocks_per_mod, tt):
    t_all, d = xt.shape
    n_exp = u_all.shape[1]
    eb = PEER_EXP
    nsteps = n_exp // eb
    head = pl.BlockSpec((P_HEADS, P_KEYS, tt), lambda i, s: (0, 0, i))
    return pl.pallas_call(
        _peer_main_kernel,
        grid=(t_all // tt, nsteps + 1),
        in_specs=[pl.BlockSpec((tt, d), lambda i, s: (i, 0)),
                  pl.BlockSpec((1, N_MOD, d), lambda i, s: (i // blocks_per_mod, 0, 0)),
                  pl.BlockSpec((d, tt), lambda i, s: (0, i)),
                  pl.BlockSpec((None, eb, d), lambda i, s: (layer, jnp.minimum(s, nsteps - 1), 0)),
                  pl.BlockSpec((None, d, eb), lambda i, s: (layer, 0, jnp.maximum(s - 1, 0))),
                  head, head, head, head],
        out_specs=pl.BlockSpec((tt, d), lambda i, s: (i, 0)),
        out_shape=jax.ShapeDtypeStruct((t_all, d), F32),
        scratch_shapes=[pltpu.VMEM((d, tt), F32), pltpu.VMEM((eb, tt), BF16), pltpu.VMEM((eb, tt), BF16)],
        compiler_params=_params("arbitrary", "arbitrary"),
        name="peer_main",
    )(xt, mod, ht, u_all, vt_all, n, c, rank, e2)


def _peer_layer(xt, mod, gain, w_q, keys, u_all, vt_all, layer, blocks_per_mod, tt):
    wq_t = w_q.T
    tq = PEER_QUERY_TOK
    ht, n, c, rank, e2 = _peer_query(xt, mod, gain, wq_t, keys, blocks_per_mod * (tt // tq), tq)
    return _peer_main(xt, mod, ht, u_all, vt_all, layer, n, c, rank, e2, blocks_per_mod, tt)


def kernel(x, c, ctx, c_ctx, mod_w, mod_b, norm1_g, norm2_g, ev_w_in, ev_w_out, attn_q_gain, attn_k_gain,
           rw_mu, rw_w0, rw_w2, rw_a0, rw_a2, rw_g2, rw_k_k, rw_k_a, rw_r_k, rw_gn_w, rw_gn_b, ssd_w_in,
           ssd_conv_w, ssd_conv_b, ssd_dt_bias, ssd_a_log, ssd_d, ssd_norm_w, ssd_w_out, peer_w_q,
           peer_keys, peer_u, peer_v):
    bsz, seq, dm = x.shape
    lc = ctx.shape[1]
    depth = mod_w.shape[0]
    assert lc == TOK_TILE and seq % PEER_TOK == 0 and (bsz * lc) % PEER_TOK == 0

    rows = -(-(bsz + 1) // 8) * 8
    cc = jnp.concatenate([c, c_ctx[None], jnp.zeros((rows - bsz - 1, dm), F32)], axis=0)
    mods = _modulation(cc, mod_w, mod_b)
    xc, xl = ctx, x
    u_all = peer_u.astype(BF16)
    vt_all = jnp.swapaxes(peer_v, 1, 2).astype(BF16)

    for i in range(depth):
        last = i == depth - 1
        j = i // 2
        ml = mods[i, :bsz].reshape(bsz, N_MOD, dm)
        mc = mods[i, bsz].reshape(1, N_MOD, dm)
        modx = jnp.stack([jnp.broadcast_to(mc, (bsz, N_MOD, dm)), ml], axis=1)
        if i % 2 == 0:
            za, zr = _in_proj(xc, xl, modx, norm1_g[i], ev_w_in[j].astype(BF16),
                              ((A_IN, F32), (B_IN, F32)))
            q, k, v = _attn_pre(za, attn_q_gain[j], attn_k_gain[j], lc)
            attn = _attention(q, k, v, lc)
            r, vv, kkn, g, bonus, lw, a, kd = _rwkv_pre(
                zr, rw_mu[j], rw_w0[j], rw_w2[j], rw_a0[j], rw_a2[j], rw_g2[j], rw_k_k[j], rw_k_a[j],
                rw_r_k[j], lc)
            wkv = _rwkv_chunks(r, vv, kkn, lw, a, kd, lc)
            xc, xl = _even_out(xc, xl, modx, attn, wkv, bonus, g, rw_gn_w[j], rw_gn_b[j],
                               ev_w_out[j].astype(BF16))
        else:
            n_heads = ssd_a_log.shape[-1]
            inner = n_heads * C_HEAD_DIM
            conv_dim = inner + 2 * C_GROUPS * C_STATE
            w_in = ssd_w_in[j]
            w_dt = jnp.zeros((dm, 2 * LANES), F32)
            w_dt = w_dt.at[:, :n_heads].set(w_in[:, inner + conv_dim:inner + conv_dim + n_heads])
            w_dt = w_dt.at[:, LANES:LANES + n_heads].set(w_in[:, inner + conv_dim + n_heads:])
            dtb = jnp.zeros((1, 2 * LANES), F32)
            dtb = dtb.at[0, :n_heads].set(ssd_dt_bias[j, 0]).at[0, LANES:LANES + n_heads].set(ssd_dt_bias[j, 1])
            w_all = jnp.concatenate([w_in[:, :inner + conv_dim], w_dt], axis=1).astype(BF16)
            zg, xbc, dtraw = _in_proj(xc, xl, modx, norm1_g[i], w_all,
                                      ((inner, BF16), (conv_dim, F32), (2 * LANES, F32)))
            xact, dtact = _ssd_pre(xbc, dtraw, ssd_conv_w[j], ssd_conv_b[j], dtb, lc)
            ys = _ssd_chunks(xact, dtact, ssd_a_log[j], lc)
            assert last, "the SSD layer keeps only the latent stream"
            xl = _odd_out(xl, modx, ys, xact, zg, ssd_d[j], ssd_norm_w[j], ssd_w_out[j].astype(BF16), lc)

        tt = PEER_TOK
        xl = _peer_layer(xl.reshape(bsz * seq, dm), ml, norm2_g[i], peer_w_q[i], peer_keys[i],
                         u_all, vt_all, i, seq // tt, tt).reshape(bsz, seq, dm)
        if not last:
            xc = _peer_layer(xc.reshape(bsz * lc, dm), mc, norm2_g[i], peer_w_q[i], peer_keys[i],
                             u_all, vt_all, i, (bsz * lc) // tt, tt).reshape(bsz, lc, dm)
    return xl
```

```python
import functools
import math

import numpy as np
import jax
import jax.numpy as jnp
from jax import lax
from jax.experimental import pallas as pl
from jax.experimental.pallas import tpu as pltpu

F32 = jnp.float32
BF16 = jnp.bfloat16
HI = lax.Precision.HIGHEST

GRID_W = 64
EPS = 1e-6
N_MOD = 6
A_HEADS, A_KV_HEADS, A_HEAD_DIM = 8, 2, 64
A_Q = A_HEADS * A_HEAD_DIM
A_KV = A_KV_HEADS * A_HEAD_DIM
A_IN = A_Q + 2 * A_KV
ROPE_THETA = 10000.0
B_HEADS, B_HEAD_DIM = 8, 64
B_WIDTH = B_HEADS * B_HEAD_DIM
B_DECAY_LORA, B_AAA_LORA, B_GATE_LORA = 64, 64, 128
B_IN = 3 * B_WIDTH + B_DECAY_LORA + B_AAA_LORA + B_GATE_LORA
B_GN_EPS = 64e-5
C_HEAD_DIM, C_GROUPS, C_STATE, C_CHUNK = 64, 4, 128, 128
P_HEADS, P_KEYS, P_KEY_DIM, P_TOPK = 8, 128, 128, 16

LANES = 128
BF16_ROWS = 16
TOK_TILE = 256
R_CHUNK = 64
PEER_TOK = 512
PEER_QUERY_TOK = 256
PEER_HEAD_GROUP = 2
PEER_EXP = 1024
PEER_COEF_TOK = 256
VMEM_LIMIT = 56 * 1024 * 1024


def _mm(a, b, prec=None):
    return lax.dot_general(a, b, (((1,), (0,)), ((), ())), precision=prec,
                           preferred_element_type=F32)


def _mm_nt(a, b, prec=None):
    return lax.dot_general(a, b, (((1,), (1,)), ((), ())), precision=prec,
                           preferred_element_type=F32)


def _mm_tn(a, b, prec=None):
    return lax.dot_general(a, b, (((0,), (0,)), ((), ())), precision=prec,
                           preferred_element_type=F32)


def _params(*sem):
    return pltpu.CompilerParams(dimension_semantics=sem, vmem_limit_bytes=VMEM_LIMIT)


def _softplus(x):
    return jnp.maximum(x, 0.0) + jnp.log1p(jnp.exp(-jnp.abs(x)))


def _sigmoid(x):
    return 1.0 / (1.0 + jnp.exp(-x))


def _silu(x):
    return x * _sigmoid(x)


def _mod_kernel(c_ref, w_ref, b_ref, o_ref):
    o_ref[0] = _mm(_silu(c_ref[...]), w_ref[0], HI) + b_ref[0]


def _modulation(cc, mod_w, mod_b):
    depth, d, n6 = mod_w.shape
    rows = cc.shape[0]
    tn = 1024
    return pl.pallas_call(
        _mod_kernel,
        grid=(depth, n6 // tn),
        in_specs=[pl.BlockSpec((rows, d), lambda l, n: (0, 0)),
                  pl.BlockSpec((1, d, tn), lambda l, n: (l, 0, n)),
                  pl.BlockSpec((1, 1, tn), lambda l, n: (l, 0, n))],
        out_specs=pl.BlockSpec((1, rows, tn), lambda l, n: (l, 0, n)),
        out_shape=jax.ShapeDtypeStruct((depth, rows, n6), F32),
        compiler_params=_params("arbitrary", "arbitrary"),
        name="modulation",
    )(cc, mod_w, mod_b.reshape(depth, 1, n6))


def _norm_mod(x, gain, shift, scale):
    ms = jnp.mean(x * x, axis=-1, keepdims=True)
    return x * lax.rsqrt(ms + EPS) * gain * (1.0 + scale) + shift


def _stream_specs(tm, d, lct):
    return [pl.BlockSpec((1, tm, d), lambda b, j: (b, jnp.minimum(j, lct - 1), 0)),
            pl.BlockSpec((1, tm, d), lambda b, j: (b, jnp.maximum(j - lct, 0), 0))]


def _stream_tile(xc_ref, xl_ref, lct):
    return jnp.where(pl.program_id(1) < lct, xc_ref[0], xl_ref[0])


def _in_proj_kernel(xc_ref, xl_ref, mod_ref, g_ref, w_ref, *o_refs, lct):
    mod = mod_ref[0, 0]
    h = _norm_mod(_stream_tile(xc_ref, xl_ref, lct), g_ref[...], mod[0:1], mod[1:2]).astype(BF16)
    off = 0
    for o_ref in o_refs:
        n = o_ref.shape[-1]
        for c0 in range(0, n, 1024):
            c1 = min(n, c0 + 1024)
            o_ref[0, :, c0:c1] = _mm(h, w_ref[:, off + c0:off + c1]).astype(o_ref.dtype)
        off += n


def _in_proj(xc, xl, modx, gain, w, splits):
    bsz, lc, d = xc.shape
    ln = lc + xl.shape[1]
    assert w.shape[1] == sum(n for n, _ in splits)
    tm = TOK_TILE
    lct = lc // tm
    return pl.pallas_call(
        functools.partial(_in_proj_kernel, lct=lct),
        grid=(bsz, ln // tm),
        in_specs=_stream_specs(tm, d, lct) + [
            pl.BlockSpec((1, 1, N_MOD, d), lambda b, j: (b, jnp.where(j < lct, 0, 1), 0, 0)),
            pl.BlockSpec((1, d), lambda b, j: (0, 0)),
            pl.BlockSpec(w.shape, lambda b, j: (0, 0))],
        out_specs=[pl.BlockSpec((1, tm, n), lambda b, j: (b, j, 0)) for n, _ in splits],
        out_shape=[jax.ShapeDtypeStruct((bsz, ln, n), dt) for n, dt in splits],
        compiler_params=_params("arbitrary", "arbitrary"),
        name="in_proj",
    )(xc, xl, modx, gain.reshape(1, d), w)


def _rope_tables(lc, seq):
    t = jnp.arange(seq)
    row = (t // GRID_W).astype(F32)
    col = (t % GRID_W).astype(F32)
    m = A_HEAD_DIM // 4
    inv = ROPE_THETA ** (-jnp.arange(m, dtype=F32) / m)
    ar, ac = row[:, None] * inv, col[:, None] * inv
    cos = jnp.concatenate([jnp.cos(ar), jnp.cos(ar), jnp.cos(ac), jnp.cos(ac)], axis=-1)
    sin = jnp.concatenate([-jnp.sin(ar), jnp.sin(ar), -jnp.sin(ac), jnp.sin(ac)], axis=-1)
    reps = LANES // A_HEAD_DIM
    cos = jnp.concatenate([jnp.ones((lc, A_HEAD_DIM), F32), cos], axis=0)
    sin = jnp.concatenate([jnp.zeros((lc, A_HEAD_DIM), F32), sin], axis=0)
    return jnp.tile(cos, (1, reps)), jnp.tile(sin, (1, reps))


def _head_block_ones(width, scale):
    idx = np.arange(LANES) // width
    return jnp.asarray((idx[:, None] == idx[None, :]).astype(np.float32) * scale)


def _attn_pre_kernel(z_ref, qg_ref, kg_ref, cos_ref, sin_ref, bones_ref, q_ref, k_ref, v_ref):
    tm = z_ref.shape[1]
    cs, sn = cos_ref[...], sin_ref[...]
    bones = bones_ref[...]
    lane = lax.broadcasted_iota(jnp.int32, (tm, LANES), 1)
    first = (lane % 32) < 16
    low = lane < A_HEAD_DIM

    def norm_rope(x, gain):
        ms = _bmm(x * x, bones)
        y = x * lax.rsqrt(ms + EPS) * gain
        partner = jnp.where(first, pltpu.roll(y, LANES - 16, 1), pltpu.roll(y, 16, 1))
        return y * cs + partner * sn

    heads_per_slab = LANES // A_HEAD_DIM
    group = A_HEADS // A_KV_HEADS
    for m in range(A_Q // LANES):
        sl = slice(m * LANES, (m + 1) * LANES)
        y = norm_rope(z_ref[0, :, sl], qg_ref[:, sl]) * (A_HEAD_DIM ** -0.5)
        swapped = pltpu.roll(y, A_HEAD_DIM, 1)
        for hh in range(heads_per_slab):
            h = m * heads_per_slab + hh
            kvh = h // group
            src = y if kvh == hh else swapped
            keep = low if kvh == 0 else jnp.logical_not(low)
            q_ref[0, h] = jnp.where(keep, src, 0.0).astype(BF16)
    k_ref[0] = norm_rope(z_ref[0, :, A_Q:A_Q + A_KV], kg_ref[...]).astype(BF16)
    v_ref[0] = z_ref[0, :, A_Q + A_KV:A_IN].astype(BF16)


def _attn_pre(za, q_gain, k_gain, lc):
    bsz, ln, _ = za.shape
    tm = TOK_TILE
    cos, sin = _rope_tables(lc, ln - lc)
    qg = jnp.tile(q_gain, A_HEADS).reshape(1, A_Q)
    kg = jnp.tile(k_gain, A_KV_HEADS).reshape(1, A_KV)
    bones = _head_block_ones(A_HEAD_DIM, 1.0 / A_HEAD_DIM)
    return pl.pallas_call(
        _attn_pre_kernel,
        grid=(bsz, ln // tm),
        in_specs=[pl.BlockSpec((1, tm, A_IN), lambda b, j: (b, j, 0)),
                  pl.BlockSpec((1, A_Q), lambda b, j: (0, 0)),
                  pl.BlockSpec((1, A_KV), lambda b, j: (0, 0)),
                  pl.BlockSpec((tm, LANES), lambda b, j: (j, 0)),
                  pl.BlockSpec((tm, LANES), lambda b, j: (j, 0)),
                  pl.BlockSpec((LANES, LANES), lambda b, j: (0, 0))],
        out_specs=[pl.BlockSpec((1, A_HEADS, tm, LANES), lambda b, j: (b, 0, j, 0)),
                   pl.BlockSpec((1, tm, A_KV), lambda b, j: (b, j, 0)),
                   pl.BlockSpec((1, tm, A_KV), lambda b, j: (b, j, 0))],
        out_shape=[jax.ShapeDtypeStruct((bsz, A_HEADS, ln, LANES), BF16),
                   jax.ShapeDtypeStruct((bsz, ln, A_KV), BF16),
                   jax.ShapeDtypeStruct((bsz, ln, A_KV), BF16)],
        compiler_params=_params("arbitrary", "arbitrary"),
        name="attn_pre",
    )(za, qg, kg, cos, sin, bones)


def _attn_kernel(q_ref, k_ref, v_ref, o_ref, *, lc):
    tq = q_ref.shape[2]
    j = pl.program_id(1)
    lane = lax.broadcasted_iota(jnp.int32, (tq, LANES), 1)
    low = lane < A_HEAD_DIM
    heads_per_slab = LANES // A_HEAD_DIM
    group = A_HEADS // A_KV_HEADS

    def run(klen):
        k = k_ref[0, :klen, :]
        v = v_ref[0, :klen, :]
        for m in range(A_Q // LANES):
            parts = []
            for hh in range(heads_per_slab):
                h = m * heads_per_slab + hh
                kvh = h // group
                s = _mm_nt(q_ref[0, h], k)
                p = jnp.exp(s - jnp.max(s, axis=-1, keepdims=True))
                l = jnp.sum(p, axis=-1, keepdims=True)
                o = _mm(p.astype(BF16), v) / l
                parts.append(o if kvh == hh else pltpu.roll(o, A_HEAD_DIM, 1))
            o_ref[0, :, m * LANES:(m + 1) * LANES] = jnp.where(low, parts[0], parts[1])

    @pl.when(j * tq < lc)
    def _():
        run(lc)

    @pl.when(j * tq >= lc)
    def _():
        run(k_ref.shape[1])


def _attention(q, k, v, lc):
    bsz, _, ln, _ = q.shape
    tq = TOK_TILE
    return pl.pallas_call(
        functools.partial(_attn_kernel, lc=lc),
        grid=(bsz, ln // tq),
        in_specs=[pl.BlockSpec((1, A_HEADS, tq, LANES), lambda b, j: (b, 0, j, 0)),
                  pl.BlockSpec((1, ln, A_KV), lambda b, j: (b, 0, 0)),
                  pl.BlockSpec((1, ln, A_KV), lambda b, j: (b, 0, 0))],
        out_specs=pl.BlockSpec((1, tq, A_Q), lambda b, j: (b, j, 0)),
        out_shape=jax.ShapeDtypeStruct((bsz, ln, A_Q), F32),
        compiler_params=_params("arbitrary", "arbitrary"),
        name="attention",
    )(q, k, v)


def _shifted_rows(z, prev_row, next_row):
    tm = z.shape[0]
    row = lax.broadcasted_iota(jnp.int32, (tm, 1), 0)
    zprev = jnp.where(row == 0, prev_row, pltpu.roll(z, 1, 0))
    znext = jnp.where(row == tm - 1, next_row, pltpu.roll(z, tm - 1, 0))
    return zprev, znext


def _halo_rows(zp_ref, zn_ref, j, lct, nj):
    prev_ok = jnp.logical_and(j != 0, j != lct)
    next_ok = jnp.logical_and(j != lct - 1, j != nj - 1)
    prev_row = jnp.where(prev_ok, zp_ref[0, 7:8, :], 0.0)
    next_row = jnp.where(next_ok, zn_ref[0, 0:1, :], 0.0)
    return prev_row, next_row


def _halo_specs(tm, width, ln):
    nb8 = tm // 8
    last = ln // 8 - 1
    return [pl.BlockSpec((1, tm, width), lambda b, j: (b, j, 0)),
            pl.BlockSpec((1, 8, width), lambda b, j: (b, jnp.maximum(j * nb8 - 1, 0), 0)),
            pl.BlockSpec((1, 8, width), lambda b, j: (b, jnp.minimum((j + 1) * nb8, last), 0))]


def _per_head_sum(x, bones):
    return jnp.concatenate(
        [_bmm(x[:, m * LANES:(m + 1) * LANES], bones) for m in range(x.shape[1] // LANES)], axis=1)


def _rwkv_pre_kernel(z_ref, zp_ref, zn_ref, mu_ref, w0_ref, w2_ref, a0_ref, a2_ref, g2_ref,
                     kk_ref, ka_ref, rk_ref, bones_ref,
                     r_o, v_o, kkn_o, g_o, bonus_o, lw_o, a_o, kd_o, *, lct):
    j = pl.program_id(1)
    z = z_ref[0]
    prev_row, next_row = _halo_rows(zp_ref, zn_ref, j, lct, pl.num_programs(1))
    zprev, znext = _shifted_rows(z, prev_row, next_row)
    zm = z + mu_ref[...] * (0.5 * (zprev + znext) - z)
    w = B_WIDTH
    r, k, v = zm[:, 0:w], zm[:, w:2 * w], zm[:, 2 * w:3 * w]
    wa = zm[:, 3 * w:3 * w + B_DECAY_LORA + B_AAA_LORA]
    gl = zm[:, 3 * w + B_DECAY_LORA + B_AAA_LORA:]
    bones = bones_ref[...]
    kk = k * kk_ref[...]
    kkn = kk / jnp.maximum(jnp.sqrt(_per_head_sum(kk * kk, bones)), 1e-12)
    r_o[0] = r
    v_o[0] = v
    kkn_o[0] = kkn
    g_o[0] = _bmm(_sigmoid(gl), g2_ref[...])
    twa = jnp.tanh(wa)
    bonus = jnp.zeros_like(r)
    for d in range(2):
        wraw = w0_ref[d] + _bmm(twa, w2_ref[d])
        lw_o[d, 0] = -jnp.exp(-_softplus(-wraw) - 0.5)
        a = _sigmoid(a0_ref[d] + _bmm(wa, a2_ref[d]))
        kd = k * (1.0 + (a - 1.0) * ka_ref[...])
        a_o[d, 0] = a
        kd_o[d, 0] = kd
        bonus = bonus + _per_head_sum(r * kd * rk_ref[...], bones) * v
    bonus_o[0] = bonus


def _rwkv_pre(zr, mu, w0, w2, a0, a2, g2, k_k, k_a, r_k, lc):
    bsz, ln, width = zr.shape
    tm = TOK_TILE
    w = B_WIDTH
    zeros = jnp.zeros((2, B_DECAY_LORA, w), F32)
    w2p = jnp.concatenate([w2, zeros], axis=1)
    a2p = jnp.concatenate([zeros, a2], axis=1)
    bones = _head_block_ones(B_HEAD_DIM, 1.0)
    full = lambda *shape: pl.BlockSpec(shape, lambda b, j: (0,) * len(shape))
    tok = pl.BlockSpec((1, tm, w), lambda b, j: (b, j, 0))
    tok2 = pl.BlockSpec((2, 1, tm, w), lambda b, j: (0, b, j, 0))
    sd = jax.ShapeDtypeStruct((bsz, ln, w), F32)
    sd2 = jax.ShapeDtypeStruct((2, bsz, ln, w), F32)
    return pl.pallas_call(
        functools.partial(_rwkv_pre_kernel, lct=lc // tm),
        grid=(bsz, ln // tm),
        in_specs=_halo_specs(tm, width, ln) + [
            full(1, width), full(2, 1, w), full(2, 2 * B_DECAY_LORA, w), full(2, 1, w),
            full(2, 2 * B_AAA_LORA, w), full(B_GATE_LORA, w), full(1, w), full(1, w), full(1, w),
            full(LANES, LANES)],
        out_specs=[tok, tok, tok, tok, tok, tok2, tok2, tok2],
        out_shape=[sd, sd, sd, sd, sd, sd2, sd2, sd2],
        compiler_params=_params("arbitrary", "arbitrary"),
        name="rwkv_pre",
    )(zr, zr, zr, mu.reshape(1, width), w0.reshape(2, 1, w), w2p, a0.reshape(2, 1, w), a2p, g2,
      k_k.reshape(1, w), k_a.reshape(1, w), r_k.reshape(1, w), bones)


def _scan_chunk(i, d, nctx, ntot):
    rev = jnp.where(i < nctx, nctx - 1 - i, ntot + nctx - 1 - i)
    return jnp.where(d == 0, i, rev)


def _order_masks(n, pair):
    t = np.arange(n * pair)
    same = (t[:, None] // n) == (t[None, :] // n)
    tt, ss = t[:, None] % n, t[None, :] % n
    incl = np.stack([same & (ss <= tt), same & (ss >= tt)]).astype(np.float32)
    strict = np.stack([same & (ss < tt), same & (ss > tt)]).astype(np.float32)
    return jnp.asarray(incl), jnp.asarray(strict)


def _pair_stack(x, low):
    return jnp.concatenate([jnp.where(low, x, 0.0), jnp.where(low, 0.0, x)], axis=0)


def _bmm(a, b):
    return _mm(a.astype(BF16), b.astype(BF16))


def _rwkv_chunk_kernel(rf_ref, vf_ref, kf_ref, lwf_ref, af_ref, kdf_ref,
                       rb_ref, vb_ref, kb_ref, lwb_ref, ab_ref, kdb_ref,
                       tri_ref, incl_ref, strict_ref, of_ref, ob_ref, state_ref):
    c = R_CHUNK
    n2 = 2 * c

    @pl.when(pl.program_id(1) == 0)
    def _():
        state_ref[...] = jnp.zeros_like(state_ref)

    low = lax.broadcasted_iota(jnp.int32, (c, LANES), 1) < B_HEAD_DIM
    eye = (lax.broadcasted_iota(jnp.int32, (LANES, LANES), 0)
           == lax.broadcasted_iota(jnp.int32, (LANES, LANES), 1)).astype(F32)
    stack = lambda x: _pair_stack(x, low)
    refs = ((rf_ref, vf_ref, kf_ref, lwf_ref, af_ref, kdf_ref, of_ref),
            (rb_ref, vb_ref, kb_ref, lwb_ref, ab_ref, kdb_ref, ob_ref))
    dirs, slabs, outs = [], [], []
    kts, bts, kds, rts, vs, pcs = [], [], [], [], [], []
    for d, (r_ref, v_ref, kkn_ref, lw_ref, a_ref, kd_ref, o_ref) in enumerate(refs):
        lw = lw_ref[0, 0]
        cum = _mm(tri_ref[d], lw, HI)
        p_in = jnp.exp(cum)
        p_inv = jnp.exp(-cum)
        p_ex = jnp.exp(cum - lw)
        p_all = jnp.exp(jnp.sum(lw, axis=0, keepdims=True))
        kkn = kkn_ref[0]
        kt = kkn * p_ex
        bt = kkn * a_ref[0, 0] * p_inv
        kd = kd_ref[0, 0] * p_inv
        rt = r_ref[0] * p_in
        v = v_ref[0]
        for m in range(B_WIDTH // LANES):
            sl = slice(m * LANES, (m + 1) * LANES)
            dirs.append(d)
            slabs.append(m)
            outs.append(o_ref)
            kts.append(stack(kt[:, sl]).astype(BF16))
            bts.append(stack(bt[:, sl]).astype(BF16))
            kds.append(stack(kd[:, sl]).astype(BF16))
            rts.append(stack(rt[:, sl]))
            vs.append(stack(v[:, sl]).astype(BF16))
            pcs.append(p_all[:, sl])
    n = len(dirs)
    incl = [incl_ref[d] > 0.5 for d in range(2)]
    strict = [strict_ref[d] > 0.5 for d in range(2)]
    gram = [_mm_nt(jnp.concatenate([kts[x], rts[x].astype(BF16)], axis=0),
                   jnp.concatenate([bts[x], kds[x]], axis=0)) for x in range(n)]
    a_b = [jnp.where(strict[dirs[x]], gram[x][:n2, :n2], 0.0) for x in range(n)]
    a_k = [jnp.where(strict[dirs[x]], gram[x][:n2, n2:], 0.0) for x in range(n)]
    m_b = [jnp.where(incl[dirs[x]], gram[x][n2:, :n2], 0.0) for x in range(n)]
    m_k = [jnp.where(incl[dirs[x]], gram[x][n2:, n2:], 0.0) for x in range(n)]
    pw = [-a for a in a_b]
    tinv = [eye + p for p in pw]
    for _ in range(int(math.log2(c)) - 1):
        pw = [_bmm(p, p) for p in pw]
        tinv = [t + _bmm(t, p) for t, p in zip(tinv, pw)]
    mv = [_bmm(jnp.concatenate([a_k[x], m_k[x]], axis=0), vs[x]) for x in range(n)]
    wy = [_bmm(tinv[x], jnp.concatenate([kts[x], mv[x][:n2].astype(BF16)], axis=1)) for x in range(n)]
    mbwy = [_bmm(m_b[x], wy[x]) for x in range(n)]
    wyb = [_mm_tn(wy[x].astype(BF16), bts[x]) for x in range(n)]
    vtk = [_mm_tn(vs[x], kds[x]) for x in range(n)]
    s0 = [state_ref[dirs[x], slabs[x]] for x in range(n)]
    o_st = [_mm_nt((rts[x] - mbwy[x][:, :LANES]).astype(BF16), s0[x].astype(BF16))
            + (mv[x][n2:] - mbwy[x][:, LANES:]) for x in range(n)]
    for x in range(n):
        g = (eye - wyb[x][:LANES]) * pcs[x]
        h = (vtk[x] - wyb[x][LANES:]) * pcs[x]
        outs[x][0, :, slabs[x] * LANES:(slabs[x] + 1) * LANES] = o_st[x][:c] + o_st[x][c:]
        state_ref[dirs[x], slabs[x]] = _bmm(s0[x], g) + h


def _rwkv_chunks(r, v, kkn, lw, a, kd, lc):
    bsz, ln, w = r.shape
    c = R_CHUNK
    nctx, ntot = lc // c, ln // c
    tri, _ = _order_masks(c, 1)
    incl, strict = _order_masks(c, 2)
    specs = []
    for d in range(2):
        tok = pl.BlockSpec((1, c, w), lambda b, i, d=d: (b, _scan_chunk(i, d, nctx, ntot), 0))
        tok2 = pl.BlockSpec((1, 1, c, w), lambda b, i, d=d: (d, b, _scan_chunk(i, d, nctx, ntot), 0))
        specs.append((tok, tok2))
    full = lambda *shape: pl.BlockSpec(shape, lambda b, i: (0,) * len(shape))
    sd = jax.ShapeDtypeStruct((bsz, ln, w), F32)
    return pl.pallas_call(
        _rwkv_chunk_kernel,
        grid=(bsz, ntot),
        in_specs=[specs[0][0]] * 3 + [specs[0][1]] * 3 + [specs[1][0]] * 3 + [specs[1][1]] * 3
        + [full(2, c, c), full(2, 2 * c, 2 * c), full(2, 2 * c, 2 * c)],
        out_specs=[specs[0][0], specs[1][0]],
        out_shape=[sd, sd],
        scratch_shapes=[pltpu.VMEM((2, w // LANES, LANES, LANES), F32)],
        compiler_params=_params("arbitrary", "arbitrary"),
        name="rwkv_chunks",
    )(r, v, kkn, lw, a, kd, r, v, kkn, lw, a, kd, tri, incl, strict)


def _even_out_kernel(xc_ref, xl_ref, mod_ref, attn_ref, wkvf_ref, wkvb_ref, bonus_ref, g_ref, gw_ref,
                     gb_ref, bones_ref, w_ref, oc_ref, ol_ref, *, lct):
    bones = bones_ref[...]
    y = bonus_ref[0]
    for wkv_ref in (wkvf_ref, wkvb_ref):
        o = wkv_ref[0]
        dlt = o - _per_head_sum(o, bones)
        var = _per_head_sum(dlt * dlt, bones)
        y = y + dlt * lax.rsqrt(var + B_GN_EPS) * gw_ref[...] + gb_ref[...]
    y = y * g_ref[0]
    mix = _mm(attn_ref[0].astype(BF16), w_ref[:A_Q, :]) + _mm(y.astype(BF16), w_ref[A_Q:, :])
    res = _stream_tile(xc_ref, xl_ref, lct) + mod_ref[0, 0][2:3] * mix
    is_ctx = pl.program_id(1) < lct

    @pl.when(is_ctx)
    def _():
        oc_ref[0] = res

    @pl.when(jnp.logical_not(is_ctx))
    def _():
        ol_ref[0] = res


def _even_out(xc, xl, modx, attn, wkv, bonus, g, gn_w, gn_b, w_out):
    bsz, lc, d = xc.shape
    seq = xl.shape[1]
    tm = TOK_TILE
    lct = lc // tm
    w = B_WIDTH
    bones = _head_block_ones(B_HEAD_DIM, 1.0 / B_HEAD_DIM)
    tokw = lambda n: pl.BlockSpec((1, tm, n), lambda b, j: (b, j, 0))
    full = lambda *shape: pl.BlockSpec(shape, lambda b, j: (0,) * len(shape))
    streams = _stream_specs(tm, d, lct)
    return pl.pallas_call(
        functools.partial(_even_out_kernel, lct=lct),
        grid=(bsz, (lc + seq) // tm),
        in_specs=streams + [
            pl.BlockSpec((1, 1, N_MOD, d), lambda b, j: (b, jnp.where(j < lct, 0, 1), 0, 0)),
            tokw(A_Q), tokw(w), tokw(w),
            tokw(w), tokw(w), full(1, w), full(1, w), full(LANES, LANES),
            full(A_Q + w, d)],
        out_specs=streams,
        out_shape=[jax.ShapeDtypeStruct((bsz, lc, d), F32), jax.ShapeDtypeStruct((bsz, seq, d), F32)],
        compiler_params=_params("arbitrary", "arbitrary"),
        name="even_out",
    )(xc, xl, modx, attn, wkv[0], wkv[1], bonus, g, gn_w.reshape(1, w), gn_b.reshape(1, w), bones, w_out)


def _ssd_pre_kernel(z_ref, zp_ref, zn_ref, dt_ref, cw_ref, cb_ref, dtb_ref, x_o, dt_o, *, lct):
    j = pl.program_id(1)
    z = z_ref[0]
    prev_row, next_row = _halo_rows(zp_ref, zn_ref, j, lct, pl.num_programs(1))
    zprev, znext = _shifted_rows(z, prev_row, next_row)
    y = zprev * cw_ref[0:1, :] + z * cw_ref[1:2, :] + znext * cw_ref[2:3, :] + cb_ref[...]
    x_o[0] = _silu(y).astype(x_o.dtype)
    dt_o[0] = _softplus(dt_ref[0] + dtb_ref[...])


def _ssd_pre(xbc, dtraw, conv_w, conv_b, dtb, lc):
    bsz, ln, width = xbc.shape
    tm = TOK_TILE
    nd = dtraw.shape[-1]
    full = lambda *shape: pl.BlockSpec(shape, lambda b, j: (0,) * len(shape))
    return pl.pallas_call(
        functools.partial(_ssd_pre_kernel, lct=lc // tm),
        grid=(bsz, ln // tm),
        in_specs=_halo_specs(tm, width, ln) + [
            pl.BlockSpec((1, tm, nd), lambda b, j: (b, j, 0)),
            full(conv_w.shape[0], width), full(1, width), full(1, nd)],
        out_specs=[pl.BlockSpec((1, tm, width), lambda b, j: (b, j, 0)),
                   pl.BlockSpec((1, tm, nd), lambda b, j: (b, j, 0))],
        out_shape=[jax.ShapeDtypeStruct((bsz, ln, width), BF16),
                   jax.ShapeDtypeStruct((bsz, ln, nd), F32)],
        compiler_params=_params("arbitrary", "arbitrary"),
        name="ssd_pre",
    )(xbc, xbc, xbc, dtraw, conv_w, conv_b.reshape(1, width), dtb)


def _ssd_chunk_kernel(x_ref, b_ref, c_ref, dt_ref, alog_ref, incl_ref, expand_ref, o_ref, state_ref,
                      *, n_heads):
    q = C_CHUNK
    i = pl.program_id(2)

    @pl.when(i == 0)
    def _():
        state_ref[...] = jnp.zeros_like(state_ref)

    lane = lax.broadcasted_iota(jnp.int32, (1, LANES), 1)
    a_neg = jnp.where(lane < n_heads, -jnp.exp(alog_ref[0]), 0.0)
    dt = dt_ref[0]
    dta = dt * a_neg
    incl_f = incl_ref[0]
    cum = _mm(incl_f, dta, HI)
    cum_t = cum.T
    total = jnp.sum(dta, axis=0, keepdims=True)
    expand = expand_ref[...]
    dt_x = _bmm(dt, expand)
    ec_x = _bmm(jnp.exp(cum), expand)
    de_x = _bmm(jnp.exp(total - cum), expand)
    cd_x = _bmm(jnp.broadcast_to(jnp.exp(total), (BF16_ROWS, LANES)), expand)[0:1]
    incl = incl_f > 0.5
    low = lax.broadcasted_iota(jnp.int32, (q, LANES), 1) < C_HEAD_DIM
    heads_per_group = n_heads // C_GROUPS
    pairs_per_group = heads_per_group // 2
    gw = heads_per_group * C_HEAD_DIM
    for g in range(C_GROUPS):
        gsl = slice(g * gw, (g + 1) * gw)
        bg = b_ref[0, :, g * C_STATE:(g + 1) * C_STATE].astype(BF16)
        cg = c_ref[0, :, g * C_STATE:(g + 1) * C_STATE].astype(BF16)
        cb = _mm_nt(cg, bg)
        xd = x_ref[0, :, gsl].astype(F32) * dt_x[:, gsl]
        s_prev = state_ref[g]
        y_off = _mm(cg, s_prev.astype(BF16)) * ec_x[:, gsl]
        state_ref[g] = s_prev * cd_x[:, gsl] + _mm_tn(bg, (xd * de_x[:, gsl]).astype(BF16))
        for pp in range(pairs_per_group):
            pr = g * pairs_per_group + pp
            sl = slice(pr * LANES, (pr + 1) * LANES)
            ms = []
            for hh in range(2):
                h = 2 * pr + hh
                seg = cum[:, h:h + 1] - cum_t[h:h + 1, :]
                ms.append((cb * jnp.exp(jnp.where(incl, seg, -1e30))).astype(BF16))
            xd_pair = _pair_stack(xd[:, pp * LANES:(pp + 1) * LANES], low).astype(BF16)
            y_diag = _mm(jnp.concatenate(ms, axis=1), xd_pair)
            o_ref[0, 0, :, sl] = (y_diag + y_off[:, pp * LANES:(pp + 1) * LANES]).astype(o_ref.dtype)


def _ssd_chunks(xact, dtact, a_log, lc):
    bsz, ln, _ = xact.shape
    q = C_CHUNK
    n_heads = a_log.shape[1]
    inner = n_heads * C_HEAD_DIM
    gn = C_GROUPS * C_STATE
    nctx, ntot = lc // q, ln // q
    incl, _ = _order_masks(q, 1)
    expand = np.zeros((LANES, inner), np.float32)
    expand[np.arange(inner) // C_HEAD_DIM, np.arange(inner)] = 1.0
    alog = jnp.zeros((2, 1, LANES), F32).at[:, 0, :n_heads].set(a_log)
    cmap = lambda off: (lambda b, d, i: (b, _scan_chunk(i, d, nctx, ntot), off))
    return pl.pallas_call(
        functools.partial(_ssd_chunk_kernel, n_heads=n_heads),
        grid=(bsz, 2, ntot),
        in_specs=[pl.BlockSpec((1, q, inner), cmap(0)),
                  pl.BlockSpec((1, q, gn), cmap(inner // gn)),
                  pl.BlockSpec((1, q, gn), cmap(inner // gn + 1)),
                  pl.BlockSpec((1, q, LANES), lambda b, d, i: (b, _scan_chunk(i, d, nctx, ntot), d)),
                  pl.BlockSpec((1, 1, LANES), lambda b, d, i: (d, 0, 0)),
                  pl.BlockSpec((1, q, q), lambda b, d, i: (d, 0, 0)),
                  pl.BlockSpec((LANES, inner), lambda b, d, i: (0, 0))],
        out_specs=pl.BlockSpec((1, 1, q, inner), lambda b, d, i: (d, b, _scan_chunk(i, d, nctx, ntot), 0)),
        out_shape=jax.ShapeDtypeStruct((2, bsz, ln, inner), BF16),
        scratch_shapes=[pltpu.VMEM((C_GROUPS, C_STATE, inner // C_GROUPS), F32)],
        compiler_params=_params("arbitrary", "arbitrary", "arbitrary"),
        name="ssd_chunks",
    )(xact, xact, xact, dtact, alog, incl, jnp.asarray(expand, dtype=BF16))


def _odd_out_kernel(x_ref, mod_ref, ys_ref, xs_ref, zg_ref, dsk_ref, nw_ref, w_ref, o_ref):
    y = (ys_ref[0, 0].astype(F32) + ys_ref[1, 0].astype(F32)) + dsk_ref[...] * xs_ref[0].astype(F32)
    yg = y * _silu(zg_ref[0].astype(F32))
    inner = yg.shape[1]
    gw = inner // C_GROUPS
    parts = []
    for g in range(C_GROUPS):
        t = yg[:, g * gw:(g + 1) * gw]
        ms = jnp.mean(t * t, axis=-1, keepdims=True)
        parts.append((t * lax.rsqrt(ms + EPS) * nw_ref[:, g * gw:(g + 1) * gw]).astype(BF16))
    yn = jnp.concatenate(parts, axis=1)
    o_ref[0] = x_ref[0] + mod_ref[0, 0][2:3] * _mm(yn, w_ref[...])


def _odd_out(x, modx, ys, xact, zg, d_skip, norm_w, w_out, lc):
    bsz, seq, d = x.shape
    ln = lc + seq
    tm = TOK_TILE
    lct = lc // tm
    inner = zg.shape[-1]
    dsk = jnp.repeat(d_skip, C_HEAD_DIM).reshape(1, inner)
    tokw = lambda n: pl.BlockSpec((1, tm, n), lambda b, j: (b, j + lct, 0))
    full = lambda *shape: pl.BlockSpec(shape, lambda b, j: (0,) * len(shape))
    return pl.pallas_call(
        _odd_out_kernel,
        grid=(bsz, (ln - lc) // tm),
        in_specs=[pl.BlockSpec((1, tm, d), lambda b, j: (b, j, 0)),
                  pl.BlockSpec((1, 1, N_MOD, d), lambda b, j: (b, 1, 0, 0)),
                  pl.BlockSpec((2, 1, tm, inner), lambda b, j: (0, b, j + lct, 0)),
                  tokw(inner), tokw(inner), full(1, inner), full(1, inner), full(inner, d)],
        out_specs=pl.BlockSpec((1, tm, d), lambda b, j: (b, j, 0)),
        out_shape=jax.ShapeDtypeStruct((bsz, ln - lc, d), F32),
        compiler_params=_params("arbitrary", "arbitrary"),
        name="odd_out",
    )(x, modx, ys, xact, zg, dsk, norm_w.reshape(1, inner), w_out)


class _Group:
    def __init__(self, parts):
        self.parts = tuple(parts)


def _ew(f, *args):
    groups = [a for a in args if isinstance(a, _Group)]
    if not groups:
        return f(*args)
    return _Group(f(*[a.parts[m] if isinstance(a, _Group) else a for a in args])
                  for m in range(len(groups[0].parts)))


def _exchange(xs, i, l, descending=True):
    hi, lo = _ew(jnp.maximum, xs[i], xs[l]), _ew(jnp.minimum, xs[i], xs[l])
    xs[i], xs[l] = (hi, lo) if descending else (lo, hi)


def _bitonic_merge(xs):
    xs = list(xs)
    j = len(xs) // 2
    while j >= 1:
        for i in range(len(xs)):
            if i ^ j > i:
                _exchange(xs, i, i ^ j)
        j //= 2
    return xs


def _bitonic_sort(xs):
    xs = list(xs)
    k = 2
    while k <= len(xs):
        j = k // 2
        while j >= 1:
            for i in range(len(xs)):
                if i ^ j > i:
                    _exchange(xs, i, i ^ j, descending=(i & k) == 0)
            j //= 2
        k *= 2
    return xs


def _merge_sublanes(xs):
    n = len(xs)
    for shift in (4, 2, 1):
        other = [_ew(lambda v: pltpu.roll(v, shift, 0), x) for x in xs]
        xs = _bitonic_merge([_ew(jnp.maximum, xs[k], other[n - 1 - k]) for k in range(n)])
    return xs


def _row_tile(s, k):
    return _ew(lambda v: v[8 * k:8 * k + 8], s)


def _top_values(s):
    return _merge_sublanes(_bitonic_sort([_row_tile(s, k) for k in range(P_KEYS // 8)]))


def _top_pair_sums(t1, t2, shape):
    sub = lax.broadcasted_iota(jnp.int32, shape, 0)
    lo, hi = t2[0], t2[8]
    for b in range(1, 8):
        lo = _ew(lambda x, y: jnp.where(sub == b, x, y), t2[b], lo)
        hi = _ew(lambda x, y: jnp.where(sub == b, x, y), t2[8 + b], hi)
    xs = [_ew(jnp.add, t1[a], lo) for a in range(P_TOPK)]
    carry = _ew(jnp.add, t1[0], hi)
    for k in range(P_TOPK):
        xs[k], carry = _ew(jnp.maximum, xs[k], carry), _ew(jnp.minimum, xs[k], carry)
    return _merge_sublanes(xs)


def _prefix_count(test, vals):
    sel = functools.partial(_ew, jnp.where)
    b3 = test(vals[7])
    b2 = test(sel(b3, vals[11], vals[3]))
    b1 = test(sel(b3, sel(b2, vals[13], vals[9]), sel(b2, vals[5], vals[1])))
    b0 = test(sel(b3, sel(b2, sel(b1, vals[14], vals[12]), sel(b1, vals[10], vals[8])),
                  sel(b2, sel(b1, vals[6], vals[4]), sel(b1, vals[2], vals[0]))))
    n = _ew(lambda p3, p2, p1, p0: (jnp.where(p3, 8.0, 0.0) + jnp.where(p2, 4.0, 0.0))
            + (jnp.where(p1, 2.0, 0.0) + jnp.where(p0, 1.0, 0.0)), b3, b2, b1, b0)
    return _ew(lambda full, m: jnp.where(full, 16.0, m), test(vals[15]), n)


def _peer_query_kernel(x_ref, mod_ref, g_ref, wq_ref, keys_ref, ht_o, n_o, c_o, rank_o, e2_o):
    assert P_TOPK == 16 and P_KEYS == 128
    mod = mod_ref[0]
    h2 = _norm_mod(x_ref[...], g_ref[...], mod[3:4], mod[4:5])
    ht = h2.T.astype(BF16)
    ht_o[...] = ht
    qt = _mm(wq_ref[...], ht).astype(BF16)
    kd = P_KEY_DIM
    tile_shape = (8, ht.shape[1])
    for h0 in range(0, P_HEADS, PEER_HEAD_GROUP):
        heads = range(h0, h0 + PEER_HEAD_GROUP)
        s1 = _Group(_mm(keys_ref[h, 0], qt[(2 * h) * kd:(2 * h + 1) * kd]) for h in heads)
        s2 = _Group(_mm(keys_ref[h, 1], qt[(2 * h + 1) * kd:(2 * h + 2) * kd]) for h in heads)
        t1 = _top_values(s1)
        t2 = _top_values(s2)
        best = _top_pair_sums(t1, t2, tile_shape)
        z = _ew(jnp.ones_like, best[0])
        for k in range(1, P_TOPK):
            z = _ew(lambda acc, v, top: acc + jnp.exp(v - top), z, best[k], best[0])
        inv_z = _ew(lambda v: 0.5 / v, z)
        tau = best[P_TOPK - 1]
        ranks, e2s = [], []
        for k in range(P_KEYS // 8):
            rows = slice(8 * k, 8 * k + 8)
            s1k, s2k = _row_tile(s1, k), _row_tile(s2, k)
            nk = _prefix_count(lambda v: _ew(lambda s, x, t: (s + x) >= t, s1k, v, tau), t2)
            ck = _ew(lambda s, top, iz: jnp.exp(s - top) * iz, s1k, t1[0], inv_z)
            ranks.append(_prefix_count(lambda v: _ew(lambda s, x: x > s, s2k, v), t2))
            e2s.append(_ew(lambda s, top: jnp.exp(s - top), s2k, t2[0]))
            for m, h in enumerate(heads):
                n_o[h, rows] = nk.parts[m]
                c_o[h, rows] = ck.parts[m]
        for m, h in enumerate(heads):
            rank_o[h] = jnp.concatenate([r.parts[m] for r in ranks], axis=0).astype(BF16)
            e2_o[h] = jnp.concatenate([e.parts[m] for e in e2s], axis=0).astype(BF16)


def _peer_query(xt, mod, gain, wq_t, keys, blocks_per_mod, tt):
    t_all, d = xt.shape
    nq = wq_t.shape[0]
    nblk = t_all // tt
    head = pl.BlockSpec((P_HEADS, P_KEYS, tt), lambda i: (0, 0, i))
    hsd = lambda dt: jax.ShapeDtypeStruct((P_HEADS, P_KEYS, t_all), dt)
    wspec = pl.BlockSpec((nq, d), lambda i: (0, 0))
    kspec = pl.BlockSpec((P_HEADS, 2, P_KEYS, P_KEY_DIM), lambda i: (0, 0, 0, 0))
    return pl.pallas_call(
        _peer_query_kernel,
        grid=(nblk,),
        in_specs=[pl.BlockSpec((tt, d), lambda i: (i, 0)),
                  pl.BlockSpec((1, N_MOD, d), lambda i: (i // blocks_per_mod, 0, 0)),
                  pl.BlockSpec((1, d), lambda i: (0, 0)),
                  wspec, kspec],
        out_specs=[pl.BlockSpec((d, tt), lambda i: (0, i)), head, head, head, head],
        out_shape=[jax.ShapeDtypeStruct((d, t_all), BF16), hsd(F32), hsd(F32), hsd(BF16), hsd(BF16)],
        compiler_params=_params("arbitrary"),
        name="peer_query",
    )(xt, mod, gain.reshape(1, d), wq_t.astype(BF16), keys.astype(BF16))


def _twice_gelu(x):
    return x * (1.0 + lax.erf(x * (2.0 ** -0.5)))


def _peer_main_kernel(x_ref, mod_ref, ht_ref, u_ref, vt_ref, n_ref, c_ref, rank_ref, e2_ref,
                      o_ref, acc_ref, act_ref, coef_ref):
    g = pl.program_id(1)
    last = pl.num_programs(1) - 1
    tt = ht_ref.shape[1]
    rows_per_step = u_ref.shape[0] // P_KEYS
    tiles = P_KEYS // BF16_ROWS
    tw = PEER_COEF_TOK

    def consume():
        acc_ref[...] += _mm(vt_ref[...], coef_ref[...] * act_ref[...])

    def prepare():
        zero = jnp.zeros((BF16_ROWS, tw), BF16)
        row0 = pl.multiple_of(g * rows_per_step, rows_per_step)
        n_rows = [n_ref[h, pl.ds(row0, rows_per_step), :] for h in range(P_HEADS)]
        c_rows = [c_ref[h, pl.ds(row0, rows_per_step), :] for h in range(P_HEADS)]
        for ii in range(rows_per_step):
            for t0 in range(0, tt, tw):
                tok = slice(t0, t0 + tw)
                coef = [None] * tiles
                for h in range(P_HEADS):
                    nb = jnp.broadcast_to(n_rows[h][ii:ii + 1, tok], (BF16_ROWS, tw)).astype(BF16)
                    cb = jnp.broadcast_to(c_rows[h][ii:ii + 1, tok], (BF16_ROWS, tw)).astype(BF16)
                    for k in range(tiles):
                        rows = slice(k * BF16_ROWS, (k + 1) * BF16_ROWS)
                        term = jnp.where(rank_ref[h, rows, tok] < nb, e2_ref[h, rows, tok] * cb, zero)
                        coef[k] = term if coef[k] is None else coef[k] + term
                for k in range(tiles):
                    rows = slice(ii * P_KEYS + k * BF16_ROWS, ii * P_KEYS + (k + 1) * BF16_ROWS)
                    coef_ref[rows, tok] = coef[k]
        act_ref[...] = _twice_gelu(_mm(u_ref[...], ht_ref[...])).astype(BF16)

    @pl.when(g == 0)
    def _():
        acc_ref[...] = jnp.zeros_like(acc_ref)
        prepare()

    @pl.when(jnp.logical_and(g > 0, g < last))
    def _():
        consume()
        prepare()

    @pl.when(g == last)
    def _():
        consume()
        o_ref[...] = x_ref[...] + mod_ref[0][5:6] * acc_ref[...].T


def _peer_main(xt, mod, ht, u_all, vt_all, layer, n, c, rank, e2, blocks_per_mod, tt):
    t_all, d = xt.shape
    n_exp = u_all.shape[1]
    eb = PEER_EXP
    nsteps = n_exp // eb
    head = pl.BlockSpec((P_HEADS, P_KEYS, tt), lambda i, s: (0, 0, i))
    return pl.pallas_call(
        _peer_main_kernel,
        grid=(t_all // tt, nsteps + 1),
        in_specs=[pl.BlockSpec((tt, d), lambda i, s: (i, 0)),
                  pl.BlockSpec((1, N_MOD, d), lambda i, s: (i // blocks_per_mod, 0, 0)),
                  pl.BlockSpec((d, tt), lambda i, s: (0, i)),
                  pl.BlockSpec((None, eb, d), lambda i, s: (layer, jnp.minimum(s, nsteps - 1), 0)),
                  pl.BlockSpec((None, d, eb), lambda i, s: (layer, 0, jnp.maximum(s - 1, 0))),
                  head, head, head, head],
        out_specs=pl.BlockSpec((tt, d), lambda i, s: (i, 0)),
        out_shape=jax.ShapeDtypeStruct((t_all, d), F32),
        scratch_shapes=[pltpu.VMEM((d, tt), F32), pltpu.VMEM((eb, tt), BF16), pltpu.VMEM((eb, tt), BF16)],
        compiler_params=_params("arbitrary", "arbitrary"),
        name="peer_main",
    )(xt, mod, ht, u_all, vt_all, n, c, rank, e2)


def _peer_layer(xt, mod, gain, w_q, keys, u_all, vt_all, layer, blocks_per_mod, tt):
    wq_t = w_q.T
    tq = PEER_QUERY_TOK
    ht, n, c, rank, e2 = _peer_query(xt, mod, gain, wq_t, keys, blocks_per_mod * (tt // tq), tq)
    return _peer_main(xt, mod, ht, u_all, vt_all, layer, n, c, rank, e2, blocks_per_mod, tt)


def kernel(x, c, ctx, c_ctx, mod_w, mod_b, norm1_g, norm2_g, ev_w_in, ev_w_out, attn_q_gain, attn_k_gain,
           rw_mu, rw_w0, rw_w2, rw_a0, rw_a2, rw_g2, rw_k_k, rw_k_a, rw_r_k, rw_gn_w, rw_gn_b, ssd_w_in,
           ssd_conv_w, ssd_conv_b, ssd_dt_bias, ssd_a_log, ssd_d, ssd_norm_w, ssd_w_out, peer_w_q,
           peer_keys, peer_u, peer_v):
    bsz, seq, dm = x.shape
    lc = ctx.shape[1]
    depth = mod_w.shape[0]
    assert lc == TOK_TILE and seq % PEER_TOK == 0 and (bsz * lc) % PEER_TOK == 0

    rows = -(-(bsz + 1) // 8) * 8
    cc = jnp.concatenate([c, c_ctx[None], jnp.zeros((rows - bsz - 1, dm), F32)], axis=0)
    mods = _modulation(cc, mod_w, mod_b)
    xc, xl = ctx, x
    u_all = peer_u.astype(BF16)
    vt_all = jnp.swapaxes(peer_v, 1, 2).astype(BF16)

    for i in range(depth):
        last = i == depth - 1
        j = i // 2
        ml = mods[i, :bsz].reshape(bsz, N_MOD, dm)
        mc = mods[i, bsz].reshape(1, N_MOD, dm)
        modx = jnp.stack([jnp.broadcast_to(mc, (bsz, N_MOD, dm)), ml], axis=1)
        if i % 2 == 0:
            za, zr = _in_proj(xc, xl, modx, norm1_g[i], ev_w_in[j].astype(BF16),
                              ((A_IN, F32), (B_IN, F32)))
            q, k, v = _attn_pre(za, attn_q_gain[j], attn_k_gain[j], lc)
            attn = _attention(q, k, v, lc)
            r, vv, kkn, g, bonus, lw, a, kd = _rwkv_pre(
                zr, rw_mu[j], rw_w0[j], rw_w2[j], rw_a0[j], rw_a2[j], rw_g2[j], rw_k_k[j], rw_k_a[j],
                rw_r_k[j], lc)
            wkv = _rwkv_chunks(r, vv, kkn, lw, a, kd, lc)
            xc, xl = _even_out(xc, xl, modx, attn, wkv, bonus, g, rw_gn_w[j], rw_gn_b[j],
                               ev_w_out[j].astype(BF16))
        else:
            n_heads = ssd_a_log.shape[-1]
            inner = n_heads * C_HEAD_DIM
            conv_dim = inner + 2 * C_GROUPS * C_STATE
            w_in = ssd_w_in[j]
            w_dt = jnp.zeros((dm, 2 * LANES), F32)
            w_dt = w_dt.at[:, :n_heads].set(w_in[:, inner + conv_dim:inner + conv_dim + n_heads])
            w_dt = w_dt.at[:, LANES:LANES + n_heads].set(w_in[:, inner + conv_dim + n_heads:])
            dtb = jnp.zeros((1, 2 * LANES), F32)
            dtb = dtb.at[0, :n_heads].set(ssd_dt_bias[j, 0]).at[0, LANES:LANES + n_heads].set(ssd_dt_bias[j, 1])
            w_all = jnp.concatenate([w_in[:, :inner + conv_dim], w_dt], axis=1).astype(BF16)
            zg, xbc, dtraw = _in_proj(xc, xl, modx, norm1_g[i], w_all,
                                      ((inner, BF16), (conv_dim, F32), (2 * LANES, F32)))
            xact, dtact = _ssd_pre(xbc, dtraw, ssd_conv_w[j], ssd_conv_b[j], dtb, lc)
            ys = _ssd_chunks(xact, dtact, ssd_a_log[j], lc)
            assert last, "the SSD layer keeps only the latent stream"
            xl = _odd_out(xl, modx, ys, xact, zg, ssd_d[j], ssd_norm_w[j], ssd_w_out[j].astype(BF16), lc)

        tt = PEER_TOK
        xl = _peer_layer(xl.reshape(bsz * seq, dm), ml, norm2_g[i], peer_w_q[i], peer_keys[i],
                         u_all, vt_all, i, seq // tt, tt).reshape(bsz, seq, dm)
        if not last:
            xc = _peer_layer(xc.reshape(bsz * lc, dm), mc, norm2_g[i], peer_w_q[i], peer_keys[i],
                             u_all, vt_all, i, (bsz * lc) // tt, tt).reshape(bsz, lc, dm)
    return xl
```

```python
import functools
import math

import numpy as np
import jax
import jax.numpy as jnp
from jax import lax
from jax.experimental import pallas as pl
from jax.experimental.pallas import tpu as pltpu

F32 = jnp.float32
BF16 = jnp.bfloat16
HI = lax.Precision.HIGHEST

GRID_W = 64
EPS = 1e-6
N_MOD = 6
A_HEADS, A_KV_HEADS, A_HEAD_DIM = 8, 2, 64
A_Q = A_HEADS * A_HEAD_DIM
A_KV = A_KV_HEADS * A_HEAD_DIM
A_IN = A_Q + 2 * A_KV
ROPE_THETA = 10000.0
B_HEADS, B_HEAD_DIM = 8, 64
B_WIDTH = B_HEADS * B_HEAD_DIM
B_DECAY_LORA, B_AAA_LORA, B_GATE_LORA = 64, 64, 128
B_IN = 3 * B_WIDTH + B_DECAY_LORA + B_AAA_LORA + B_GATE_LORA
B_GN_EPS = 64e-5
C_HEAD_DIM, C_GROUPS, C_STATE, C_CHUNK = 64, 4, 128, 128
P_HEADS, P_KEYS, P_KEY_DIM, P_TOPK = 8, 128, 128, 16

LANES = 128
BF16_ROWS = 16
TOK_TILE = 256
R_CHUNK = 64
PEER_TOK = 512
PEER_QUERY_TOK = 256
PEER_HEAD_GROUP = 2
PEER_EXP = 1024
PEER_COEF_TOK = 256
VMEM_LIMIT = 56 * 1024 * 1024


def _mm(a, b, prec=None):
    return lax.dot_general(a, b, (((1,), (0,)), ((), ())), precision=prec,
                           preferred_element_type=F32)


def _mm_nt(a, b, prec=None):
    return lax.dot_general(a, b, (((1,), (1,)), ((), ())), precision=prec,
                           preferred_element_type=F32)


def _mm_tn(a, b, prec=None):
    return lax.dot_general(a, b, (((0,), (0,)), ((), ())), precision=prec,
                           preferred_element_type=F32)


def _params(*sem):
    return pltpu.CompilerParams(dimension_semantics=sem, vmem_limit_bytes=VMEM_LIMIT)


def _softplus(x):
    return jnp.maximum(x, 0.0) + jnp.log1p(jnp.exp(-jnp.abs(x)))


def _sigmoid(x):
    return 1.0 / (1.0 + jnp.exp(-x))


def _silu(x):
    return x * _sigmoid(x)


def _mod_kernel(c_ref, w_ref, b_ref, o_ref):
    o_ref[0] = _mm(_silu(c_ref[...]), w_ref[0], HI) + b_ref[0]


def _modulation(cc, mod_w, mod_b):
    depth, d, n6 = mod_w.shape
    rows = cc.shape[0]
    tn = 1024
    return pl.pallas_call(
        _mod_kernel,
        grid=(depth, n6 // tn),
        in_specs=[pl.BlockSpec((rows, d), lambda l, n: (0, 0)),
                  pl.BlockSpec((1, d, tn), lambda l, n: (l, 0, n)),
                  pl.BlockSpec((1, 1, tn), lambda l, n: (l, 0, n))],
        out_specs=pl.BlockSpec((1, rows, tn), lambda l, n: (l, 0, n)),
        out_shape=jax.ShapeDtypeStruct((depth, rows, n6), F32),
        compiler_params=_params("arbitrary", "arbitrary"),
        name="modulation",
    )(cc, mod_w, mod_b.reshape(depth, 1, n6))


def _norm_mod(x, gain, shift, scale):
    ms = jnp.mean(x * x, axis=-1, keepdims=True)
    return x * lax.rsqrt(ms + EPS) * gain * (1.0 + scale) + shift


def _stream_specs(tm, d, lct):
    return [pl.BlockSpec((1, tm, d), lambda b, j: (b, jnp.minimum(j, lct - 1), 0)),
            pl.BlockSpec((1, tm, d), lambda b, j: (b, jnp.maximum(j - lct, 0), 0))]


def _stream_tile(xc_ref, xl_ref, lct):
    return jnp.where(pl.program_id(1) < lct, xc_ref[0], xl_ref[0])


def _in_proj_kernel(xc_ref, xl_ref, mod_ref, g_ref, w_ref, *o_refs, lct):
    mod = mod_ref[0, 0]
    h = _norm_mod(_stream_tile(xc_ref, xl_ref, lct), g_ref[...], mod[0:1], mod[1:2]).astype(BF16)
    off = 0
    for o_ref in o_refs:
        n = o_ref.shape[-1]
        for c0 in range(0, n, 1024):
            c1 = min(n, c0 + 1024)
            o_ref[0, :, c0:c1] = _mm(h, w_ref[:, off + c0:off + c1]).astype(o_ref.dtype)
        off += n


def _in_proj(xc, xl, modx, gain, w, splits):
    bsz, lc, d = xc.shape
    ln = lc + xl.shape[1]
    assert w.shape[1] == sum(n for n, _ in splits)
    tm = TOK_TILE
    lct = lc // tm
    return pl.pallas_call(
        functools.partial(_in_proj_kernel, lct=lct),
        grid=(bsz, ln // tm),
        in_specs=_stream_specs(tm, d, lct) + [
            pl.BlockSpec((1, 1, N_MOD, d), lambda b, j: (b, jnp.where(j < lct, 0, 1), 0, 0)),
            pl.BlockSpec((1, d), lambda b, j: (0, 0)),
            pl.BlockSpec(w.shape, lambda b, j: (0, 0))],
        out_specs=[pl.BlockSpec((1, tm, n), lambda b, j: (b, j, 0)) for n, _ in splits],
        out_shape=[jax.ShapeDtypeStruct((bsz, ln, n), dt) for n, dt in splits],
        compiler_params=_params("arbitrary", "arbitrary"),
        name="in_proj",
    )(xc, xl, modx, gain.reshape(1, d), w)


def _rope_tables(lc, seq):
    t = jnp.arange(seq)
    row = (t // GRID_W).astype(F32)
    col = (t % GRID_W).astype(F32)
    m = A_HEAD_DIM // 4
    inv = ROPE_THETA ** (-jnp.arange(m, dtype=F32) / m)
    ar, ac = row[:, None] * inv, col[:, None] * inv
    cos = jnp.concatenate([jnp.cos(ar), jnp.cos(ar), jnp.cos(ac), jnp.cos(ac)], axis=-1)
    sin = jnp.concatenate([-jnp.sin(ar), jnp.sin(ar), -jnp.sin(ac), jnp.sin(ac)], axis=-1)
    reps = LANES // A_HEAD_DIM
    cos = jnp.concatenate([jnp.ones((lc, A_HEAD_DIM), F32), cos], axis=0)
    sin = jnp.concatenate([jnp.zeros((lc, A_HEAD_DIM), F32), sin], axis=0)
    return jnp.tile(cos, (1, reps)), jnp.tile(sin, (1, reps))


def _head_block_ones(width, scale):
    idx = np.arange(LANES) // width
    return jnp.asarray((idx[:, None] == idx[None, :]).astype(np.float32) * scale)


def _attn_pre_kernel(z_ref, qg_ref, kg_ref, cos_ref, sin_ref, bones_ref, q_ref, k_ref, v_ref):
    tm = z_ref.shape[1]
    cs, sn = cos_ref[...], sin_ref[...]
    bones = bones_ref[...]
    lane = lax.broadcasted_iota(jnp.int32, (tm, LANES), 1)
    first = (lane % 32) < 16
    low = lane < A_HEAD_DIM

    def norm_rope(x, gain):
        ms = _bmm(x * x, bones)
        y = x * lax.rsqrt(ms + EPS) * gain
        partner = jnp.where(first, pltpu.roll(y, LANES - 16, 1), pltpu.roll(y, 16, 1))
        return y * cs + partner * sn

    heads_per_slab = LANES // A_HEAD_DIM
    group = A_HEADS // A_KV_HEADS
    for m in range(A_Q // LANES):
        sl = slice(m * LANES, (m + 1) * LANES)
        y = norm_rope(z_ref[0, :, sl], qg_ref[:, sl]) * (A_HEAD_DIM ** -0.5)
        swapped = pltpu.roll(y, A_HEAD_DIM, 1)
        for hh in range(heads_per_slab):
            h = m * heads_per_slab + hh
            kvh = h // group
            src = y if kvh == hh else swapped
            keep = low if kvh == 0 else jnp.logical_not(low)
            q_ref[0, h] = jnp.where(keep, src, 0.0).astype(BF16)
    k_ref[0] = norm_rope(z_ref[0, :, A_Q:A_Q + A_KV], kg_ref[...]).astype(BF16)
    v_ref[0] = z_ref[0, :, A_Q + A_KV:A_IN].astype(BF16)


def _attn_pre(za, q_gain, k_gain, lc):
    bsz, ln, _ = za.shape
    tm = TOK_TILE
    cos, sin = _rope_tables(lc, ln - lc)
    qg = jnp.tile(q_gain, A_HEADS).reshape(1, A_Q)
    kg = jnp.tile(k_gain, A_KV_HEADS).reshape(1, A_KV)
    bones = _head_block_ones(A_HEAD_DIM, 1.0 / A_HEAD_DIM)
    return pl.pallas_call(
        _attn_pre_kernel,
        grid=(bsz, ln // tm),
        in_specs=[pl.BlockSpec((1, tm, A_IN), lambda b, j: (b, j, 0)),
                  pl.BlockSpec((1, A_Q), lambda b, j: (0, 0)),
                  pl.BlockSpec((1, A_KV), lambda b, j: (0, 0)),
                  pl.BlockSpec((tm, LANES), lambda b, j: (j, 0)),
                  pl.BlockSpec((tm, LANES), lambda b, j: (j, 0)),
                  pl.BlockSpec((LANES, LANES), lambda b, j: (0, 0))],
        out_specs=[pl.BlockSpec((1, A_HEADS, tm, LANES), lambda b, j: (b, 0, j, 0)),
                   pl.BlockSpec((1, tm, A_KV), lambda b, j: (b, j, 0)),
                   pl.BlockSpec((1, tm, A_KV), lambda b, j: (b, j, 0))],
        out_shape=[jax.ShapeDtypeStruct((bsz, A_HEADS, ln, LANES), BF16),
                   jax.ShapeDtypeStruct((bsz, ln, A_KV), BF16),
                   jax.ShapeDtypeStruct((bsz, ln, A_KV), BF16)],
        compiler_params=_params("arbitrary", "arbitrary"),
        name="attn_pre",
    )(za, qg, kg, cos, sin, bones)


def _attn_kernel(q_ref, k_ref, v_ref, o_ref, *, lc):
    tq = q_ref.shape[2]
    j = pl.program_id(1)
    lane = lax.broadcasted_iota(jnp.int32, (tq, LANES), 1)
    low = lane < A_HEAD_DIM
    heads_per_slab = LANES // A_HEAD_DIM
    group = A_HEADS // A_KV_HEADS

    def run(klen):
        k = k_ref[0, :klen, :]
        v = v_ref[0, :klen, :]
        for m in range(A_Q // LANES):
            parts = []
            for hh in range(heads_per_slab):
                h = m * heads_per_slab + hh
                kvh = h // group
                s = _mm_nt(q_ref[0, h], k)
                p = jnp.exp(s - jnp.max(s, axis=-1, keepdims=True))
                l = jnp.sum(p, axis=-1, keepdims=True)
                o = _mm(p.astype(BF16), v) / l
                parts.append(o if kvh == hh else pltpu.roll(o, A_HEAD_DIM, 1))
            o_ref[0, :, m * LANES:(m + 1) * LANES] = jnp.where(low, parts[0], parts[1]).astype(o_ref.dtype)

    @pl.when(j * tq < lc)
    def _():
        run(lc)

    @pl.when(j * tq >= lc)
    def _():
        run(k_ref.shape[1])


def _attention(q, k, v, lc):
    bsz, _, ln, _ = q.shape
    tq = TOK_TILE
    return pl.pallas_call(
        functools.partial(_attn_kernel, lc=lc),
        grid=(bsz, ln // tq),
        in_specs=[pl.BlockSpec((1, A_HEADS, tq, LANES), lambda b, j: (b, 0, j, 0)),
                  pl.BlockSpec((1, ln, A_KV), lambda b, j: (b, 0, 0)),
                  pl.BlockSpec((1, ln, A_KV), lambda b, j: (b, 0, 0))],
        out_specs=pl.BlockSpec((1, tq, A_Q), lambda b, j: (b, j, 0)),
        out_shape=jax.ShapeDtypeStruct((bsz, ln, A_Q), BF16),
        compiler_params=_params("arbitrary", "arbitrary"),
        name="attention",
    )(q, k, v)


def _shifted_rows(z, prev_row, next_row):
    tm = z.shape[0]
    row = lax.broadcasted_iota(jnp.int32, (tm, 1), 0)
    zprev = jnp.where(row == 0, prev_row, pltpu.roll(z, 1, 0))
    znext = jnp.where(row == tm - 1, next_row, pltpu.roll(z, tm - 1, 0))
    return zprev, znext


def _halo_rows(zp_ref, zn_ref, j, lct, nj):
    prev_ok = jnp.logical_and(j != 0, j != lct)
    next_ok = jnp.logical_and(j != lct - 1, j != nj - 1)
    prev_row = jnp.where(prev_ok, zp_ref[0, 7:8, :], 0.0)
    next_row = jnp.where(next_ok, zn_ref[0, 0:1, :], 0.0)
    return prev_row, next_row


def _halo_specs(tm, width, ln):
    nb8 = tm // 8
    last = ln // 8 - 1
    return [pl.BlockSpec((1, tm, width), lambda b, j: (b, j, 0)),
            pl.BlockSpec((1, 8, width), lambda b, j: (b, jnp.maximum(j * nb8 - 1, 0), 0)),
            pl.BlockSpec((1, 8, width), lambda b, j: (b, jnp.minimum((j + 1) * nb8, last), 0))]


def _per_head_sum(x, bones):
    return jnp.concatenate(
        [_bmm(x[:, m * LANES:(m + 1) * LANES], bones) for m in range(x.shape[1] // LANES)], axis=1)


def _rwkv_pre_kernel(z_ref, zp_ref, zn_ref, mu_ref, w0_ref, w2_ref, a0_ref, a2_ref, g2_ref,
                     kk_ref, ka_ref, rk_ref, bones_ref,
                     r_o, v_o, kkn_o, g_o, bonus_o, lw_o, a_o, kd_o, *, lct):
    j = pl.program_id(1)
    z = z_ref[0]
    prev_row, next_row = _halo_rows(zp_ref, zn_ref, j, lct, pl.num_programs(1))
    zprev, znext = _shifted_rows(z, prev_row, next_row)
    zm = z + mu_ref[...] * (0.5 * (zprev + znext) - z)
    w = B_WIDTH
    r, k, v = zm[:, 0:w], zm[:, w:2 * w], zm[:, 2 * w:3 * w]
    wa = zm[:, 3 * w:3 * w + B_DECAY_LORA + B_AAA_LORA]
    gl = zm[:, 3 * w + B_DECAY_LORA + B_AAA_LORA:]
    bones = bones_ref[...]
    kk = k * kk_ref[...]
    kkn = kk / jnp.maximum(jnp.sqrt(_per_head_sum(kk * kk, bones)), 1e-12)
    r_o[0] = r.astype(r_o.dtype)
    v_o[0] = v.astype(v_o.dtype)
    kkn_o[0] = kkn.astype(kkn_o.dtype)
    g_o[0] = _bmm(_sigmoid(gl), g2_ref[...]).astype(g_o.dtype)
    twa = jnp.tanh(wa)
    bonus = jnp.zeros_like(r)
    for d in range(2):
        wraw = w0_ref[d] + _bmm(twa, w2_ref[d])
        lw_o[d, 0] = -jnp.exp(-_softplus(-wraw) - 0.5)
        a = _sigmoid(a0_ref[d] + _bmm(wa, a2_ref[d]))
        kd = k * (1.0 + (a - 1.0) * ka_ref[...])
        a_o[d, 0] = a.astype(a_o.dtype)
        kd_o[d, 0] = kd.astype(kd_o.dtype)
        bonus = bonus + _per_head_sum(r * kd * rk_ref[...], bones) * v
    bonus_o[0] = bonus.astype(bonus_o.dtype)


def _rwkv_pre(zr, mu, w0, w2, a0, a2, g2, k_k, k_a, r_k, lc):
    bsz, ln, width = zr.shape
    tm = TOK_TILE
    w = B_WIDTH
    zeros = jnp.zeros((2, B_DECAY_LORA, w), F32)
    w2p = jnp.concatenate([w2, zeros], axis=1)
    a2p = jnp.concatenate([zeros, a2], axis=1)
    bones = _head_block_ones(B_HEAD_DIM, 1.0)
    full = lambda *shape: pl.BlockSpec(shape, lambda b, j: (0,) * len(shape))
    tok = pl.BlockSpec((1, tm, w), lambda b, j: (b, j, 0))
    tok2 = pl.BlockSpec((2, 1, tm, w), lambda b, j: (0, b, j, 0))
    sd = jax.ShapeDtypeStruct((bsz, ln, w), BF16)
    sd2 = jax.ShapeDtypeStruct((2, bsz, ln, w), BF16)
    lw_sd = jax.ShapeDtypeStruct((2, bsz, ln, w), F32)
    return pl.pallas_call(
        functools.partial(_rwkv_pre_kernel, lct=lc // tm),
        grid=(bsz, ln // tm),
        in_specs=_halo_specs(tm, width, ln) + [
            full(1, width), full(2, 1, w), full(2, 2 * B_DECAY_LORA, w), full(2, 1, w),
            full(2, 2 * B_AAA_LORA, w), full(B_GATE_LORA, w), full(1, w), full(1, w), full(1, w),
            full(LANES, LANES)],
        out_specs=[tok, tok, tok, tok, tok, tok2, tok2, tok2],
        out_shape=[sd, sd, sd, sd, sd, lw_sd, sd2, sd2],
        compiler_params=_params("arbitrary", "arbitrary"),
        name="rwkv_pre",
    )(zr, zr, zr, mu.reshape(1, width), w0.reshape(2, 1, w), w2p, a0.reshape(2, 1, w), a2p, g2,
      k_k.reshape(1, w), k_a.reshape(1, w), r_k.reshape(1, w), bones)


def _scan_chunk(i, d, nctx, ntot):
    rev = jnp.where(i < nctx, nctx - 1 - i, ntot + nctx - 1 - i)
    return jnp.where(d == 0, i, rev)


def _order_masks(n, pair):
    t = np.arange(n * pair)
    same = (t[:, None] // n) == (t[None, :] // n)
    tt, ss = t[:, None] % n, t[None, :] % n
    incl = np.stack([same & (ss <= tt), same & (ss >= tt)]).astype(np.float32)
    strict = np.stack([same & (ss < tt), same & (ss > tt)]).astype(np.float32)
    return jnp.asarray(incl), jnp.asarray(strict)


def _pair_stack(x, low):
    return jnp.concatenate([jnp.where(low, x, 0.0), jnp.where(low, 0.0, x)], axis=0)


def _bmm(a, b):
    return _mm(a.astype(BF16), b.astype(BF16))


def _rwkv_chunk_kernel(rf_ref, vf_ref, kf_ref, lwf_ref, af_ref, kdf_ref,
                       rb_ref, vb_ref, kb_ref, lwb_ref, ab_ref, kdb_ref,
                       tri_ref, incl_ref, strict_ref, of_ref, ob_ref, state_ref):
    c = R_CHUNK
    n2 = 2 * c

    @pl.when(pl.program_id(1) == 0)
    def _():
        state_ref[...] = jnp.zeros_like(state_ref)

    low = lax.broadcasted_iota(jnp.int32, (c, LANES), 1) < B_HEAD_DIM
    eye = (lax.broadcasted_iota(jnp.int32, (LANES, LANES), 0)
           == lax.broadcasted_iota(jnp.int32, (LANES, LANES), 1)).astype(F32)
    stack = lambda x: _pair_stack(x, low)
    refs = ((rf_ref, vf_ref, kf_ref, lwf_ref, af_ref, kdf_ref, of_ref),
            (rb_ref, vb_ref, kb_ref, lwb_ref, ab_ref, kdb_ref, ob_ref))
    dirs, slabs, outs = [], [], []
    kts, bts, kds, rts, vs, pcs = [], [], [], [], [], []
    for d, (r_ref, v_ref, kkn_ref, lw_ref, a_ref, kd_ref, o_ref) in enumerate(refs):
        lw = lw_ref[0, 0]
        cum = _mm(tri_ref[d], lw, HI)
        p_in = jnp.exp(cum)
        p_inv = jnp.exp(-cum)
        p_ex = jnp.exp(cum - lw)
        p_all = jnp.exp(jnp.sum(lw, axis=0, keepdims=True))
        kkn = kkn_ref[0].astype(F32)
        kt = kkn * p_ex
        bt = kkn * a_ref[0, 0].astype(F32) * p_inv
        kd = kd_ref[0, 0].astype(F32) * p_inv
        rt = r_ref[0].astype(F32) * p_in
        v = v_ref[0].astype(F32)
        for m in range(B_WIDTH // LANES):
            sl = slice(m * LANES, (m + 1) * LANES)
            dirs.append(d)
            slabs.append(m)
            outs.append(o_ref)
            kts.append(stack(kt[:, sl]).astype(BF16))
            bts.append(stack(bt[:, sl]).astype(BF16))
            kds.append(stack(kd[:, sl]).astype(BF16))
            rts.append(stack(rt[:, sl]))
            vs.append(stack(v[:, sl]).astype(BF16))
            pcs.append(p_all[:, sl])
    n = len(dirs)
    incl = [incl_ref[d] > 0.5 for d in range(2)]
    strict = [strict_ref[d] > 0.5 for d in range(2)]
    gram = [_mm_nt(jnp.concatenate([kts[x], rts[x].astype(BF16)], axis=0),
                   jnp.concatenate([bts[x], kds[x]], axis=0)) for x in range(n)]
    a_b = [jnp.where(strict[dirs[x]], gram[x][:n2, :n2], 0.0) for x in range(n)]
    a_k = [jnp.where(strict[dirs[x]], gram[x][:n2, n2:], 0.0) for x in range(n)]
    m_b = [jnp.where(incl[dirs[x]], gram[x][n2:, :n2], 0.0) for x in range(n)]
    m_k = [jnp.where(incl[dirs[x]], gram[x][n2:, n2:], 0.0) for x in range(n)]
    pw = [-a for a in a_b]
    tinv = [eye + p for p in pw]
    for _ in range(int(math.log2(c)) - 1):
        pw = [_bmm(p, p) for p in pw]
        tinv = [t + _bmm(t, p) for t, p in zip(tinv, pw)]
    mv = [_bmm(jnp.concatenate([a_k[x], m_k[x]], axis=0), vs[x]) for x in range(n)]
    wy = [_bmm(tinv[x], jnp.concatenate([kts[x], mv[x][:n2].astype(BF16)], axis=1)) for x in range(n)]
    mbwy = [_bmm(m_b[x], wy[x]) for x in range(n)]
    wyb = [_mm_tn(wy[x].astype(BF16), bts[x]) for x in range(n)]
    vtk = [_mm_tn(vs[x], kds[x]) for x in range(n)]
    s0 = [state_ref[dirs[x], slabs[x]] for x in range(n)]
    o_st = [_mm_nt((rts[x] - mbwy[x][:, :LANES]).astype(BF16), s0[x].astype(BF16))
            + (mv[x][n2:] - mbwy[x][:, LANES:]) for x in range(n)]
    for x in range(n):
        g = (eye - wyb[x][:LANES]) * pcs[x]
        h = (vtk[x] - wyb[x][LANES:]) * pcs[x]
        outs[x][0, :, slabs[x] * LANES:(slabs[x] + 1) * LANES] = (o_st[x][:c] + o_st[x][c:]).astype(outs[x].dtype)
        state_ref[dirs[x], slabs[x]] = _bmm(s0[x], g) + h


def _rwkv_chunks(r, v, kkn, lw, a, kd, lc):
    bsz, ln, w = r.shape
    c = R_CHUNK
    nctx, ntot = lc // c, ln // c
    tri, _ = _order_masks(c, 1)
    incl, strict = _order_masks(c, 2)
    specs = []
    for d in range(2):
        tok = pl.BlockSpec((1, c, w), lambda b, i, d=d: (b, _scan_chunk(i, d, nctx, ntot), 0))
        tok2 = pl.BlockSpec((1, 1, c, w), lambda b, i, d=d: (d, b, _scan_chunk(i, d, nctx, ntot), 0))
        specs.append((tok, tok2))
    full = lambda *shape: pl.BlockSpec(shape, lambda b, i: (0,) * len(shape))
    sd = jax.ShapeDtypeStruct((bsz, ln, w), BF16)
    return pl.pallas_call(
        _rwkv_chunk_kernel,
        grid=(bsz, ntot),
        in_specs=[specs[0][0]] * 3 + [specs[0][1]] * 3 + [specs[1][0]] * 3 + [specs[1][1]] * 3
        + [full(2, c, c), full(2, 2 * c, 2 * c), full(2, 2 * c, 2 * c)],
        out_specs=[specs[0][0], specs[1][0]],
        out_shape=[sd, sd],
        scratch_shapes=[pltpu.VMEM((2, w // LANES, LANES, LANES), F32)],
        compiler_params=_params("arbitrary", "arbitrary"),
        name="rwkv_chunks",
    )(r, v, kkn, lw, a, kd, r, v, kkn, lw, a, kd, tri, incl, strict)


def _even_out_kernel(xc_ref, xl_ref, mod_ref, attn_ref, wkvf_ref, wkvb_ref, bonus_ref, g_ref, gw_ref,
                     gb_ref, bones_ref, w_ref, oc_ref, ol_ref, *, lct):
    bones = bones_ref[...]
    y = bonus_ref[0].astype(F32)
    for wkv_ref in (wkvf_ref, wkvb_ref):
        o = wkv_ref[0].astype(F32)
        dlt = o - _per_head_sum(o, bones)
        var = _per_head_sum(dlt * dlt, bones)
        y = y + dlt * lax.rsqrt(var + B_GN_EPS) * gw_ref[...] + gb_ref[...]
    y = y * g_ref[0].astype(F32)
    mix = _mm(attn_ref[0].astype(BF16), w_ref[:A_Q, :]) + _mm(y.astype(BF16), w_ref[A_Q:, :])
    res = _stream_tile(xc_ref, xl_ref, lct) + mod_ref[0, 0][2:3] * mix
    is_ctx = pl.program_id(1) < lct

    @pl.when(is_ctx)
    def _():
        oc_ref[0] = res

    @pl.when(jnp.logical_not(is_ctx))
    def _():
        ol_ref[0] = res


def _even_out(xc, xl, modx, attn, wkv, bonus, g, gn_w, gn_b, w_out):
    bsz, lc, d = xc.shape
    seq = xl.shape[1]
    tm = TOK_TILE
    lct = lc // tm
    w = B_WIDTH
    bones = _head_block_ones(B_HEAD_DIM, 1.0 / B_HEAD_DIM)
    tokw = lambda n: pl.BlockSpec((1, tm, n), lambda b, j: (b, j, 0))
    full = lambda *shape: pl.BlockSpec(shape, lambda b, j: (0,) * len(shape))
    streams = _stream_specs(tm, d, lct)
    return pl.pallas_call(
        functools.partial(_even_out_kernel, lct=lct),
        grid=(bsz, (lc + seq) // tm),
        in_specs=streams + [
            pl.BlockSpec((1, 1, N_MOD, d), lambda b, j: (b, jnp.where(j < lct, 0, 1), 0, 0)),
            tokw(A_Q), tokw(w), tokw(w),
            tokw(w), tokw(w), full(1, w), full(1, w), full(LANES, LANES),
            full(A_Q + w, d)],
        out_specs=streams,
        out_shape=[jax.ShapeDtypeStruct((bsz, lc, d), F32), jax.ShapeDtypeStruct((bsz, seq, d), F32)],
        compiler_params=_params("arbitrary", "arbitrary"),
        name="even_out",
    )(xc, xl, modx, attn, wkv[0], wkv[1], bonus, g, gn_w.reshape(1, w), gn_b.reshape(1, w), bones, w_out)


def _ssd_pre_kernel(z_ref, zp_ref, zn_ref, dt_ref, cw_ref, cb_ref, dtb_ref, x_o, dt_o, *, lct):
    j = pl.program_id(1)
    z = z_ref[0]
    prev_row, next_row = _halo_rows(zp_ref, zn_ref, j, lct, pl.num_programs(1))
    zprev, znext = _shifted_rows(z, prev_row, next_row)
    y = zprev * cw_ref[0:1, :] + z * cw_ref[1:2, :] + znext * cw_ref[2:3, :] + cb_ref[...]
    x_o[0] = _silu(y).astype(x_o.dtype)
    dt_o[0] = _softplus(dt_ref[0] + dtb_ref[...])


def _ssd_pre(xbc, dtraw, conv_w, conv_b, dtb, lc):
    bsz, ln, width = xbc.shape
    tm = TOK_TILE
    nd = dtraw.shape[-1]
    full = lambda *shape: pl.BlockSpec(shape, lambda b, j: (0,) * len(shape))
    return pl.pallas_call(
        functools.partial(_ssd_pre_kernel, lct=lc // tm),
        grid=(bsz, ln // tm),
        in_specs=_halo_specs(tm, width, ln) + [
            pl.BlockSpec((1, tm, nd), lambda b, j: (b, j, 0)),
            full(conv_w.shape[0], width), full(1, width), full(1, nd)],
        out_specs=[pl.BlockSpec((1, tm, width), lambda b, j: (b, j, 0)),
                   pl.BlockSpec((1, tm, nd), lambda b, j: (b, j, 0))],
        out_shape=[jax.ShapeDtypeStruct((bsz, ln, width), BF16),
                   jax.ShapeDtypeStruct((bsz, ln, nd), F32)],
        compiler_params=_params("arbitrary", "arbitrary"),
        name="ssd_pre",
    )(xbc, xbc, xbc, dtraw, conv_w, conv_b.reshape(1, width), dtb)


def _ssd_chunk_kernel(x_ref, b_ref, c_ref, dt_ref, alog_ref, incl_ref, expand_ref, o_ref, state_ref,
                      *, n_heads):
    q = C_CHUNK
    i = pl.program_id(2)

    @pl.when(i == 0)
    def _():
        state_ref[...] = jnp.zeros_like(state_ref)

    lane = lax.broadcasted_iota(jnp.int32, (1, LANES), 1)
    a_neg = jnp.where(lane < n_heads, -jnp.exp(alog_ref[0]), 0.0)
    dt = dt_ref[0]
    dta = dt * a_neg
    incl_f = incl_ref[0]
    cum = _mm(incl_f, dta, HI)
    cum_t = cum.T
    total = jnp.sum(dta, axis=0, keepdims=True)
    expand = expand_ref[...]
    dt_x = _bmm(dt, expand)
    ec_x = _bmm(jnp.exp(cum), expand)
    de_x = _bmm(jnp.exp(total - cum), expand)
    cd_x = _bmm(jnp.broadcast_to(jnp.exp(total), (BF16_ROWS, LANES)), expand)[0:1]
    incl = incl_f > 0.5
    low = lax.broadcasted_iota(jnp.int32, (q, LANES), 1) < C_HEAD_DIM
    heads_per_group = n_heads // C_GROUPS
    pairs_per_group = heads_per_group // 2
    gw = heads_per_group * C_HEAD_DIM
    for g in range(C_GROUPS):
        gsl = slice(g * gw, (g + 1) * gw)
        bg = b_ref[0, :, g * C_STATE:(g + 1) * C_STATE].astype(BF16)
        cg = c_ref[0, :, g * C_STATE:(g + 1) * C_STATE].astype(BF16)
        cb = _mm_nt(cg, bg)
        xd = x_ref[0, :, gsl].astype(F32) * dt_x[:, gsl]
        s_prev = state_ref[g]
        y_off = _mm(cg, s_prev.astype(BF16)) * ec_x[:, gsl]
        state_ref[g] = s_prev * cd_x[:, gsl] + _mm_tn(bg, (xd * de_x[:, gsl]).astype(BF16))
        for pp in range(pairs_per_group):
            pr = g * pairs_per_group + pp
            sl = slice(pr * LANES, (pr + 1) * LANES)
            ms = []
            for hh in range(2):
                h = 2 * pr + hh
                seg = cum[:, h:h + 1] - cum_t[h:h + 1, :]
                ms.append((cb * jnp.exp(jnp.where(incl, seg, -1e30))).astype(BF16))
            xd_pair = _pair_stack(xd[:, pp * LANES:(pp + 1) * LANES], low).astype(BF16)
            y_diag = _mm(jnp.concatenate(ms, axis=1), xd_pair)
            o_ref[0, 0, :, sl] = (y_diag + y_off[:, pp * LANES:(pp + 1) * LANES]).astype(o_ref.dtype)


def _ssd_chunks(xact, dtact, a_log, lc):
    bsz, ln, _ = xact.shape
    q = C_CHUNK
    n_heads = a_log.shape[1]
    inner = n_heads * C_HEAD_DIM
    gn = C_GROUPS * C_STATE
    nctx, ntot = lc // q, ln // q
    incl, _ = _order_masks(q, 1)
    expand = np.zeros((LANES, inner), np.float32)
    expand[np.arange(inner) // C_HEAD_DIM, np.arange(inner)] = 1.0
    alog = jnp.zeros((2, 1, LANES), F32).at[:, 0, :n_heads].set(a_log)
    cmap = lambda off: (lambda b, d, i: (b, _scan_chunk(i, d, nctx, ntot), off))
    return pl.pallas_call(
        functools.partial(_ssd_chunk_kernel, n_heads=n_heads),
        grid=(bsz, 2, ntot),
        in_specs=[pl.BlockSpec((1, q, inner), cmap(0)),
                  pl.BlockSpec((1, q, gn), cmap(inner // gn)),
                  pl.BlockSpec((1, q, gn), cmap(inner // gn + 1)),
                  pl.BlockSpec((1, q, LANES), lambda b, d, i: (b, _scan_chunk(i, d, nctx, ntot), d)),
                  pl.BlockSpec((1, 1, LANES), lambda b, d, i: (d, 0, 0)),
                  pl.BlockSpec((1, q, q), lambda b, d, i: (d, 0, 0)),
                  pl.BlockSpec((LANES, inner), lambda b, d, i: (0, 0))],
        out_specs=pl.BlockSpec((1, 1, q, inner), lambda b, d, i: (d, b, _scan_chunk(i, d, nctx, ntot), 0)),
        out_shape=jax.ShapeDtypeStruct((2, bsz, ln, inner), BF16),
        scratch_shapes=[pltpu.VMEM((C_GROUPS, C_STATE, inner // C_GROUPS), F32)],
        compiler_params=_params("arbitrary", "arbitrary", "arbitrary"),
        name="ssd_chunks",
    )(xact, xact, xact, dtact, alog, incl, jnp.asarray(expand, dtype=BF16))


def _odd_out_kernel(x_ref, mod_ref, ys_ref, xs_ref, zg_ref, dsk_ref, nw_ref, w_ref, o_ref):
    y = (ys_ref[0, 0].astype(F32) + ys_ref[1, 0].astype(F32)) + dsk_ref[...] * xs_ref[0].astype(F32)
    yg = y * _silu(zg_ref[0].astype(F32))
    inner = yg.shape[1]
    gw = inner // C_GROUPS
    parts = []
    for g in range(C_GROUPS):
        t = yg[:, g * gw:(g + 1) * gw]
        ms = jnp.mean(t * t, axis=-1, keepdims=True)
        parts.append((t * lax.rsqrt(ms + EPS) * nw_ref[:, g * gw:(g + 1) * gw]).astype(BF16))
    yn = jnp.concatenate(parts, axis=1)
    o_ref[0] = x_ref[0] + mod_ref[0, 0][2:3] * _mm(yn, w_ref[...])


def _odd_out(x, modx, ys, xact, zg, d_skip, norm_w, w_out, lc):
    bsz, seq, d = x.shape
    ln = lc + seq
    tm = TOK_TILE
    lct = lc // tm
    inner = zg.shape[-1]
    dsk = jnp.repeat(d_skip, C_HEAD_DIM).reshape(1, inner)
    tokw = lambda n: pl.BlockSpec((1, tm, n), lambda b, j: (b, j + lct, 0))
    full = lambda *shape: pl.BlockSpec(shape, lambda b, j: (0,) * len(shape))
    return pl.pallas_call(
        _odd_out_kernel,
        grid=(bsz, (ln - lc) // tm),
        in_specs=[pl.BlockSpec((1, tm, d), lambda b, j: (b, j, 0)),
                  pl.BlockSpec((1, 1, N_MOD, d), lambda b, j: (b, 1, 0, 0)),
                  pl.BlockSpec((2, 1, tm, inner), lambda b, j: (0, b, j + lct, 0)),
                  tokw(inner), tokw(inner), full(1, inner), full(1, inner), full(inner, d)],
        out_specs=pl.BlockSpec((1, tm, d), lambda b, j: (b, j, 0)),
        out_shape=jax.ShapeDtypeStruct((bsz, ln - lc, d), F32),
        compiler_params=_params("arbitrary", "arbitrary"),
        name="odd_out",
    )(x, modx, ys, xact, zg, dsk, norm_w.reshape(1, inner), w_out)


class _Group:
    def __init__(self, parts):
        self.parts = tuple(parts)


def _ew(f, *args):
    groups = [a for a in args if isinstance(a, _Group)]
    if not groups:
        return f(*args)
    return _Group(f(*[a.parts[m] if isinstance(a, _Group) else a for a in args])
                  for m in range(len(groups[0].parts)))


def _exchange(xs, i, l, descending=True):
    hi, lo = _ew(jnp.maximum, xs[i], xs[l]), _ew(jnp.minimum, xs[i], xs[l])
    xs[i], xs[l] = (hi, lo) if descending else (lo, hi)


def _bitonic_merge(xs):
    xs = list(xs)
    j = len(xs) // 2
    while j >= 1:
        for i in range(len(xs)):
            if i ^ j > i:
                _exchange(xs, i, i ^ j)
        j //= 2
    return xs


def _bitonic_sort(xs):
    xs = list(xs)
    k = 2
    while k <= len(xs):
        j = k // 2
        while j >= 1:
            for i in range(len(xs)):
                if i ^ j > i:
                    _exchange(xs, i, i ^ j, descending=(i & k) == 0)
            j //= 2
        k *= 2
    return xs


def _merge_sublanes(xs):
    n = len(xs)
    for shift in (4, 2, 1):
        other = [_ew(lambda v: pltpu.roll(v, shift, 0), x) for x in xs]
        xs = _bitonic_merge([_ew(jnp.maximum, xs[k], other[n - 1 - k]) for k in range(n)])
    return xs


def _row_tile(s, k):
    return _ew(lambda v: v[8 * k:8 * k + 8], s)


def _top_values(s):
    return _merge_sublanes(_bitonic_sort([_row_tile(s, k) for k in range(P_KEYS // 8)]))


def _top_pair_sums(t1, t2, shape):
    sub = lax.broadcasted_iota(jnp.int32, shape, 0)
    lo, hi = t2[0], t2[8]
    for b in range(1, 8):
        lo = _ew(lambda x, y: jnp.where(sub == b, x, y), t2[b], lo)
        hi = _ew(lambda x, y: jnp.where(sub == b, x, y), t2[8 + b], hi)
    xs = [_ew(jnp.add, t1[a], lo) for a in range(P_TOPK)]
    carry = _ew(jnp.add, t1[0], hi)
    for k in range(P_TOPK):
        xs[k], carry = _ew(jnp.maximum, xs[k], carry), _ew(jnp.minimum, xs[k], carry)
    return _merge_sublanes(xs)


def _prefix_count(test, vals):
    sel = functools.partial(_ew, jnp.where)
    b3 = test(vals[7])
    b2 = test(sel(b3, vals[11], vals[3]))
    b1 = test(sel(b3, sel(b2, vals[13], vals[9]), sel(b2, vals[5], vals[1])))
    b0 = test(sel(b3, sel(b2, sel(b1, vals[14], vals[12]), sel(b1, vals[10], vals[8])),
                  sel(b2, sel(b1, vals[6], vals[4]), sel(b1, vals[2], vals[0]))))
    n = _ew(lambda p3, p2, p1, p0: (jnp.where(p3, 8.0, 0.0) + jnp.where(p2, 4.0, 0.0))
            + (jnp.where(p1, 2.0, 0.0) + jnp.where(p0, 1.0, 0.0)), b3, b2, b1, b0)
    return _ew(lambda full, m: jnp.where(full, 16.0, m), test(vals[15]), n)


def _peer_query_kernel(x_ref, mod_ref, g_ref, wq_ref, keys_ref, ht_o, n_o, c_o, rank_o, e2_o):
    assert P_TOPK == 16 and P_KEYS == 128
    mod = mod_ref[0]
    h2 = _norm_mod(x_ref[...], g_ref[...], mod[3:4], mod[4:5])
    ht = h2.T.astype(BF16)
    ht_o[...] = ht
    qt = _mm(wq_ref[...], ht).astype(BF16)
    kd = P_KEY_DIM
    tile_shape = (8, ht.shape[1])
    for h0 in range(0, P_HEADS, PEER_HEAD_GROUP):
        heads = range(h0, h0 + PEER_HEAD_GROUP)
        s1 = _Group(_mm(keys_ref[h, 0], qt[(2 * h) * kd:(2 * h + 1) * kd]) for h in heads)
        s2 = _Group(_mm(keys_ref[h, 1], qt[(2 * h + 1) * kd:(2 * h + 2) * kd]) for h in heads)
        t1 = _top_values(s1)
        t2 = _top_values(s2)
        best = _top_pair_sums(t1, t2, tile_shape)
        z = _ew(jnp.ones_like, best[0])
        for k in range(1, P_TOPK):
            z = _ew(lambda acc, v, top: acc + jnp.exp(v - top), z, best[k], best[0])
        inv_z = _ew(lambda v: 0.5 / v, z)
        tau = best[P_TOPK - 1]
        ranks, e2s = [], []
        for k in range(P_KEYS // 8):
            rows = slice(8 * k, 8 * k + 8)
            s1k, s2k = _row_tile(s1, k), _row_tile(s2, k)
            nk = _prefix_count(lambda v: _ew(lambda s, x, t: (s + x) >= t, s1k, v, tau), t2)
            ck = _ew(lambda s, top, iz: jnp.exp(s - top) * iz, s1k, t1[0], inv_z)
            ranks.append(_prefix_count(lambda v: _ew(lambda s, x: x > s, s2k, v), t2))
            e2s.append(_ew(lambda s, top: jnp.exp(s - top), s2k, t2[0]))
            for m, h in enumerate(heads):
                n_o[h, rows] = nk.parts[m]
                c_o[h, rows] = ck.parts[m]
        for m, h in enumerate(heads):
            rank_o[h] = jnp.concatenate([r.parts[m] for r in ranks], axis=0).astype(BF16)
            e2_o[h] = jnp.concatenate([e.parts[m] for e in e2s], axis=0).astype(BF16)


def _peer_query(xt, mod, gain, wq_t, keys, blocks_per_mod, tt):
    t_all, d = xt.shape
    nq = wq_t.shape[0]
    nblk = t_all // tt
    head = pl.BlockSpec((P_HEADS, P_KEYS, tt), lambda i: (0, 0, i))
    hsd = lambda dt: jax.ShapeDtypeStruct((P_HEADS, P_KEYS, t_all), dt)
    wspec = pl.BlockSpec((nq, d), lambda i: (0, 0))
    kspec = pl.BlockSpec((P_HEADS, 2, P_KEYS, P_KEY_DIM), lambda i: (0, 0, 0, 0))
    return pl.pallas_call(
        _peer_query_kernel,
        grid=(nblk,),
        in_specs=[pl.BlockSpec((tt, d), lambda i: (i, 0)),
                  pl.BlockSpec((1, N_MOD, d), lambda i: (i // blocks_per_mod, 0, 0)),
                  pl.BlockSpec((1, d), lambda i: (0, 0)),
                  wspec, kspec],
        out_specs=[pl.BlockSpec((d, tt), lambda i: (0, i)), head, head, head, head],
        out_shape=[jax.ShapeDtypeStruct((d, t_all), BF16), hsd(F32), hsd(F32), hsd(BF16), hsd(BF16)],
        compiler_params=_params("arbitrary"),
        name="peer_query",
    )(xt, mod, gain.reshape(1, d), wq_t.astype(BF16), keys.astype(BF16))


def _twice_gelu(x):
    return x * (1.0 + lax.erf(x * (2.0 ** -0.5)))


def _peer_main_kernel(x_ref, mod_ref, ht_ref, u_ref, vt_ref, n_ref, c_ref, rank_ref, e2_ref,
                      o_ref, acc_ref, act_ref, coef_ref):
    g = pl.program_id(1)
    last = pl.num_programs(1) - 1
    tt = ht_ref.shape[1]
    rows_per_step = u_ref.shape[0] // P_KEYS
    tiles = P_KEYS // BF16_ROWS
    tw = PEER_COEF_TOK

    def consume():
        acc_ref[...] += _mm(vt_ref[...], coef_ref[...] * act_ref[...])

    def prepare():
        zero = jnp.zeros((BF16_ROWS, tw), BF16)
        row0 = pl.multiple_of(g * rows_per_step, rows_per_step)
        n_rows = [n_ref[h, pl.ds(row0, rows_per_step), :] for h in range(P_HEADS)]
        c_rows = [c_ref[h, pl.ds(row0, rows_per_step), :] for h in range(P_HEADS)]
        for ii in range(rows_per_step):
            for t0 in range(0, tt, tw):
                tok = slice(t0, t0 + tw)
                coef = [None] * tiles
                for h in range(P_HEADS):
                    nb = jnp.broadcast_to(n_rows[h][ii:ii + 1, tok], (BF16_ROWS, tw)).astype(BF16)
                    cb = jnp.broadcast_to(c_rows[h][ii:ii + 1, tok], (BF16_ROWS, tw)).astype(BF16)
                    for k in range(tiles):
                        rows = slice(k * BF16_ROWS, (k + 1) * BF16_ROWS)
                        term = jnp.where(rank_ref[h, rows, tok] < nb, e2_ref[h, rows, tok] * cb, zero)
                        coef[k] = term if coef[k] is None else coef[k] + term
                for k in range(tiles):
                    rows = slice(ii * P_KEYS + k * BF16_ROWS, ii * P_KEYS + (k + 1) * BF16_ROWS)
                    coef_ref[rows, tok] = coef[k]
        act_ref[...] = _twice_gelu(_mm(u_ref[...], ht_ref[...])).astype(BF16)

    @pl.when(g == 0)
    def _():
        acc_ref[...] = jnp.zeros_like(acc_ref)
        prepare()

    @pl.when(jnp.logical_and(g > 0, g < last))
    def _():
        consume()
        prepare()

    @pl.when(g == last)
    def _():
        consume()
        o_ref[...] = x_ref[...] + mod_ref[0][5:6] * acc_ref[...].T


def _peer_main(xt, mod, ht, u_all, vt_all, layer, n, c, rank, e2, blocks_per_mod, tt):
    t_all, d = xt.shape
    n_exp = u_all.shape[1]
    eb = PEER_EXP
    nsteps = n_exp // eb
    head = pl.BlockSpec((P_HEADS, P_KEYS, tt), lambda i, s: (0, 0, i))
    return pl.pallas_call(
        _peer_main_kernel,
        grid=(t_all // tt, nsteps + 1),
        in_specs=[pl.BlockSpec((tt, d), lambda i, s: (i, 0)),
                  pl.BlockSpec((1, N_MOD, d), lambda i, s: (i // blocks_per_mod, 0, 0)),
                  pl.BlockSpec((d, tt), lambda i, s: (0, i)),
                  pl.BlockSpec((None, eb, d), lambda i, s: (layer, jnp.minimum(s, nsteps - 1), 0)),
                  pl.BlockSpec((None, d, eb), lambda i, s: (layer, 0, jnp.maximum(s - 1, 0))),
                  head, head, head, head],
        out_specs=pl.BlockSpec((tt, d), lambda i, s: (i, 0)),
        out_shape=jax.ShapeDtypeStruct((t_all, d), F32),
        scratch_shapes=[pltpu.VMEM((d, tt), F32), pltpu.VMEM((eb, tt), BF16), pltpu.VMEM((eb, tt), BF16)],
        compiler_params=_params("arbitrary", "arbitrary"),
        name="peer_main",
    )(xt, mod, ht, u_all, vt_all, n, c, rank, e2)


def _peer_layer(xt, mod, gain, w_q, keys, u_all, vt_all, layer, blocks_per_mod, tt):
    wq_t = w_q.T
    tq = PEER_QUERY_TOK
    ht, n, c, rank, e2 = _peer_query(xt, mod, gain, wq_t, keys, blocks_per_mod * (tt // tq), tq)
    return _peer_main(xt, mod, ht, u_all, vt_all, layer, n, c, rank, e2, blocks_per_mod, tt)


def kernel(x, c, ctx, c_ctx, mod_w, mod_b, norm1_g, norm2_g, ev_w_in, ev_w_out, attn_q_gain, attn_k_gain,
           rw_mu, rw_w0, rw_w2, rw_a0, rw_a2, rw_g2, rw_k_k, rw_k_a, rw_r_k, rw_gn_w, rw_gn_b, ssd_w_in,
           ssd_conv_w, ssd_conv_b, ssd_dt_bias, ssd_a_log, ssd_d, ssd_norm_w, ssd_w_out, peer_w_q,
           peer_keys, peer_u, peer_v):
    bsz, seq, dm = x.shape
    lc = ctx.shape[1]
    depth = mod_w.shape[0]
    assert lc == TOK_TILE and seq % PEER_TOK == 0 and (bsz * lc) % PEER_TOK == 0

    rows = -(-(bsz + 1) // 8) * 8
    cc = jnp.concatenate([c, c_ctx[None], jnp.zeros((rows - bsz - 1, dm), F32)], axis=0)
    mods = _modulation(cc, mod_w, mod_b)
    xc, xl = ctx, x
    u_all = peer_u.astype(BF16)
    vt_all = jnp.swapaxes(peer_v, 1, 2).astype(BF16)

    for i in range(depth):
        last = i == depth - 1
        j = i // 2
        ml = mods[i, :bsz].reshape(bsz, N_MOD, dm)
        mc = mods[i, bsz].reshape(1, N_MOD, dm)
        modx = jnp.stack([jnp.broadcast_to(mc, (bsz, N_MOD, dm)), ml], axis=1)
        if i % 2 == 0:
            za, zr = _in_proj(xc, xl, modx, norm1_g[i], ev_w_in[j].astype(BF16),
                              ((A_IN, F32), (B_IN, F32)))
            q, k, v = _attn_pre(za, attn_q_gain[j], attn_k_gain[j], lc)
            attn = _attention(q, k, v, lc)
            r, vv, kkn, g, bonus, lw, a, kd = _rwkv_pre(
                zr, rw_mu[j], rw_w0[j], rw_w2[j], rw_a0[j], rw_a2[j], rw_g2[j], rw_k_k[j], rw_k_a[j],
                rw_r_k[j], lc)
            wkv = _rwkv_chunks(r, vv, kkn, lw, a, kd, lc)
            xc, xl = _even_out(xc, xl, modx, attn, wkv, bonus, g, rw_gn_w[j], rw_gn_b[j],
                               ev_w_out[j].astype(BF16))
        else:
            n_heads = ssd_a_log.shape[-1]
            inner = n_heads * C_HEAD_DIM
            conv_dim = inner + 2 * C_GROUPS * C_STATE
            w_in = ssd_w_in[j]
            w_dt = jnp.zeros((dm, 2 * LANES), F32)
            w_dt = w_dt.at[:, :n_heads].set(w_in[:, inner + conv_dim:inner + conv_dim + n_heads])
            w_dt = w_dt.at[:, LANES:LANES + n_heads].set(w_in[:, inner + conv_dim + n_heads:])
            dtb = jnp.zeros((1, 2 * LANES), F32)
            dtb = dtb.at[0, :n_heads].set(ssd_dt_bias[j, 0]).at[0, LANES:LANES + n_heads].set(ssd_dt_bias[j, 1])
            w_all = jnp.concatenate([w_in[:, :inner + conv_dim], w_dt], axis=1).astype(BF16)
            zg, xbc, dtraw = _in_proj(xc, xl, modx, norm1_g[i], w_all,
                                      ((inner, BF16), (conv_dim, F32), (2 * LANES, F32)))
            xact, dtact = _ssd_pre(xbc, dtraw, ssd_conv_w[j], ssd_conv_b[j], dtb, lc)
            ys = _ssd_chunks(xact, dtact, ssd_a_log[j], lc)
            assert last, "the SSD layer keeps only the latent stream"
            xl = _odd_out(xl, modx, ys, xact, zg, ssd_d[j], ssd_norm_w[j], ssd_w_out[j].astype(BF16), lc)

        tt = PEER_TOK
        xl = _peer_layer(xl.reshape(bsz * seq, dm), ml, norm2_g[i], peer_w_q[i], peer_keys[i],
                         u_all, vt_all, i, seq // tt, tt).reshape(bsz, seq, dm)
        if not last:
            xc = _peer_layer(xc.reshape(bsz * lc, dm), mc, norm2_g[i], peer_w_q[i], peer_keys[i],
                             u_all, vt_all, i, (bsz * lc) // tt, tt).reshape(bsz, lc, dm)
    return xl
```

```python
import functools
import math

import numpy as np
import jax
import jax.numpy as jnp
from jax import lax
from jax.experimental import pallas as pl
from jax.experimental.pallas import tpu as pltpu

F32 = jnp.float32
BF16 = jnp.bfloat16
HI = lax.Precision.HIGHEST

GRID_W = 64
EPS = 1e-6
N_MOD = 6
A_HEADS, A_KV_HEADS, A_HEAD_DIM = 8, 2, 64
A_Q = A_HEADS * A_HEAD_DIM
A_KV = A_KV_HEADS * A_HEAD_DIM
A_IN = A_Q + 2 * A_KV
ROPE_THETA = 10000.0
B_HEADS, B_HEAD_DIM = 8, 64
B_WIDTH = B_HEADS * B_HEAD_DIM
B_DECAY_LORA, B_AAA_LORA, B_GATE_LORA = 64, 64, 128
B_IN = 3 * B_WIDTH + B_DECAY_LORA + B_AAA_LORA + B_GATE_LORA
B_GN_EPS = 64e-5
C_HEAD_DIM, C_GROUPS, C_STATE, C_CHUNK = 64, 4, 128, 128
P_HEADS, P_KEYS, P_KEY_DIM, P_TOPK = 8, 128, 128, 16

LANES = 128
BF16_ROWS = 16
TOK_TILE = 256
R_CHUNK = 64
PEER_TOK = 512
PEER_QUERY_TOK = 256
PEER_HEAD_GROUP = 2
PEER_EXP = 1024
PEER_COEF_TOK = 256
VMEM_LIMIT = 56 * 1024 * 1024


def _mm(a, b, prec=None):
    return lax.dot_general(a, b, (((1,), (0,)), ((), ())), precision=prec,
                           preferred_element_type=F32)


def _mm_nt(a, b, prec=None):
    return lax.dot_general(a, b, (((1,), (1,)), ((), ())), precision=prec,
                           preferred_element_type=F32)


def _mm_tn(a, b, prec=None):
    return lax.dot_general(a, b, (((0,), (0,)), ((), ())), precision=prec,
                           preferred_element_type=F32)


def _params(*sem):
    return pltpu.CompilerParams(dimension_semantics=sem, vmem_limit_bytes=VMEM_LIMIT)


def _softplus(x):
    return jnp.maximum(x, 0.0) + jnp.log1p(jnp.exp(-jnp.abs(x)))


def _sigmoid(x):
    return 1.0 / (1.0 + jnp.exp(-x))


def _silu(x):
    return x * _sigmoid(x)


def _mod_kernel(c_ref, w_ref, b_ref, o_ref):
    o_ref[0] = _mm(_silu(c_ref[...]), w_ref[0], HI) + b_ref[0]


def _modulation(cc, mod_w, mod_b):
    depth, d, n6 = mod_w.shape
    rows = cc.shape[0]
    tn = 1024
    return pl.pallas_call(
        _mod_kernel,
        grid=(depth, n6 // tn),
        in_specs=[pl.BlockSpec((rows, d), lambda l, n: (0, 0)),
                  pl.BlockSpec((1, d, tn), lambda l, n: (l, 0, n)),
                  pl.BlockSpec((1, 1, tn), lambda l, n: (l, 0, n))],
        out_specs=pl.BlockSpec((1, rows, tn), lambda l, n: (l, 0, n)),
        out_shape=jax.ShapeDtypeStruct((depth, rows, n6), F32),
        compiler_params=_params("arbitrary", "arbitrary"),
        name="modulation",
    )(cc, mod_w, mod_b.reshape(depth, 1, n6))


def _norm_mod(x, gain, shift, scale):
    ms = jnp.mean(x * x, axis=-1, keepdims=True)
    return x * lax.rsqrt(ms + EPS) * gain * (1.0 + scale) + shift


def _stream_specs(tm, d, lct):
    return [pl.BlockSpec((1, tm, d), lambda b, j: (b, jnp.minimum(j, lct - 1), 0)),
            pl.BlockSpec((1, tm, d), lambda b, j: (b, jnp.maximum(j - lct, 0), 0))]


def _stream_tile(xc_ref, xl_ref, lct):
    return jnp.where(pl.program_id(1) < lct, xc_ref[0], xl_ref[0])


def _in_proj_kernel(xc_ref, xl_ref, mod_ref, g_ref, w_ref, *o_refs, lct):
    mod = mod_ref[0, 0]
    h = _norm_mod(_stream_tile(xc_ref, xl_ref, lct), g_ref[...], mod[0:1], mod[1:2]).astype(BF16)
    off = 0
    for o_ref in o_refs:
        n = o_ref.shape[-1]
        for c0 in range(0, n, 1024):
            c1 = min(n, c0 + 1024)
            o_ref[0, :, c0:c1] = _mm(h, w_ref[:, off + c0:off + c1]).astype(o_ref.dtype)
        off += n


def _in_proj(xc, xl, modx, gain, w, splits):
    bsz, lc, d = xc.shape
    ln = lc + xl.shape[1]
    assert w.shape[1] == sum(n for n, _ in splits)
    tm = TOK_TILE
    lct = lc // tm
    return pl.pallas_call(
        functools.partial(_in_proj_kernel, lct=lct),
        grid=(bsz, ln // tm),
        in_specs=_stream_specs(tm, d, lct) + [
            pl.BlockSpec((1, 1, N_MOD, d), lambda b, j: (b, jnp.where(j < lct, 0, 1), 0, 0)),
            pl.BlockSpec((1, d), lambda b, j: (0, 0)),
            pl.BlockSpec(w.shape, lambda b, j: (0, 0))],
        out_specs=[pl.BlockSpec((1, tm, n), lambda b, j: (b, j, 0)) for n, _ in splits],
        out_shape=[jax.ShapeDtypeStruct((bsz, ln, n), dt) for n, dt in splits],
        compiler_params=_params("arbitrary", "arbitrary"),
        name="in_proj",
    )(xc, xl, modx, gain.reshape(1, d), w)


def _rope_tables(lc, seq):
    t = jnp.arange(seq)
    row = (t // GRID_W).astype(F32)
    col = (t % GRID_W).astype(F32)
    m = A_HEAD_DIM // 4
    inv = ROPE_THETA ** (-jnp.arange(m, dtype=F32) / m)
    ar, ac = row[:, None] * inv, col[:, None] * inv
    cos = jnp.concatenate([jnp.cos(ar), jnp.cos(ar), jnp.cos(ac), jnp.cos(ac)], axis=-1)
    sin = jnp.concatenate([-jnp.sin(ar), jnp.sin(ar), -jnp.sin(ac), jnp.sin(ac)], axis=-1)
    reps = LANES // A_HEAD_DIM
    cos = jnp.concatenate([jnp.ones((lc, A_HEAD_DIM), F32), cos], axis=0)
    sin = jnp.concatenate([jnp.zeros((lc, A_HEAD_DIM), F32), sin], axis=0)
    return jnp.tile(cos, (1, reps)), jnp.tile(sin, (1, reps))


def _head_block_ones(width, scale):
    idx = np.arange(LANES) // width
    return jnp.asarray((idx[:, None] == idx[None, :]).astype(np.float32) * scale)


def _attn_pre_kernel(z_ref, qg_ref, kg_ref, cos_ref, sin_ref, bones_ref, q_ref, k_ref, v_ref):
    tm = z_ref.shape[1]
    cs, sn = cos_ref[...], sin_ref[...]
    bones = bones_ref[...]
    lane = lax.broadcasted_iota(jnp.int32, (tm, LANES), 1)
    first = (lane % 32) < 16
    low = lane < A_HEAD_DIM

    def norm_rope(x, gain):
        ms = _bmm(x * x, bones)
        y = x * lax.rsqrt(ms + EPS) * gain
        partner = jnp.where(first, pltpu.roll(y, LANES - 16, 1), pltpu.roll(y, 16, 1))
        return y * cs + partner * sn

    heads_per_slab = LANES // A_HEAD_DIM
    group = A_HEADS // A_KV_HEADS
    for m in range(A_Q // LANES):
        sl = slice(m * LANES, (m + 1) * LANES)
        y = norm_rope(z_ref[0, :, sl].astype(F32), qg_ref[:, sl]) * (A_HEAD_DIM ** -0.5)
        swapped = pltpu.roll(y, A_HEAD_DIM, 1)
        for hh in range(heads_per_slab):
            h = m * heads_per_slab + hh
            kvh = h // group
            src = y if kvh == hh else swapped
            keep = low if kvh == 0 else jnp.logical_not(low)
            q_ref[0, h] = jnp.where(keep, src, 0.0).astype(BF16)
    k_ref[0] = norm_rope(z_ref[0, :, A_Q:A_Q + A_KV].astype(F32), kg_ref[...]).astype(BF16)
    v_ref[0] = z_ref[0, :, A_Q + A_KV:A_IN].astype(BF16)


def _attn_pre(za, q_gain, k_gain, lc):
    bsz, ln, _ = za.shape
    tm = TOK_TILE
    cos, sin = _rope_tables(lc, ln - lc)
    qg = jnp.tile(q_gain, A_HEADS).reshape(1, A_Q)
    kg = jnp.tile(k_gain, A_KV_HEADS).reshape(1, A_KV)
    bones = _head_block_ones(A_HEAD_DIM, 1.0 / A_HEAD_DIM)
    return pl.pallas_call(
        _attn_pre_kernel,
        grid=(bsz, ln // tm),
        in_specs=[pl.BlockSpec((1, tm, A_IN), lambda b, j: (b, j, 0)),
                  pl.BlockSpec((1, A_Q), lambda b, j: (0, 0)),
                  pl.BlockSpec((1, A_KV), lambda b, j: (0, 0)),
                  pl.BlockSpec((tm, LANES), lambda b, j: (j, 0)),
                  pl.BlockSpec((tm, LANES), lambda b, j: (j, 0)),
                  pl.BlockSpec((LANES, LANES), lambda b, j: (0, 0))],
        out_specs=[pl.BlockSpec((1, A_HEADS, tm, LANES), lambda b, j: (b, 0, j, 0)),
                   pl.BlockSpec((1, tm, A_KV), lambda b, j: (b, j, 0)),
                   pl.BlockSpec((1, tm, A_KV), lambda b, j: (b, j, 0))],
        out_shape=[jax.ShapeDtypeStruct((bsz, A_HEADS, ln, LANES), BF16),
                   jax.ShapeDtypeStruct((bsz, ln, A_KV), BF16),
                   jax.ShapeDtypeStruct((bsz, ln, A_KV), BF16)],
        compiler_params=_params("arbitrary", "arbitrary"),
        name="attn_pre",
    )(za, qg, kg, cos, sin, bones)


def _attn_kernel(q_ref, k_ref, v_ref, o_ref, *, lc):
    tq = q_ref.shape[2]
    j = pl.program_id(1)
    lane = lax.broadcasted_iota(jnp.int32, (tq, LANES), 1)
    low = lane < A_HEAD_DIM
    heads_per_slab = LANES // A_HEAD_DIM
    group = A_HEADS // A_KV_HEADS

    def run(klen):
        k = k_ref[0, :klen, :]
        v = v_ref[0, :klen, :]
        for m in range(A_Q // LANES):
            parts = []
            for hh in range(heads_per_slab):
                h = m * heads_per_slab + hh
                kvh = h // group
                s = _mm_nt(q_ref[0, h], k)
                p = jnp.exp(s - jnp.max(s, axis=-1, keepdims=True))
                l = jnp.sum(p, axis=-1, keepdims=True)
                o = _mm(p.astype(BF16), v) / l
                parts.append(o if kvh == hh else pltpu.roll(o, A_HEAD_DIM, 1))
            o_ref[0, :, m * LANES:(m + 1) * LANES] = jnp.where(low, parts[0], parts[1]).astype(o_ref.dtype)

    @pl.when(j * tq < lc)
    def _():
        run(lc)

    @pl.when(j * tq >= lc)
    def _():
        run(k_ref.shape[1])


def _attention(q, k, v, lc):
    bsz, _, ln, _ = q.shape
    tq = TOK_TILE
    return pl.pallas_call(
        functools.partial(_attn_kernel, lc=lc),
        grid=(bsz, ln // tq),
        in_specs=[pl.BlockSpec((1, A_HEADS, tq, LANES), lambda b, j: (b, 0, j, 0)),
                  pl.BlockSpec((1, ln, A_KV), lambda b, j: (b, 0, 0)),
                  pl.BlockSpec((1, ln, A_KV), lambda b, j: (b, 0, 0))],
        out_specs=pl.BlockSpec((1, tq, A_Q), lambda b, j: (b, j, 0)),
        out_shape=jax.ShapeDtypeStruct((bsz, ln, A_Q), BF16),
        compiler_params=_params("arbitrary", "arbitrary"),
        name="attention",
    )(q, k, v)


def _shifted_rows(z, prev_row, next_row):
    tm = z.shape[0]
    row = lax.broadcasted_iota(jnp.int32, (tm, 1), 0)
    zprev = jnp.where(row == 0, prev_row, pltpu.roll(z, 1, 0))
    znext = jnp.where(row == tm - 1, next_row, pltpu.roll(z, tm - 1, 0))
    return zprev, znext


def _halo_rows(zp_ref, zn_ref, j, lct, nj):
    prev_ok = jnp.logical_and(j != 0, j != lct)
    next_ok = jnp.logical_and(j != lct - 1, j != nj - 1)
    rows = zp_ref.shape[1]
    prev_row = jnp.where(prev_ok, zp_ref[0].astype(F32)[rows - 1:rows, :], 0.0)
    next_row = jnp.where(next_ok, zn_ref[0].astype(F32)[0:1, :], 0.0)
    return prev_row, next_row


def _halo_specs(tm, width, ln, rows=BF16_ROWS):
    nb = tm // rows
    last = ln // rows - 1
    return [pl.BlockSpec((1, tm, width), lambda b, j: (b, j, 0)),
            pl.BlockSpec((1, rows, width), lambda b, j: (b, jnp.maximum(j * nb - 1, 0), 0)),
            pl.BlockSpec((1, rows, width), lambda b, j: (b, jnp.minimum((j + 1) * nb, last), 0))]


def _per_head_sum(x, bones):
    return jnp.concatenate(
        [_bmm(x[:, m * LANES:(m + 1) * LANES], bones) for m in range(x.shape[1] // LANES)], axis=1)


def _rwkv_pre_kernel(z_ref, zp_ref, zn_ref, mu_ref, w0_ref, w2_ref, a0_ref, a2_ref, g2_ref,
                     kk_ref, ka_ref, rk_ref, bones_ref,
                     r_o, v_o, kkn_o, g_o, bonus_o, lw_o, a_o, kd_o, *, lct):
    j = pl.program_id(1)
    z = z_ref[0].astype(F32)
    prev_row, next_row = _halo_rows(zp_ref, zn_ref, j, lct, pl.num_programs(1))
    zprev, znext = _shifted_rows(z, prev_row, next_row)
    zm = z + mu_ref[...] * (0.5 * (zprev + znext) - z)
    w = B_WIDTH
    r, k, v = zm[:, 0:w], zm[:, w:2 * w], zm[:, 2 * w:3 * w]
    wa = zm[:, 3 * w:3 * w + B_DECAY_LORA + B_AAA_LORA]
    gl = zm[:, 3 * w + B_DECAY_LORA + B_AAA_LORA:]
    bones = bones_ref[...]
    kk = k * kk_ref[...]
    kkn = kk / jnp.maximum(jnp.sqrt(_per_head_sum(kk * kk, bones)), 1e-12)
    r_o[0] = r.astype(r_o.dtype)
    v_o[0] = v.astype(v_o.dtype)
    kkn_o[0] = kkn.astype(kkn_o.dtype)
    g_o[0] = _bmm(_sigmoid(gl), g2_ref[...]).astype(g_o.dtype)
    twa = jnp.tanh(wa)
    bonus = jnp.zeros_like(r)
    for d in range(2):
        wraw = w0_ref[d] + _bmm(twa, w2_ref[d])
        lw_o[d, 0] = -jnp.exp(-_softplus(-wraw) - 0.5)
        a = _sigmoid(a0_ref[d] + _bmm(wa, a2_ref[d]))
        kd = k * (1.0 + (a - 1.0) * ka_ref[...])
        a_o[d, 0] = a.astype(a_o.dtype)
        kd_o[d, 0] = kd.astype(kd_o.dtype)
        bonus = bonus + _per_head_sum(r * kd * rk_ref[...], bones) * v
    bonus_o[0] = bonus.astype(bonus_o.dtype)


def _rwkv_pre(zr, mu, w0, w2, a0, a2, g2, k_k, k_a, r_k, lc):
    bsz, ln, width = zr.shape
    tm = TOK_TILE
    w = B_WIDTH
    zeros = jnp.zeros((2, B_DECAY_LORA, w), F32)
    w2p = jnp.concatenate([w2, zeros], axis=1)
    a2p = jnp.concatenate([zeros, a2], axis=1)
    bones = _head_block_ones(B_HEAD_DIM, 1.0)
    full = lambda *shape: pl.BlockSpec(shape, lambda b, j: (0,) * len(shape))
    tok = pl.BlockSpec((1, tm, w), lambda b, j: (b, j, 0))
    tok2 = pl.BlockSpec((2, 1, tm, w), lambda b, j: (0, b, j, 0))
    sd = jax.ShapeDtypeStruct((bsz, ln, w), BF16)
    sd2 = jax.ShapeDtypeStruct((2, bsz, ln, w), BF16)
    lw_sd = jax.ShapeDtypeStruct((2, bsz, ln, w), F32)
    return pl.pallas_call(
        functools.partial(_rwkv_pre_kernel, lct=lc // tm),
        grid=(bsz, ln // tm),
        in_specs=_halo_specs(tm, width, ln) + [
            full(1, width), full(2, 1, w), full(2, 2 * B_DECAY_LORA, w), full(2, 1, w),
            full(2, 2 * B_AAA_LORA, w), full(B_GATE_LORA, w), full(1, w), full(1, w), full(1, w),
            full(LANES, LANES)],
        out_specs=[tok, tok, tok, tok, tok, tok2, tok2, tok2],
        out_shape=[sd, sd, sd, sd, sd, lw_sd, sd2, sd2],
        compiler_params=_params("arbitrary", "arbitrary"),
        name="rwkv_pre",
    )(zr, zr, zr, mu.reshape(1, width), w0.reshape(2, 1, w), w2p, a0.reshape(2, 1, w), a2p, g2,
      k_k.reshape(1, w), k_a.reshape(1, w), r_k.reshape(1, w), bones)


def _scan_chunk(i, d, nctx, ntot):
    rev = jnp.where(i < nctx, nctx - 1 - i, ntot + nctx - 1 - i)
    return jnp.where(d == 0, i, rev)


def _order_masks(n, pair):
    t = np.arange(n * pair)
    same = (t[:, None] // n) == (t[None, :] // n)
    tt, ss = t[:, None] % n, t[None, :] % n
    incl = np.stack([same & (ss <= tt), same & (ss >= tt)]).astype(np.float32)
    strict = np.stack([same & (ss < tt), same & (ss > tt)]).astype(np.float32)
    return jnp.asarray(incl), jnp.asarray(strict)


def _pair_stack(x, low):
    return jnp.concatenate([jnp.where(low, x, 0.0), jnp.where(low, 0.0, x)], axis=0)


def _bmm(a, b):
    return _mm(a.astype(BF16), b.astype(BF16))


def _rwkv_chunk_kernel(rf_ref, vf_ref, kf_ref, lwf_ref, af_ref, kdf_ref,
                       rb_ref, vb_ref, kb_ref, lwb_ref, ab_ref, kdb_ref,
                       tri_ref, incl_ref, strict_ref, of_ref, ob_ref, state_ref):
    c = R_CHUNK
    n2 = 2 * c

    @pl.when(pl.program_id(1) == 0)
    def _():
        state_ref[...] = jnp.zeros_like(state_ref)

    low = lax.broadcasted_iota(jnp.int32, (c, LANES), 1) < B_HEAD_DIM
    eye = (lax.broadcasted_iota(jnp.int32, (LANES, LANES), 0)
           == lax.broadcasted_iota(jnp.int32, (LANES, LANES), 1)).astype(F32)
    stack = lambda x: _pair_stack(x, low)
    refs = ((rf_ref, vf_ref, kf_ref, lwf_ref, af_ref, kdf_ref, of_ref),
            (rb_ref, vb_ref, kb_ref, lwb_ref, ab_ref, kdb_ref, ob_ref))
    dirs, slabs, outs = [], [], []
    kts, bts, kds, rts, vs, pcs = [], [], [], [], [], []
    for d, (r_ref, v_ref, kkn_ref, lw_ref, a_ref, kd_ref, o_ref) in enumerate(refs):
        lw = lw_ref[0, 0]
        cum = _mm(tri_ref[d], lw, HI)
        p_in = jnp.exp(cum)
        p_inv = jnp.exp(-cum)
        p_ex = jnp.exp(cum - lw)
        p_all = jnp.exp(jnp.sum(lw, axis=0, keepdims=True))
        kkn = kkn_ref[0].astype(F32)
        kt = kkn * p_ex
        bt = kkn * a_ref[0, 0].astype(F32) * p_inv
        kd = kd_ref[0, 0].astype(F32) * p_inv
        rt = r_ref[0].astype(F32) * p_in
        v = v_ref[0].astype(F32)
        for m in range(B_WIDTH // LANES):
            sl = slice(m * LANES, (m + 1) * LANES)
            dirs.append(d)
            slabs.append(m)
            outs.append(o_ref)
            kts.append(stack(kt[:, sl]).astype(BF16))
            bts.append(stack(bt[:, sl]).astype(BF16))
            kds.append(stack(kd[:, sl]).astype(BF16))
            rts.append(stack(rt[:, sl]))
            vs.append(stack(v[:, sl]).astype(BF16))
            pcs.append(p_all[:, sl])
    n = len(dirs)
    incl = [incl_ref[d] > 0.5 for d in range(2)]
    strict = [strict_ref[d] > 0.5 for d in range(2)]
    gram = [_mm_nt(jnp.concatenate([kts[x], rts[x].astype(BF16)], axis=0),
                   jnp.concatenate([bts[x], kds[x]], axis=0)) for x in range(n)]
    a_b = [jnp.where(strict[dirs[x]], gram[x][:n2, :n2], 0.0) for x in range(n)]
    a_k = [jnp.where(strict[dirs[x]], gram[x][:n2, n2:], 0.0) for x in range(n)]
    m_b = [jnp.where(incl[dirs[x]], gram[x][n2:, :n2], 0.0) for x in range(n)]
    m_k = [jnp.where(incl[dirs[x]], gram[x][n2:, n2:], 0.0) for x in range(n)]
    pw = [-a for a in a_b]
    tinv = [eye + p for p in pw]
    for _ in range(int(math.log2(c)) - 1):
        pw = [_bmm(p, p) for p in pw]
        tinv = [t + _bmm(t, p) for t, p in zip(tinv, pw)]
    mv = [_bmm(jnp.concatenate([a_k[x], m_k[x]], axis=0), vs[x]) for x in range(n)]
    wy = [_bmm(tinv[x], jnp.concatenate([kts[x], mv[x][:n2].astype(BF16)], axis=1)) for x in range(n)]
    mbwy = [_bmm(m_b[x], wy[x]) for x in range(n)]
    wyb = [_mm_tn(wy[x].astype(BF16), bts[x]) for x in range(n)]
    vtk = [_mm_tn(vs[x], kds[x]) for x in range(n)]
    s0 = [state_ref[dirs[x], slabs[x]] for x in range(n)]
    o_st = [_mm_nt((rts[x] - mbwy[x][:, :LANES]).astype(BF16), s0[x].astype(BF16))
            + (mv[x][n2:] - mbwy[x][:, LANES:]) for x in range(n)]
    for x in range(n):
        g = (eye - wyb[x][:LANES]) * pcs[x]
        h = (vtk[x] - wyb[x][LANES:]) * pcs[x]
        outs[x][0, :, slabs[x] * LANES:(slabs[x] + 1) * LANES] = (o_st[x][:c] + o_st[x][c:]).astype(outs[x].dtype)
        state_ref[dirs[x], slabs[x]] = _bmm(s0[x], g) + h


def _rwkv_chunks(r, v, kkn, lw, a, kd, lc):
    bsz, ln, w = r.shape
    c = R_CHUNK
    nctx, ntot = lc // c, ln // c
    tri, _ = _order_masks(c, 1)
    incl, strict = _order_masks(c, 2)
    specs = []
    for d in range(2):
        tok = pl.BlockSpec((1, c, w), lambda b, i, d=d: (b, _scan_chunk(i, d, nctx, ntot), 0))
        tok2 = pl.BlockSpec((1, 1, c, w), lambda b, i, d=d: (d, b, _scan_chunk(i, d, nctx, ntot), 0))
        specs.append((tok, tok2))
    full = lambda *shape: pl.BlockSpec(shape, lambda b, i: (0,) * len(shape))
    sd = jax.ShapeDtypeStruct((bsz, ln, w), BF16)
    return pl.pallas_call(
        _rwkv_chunk_kernel,
        grid=(bsz, ntot),
        in_specs=[specs[0][0]] * 3 + [specs[0][1]] * 3 + [specs[1][0]] * 3 + [specs[1][1]] * 3
        + [full(2, c, c), full(2, 2 * c, 2 * c), full(2, 2 * c, 2 * c)],
        out_specs=[specs[0][0], specs[1][0]],
        out_shape=[sd, sd],
        scratch_shapes=[pltpu.VMEM((2, w // LANES, LANES, LANES), F32)],
        compiler_params=_params("arbitrary", "arbitrary"),
        name="rwkv_chunks",
    )(r, v, kkn, lw, a, kd, r, v, kkn, lw, a, kd, tri, incl, strict)


def _even_out_kernel(xc_ref, xl_ref, mod_ref, attn_ref, wkvf_ref, wkvb_ref, bonus_ref, g_ref, gw_ref,
                     gb_ref, bones_ref, w_ref, oc_ref, ol_ref, *, lct):
    bones = bones_ref[...]
    y = bonus_ref[0].astype(F32)
    for wkv_ref in (wkvf_ref, wkvb_ref):
        o = wkv_ref[0].astype(F32)
        dlt = o - _per_head_sum(o, bones)
        var = _per_head_sum(dlt * dlt, bones)
        y = y + dlt * lax.rsqrt(var + B_GN_EPS) * gw_ref[...] + gb_ref[...]
    y = y * g_ref[0].astype(F32)
    mix = _mm(attn_ref[0].astype(BF16), w_ref[:A_Q, :]) + _mm(y.astype(BF16), w_ref[A_Q:, :])
    res = _stream_tile(xc_ref, xl_ref, lct) + mod_ref[0, 0][2:3] * mix
    is_ctx = pl.program_id(1) < lct

    @pl.when(is_ctx)
    def _():
        oc_ref[0] = res

    @pl.when(jnp.logical_not(is_ctx))
    def _():
        ol_ref[0] = res


def _even_out(xc, xl, modx, attn, wkv, bonus, g, gn_w, gn_b, w_out):
    bsz, lc, d = xc.shape
    seq = xl.shape[1]
    tm = TOK_TILE
    lct = lc // tm
    w = B_WIDTH
    bones = _head_block_ones(B_HEAD_DIM, 1.0 / B_HEAD_DIM)
    tokw = lambda n: pl.BlockSpec((1, tm, n), lambda b, j: (b, j, 0))
    full = lambda *shape: pl.BlockSpec(shape, lambda b, j: (0,) * len(shape))
    streams = _stream_specs(tm, d, lct)
    return pl.pallas_call(
        functools.partial(_even_out_kernel, lct=lct),
        grid=(bsz, (lc + seq) // tm),
        in_specs=streams + [
            pl.BlockSpec((1, 1, N_MOD, d), lambda b, j: (b, jnp.where(j < lct, 0, 1), 0, 0)),
            tokw(A_Q), tokw(w), tokw(w),
            tokw(w), tokw(w), full(1, w), full(1, w), full(LANES, LANES),
            full(A_Q + w, d)],
        out_specs=streams,
        out_shape=[jax.ShapeDtypeStruct((bsz, lc, d), F32), jax.ShapeDtypeStruct((bsz, seq, d), F32)],
        compiler_params=_params("arbitrary", "arbitrary"),
        name="even_out",
    )(xc, xl, modx, attn, wkv[0], wkv[1], bonus, g, gn_w.reshape(1, w), gn_b.reshape(1, w), bones, w_out)


def _ssd_pre_kernel(z_ref, zp_ref, zn_ref, dt_ref, cw_ref, cb_ref, dtb_ref, x_o, dt_o, *, lct):
    j = pl.program_id(1)
    z = z_ref[0].astype(F32)
    prev_row, next_row = _halo_rows(zp_ref, zn_ref, j, lct, pl.num_programs(1))
    zprev, znext = _shifted_rows(z, prev_row, next_row)
    y = zprev * cw_ref[0:1, :] + z * cw_ref[1:2, :] + znext * cw_ref[2:3, :] + cb_ref[...]
    x_o[0] = _silu(y).astype(x_o.dtype)
    dt_o[0] = _softplus(dt_ref[0] + dtb_ref[...])


def _ssd_pre(xbc, dtraw, conv_w, conv_b, dtb, lc):
    bsz, ln, width = xbc.shape
    tm = TOK_TILE
    nd = dtraw.shape[-1]
    full = lambda *shape: pl.BlockSpec(shape, lambda b, j: (0,) * len(shape))
    return pl.pallas_call(
        functools.partial(_ssd_pre_kernel, lct=lc // tm),
        grid=(bsz, ln // tm),
        in_specs=_halo_specs(tm, width, ln) + [
            pl.BlockSpec((1, tm, nd), lambda b, j: (b, j, 0)),
            full(conv_w.shape[0], width), full(1, width), full(1, nd)],
        out_specs=[pl.BlockSpec((1, tm, width), lambda b, j: (b, j, 0)),
                   pl.BlockSpec((1, tm, nd), lambda b, j: (b, j, 0))],
        out_shape=[jax.ShapeDtypeStruct((bsz, ln, width), BF16),
                   jax.ShapeDtypeStruct((bsz, ln, nd), F32)],
        compiler_params=_params("arbitrary", "arbitrary"),
        name="ssd_pre",
    )(xbc, xbc, xbc, dtraw, conv_w, conv_b.reshape(1, width), dtb)


def _ssd_chunk_kernel(x_ref, b_ref, c_ref, dt_ref, alog_ref, incl_ref, expand_ref, o_ref, state_ref,
                      *, n_heads):
    q = C_CHUNK
    i = pl.program_id(2)

    @pl.when(i == 0)
    def _():
        state_ref[...] = jnp.zeros_like(state_ref)

    lane = lax.broadcasted_iota(jnp.int32, (1, LANES), 1)
    a_neg = jnp.where(lane < n_heads, -jnp.exp(alog_ref[0]), 0.0)
    dt = dt_ref[0]
    dta = dt * a_neg
    incl_f = incl_ref[0]
    cum = _mm(incl_f, dta, HI)
    cum_t = cum.T
    total = jnp.sum(dta, axis=0, keepdims=True)
    expand = expand_ref[...]
    dt_x = _bmm(dt, expand)
    ec_x = _bmm(jnp.exp(cum), expand)
    de_x = _bmm(jnp.exp(total - cum), expand)
    cd_x = _bmm(jnp.broadcast_to(jnp.exp(total), (BF16_ROWS, LANES)), expand)[0:1]
    incl = incl_f > 0.5
    low = lax.broadcasted_iota(jnp.int32, (q, LANES), 1) < C_HEAD_DIM
    heads_per_group = n_heads // C_GROUPS
    pairs_per_group = heads_per_group // 2
    gw = heads_per_group * C_HEAD_DIM
    for g in range(C_GROUPS):
        gsl = slice(g * gw, (g + 1) * gw)
        bg = b_ref[0, :, g * C_STATE:(g + 1) * C_STATE].astype(BF16)
        cg = c_ref[0, :, g * C_STATE:(g + 1) * C_STATE].astype(BF16)
        cb = _mm_nt(cg, bg)
        xd = x_ref[0, :, gsl].astype(F32) * dt_x[:, gsl]
        s_prev = state_ref[g]
        y_off = _mm(cg, s_prev.astype(BF16)) * ec_x[:, gsl]
        state_ref[g] = s_prev * cd_x[:, gsl] + _mm_tn(bg, (xd * de_x[:, gsl]).astype(BF16))
        for pp in range(pairs_per_group):
            pr = g * pairs_per_group + pp
            sl = slice(pr * LANES, (pr + 1) * LANES)
            ms = []
            for hh in range(2):
                h = 2 * pr + hh
                seg = cum[:, h:h + 1] - cum_t[h:h + 1, :]
                ms.append((cb * jnp.exp(jnp.where(incl, seg, -1e30))).astype(BF16))
            xd_pair = _pair_stack(xd[:, pp * LANES:(pp + 1) * LANES], low).astype(BF16)
            y_diag = _mm(jnp.concatenate(ms, axis=1), xd_pair)
            o_ref[0, 0, :, sl] = (y_diag + y_off[:, pp * LANES:(pp + 1) * LANES]).astype(o_ref.dtype)


def _ssd_chunks(xact, dtact, a_log, lc):
    bsz, ln, _ = xact.shape
    q = C_CHUNK
    n_heads = a_log.shape[1]
    inner = n_heads * C_HEAD_DIM
    gn = C_GROUPS * C_STATE
    nctx, ntot = lc // q, ln // q
    incl, _ = _order_masks(q, 1)
    expand = np.zeros((LANES, inner), np.float32)
    expand[np.arange(inner) // C_HEAD_DIM, np.arange(inner)] = 1.0
    alog = jnp.zeros((2, 1, LANES), F32).at[:, 0, :n_heads].set(a_log)
    cmap = lambda off: (lambda b, d, i: (b, _scan_chunk(i, d, nctx, ntot), off))
    return pl.pallas_call(
        functools.partial(_ssd_chunk_kernel, n_heads=n_heads),
        grid=(bsz, 2, ntot),
        in_specs=[pl.BlockSpec((1, q, inner), cmap(0)),
                  pl.BlockSpec((1, q, gn), cmap(inner // gn)),
                  pl.BlockSpec((1, q, gn), cmap(inner // gn + 1)),
                  pl.BlockSpec((1, q, LANES), lambda b, d, i: (b, _scan_chunk(i, d, nctx, ntot), d)),
                  pl.BlockSpec((1, 1, LANES), lambda b, d, i: (d, 0, 0)),
                  pl.BlockSpec((1, q, q), lambda b, d, i: (d, 0, 0)),
                  pl.BlockSpec((LANES, inner), lambda b, d, i: (0, 0))],
        out_specs=pl.BlockSpec((1, 1, q, inner), lambda b, d, i: (d, b, _scan_chunk(i, d, nctx, ntot), 0)),
        out_shape=jax.ShapeDtypeStruct((2, bsz, ln, inner), BF16),
        scratch_shapes=[pltpu.VMEM((C_GROUPS, C_STATE, inner // C_GROUPS), F32)],
        compiler_params=_params("arbitrary", "arbitrary", "arbitrary"),
        name="ssd_chunks",
    )(xact, xact, xact, dtact, alog, incl, jnp.asarray(expand, dtype=BF16))


def _odd_out_kernel(x_ref, mod_ref, ys_ref, xs_ref, zg_ref, dsk_ref, nw_ref, w_ref, o_ref):
    y = (ys_ref[0, 0].astype(F32) + ys_ref[1, 0].astype(F32)) + dsk_ref[...] * xs_ref[0].astype(F32)
    yg = y * _silu(zg_ref[0].astype(F32))
    inner = yg.shape[1]
    gw = inner // C_GROUPS
    parts = []
    for g in range(C_GROUPS):
        t = yg[:, g * gw:(g + 1) * gw]
        ms = jnp.mean(t * t, axis=-1, keepdims=True)
        parts.append((t * lax.rsqrt(ms + EPS) * nw_ref[:, g * gw:(g + 1) * gw]).astype(BF16))
    yn = jnp.concatenate(parts, axis=1)
    o_ref[0] = x_ref[0] + mod_ref[0, 0][2:3] * _mm(yn, w_ref[...])


def _odd_out(x, modx, ys, xact, zg, d_skip, norm_w, w_out, lc):
    bsz, seq, d = x.shape
    ln = lc + seq
    tm = TOK_TILE
    lct = lc // tm
    inner = zg.shape[-1]
    dsk = jnp.repeat(d_skip, C_HEAD_DIM).reshape(1, inner)
    tokw = lambda n: pl.BlockSpec((1, tm, n), lambda b, j: (b, j + lct, 0))
    full = lambda *shape: pl.BlockSpec(shape, lambda b, j: (0,) * len(shape))
    return pl.pallas_call(
        _odd_out_kernel,
        grid=(bsz, (ln - lc) // tm),
        in_specs=[pl.BlockSpec((1, tm, d), lambda b, j: (b, j, 0)),
                  pl.BlockSpec((1, 1, N_MOD, d), lambda b, j: (b, 1, 0, 0)),
                  pl.BlockSpec((2, 1, tm, inner), lambda b, j: (0, b, j + lct, 0)),
                  tokw(inner), tokw(inner), full(1, inner), full(1, inner), full(inner, d)],
        out_specs=pl.BlockSpec((1, tm, d), lambda b, j: (b, j, 0)),
        out_shape=jax.ShapeDtypeStruct((bsz, ln - lc, d), F32),
        compiler_params=_params("arbitrary", "arbitrary"),
        name="odd_out",
    )(x, modx, ys, xact, zg, dsk, norm_w.reshape(1, inner), w_out)


class _Group:
    def __init__(self, parts):
        self.parts = tuple(parts)


def _ew(f, *args):
    groups = [a for a in args if isinstance(a, _Group)]
    if not groups:
        return f(*args)
    return _Group(f(*[a.parts[m] if isinstance(a, _Group) else a for a in args])
                  for m in range(len(groups[0].parts)))


def _exchange(xs, i, l, descending=True):
    hi, lo = _ew(jnp.maximum, xs[i], xs[l]), _ew(jnp.minimum, xs[i], xs[l])
    xs[i], xs[l] = (hi, lo) if descending else (lo, hi)


def _bitonic_merge(xs):
    xs = list(xs)
    j = len(xs) // 2
    while j >= 1:
        for i in range(len(xs)):
            if i ^ j > i:
                _exchange(xs, i, i ^ j)
        j //= 2
    return xs


def _bitonic_sort(xs):
    xs = list(xs)
    k = 2
    while k <= len(xs):
        j = k // 2
        while j >= 1:
            for i in range(len(xs)):
                if i ^ j > i:
                    _exchange(xs, i, i ^ j, descending=(i & k) == 0)
            j //= 2
        k *= 2
    return xs


def _merge_sublanes(xs):
    n = len(xs)
    for shift in (4, 2, 1):
        other = [_ew(lambda v: pltpu.roll(v, shift, 0), x) for x in xs]
        xs = _bitonic_merge([_ew(jnp.maximum, xs[k], other[n - 1 - k]) for k in range(n)])
    return xs


def _row_tile(s, k):
    return _ew(lambda v: v[8 * k:8 * k + 8], s)


def _top_values(s):
    return _merge_sublanes(_bitonic_sort([_row_tile(s, k) for k in range(P_KEYS // 8)]))


def _top_pair_sums(t1, t2, shape):
    sub = lax.broadcasted_iota(jnp.int32, shape, 0)
    lo, hi = t2[0], t2[8]
    for b in range(1, 8):
        lo = _ew(lambda x, y: jnp.where(sub == b, x, y), t2[b], lo)
        hi = _ew(lambda x, y: jnp.where(sub == b, x, y), t2[8 + b], hi)
    xs = [_ew(jnp.add, t1[a], lo) for a in range(P_TOPK)]
    carry = _ew(jnp.add, t1[0], hi)
    for k in range(P_TOPK):
        xs[k], carry = _ew(jnp.maximum, xs[k], carry), _ew(jnp.minimum, xs[k], carry)
    return _merge_sublanes(xs)


def _prefix_count(test, vals):
    sel = functools.partial(_ew, jnp.where)
    b3 = test(vals[7])
    b2 = test(sel(b3, vals[11], vals[3]))
    b1 = test(sel(b3, sel(b2, vals[13], vals[9]), sel(b2, vals[5], vals[1])))
    b0 = test(sel(b3, sel(b2, sel(b1, vals[14], vals[12]), sel(b1, vals[10], vals[8])),
                  sel(b2, sel(b1, vals[6], vals[4]), sel(b1, vals[2], vals[0]))))
    n = _ew(lambda p3, p2, p1, p0: (jnp.where(p3, 8.0, 0.0) + jnp.where(p2, 4.0, 0.0))
            + (jnp.where(p1, 2.0, 0.0) + jnp.where(p0, 1.0, 0.0)), b3, b2, b1, b0)
    return _ew(lambda full, m: jnp.where(full, 16.0, m), test(vals[15]), n)


def _peer_query_kernel(x_ref, mod_ref, g_ref, wq_ref, keys_ref, ht_o, n_o, c_o, rank_o, e2_o):
    assert P_TOPK == 16 and P_KEYS == 128
    mod = mod_ref[0]
    h2 = _norm_mod(x_ref[...], g_ref[...], mod[3:4], mod[4:5])
    ht = h2.T.astype(BF16)
    ht_o[...] = ht
    qt = _mm(wq_ref[...], ht).astype(BF16)
    kd = P_KEY_DIM
    tile_shape = (8, ht.shape[1])
    for h0 in range(0, P_HEADS, PEER_HEAD_GROUP):
        heads = range(h0, h0 + PEER_HEAD_GROUP)
        s1 = _Group(_mm(keys_ref[h, 0], qt[(2 * h) * kd:(2 * h + 1) * kd]) for h in heads)
        s2 = _Group(_mm(keys_ref[h, 1], qt[(2 * h + 1) * kd:(2 * h + 2) * kd]) for h in heads)
        t1 = _top_values(s1)
        t2 = _top_values(s2)
        best = _top_pair_sums(t1, t2, tile_shape)
        z = _ew(jnp.ones_like, best[0])
        for k in range(1, P_TOPK):
            z = _ew(lambda acc, v, top: acc + jnp.exp(v - top), z, best[k], best[0])
        inv_z = _ew(lambda v: 0.5 / v, z)
        tau = best[P_TOPK - 1]
        ranks, e2s = [], []
        for k in range(P_KEYS // 8):
            rows = slice(8 * k, 8 * k + 8)
            s1k, s2k = _row_tile(s1, k), _row_tile(s2, k)
            nk = _prefix_count(lambda v: _ew(lambda s, x, t: (s + x) >= t, s1k, v, tau), t2)
            ck = _ew(lambda s, top, iz: jnp.exp(s - top) * iz, s1k, t1[0], inv_z)
            ranks.append(_prefix_count(lambda v: _ew(lambda s, x: x > s, s2k, v), t2))
            e2s.append(_ew(lambda s, top: jnp.exp(s - top), s2k, t2[0]))
            for m, h in enumerate(heads):
                n_o[h, rows] = nk.parts[m]
                c_o[h, rows] = ck.parts[m]
        for m, h in enumerate(heads):
            rank_o[h] = jnp.concatenate([r.parts[m] for r in ranks], axis=0).astype(BF16)
            e2_o[h] = jnp.concatenate([e.parts[m] for e in e2s], axis=0).astype(BF16)


def _peer_query(xt, mod, gain, wq_t, keys, blocks_per_mod, tt):
    t_all, d = xt.shape
    nq = wq_t.shape[0]
    nblk = t_all // tt
    head = pl.BlockSpec((P_HEADS, P_KEYS, tt), lambda i: (0, 0, i))
    hsd = lambda dt: jax.ShapeDtypeStruct((P_HEADS, P_KEYS, t_all), dt)
    wspec = pl.BlockSpec((nq, d), lambda i: (0, 0))
    kspec = pl.BlockSpec((P_HEADS, 2, P_KEYS, P_KEY_DIM), lambda i: (0, 0, 0, 0))
    return pl.pallas_call(
        _peer_query_kernel,
        grid=(nblk,),
        in_specs=[pl.BlockSpec((tt, d), lambda i: (i, 0)),
                  pl.BlockSpec((1, N_MOD, d), lambda i: (i // blocks_per_mod, 0, 0)),
                  pl.BlockSpec((1, d), lambda i: (0, 0)),
                  wspec, kspec],
        out_specs=[pl.BlockSpec((d, tt), lambda i: (0, i)), head, head, head, head],
        out_shape=[jax.ShapeDtypeStruct((d, t_all), BF16), hsd(F32), hsd(F32), hsd(BF16), hsd(BF16)],
        compiler_params=_params("arbitrary"),
        name="peer_query",
    )(xt, mod, gain.reshape(1, d), wq_t.astype(BF16), keys.astype(BF16))


def _twice_gelu(x):
    return x * (1.0 + lax.erf(x * (2.0 ** -0.5)))


def _peer_main_kernel(x_ref, mod_ref, ht_ref, u_ref, vt_ref, n_ref, c_ref, rank_ref, e2_ref,
                      o_ref, acc_ref, act_ref, coef_ref):
    g = pl.program_id(1)
    last = pl.num_programs(1) - 1
    tt = ht_ref.shape[1]
    rows_per_step = u_ref.shape[0] // P_KEYS
    tiles = P_KEYS // BF16_ROWS
    tw = PEER_COEF_TOK

    def consume():
        acc_ref[...] += _mm(vt_ref[...], coef_ref[...] * act_ref[...])

    def prepare():
        zero = jnp.zeros((BF16_ROWS, tw), BF16)
        row0 = pl.multiple_of(g * rows_per_step, rows_per_step)
        n_rows = [n_ref[h, pl.ds(row0, rows_per_step), :] for h in range(P_HEADS)]
        c_rows = [c_ref[h, pl.ds(row0, rows_per_step), :] for h in range(P_HEADS)]
        for ii in range(rows_per_step):
            for t0 in range(0, tt, tw):
                tok = slice(t0, t0 + tw)
                coef = [None] * tiles
                for h in range(P_HEADS):
                    nb = jnp.broadcast_to(n_rows[h][ii:ii + 1, tok], (BF16_ROWS, tw)).astype(BF16)
                    cb = jnp.broadcast_to(c_rows[h][ii:ii + 1, tok], (BF16_ROWS, tw)).astype(BF16)
                    for k in range(tiles):
                        rows = slice(k * BF16_ROWS, (k + 1) * BF16_ROWS)
                        term = jnp.where(rank_ref[h, rows, tok] < nb, e2_ref[h, rows, tok] * cb, zero)
                        coef[k] = term if coef[k] is None else coef[k] + term
                for k in range(tiles):
                    rows = slice(ii * P_KEYS + k * BF16_ROWS, ii * P_KEYS + (k + 1) * BF16_ROWS)
                    coef_ref[rows, tok] = coef[k]
        act_ref[...] = _twice_gelu(_mm(u_ref[...], ht_ref[...])).astype(BF16)

    @pl.when(g == 0)
    def _():
        acc_ref[...] = jnp.zeros_like(acc_ref)
        prepare()

    @pl.when(jnp.logical_and(g > 0, g < last))
    def _():
        consume()
        prepare()

    @pl.when(g == last)
    def _():
        consume()
        o_ref[...] = x_ref[...] + mod_ref[0][5:6] * acc_ref[...].T


def _peer_main(xt, mod, ht, u_all, vt_all, layer, n, c, rank, e2, blocks_per_mod, tt):
    t_all, d = xt.shape
    n_exp = u_all.shape[1]
    eb = PEER_EXP
    nsteps = n_exp // eb
    head = pl.BlockSpec((P_HEADS, P_KEYS, tt), lambda i, s: (0, 0, i))
    return pl.pallas_call(
        _peer_main_kernel,
        grid=(t_all // tt, nsteps + 1),
        in_specs=[pl.BlockSpec((tt, d), lambda i, s: (i, 0)),
                  pl.BlockSpec((1, N_MOD, d), lambda i, s: (i // blocks_per_mod, 0, 0)),
                  pl.BlockSpec((d, tt), lambda i, s: (0, i)),
                  pl.BlockSpec((None, eb, d), lambda i, s: (layer, jnp.minimum(s, nsteps - 1), 0)),
                  pl.BlockSpec((None, d, eb), lambda i, s: (layer, 0, jnp.maximum(s - 1, 0))),
                  head, head, head, head],
        out_specs=pl.BlockSpec((tt, d), lambda i, s: (i, 0)),
        out_shape=jax.ShapeDtypeStruct((t_all, d), F32),
        scratch_shapes=[pltpu.VMEM((d, tt), F32), pltpu.VMEM((eb, tt), BF16), pltpu.VMEM((eb, tt), BF16)],
        compiler_params=_params("arbitrary", "arbitrary"),
        name="peer_main",
    )(xt, mod, ht, u_all, vt_all, n, c, rank, e2)


def _peer_layer(xt, mod, gain, w_q, keys, u_all, vt_all, layer, blocks_per_mod, tt):
    wq_t = w_q.T
    tq = PEER_QUERY_TOK
    ht, n, c, rank, e2 = _peer_query(xt, mod, gain, wq_t, keys, blocks_per_mod * (tt // tq), tq)
    return _peer_main(xt, mod, ht, u_all, vt_all, layer, n, c, rank, e2, blocks_per_mod, tt)


def kernel(x, c, ctx, c_ctx, mod_w, mod_b, norm1_g, norm2_g, ev_w_in, ev_w_out, attn_q_gain, attn_k_gain,
           rw_mu, rw_w0, rw_w2, rw_a0, rw_a2, rw_g2, rw_k_k, rw_k_a, rw_r_k, rw_gn_w, rw_gn_b, ssd_w_in,
           ssd_conv_w, ssd_conv_b, ssd_dt_bias, ssd_a_log, ssd_d, ssd_norm_w, ssd_w_out, peer_w_q,
           peer_keys, peer_u, peer_v):
    bsz, seq, dm = x.shape
    lc = ctx.shape[1]
    depth = mod_w.shape[0]
    assert lc == TOK_TILE and seq % PEER_TOK == 0 and (bsz * lc) % PEER_TOK == 0

    rows = -(-(bsz + 1) // 8) * 8
    cc = jnp.concatenate([c, c_ctx[None], jnp.zeros((rows - bsz - 1, dm), F32)], axis=0)
    mods = _modulation(cc, mod_w, mod_b)
    xc, xl = ctx, x
    u_all = peer_u.astype(BF16)
    vt_all = jnp.swapaxes(peer_v, 1, 2).astype(BF16)

    for i in range(depth):
        last = i == depth - 1
        j = i // 2
        ml = mods[i, :bsz].reshape(bsz, N_MOD, dm)
        mc = mods[i, bsz].reshape(1, N_MOD, dm)
        modx = jnp.stack([jnp.broadcast_to(mc, (bsz, N_MOD, dm)), ml], axis=1)
        if i % 2 == 0:
            za, zr = _in_proj(xc, xl, modx, norm1_g[i], ev_w_in[j].astype(BF16),
                              ((A_IN, BF16), (B_IN, BF16)))
            q, k, v = _attn_pre(za, attn_q_gain[j], attn_k_gain[j], lc)
            attn = _attention(q, k, v, lc)
            r, vv, kkn, g, bonus, lw, a, kd = _rwkv_pre(
                zr, rw_mu[j], rw_w0[j], rw_w2[j], rw_a0[j], rw_a2[j], rw_g2[j], rw_k_k[j], rw_k_a[j],
                rw_r_k[j], lc)
            wkv = _rwkv_chunks(r, vv, kkn, lw, a, kd, lc)
            xc, xl = _even_out(xc, xl, modx, attn, wkv, bonus, g, rw_gn_w[j], rw_gn_b[j],
                               ev_w_out[j].astype(BF16))
        else:
            n_heads = ssd_a_log.shape[-1]
            inner = n_heads * C_HEAD_DIM
            conv_dim = inner + 2 * C_GROUPS * C_STATE
            w_in = ssd_w_in[j]
            w_dt = jnp.zeros((dm, 2 * LANES), F32)
            w_dt = w_dt.at[:, :n_heads].set(w_in[:, inner + conv_dim:inner + conv_dim + n_heads])
            w_dt = w_dt.at[:, LANES:LANES + n_heads].set(w_in[:, inner + conv_dim + n_heads:])
            dtb = jnp.zeros((1, 2 * LANES), F32)
            dtb = dtb.at[0, :n_heads].set(ssd_dt_bias[j, 0]).at[0, LANES:LANES + n_heads].set(ssd_dt_bias[j, 1])
            w_all = jnp.concatenate([w_in[:, :inner + conv_dim], w_dt], axis=1).astype(BF16)
            zg, xbc, dtraw = _in_proj(xc, xl, modx, norm1_g[i], w_all,
                                      ((inner, BF16), (conv_dim, BF16), (2 * LANES, F32)))
            xact, dtact = _ssd_pre(xbc, dtraw, ssd_conv_w[j], ssd_conv_b[j], dtb, lc)
            ys = _ssd_chunks(xact, dtact, ssd_a_log[j], lc)
            assert last, "the SSD layer keeps only the latent stream"
            xl = _odd_out(xl, modx, ys, xact, zg, ssd_d[j], ssd_norm_w[j], ssd_w_out[j].astype(BF16), lc)

        tt = PEER_TOK
        xl = _peer_layer(xl.reshape(bsz * seq, dm), ml, norm2_g[i], peer_w_q[i], peer_keys[i],
                         u_all, vt_all, i, seq // tt, tt).reshape(bsz, seq, dm)
        if not last:
            xc = _peer_layer(xc.reshape(bsz * lc, dm), mc, norm2_g[i], peer_w_q[i], peer_keys[i],
                             u_all, vt_all, i, (bsz * lc) // tt, tt).reshape(bsz, lc, dm)
    return xl
```

```python
import functools
import math

import numpy as np
import jax
import jax.numpy as jnp
from jax import lax
from jax.experimental import pallas as pl
from jax.experimental.pallas import tpu as pltpu

F32 = jnp.float32
BF16 = jnp.bfloat16
HI = lax.Precision.HIGHEST

GRID_W = 64
EPS = 1e-6
N_MOD = 6
A_HEADS, A_KV_HEADS, A_HEAD_DIM = 8, 2, 64
A_Q = A_HEADS * A_HEAD_DIM
A_KV = A_KV_HEADS * A_HEAD_DIM
A_IN = A_Q + 2 * A_KV
ROPE_THETA = 10000.0
B_HEADS, B_HEAD_DIM = 8, 64
B_WIDTH = B_HEADS * B_HEAD_DIM
B_DECAY_LORA, B_AAA_LORA, B_GATE_LORA = 64, 64, 128
B_IN = 3 * B_WIDTH + B_DECAY_LORA + B_AAA_LORA + B_GATE_LORA
B_GN_EPS = 64e-5
C_HEAD_DIM, C_GROUPS, C_STATE, C_CHUNK = 64, 4, 128, 128
P_HEADS, P_KEYS, P_KEY_DIM, P_TOPK = 8, 128, 128, 16

LANES = 128
BF16_ROWS = 16
TOK_TILE = 256
R_CHUNK = 64
PEER_TOK = 512
PEER_QUERY_TOK = 256
PEER_HEAD_GROUP = 2
PEER_EXP = 1024
PEER_COEF_TOK = 256
VMEM_LIMIT = 56 * 1024 * 1024


def _mm(a, b, prec=None):
    return lax.dot_general(a, b, (((1,), (0,)), ((), ())), precision=prec,
                           preferred_element_type=F32)


def _mm_nt(a, b, prec=None):
    return lax.dot_general(a, b, (((1,), (1,)), ((), ())), precision=prec,
                           preferred_element_type=F32)


def _mm_tn(a, b, prec=None):
    return lax.dot_general(a, b, (((0,), (0,)), ((), ())), precision=prec,
                           preferred_element_type=F32)


def _params(*sem):
    return pltpu.CompilerParams(dimension_semantics=sem, vmem_limit_bytes=VMEM_LIMIT)


def _softplus(x):
    return jnp.maximum(x, 0.0) + jnp.log1p(jnp.exp(-jnp.abs(x)))


def _sigmoid(x):
    return 1.0 / (1.0 + jnp.exp(-x))


def _silu(x):
    return x * _sigmoid(x)


def _mod_kernel(c_ref, w_ref, b_ref, o_ref):
    o_ref[0] = _mm(_silu(c_ref[...]), w_ref[0], HI) + b_ref[0]


def _modulation(cc, mod_w, mod_b):
    depth, d, n6 = mod_w.shape
    rows = cc.shape[0]
    tn = 1024
    return pl.pallas_call(
        _mod_kernel,
        grid=(depth, n6 // tn),
        in_specs=[pl.BlockSpec((rows, d), lambda l, n: (0, 0)),
                  pl.BlockSpec((1, d, tn), lambda l, n: (l, 0, n)),
                  pl.BlockSpec((1, 1, tn), lambda l, n: (l, 0, n))],
        out_specs=pl.BlockSpec((1, rows, tn), lambda l, n: (l, 0, n)),
        out_shape=jax.ShapeDtypeStruct((depth, rows, n6), F32),
        compiler_params=_params("arbitrary", "arbitrary"),
        name="modulation",
    )(cc, mod_w, mod_b.reshape(depth, 1, n6))


def _norm_mod(x, gain, shift, scale):
    ms = jnp.mean(x * x, axis=-1, keepdims=True)
    return x * lax.rsqrt(ms + EPS) * gain * (1.0 + scale) + shift


def _stream_specs(tm, d, lct):
    return [pl.BlockSpec((1, tm, d), lambda b, j: (b, jnp.minimum(j, lct - 1), 0)),
            pl.BlockSpec((1, tm, d), lambda b, j: (b, jnp.maximum(j - lct, 0), 0))]


def _stream_tile(xc_ref, xl_ref, lct):
    return jnp.where(pl.program_id(1) < lct, xc_ref[0], xl_ref[0])


def _in_proj_kernel(xc_ref, xl_ref, mod_ref, g_ref, w_ref, *o_refs, lct):
    mod = mod_ref[0, 0]
    h = _norm_mod(_stream_tile(xc_ref, xl_ref, lct), g_ref[...], mod[0:1], mod[1:2]).astype(BF16)
    off = 0
    for o_ref in o_refs:
        n = o_ref.shape[-1]
        for c0 in range(0, n, 1024):
            c1 = min(n, c0 + 1024)
            o_ref[0, :, c0:c1] = _mm(h, w_ref[:, off + c0:off + c1]).astype(o_ref.dtype)
        off += n


def _in_proj(xc, xl, modx, gain, w, splits):
    bsz, lc, d = xc.shape
    ln = lc + xl.shape[1]
    assert w.shape[1] == sum(n for n, _ in splits)
    tm = TOK_TILE
    lct = lc // tm
    return pl.pallas_call(
        functools.partial(_in_proj_kernel, lct=lct),
        grid=(bsz, ln // tm),
        in_specs=_stream_specs(tm, d, lct) + [
            pl.BlockSpec((1, 1, N_MOD, d), lambda b, j: (b, jnp.where(j < lct, 0, 1), 0, 0)),
            pl.BlockSpec((1, d), lambda b, j: (0, 0)),
            pl.BlockSpec(w.shape, lambda b, j: (0, 0))],
        out_specs=[pl.BlockSpec((1, tm, n), lambda b, j: (b, j, 0)) for n, _ in splits],
        out_shape=[jax.ShapeDtypeStruct((bsz, ln, n), dt) for n, dt in splits],
        compiler_params=_params("arbitrary", "arbitrary"),
        name="in_proj",
    )(xc, xl, modx, gain.reshape(1, d), w)


def _rope_tables(lc, seq):
    t = jnp.arange(seq)
    row = (t // GRID_W).astype(F32)
    col = (t % GRID_W).astype(F32)
    m = A_HEAD_DIM // 4
    inv = ROPE_THETA ** (-jnp.arange(m, dtype=F32) / m)
    ar, ac = row[:, None] * inv, col[:, None] * inv
    cos = jnp.concatenate([jnp.cos(ar), jnp.cos(ar), jnp.cos(ac), jnp.cos(ac)], axis=-1)
    sin = jnp.concatenate([-jnp.sin(ar), jnp.sin(ar), -jnp.sin(ac), jnp.sin(ac)], axis=-1)
    reps = LANES // A_HEAD_DIM
    cos = jnp.concatenate([jnp.ones((lc, A_HEAD_DIM), F32), cos], axis=0)
    sin = jnp.concatenate([jnp.zeros((lc, A_HEAD_DIM), F32), sin], axis=0)
    return jnp.tile(cos, (1, reps)), jnp.tile(sin, (1, reps))


def _head_block_ones(width, scale):
    idx = np.arange(LANES) // width
    return jnp.asarray((idx[:, None] == idx[None, :]).astype(np.float32) * scale)


def _attn_pre_kernel(z_ref, qg_ref, kg_ref, cos_ref, sin_ref, bones_ref, q_ref, k_ref, v_ref):
    tm = z_ref.shape[1]
    cs, sn = cos_ref[...], sin_ref[...]
    bones = bones_ref[...]
    lane = lax.broadcasted_iota(jnp.int32, (tm, LANES), 1)
    first = (lane % 32) < 16
    low = lane < A_HEAD_DIM

    def norm_rope(x, gain):
        ms = _bmm(x * x, bones)
        y = x * lax.rsqrt(ms + EPS) * gain
        partner = jnp.where(first, pltpu.roll(y, LANES - 16, 1), pltpu.roll(y, 16, 1))
        return y * cs + partner * sn

    heads_per_slab = LANES // A_HEAD_DIM
    group = A_HEADS // A_KV_HEADS
    for m in range(A_Q // LANES):
        sl = slice(m * LANES, (m + 1) * LANES)
        y = norm_rope(z_ref[0, :, sl], qg_ref[:, sl]) * (A_HEAD_DIM ** -0.5)
        swapped = pltpu.roll(y, A_HEAD_DIM, 1)
        for hh in range(heads_per_slab):
            h = m * heads_per_slab + hh
            kvh = h // group
            src = y if kvh == hh else swapped
            keep = low if kvh == 0 else jnp.logical_not(low)
            q_ref[0, h] = jnp.where(keep, src, 0.0).astype(BF16)
    k_ref[0] = norm_rope(z_ref[0, :, A_Q:A_Q + A_KV], kg_ref[...]).astype(BF16)
    v_ref[0] = z_ref[0, :, A_Q + A_KV:A_IN].astype(BF16)


def _attn_pre(za, q_gain, k_gain, lc):
    bsz, ln, _ = za.shape
    tm = TOK_TILE
    cos, sin = _rope_tables(lc, ln - lc)
    qg = jnp.tile(q_gain, A_HEADS).reshape(1, A_Q)
    kg = jnp.tile(k_gain, A_KV_HEADS).reshape(1, A_KV)
    bones = _head_block_ones(A_HEAD_DIM, 1.0 / A_HEAD_DIM)
    return pl.pallas_call(
        _attn_pre_kernel,
        grid=(bsz, ln // tm),
        in_specs=[pl.BlockSpec((1, tm, A_IN), lambda b, j: (b, j, 0)),
                  pl.BlockSpec((1, A_Q), lambda b, j: (0, 0)),
                  pl.BlockSpec((1, A_KV), lambda b, j: (0, 0)),
                  pl.BlockSpec((tm, LANES), lambda b, j: (j, 0)),
                  pl.BlockSpec((tm, LANES), lambda b, j: (j, 0)),
                  pl.BlockSpec((LANES, LANES), lambda b, j: (0, 0))],
        out_specs=[pl.BlockSpec((1, A_HEADS, tm, LANES), lambda b, j: (b, 0, j, 0)),
                   pl.BlockSpec((1, tm, A_KV), lambda b, j: (b, j, 0)),
                   pl.BlockSpec((1, tm, A_KV), lambda b, j: (b, j, 0))],
        out_shape=[jax.ShapeDtypeStruct((bsz, A_HEADS, ln, LANES), BF16),
                   jax.ShapeDtypeStruct((bsz, ln, A_KV), BF16),
                   jax.ShapeDtypeStruct((bsz, ln, A_KV), BF16)],
        compiler_params=_params("arbitrary", "arbitrary"),
        name="attn_pre",
    )(za, qg, kg, cos, sin, bones)


def _attn_kernel(q_ref, k_ref, v_ref, o_ref, *, lc):
    tq = q_ref.shape[2]
    j = pl.program_id(1)
    lane = lax.broadcasted_iota(jnp.int32, (tq, LANES), 1)
    low = lane < A_HEAD_DIM
    heads_per_slab = LANES // A_HEAD_DIM
    group = A_HEADS // A_KV_HEADS

    def run(klen):
        k = k_ref[0, :klen, :]
        v = v_ref[0, :klen, :]
        for m in range(A_Q // LANES):
            parts = []
            for hh in range(heads_per_slab):
                h = m * heads_per_slab + hh
                kvh = h // group
                s = _mm_nt(q_ref[0, h], k)
                p = jnp.exp(s - jnp.max(s, axis=-1, keepdims=True))
                l = jnp.sum(p, axis=-1, keepdims=True)
                o = _mm(p.astype(BF16), v) / l
                parts.append(o if kvh == hh else pltpu.roll(o, A_HEAD_DIM, 1))
            o_ref[0, :, m * LANES:(m + 1) * LANES] = jnp.where(low, parts[0], parts[1]).astype(o_ref.dtype)

    @pl.when(j * tq < lc)
    def _():
        run(lc)

    @pl.when(j * tq >= lc)
    def _():
        run(k_ref.shape[1])


def _attention(q, k, v, lc):
    bsz, _, ln, _ = q.shape
    tq = TOK_TILE
    return pl.pallas_call(
        functools.partial(_attn_kernel, lc=lc),
        grid=(bsz, ln // tq),
        in_specs=[pl.BlockSpec((1, A_HEADS, tq, LANES), lambda b, j: (b, 0, j, 0)),
                  pl.BlockSpec((1, ln, A_KV), lambda b, j: (b, 0, 0)),
                  pl.BlockSpec((1, ln, A_KV), lambda b, j: (b, 0, 0))],
        out_specs=pl.BlockSpec((1, tq, A_Q), lambda b, j: (b, j, 0)),
        out_shape=jax.ShapeDtypeStruct((bsz, ln, A_Q), BF16),
        compiler_params=_params("arbitrary", "arbitrary"),
        name="attention",
    )(q, k, v)


def _shifted_rows(z, prev_row, next_row):
    tm = z.shape[0]
    row = lax.broadcasted_iota(jnp.int32, (tm, 1), 0)
    zprev = jnp.where(row == 0, prev_row, pltpu.roll(z, 1, 0))
    znext = jnp.where(row == tm - 1, next_row, pltpu.roll(z, tm - 1, 0))
    return zprev, znext


def _halo_rows(zp_ref, zn_ref, j, lct, nj):
    prev_ok = jnp.logical_and(j != 0, j != lct)
    next_ok = jnp.logical_and(j != lct - 1, j != nj - 1)
    prev_row = jnp.where(prev_ok, zp_ref[0, 7:8, :], 0.0)
    next_row = jnp.where(next_ok, zn_ref[0, 0:1, :], 0.0)
    return prev_row, next_row


def _halo_specs(tm, width, ln):
    nb8 = tm // 8
    last = ln // 8 - 1
    return [pl.BlockSpec((1, tm, width), lambda b, j: (b, j, 0)),
            pl.BlockSpec((1, 8, width), lambda b, j: (b, jnp.maximum(j * nb8 - 1, 0), 0)),
            pl.BlockSpec((1, 8, width), lambda b, j: (b, jnp.minimum((j + 1) * nb8, last), 0))]


def _per_head_sum(x, bones):
    return jnp.concatenate(
        [_bmm(x[:, m * LANES:(m + 1) * LANES], bones) for m in range(x.shape[1] // LANES)], axis=1)


def _rwkv_pre_kernel(z_ref, zp_ref, zn_ref, mu_ref, w0_ref, w2_ref, a0_ref, a2_ref, g2_ref,
                     kk_ref, ka_ref, rk_ref, bones_ref,
                     r_o, v_o, kkn_o, g_o, bonus_o, lw_o, a_o, kd_o, *, lct):
    j = pl.program_id(1)
    z = z_ref[0]
    prev_row, next_row = _halo_rows(zp_ref, zn_ref, j, lct, pl.num_programs(1))
    zprev, znext = _shifted_rows(z, prev_row, next_row)
    zm = z + mu_ref[...] * (0.5 * (zprev + znext) - z)
    w = B_WIDTH
    r, k, v = zm[:, 0:w], zm[:, w:2 * w], zm[:, 2 * w:3 * w]
    wa = zm[:, 3 * w:3 * w + B_DECAY_LORA + B_AAA_LORA]
    gl = zm[:, 3 * w + B_DECAY_LORA + B_AAA_LORA:]
    bones = bones_ref[...]
    kk = k * kk_ref[...]
    kkn = kk / jnp.maximum(jnp.sqrt(_per_head_sum(kk * kk, bones)), 1e-12)
    r_o[0] = r.astype(r_o.dtype)
    v_o[0] = v.astype(v_o.dtype)
    kkn_o[0] = kkn.astype(kkn_o.dtype)
    g_o[0] = _bmm(_sigmoid(gl), g2_ref[...]).astype(g_o.dtype)
    twa = jnp.tanh(wa)
    bonus = jnp.zeros_like(r)
    for d in range(2):
        wraw = w0_ref[d] + _bmm(twa, w2_ref[d])
        lw_o[d, 0] = -jnp.exp(-_softplus(-wraw) - 0.5)
        a = _sigmoid(a0_ref[d] + _bmm(wa, a2_ref[d]))
        kd = k * (1.0 + (a - 1.0) * ka_ref[...])
        a_o[d, 0] = a.astype(a_o.dtype)
        kd_o[d, 0] = kd.astype(kd_o.dtype)
        bonus = bonus + _per_head_sum(r * kd * rk_ref[...], bones) * v
    bonus_o[0] = bonus.astype(bonus_o.dtype)


def _rwkv_pre(zr, mu, w0, w2, a0, a2, g2, k_k, k_a, r_k, lc):
    bsz, ln, width = zr.shape
    tm = TOK_TILE
    w = B_WIDTH
    zeros = jnp.zeros((2, B_DECAY_LORA, w), F32)
    w2p = jnp.concatenate([w2, zeros], axis=1)
    a2p = jnp.concatenate([zeros, a2], axis=1)
    bones = _head_block_ones(B_HEAD_DIM, 1.0)
    full = lambda *shape: pl.BlockSpec(shape, lambda b, j: (0,) * len(shape))
    tok = pl.BlockSpec((1, tm, w), lambda b, j: (b, j, 0))
    tok2 = pl.BlockSpec((2, 1, tm, w), lambda b, j: (0, b, j, 0))
    sd = jax.ShapeDtypeStruct((bsz, ln, w), BF16)
    sd2 = jax.ShapeDtypeStruct((2, bsz, ln, w), BF16)
    lw_sd = jax.ShapeDtypeStruct((2, bsz, ln, w), F32)
    return pl.pallas_call(
        functools.partial(_rwkv_pre_kernel, lct=lc // tm),
        grid=(bsz, ln // tm),
        in_specs=_halo_specs(tm, width, ln) + [
            full(1, width), full(2, 1, w), full(2, 2 * B_DECAY_LORA, w), full(2, 1, w),
            full(2, 2 * B_AAA_LORA, w), full(B_GATE_LORA, w), full(1, w), full(1, w), full(1, w),
            full(LANES, LANES)],
        out_specs=[tok, tok, tok, tok, tok, tok2, tok2, tok2],
        out_shape=[sd, sd, sd, sd, sd, lw_sd, sd2, sd2],
        compiler_params=_params("arbitrary", "arbitrary"),
        name="rwkv_pre",
    )(zr, zr, zr, mu.reshape(1, width), w0.reshape(2, 1, w), w2p, a0.reshape(2, 1, w), a2p, g2,
      k_k.reshape(1, w), k_a.reshape(1, w), r_k.reshape(1, w), bones)


def _scan_chunk(i, d, nctx, ntot):
    rev = jnp.where(i < nctx, nctx - 1 - i, ntot + nctx - 1 - i)
    return jnp.where(d == 0, i, rev)


def _order_masks(n, pair):
    t = np.arange(n * pair)
    same = (t[:, None] // n) == (t[None, :] // n)
    tt, ss = t[:, None] % n, t[None, :] % n
    incl = np.stack([same & (ss <= tt), same & (ss >= tt)]).astype(np.float32)
    strict = np.stack([same & (ss < tt), same & (ss > tt)]).astype(np.float32)
    return jnp.asarray(incl), jnp.asarray(strict)


def _pair_stack(x, low):
    return jnp.concatenate([jnp.where(low, x, 0.0), jnp.where(low, 0.0, x)], axis=0)


def _bmm(a, b):
    return _mm(a.astype(BF16), b.astype(BF16))


def _rwkv_chunk_kernel(rf_ref, vf_ref, kf_ref, lwf_ref, af_ref, kdf_ref,
                       rb_ref, vb_ref, kb_ref, lwb_ref, ab_ref, kdb_ref,
                       tri_ref, incl_ref, strict_ref, of_ref, ob_ref, state_ref):
    c = R_CHUNK
    n2 = 2 * c

    @pl.when(pl.program_id(1) == 0)
    def _():
        state_ref[...] = jnp.zeros_like(state_ref)

    low = lax.broadcasted_iota(jnp.int32, (c, LANES), 1) < B_HEAD_DIM
    eye = (lax.broadcasted_iota(jnp.int32, (LANES, LANES), 0)
           == lax.broadcasted_iota(jnp.int32, (LANES, LANES), 1)).astype(F32)
    stack = lambda x: _pair_stack(x, low)
    refs = ((rf_ref, vf_ref, kf_ref, lwf_ref, af_ref, kdf_ref, of_ref),
            (rb_ref, vb_ref, kb_ref, lwb_ref, ab_ref, kdb_ref, ob_ref))
    dirs, slabs, outs = [], [], []
    kts, bts, kds, rts, vs, pcs = [], [], [], [], [], []
    for d, (r_ref, v_ref, kkn_ref, lw_ref, a_ref, kd_ref, o_ref) in enumerate(refs):
        lw = lw_ref[0, 0]
        cum = _mm(tri_ref[d], lw, HI)
        p_in = jnp.exp(cum)
        p_inv = jnp.exp(-cum)
        p_ex = jnp.exp(cum - lw)
        p_all = jnp.exp(jnp.sum(lw, axis=0, keepdims=True))
        kkn = kkn_ref[0].astype(F32)
        kt = kkn * p_ex
        bt = kkn * a_ref[0, 0].astype(F32) * p_inv
        kd = kd_ref[0, 0].astype(F32) * p_inv
        rt = r_ref[0].astype(F32) * p_in
        v = v_ref[0].astype(F32)
        for m in range(B_WIDTH // LANES):
            sl = slice(m * LANES, (m + 1) * LANES)
            dirs.append(d)
            slabs.append(m)
            outs.append(o_ref)
            kts.append(stack(kt[:, sl]).astype(BF16))
            bts.append(stack(bt[:, sl]).astype(BF16))
            kds.append(stack(kd[:, sl]).astype(BF16))
            rts.append(stack(rt[:, sl]))
            vs.append(stack(v[:, sl]).astype(BF16))
            pcs.append(p_all[:, sl])
    n = len(dirs)
    incl = [incl_ref[d] > 0.5 for d in range(2)]
    strict = [strict_ref[d] > 0.5 for d in range(2)]
    gram = [_mm_nt(jnp.concatenate([kts[x], rts[x].astype(BF16)], axis=0),
                   jnp.concatenate([bts[x], kds[x]], axis=0)) for x in range(n)]
    a_b = [jnp.where(strict[dirs[x]], gram[x][:n2, :n2], 0.0) for x in range(n)]
    a_k = [jnp.where(strict[dirs[x]], gram[x][:n2, n2:], 0.0) for x in range(n)]
    m_b = [jnp.where(incl[dirs[x]], gram[x][n2:, :n2], 0.0) for x in range(n)]
    m_k = [jnp.where(incl[dirs[x]], gram[x][n2:, n2:], 0.0) for x in range(n)]
    pw = [-a for a in a_b]
    tinv = [eye + p for p in pw]
    for _ in range(int(math.log2(c)) - 1):
        pw = [_bmm(p, p) for p in pw]
        tinv = [t + _bmm(t, p) for t, p in zip(tinv, pw)]
    mv = [_bmm(jnp.concatenate([a_k[x], m_k[x]], axis=0), vs[x]) for x in range(n)]
    wy = [_bmm(tinv[x], jnp.concatenate([kts[x], mv[x][:n2].astype(BF16)], axis=1)) for x in range(n)]
    mbwy = [_bmm(m_b[x], wy[x]) for x in range(n)]
    wyb = [_mm_tn(wy[x].astype(BF16), bts[x]) for x in range(n)]
    vtk = [_mm_tn(vs[x], kds[x]) for x in range(n)]
    s0 = [state_ref[dirs[x], slabs[x]] for x in range(n)]
    o_st = [_mm_nt((rts[x] - mbwy[x][:, :LANES]).astype(BF16), s0[x].astype(BF16))
            + (mv[x][n2:] - mbwy[x][:, LANES:]) for x in range(n)]
    for x in range(n):
        g = (eye - wyb[x][:LANES]) * pcs[x]
        h = (vtk[x] - wyb[x][LANES:]) * pcs[x]
        outs[x][0, :, slabs[x] * LANES:(slabs[x] + 1) * LANES] = (o_st[x][:c] + o_st[x][c:]).astype(outs[x].dtype)
        state_ref[dirs[x], slabs[x]] = _bmm(s0[x], g) + h


def _rwkv_chunks(r, v, kkn, lw, a, kd, lc):
    bsz, ln, w = r.shape
    c = R_CHUNK
    nctx, ntot = lc // c, ln // c
    tri, _ = _order_masks(c, 1)
    incl, strict = _order_masks(c, 2)
    specs = []
    for d in range(2):
        tok = pl.BlockSpec((1, c, w), lambda b, i, d=d: (b, _scan_chunk(i, d, nctx, ntot), 0))
        tok2 = pl.BlockSpec((1, 1, c, w), lambda b, i, d=d: (d, b, _scan_chunk(i, d, nctx, ntot), 0))
        specs.append((tok, tok2))
    full = lambda *shape: pl.BlockSpec(shape, lambda b, i: (0,) * len(shape))
    sd = jax.ShapeDtypeStruct((bsz, ln, w), BF16)
    return pl.pallas_call(
        _rwkv_chunk_kernel,
        grid=(bsz, ntot),
        in_specs=[specs[0][0]] * 3 + [specs[0][1]] * 3 + [specs[1][0]] * 3 + [specs[1][1]] * 3
        + [full(2, c, c), full(2, 2 * c, 2 * c), full(2, 2 * c, 2 * c)],
        out_specs=[specs[0][0], specs[1][0]],
        out_shape=[sd, sd],
        scratch_shapes=[pltpu.VMEM((2, w // LANES, LANES, LANES), F32)],
        compiler_params=_params("arbitrary", "arbitrary"),
        name="rwkv_chunks",
    )(r, v, kkn, lw, a, kd, r, v, kkn, lw, a, kd, tri, incl, strict)


def _even_out_kernel(xc_ref, xl_ref, mod_ref, attn_ref, wkvf_ref, wkvb_ref, bonus_ref, g_ref, gw_ref,
                     gb_ref, bones_ref, w_ref, oc_ref, ol_ref, *, lct):
    bones = bones_ref[...]
    y = bonus_ref[0].astype(F32)
    for wkv_ref in (wkvf_ref, wkvb_ref):
        o = wkv_ref[0].astype(F32)
        dlt = o - _per_head_sum(o, bones)
        var = _per_head_sum(dlt * dlt, bones)
        y = y + dlt * lax.rsqrt(var + B_GN_EPS) * gw_ref[...] + gb_ref[...]
    y = y * g_ref[0].astype(F32)
    mix = _mm(attn_ref[0].astype(BF16), w_ref[:A_Q, :]) + _mm(y.astype(BF16), w_ref[A_Q:, :])
    res = _stream_tile(xc_ref, xl_ref, lct) + mod_ref[0, 0][2:3] * mix
    is_ctx = pl.program_id(1) < lct

    @pl.when(is_ctx)
    def _():
        oc_ref[0] = res

    @pl.when(jnp.logical_not(is_ctx))
    def _():
        ol_ref[0] = res


def _even_out(xc, xl, modx, attn, wkv, bonus, g, gn_w, gn_b, w_out):
    bsz, lc, d = xc.shape
    seq = xl.shape[1]
    tm = TOK_TILE
    lct = lc // tm
    w = B_WIDTH
    bones = _head_block_ones(B_HEAD_DIM, 1.0 / B_HEAD_DIM)
    tokw = lambda n: pl.BlockSpec((1, tm, n), lambda b, j: (b, j, 0))
    full = lambda *shape: pl.BlockSpec(shape, lambda b, j: (0,) * len(shape))
    streams = _stream_specs(tm, d, lct)
    return pl.pallas_call(
        functools.partial(_even_out_kernel, lct=lct),
        grid=(bsz, (lc + seq) // tm),
        in_specs=streams + [
            pl.BlockSpec((1, 1, N_MOD, d), lambda b, j: (b, jnp.where(j < lct, 0, 1), 0, 0)),
            tokw(A_Q), tokw(w), tokw(w),
            tokw(w), tokw(w), full(1, w), full(1, w), full(LANES, LANES),
            full(A_Q + w, d)],
        out_specs=streams,
        out_shape=[jax.ShapeDtypeStruct((bsz, lc, d), F32), jax.ShapeDtypeStruct((bsz, seq, d), F32)],
        compiler_params=_params("arbitrary", "arbitrary"),
        name="even_out",
    )(xc, xl, modx, attn, wkv[0], wkv[1], bonus, g, gn_w.reshape(1, w), gn_b.reshape(1, w), bones, w_out)


def _ssd_pre_kernel(z_ref, zp_ref, zn_ref, dt_ref, cw_ref, cb_ref, dtb_ref, x_o, dt_o, *, lct):
    j = pl.program_id(1)
    z = z_ref[0]
    prev_row, next_row = _halo_rows(zp_ref, zn_ref, j, lct, pl.num_programs(1))
    zprev, znext = _shifted_rows(z, prev_row, next_row)
    y = zprev * cw_ref[0:1, :] + z * cw_ref[1:2, :] + znext * cw_ref[2:3, :] + cb_ref[...]
    x_o[0] = _silu(y).astype(x_o.dtype)
    dt_o[0] = _softplus(dt_ref[0] + dtb_ref[...])


def _ssd_pre(xbc, dtraw, conv_w, conv_b, dtb, lc):
    bsz, ln, width = xbc.shape
    tm = TOK_TILE
    nd = dtraw.shape[-1]
    full = lambda *shape: pl.BlockSpec(shape, lambda b, j: (0,) * len(shape))
    return pl.pallas_call(
        functools.partial(_ssd_pre_kernel, lct=lc // tm),
        grid=(bsz, ln // tm),
        in_specs=_halo_specs(tm, width, ln) + [
            pl.BlockSpec((1, tm, nd), lambda b, j: (b, j, 0)),
            full(conv_w.shape[0], width), full(1, width), full(1, nd)],
        out_specs=[pl.BlockSpec((1, tm, width), lambda b, j: (b, j, 0)),
                   pl.BlockSpec((1, tm, nd), lambda b, j: (b, j, 0))],
        out_shape=[jax.ShapeDtypeStruct((bsz, ln, width), BF16),
                   jax.ShapeDtypeStruct((bsz, ln, nd), F32)],
        compiler_params=_params("arbitrary", "arbitrary"),
        name="ssd_pre",
    )(xbc, xbc, xbc, dtraw, conv_w, conv_b.reshape(1, width), dtb)


def _ssd_chunk_kernel(x_ref, b_ref, c_ref, dt_ref, alog_ref, incl_ref, expand_ref, o_ref, state_ref,
                      *, n_heads):
    q = C_CHUNK
    i = pl.program_id(2)

    @pl.when(i == 0)
    def _():
        state_ref[...] = jnp.zeros_like(state_ref)

    lane = lax.broadcasted_iota(jnp.int32, (1, LANES), 1)
    a_neg = jnp.where(lane < n_heads, -jnp.exp(alog_ref[0]), 0.0)
    dt = dt_ref[0]
    dta = dt * a_neg
    incl_f = incl_ref[0]
    cum = _mm(incl_f, dta, HI)
    cum_t = cum.T
    total = jnp.sum(dta, axis=0, keepdims=True)
    expand = expand_ref[...]
    dt_x = _bmm(dt, expand)
    ec_x = _bmm(jnp.exp(cum), expand)
    de_x = _bmm(jnp.exp(total - cum), expand)
    cd_x = _bmm(jnp.broadcast_to(jnp.exp(total), (BF16_ROWS, LANES)), expand)[0:1]
    incl = incl_f > 0.5
    low = lax.broadcasted_iota(jnp.int32, (q, LANES), 1) < C_HEAD_DIM
    heads_per_group = n_heads // C_GROUPS
    pairs_per_group = heads_per_group // 2
    gw = heads_per_group * C_HEAD_DIM
    for g in range(C_GROUPS):
        gsl = slice(g * gw, (g + 1) * gw)
        bg = b_ref[0, :, g * C_STATE:(g + 1) * C_STATE].astype(BF16)
        cg = c_ref[0, :, g * C_STATE:(g + 1) * C_STATE].astype(BF16)
        cb = _mm_nt(cg, bg)
        xd = x_ref[0, :, gsl].astype(F32) * dt_x[:, gsl]
        s_prev = state_ref[g]
        y_off = _mm(cg, s_prev.astype(BF16)) * ec_x[:, gsl]
        state_ref[g] = s_prev * cd_x[:, gsl] + _mm_tn(bg, (xd * de_x[:, gsl]).astype(BF16))
        for pp in range(pairs_per_group):
            pr = g * pairs_per_group + pp
            sl = slice(pr * LANES, (pr + 1) * LANES)
            ms = []
            for hh in range(2):
                h = 2 * pr + hh
                seg = cum[:, h:h + 1] - cum_t[h:h + 1, :]
                ms.append((cb * jnp.exp(jnp.where(incl, seg, -1e30))).astype(BF16))
            xd_pair = _pair_stack(xd[:, pp * LANES:(pp + 1) * LANES], low).astype(BF16)
            y_diag = _mm(jnp.concatenate(ms, axis=1), xd_pair)
            o_ref[0, 0, :, sl] = (y_diag + y_off[:, pp * LANES:(pp + 1) * LANES]).astype(o_ref.dtype)


def _ssd_chunks(xact, dtact, a_log, lc):
    bsz, ln, _ = xact.shape
    q = C_CHUNK
    n_heads = a_log.shape[1]
    inner = n_heads * C_HEAD_DIM
    gn = C_GROUPS * C_STATE
    nctx, ntot = lc // q, ln // q
    incl, _ = _order_masks(q, 1)
    expand = np.zeros((LANES, inner), np.float32)
    expand[np.arange(inner) // C_HEAD_DIM, np.arange(inner)] = 1.0
    alog = jnp.zeros((2, 1, LANES), F32).at[:, 0, :n_heads].set(a_log)
    cmap = lambda off: (lambda b, d, i: (b, _scan_chunk(i, d, nctx, ntot), off))
    return pl.pallas_call(
        functools.partial(_ssd_chunk_kernel, n_heads=n_heads),
        grid=(bsz, 2, ntot),
        in_specs=[pl.BlockSpec((1, q, inner), cmap(0)),
                  pl.BlockSpec((1, q, gn), cmap(inner // gn)),
                  pl.BlockSpec((1, q, gn), cmap(inner // gn + 1)),
                  pl.BlockSpec((1, q, LANES), lambda b, d, i: (b, _scan_chunk(i, d, nctx, ntot), d)),
                  pl.BlockSpec((1, 1, LANES), lambda b, d, i: (d, 0, 0)),
                  pl.BlockSpec((1, q, q), lambda b, d, i: (d, 0, 0)),
                  pl.BlockSpec((LANES, inner), lambda b, d, i: (0, 0))],
        out_specs=pl.BlockSpec((1, 1, q, inner), lambda b, d, i: (d, b, _scan_chunk(i, d, nctx, ntot), 0)),
        out_shape=jax.ShapeDtypeStruct((2, bsz, ln, inner), BF16),
        scratch_shapes=[pltpu.VMEM((C_GROUPS, C_STATE, inner // C_GROUPS), F32)],
        compiler_params=_params("arbitrary", "arbitrary", "arbitrary"),
        name="ssd_chunks",
    )(xact, xact, xact, dtact, alog, incl, jnp.asarray(expand, dtype=BF16))


def _odd_out_kernel(x_ref, mod_ref, ys_ref, xs_ref, zg_ref, dsk_ref, nw_ref, w_ref, o_ref):
    y = (ys_ref[0, 0].astype(F32) + ys_ref[1, 0].astype(F32)) + dsk_ref[...] * xs_ref[0].astype(F32)
    yg = y * _silu(zg_ref[0].astype(F32))
    inner = yg.shape[1]
    gw = inner // C_GROUPS
    parts = []
    for g in range(C_GROUPS):
        t = yg[:, g * gw:(g + 1) * gw]
        ms = jnp.mean(t * t, axis=-1, keepdims=True)
        parts.append((t * lax.rsqrt(ms + EPS) * nw_ref[:, g * gw:(g + 1) * gw]).astype(BF16))
    yn = jnp.concatenate(parts, axis=1)
    o_ref[0] = x_ref[0] + mod_ref[0, 0][2:3] * _mm(yn, w_ref[...])


def _odd_out(x, modx, ys, xact, zg, d_skip, norm_w, w_out, lc):
    bsz, seq, d = x.shape
    ln = lc + seq
    tm = TOK_TILE
    lct = lc // tm
    inner = zg.shape[-1]
    dsk = jnp.repeat(d_skip, C_HEAD_DIM).reshape(1, inner)
    tokw = lambda n: pl.BlockSpec((1, tm, n), lambda b, j: (b, j + lct, 0))
    full = lambda *shape: pl.BlockSpec(shape, lambda b, j: (0,) * len(shape))
    return pl.pallas_call(
        _odd_out_kernel,
        grid=(bsz, (ln - lc) // tm),
        in_specs=[pl.BlockSpec((1, tm, d), lambda b, j: (b, j, 0)),
                  pl.BlockSpec((1, 1, N_MOD, d), lambda b, j: (b, 1, 0, 0)),
                  pl.BlockSpec((2, 1, tm, inner), lambda b, j: (0, b, j + lct, 0)),
                  tokw(inner), tokw(inner), full(1, inner), full(1, inner), full(inner, d)],
        out_specs=pl.BlockSpec((1, tm, d), lambda b, j: (b, j, 0)),
        out_shape=jax.ShapeDtypeStruct((bsz, ln - lc, d), F32),
        compiler_params=_params("arbitrary", "arbitrary"),
        name="odd_out",
    )(x, modx, ys, xact, zg, dsk, norm_w.reshape(1, inner), w_out)


class _Group:
    def __init__(self, parts):
        self.parts = tuple(parts)


def _ew(f, *args):
    groups = [a for a in args if isinstance(a, _Group)]
    if not groups:
        return f(*args)
    return _Group(f(*[a.parts[m] if isinstance(a, _Group) else a for a in args])
                  for m in range(len(groups[0].parts)))


def _exchange(xs, i, l, descending=True):
    hi, lo = _ew(jnp.maximum, xs[i], xs[l]), _ew(jnp.minimum, xs[i], xs[l])
    xs[i], xs[l] = (hi, lo) if descending else (lo, hi)


def _bitonic_merge(xs):
    xs = list(xs)
    j = len(xs) // 2
    while j >= 1:
        for i in range(len(xs)):
            if i ^ j > i:
                _exchange(xs, i, i ^ j)
        j //= 2
    return xs


def _bitonic_sort(xs):
    xs = list(xs)
    k = 2
    while k <= len(xs):
        j = k // 2
        while j >= 1:
            for i in range(len(xs)):
                if i ^ j > i:
                    _exchange(xs, i, i ^ j, descending=(i & k) == 0)
            j //= 2
        k *= 2
    return xs


def _merge_sublanes(xs):
    n = len(xs)
    for shift in (4, 2, 1):
        other = [_ew(lambda v: pltpu.roll(v, shift, 0), x) for x in xs]
        xs = _bitonic_merge([_ew(jnp.maximum, xs[k], other[n - 1 - k]) for k in range(n)])
    return xs


def _row_tile(s, k):
    return _ew(lambda v: v[8 * k:8 * k + 8], s)


def _top_values(s):
    return _merge_sublanes(_bitonic_sort([_row_tile(s, k) for k in range(P_KEYS // 8)]))


def _top_pair_sums(t1, t2, shape):
    sub = lax.broadcasted_iota(jnp.int32, shape, 0)
    lo, hi = t2[0], t2[8]
    for b in range(1, 8):
        lo = _ew(lambda x, y: jnp.where(sub == b, x, y), t2[b], lo)
        hi = _ew(lambda x, y: jnp.where(sub == b, x, y), t2[8 + b], hi)
    xs = [_ew(jnp.add, t1[a], lo) for a in range(P_TOPK)]
    carry = _ew(jnp.add, t1[0], hi)
    for k in range(P_TOPK):
        xs[k], carry = _ew(jnp.maximum, xs[k], carry), _ew(jnp.minimum, xs[k], carry)
    return _merge_sublanes(xs)


def _prefix_count(test, vals):
    sel = functools.partial(_ew, jnp.where)
    b3 = test(vals[7])
    b2 = test(sel(b3, vals[11], vals[3]))
    b1 = test(sel(b3, sel(b2, vals[13], vals[9]), sel(b2, vals[5], vals[1])))
    b0 = test(sel(b3, sel(b2, sel(b1, vals[14], vals[12]), sel(b1, vals[10], vals[8])),
                  sel(b2, sel(b1, vals[6], vals[4]), sel(b1, vals[2], vals[0]))))
    n = _ew(lambda p3, p2, p1, p0: (jnp.where(p3, 8.0, 0.0) + jnp.where(p2, 4.0, 0.0))
            + (jnp.where(p1, 2.0, 0.0) + jnp.where(p0, 1.0, 0.0)), b3, b2, b1, b0)
    return _ew(lambda full, m: jnp.where(full, 16.0, m), test(vals[15]), n)


def _peer_query_kernel(x_ref, mod_ref, g_ref, wq_ref, keys_ref, ht_o, n_o, c_o, rank_o, e2_o):
    assert P_TOPK == 16 and P_KEYS == 128
    mod = mod_ref[0]
    h2 = _norm_mod(x_ref[...], g_ref[...], mod[3:4], mod[4:5])
    ht = h2.T.astype(BF16)
    ht_o[...] = ht
    qt = _mm(wq_ref[...], ht).astype(BF16)
    kd = P_KEY_DIM
    tile_shape = (8, ht.shape[1])
    for h0 in range(0, P_HEADS, PEER_HEAD_GROUP):
        heads = range(h0, h0 + PEER_HEAD_GROUP)
        s1 = _Group(_mm(keys_ref[h, 0], qt[(2 * h) * kd:(2 * h + 1) * kd]) for h in heads)
        s2 = _Group(_mm(keys_ref[h, 1], qt[(2 * h + 1) * kd:(2 * h + 2) * kd]) for h in heads)
        t1 = _top_values(s1)
        t2 = _top_values(s2)
        best = _top_pair_sums(t1, t2, tile_shape)
        z = _ew(jnp.ones_like, best[0])
        for k in range(1, P_TOPK):
            z = _ew(lambda acc, v, top: acc + jnp.exp(v - top), z, best[k], best[0])
        inv_z = _ew(lambda v: 0.5 / v, z)
        tau = best[P_TOPK - 1]
        ranks, e2s = [], []
        for k in range(P_KEYS // 8):
            rows = slice(8 * k, 8 * k + 8)
            s1k, s2k = _row_tile(s1, k), _row_tile(s2, k)
            nk = _prefix_count(lambda v: _ew(lambda s, x, t: (s + x) >= t, s1k, v, tau), t2)
            ck = _ew(lambda s, top, iz: jnp.exp(s - top) * iz, s1k, t1[0], inv_z)
            ranks.append(_prefix_count(lambda v: _ew(lambda s, x: x > s, s2k, v), t2))
            e2s.append(_ew(lambda s, top: jnp.exp(s - top), s2k, t2[0]))
            for m, h in enumerate(heads):
                n_o[h, rows] = nk.parts[m]
                c_o[h, rows] = ck.parts[m]
        for m, h in enumerate(heads):
            rank_o[h] = jnp.concatenate([r.parts[m] for r in ranks], axis=0).astype(BF16)
            e2_o[h] = jnp.concatenate([e.parts[m] for e in e2s], axis=0).astype(BF16)


def _peer_query(xt, mod, gain, wq_t, keys, blocks_per_mod, tt):
    t_all, d = xt.shape
    nq = wq_t.shape[0]
    nblk = t_all // tt
    head = pl.BlockSpec((P_HEADS, P_KEYS, tt), lambda i: (0, 0, i))
    hsd = lambda dt: jax.ShapeDtypeStruct((P_HEADS, P_KEYS, t_all), dt)
    wspec = pl.BlockSpec((nq, d), lambda i: (0, 0))
    kspec = pl.BlockSpec((P_HEADS, 2, P_KEYS, P_KEY_DIM), lambda i: (0, 0, 0, 0))
    return pl.pallas_call(
        _peer_query_kernel,
        grid=(nblk,),
        in_specs=[pl.BlockSpec((tt, d), lambda i: (i, 0)),
                  pl.BlockSpec((1, N_MOD, d), lambda i: (i // blocks_per_mod, 0, 0)),
                  pl.BlockSpec((1, d), lambda i: (0, 0)),
                  wspec, kspec],
        out_specs=[pl.BlockSpec((d, tt), lambda i: (0, i)), head, head, head, head],
        out_shape=[jax.ShapeDtypeStruct((d, t_all), BF16), hsd(F32), hsd(F32), hsd(BF16), hsd(BF16)],
        compiler_params=_params("arbitrary"),
        name="peer_query",
    )(xt, mod, gain.reshape(1, d), wq_t.astype(BF16), keys.astype(BF16))


def _twice_gelu(x):
    return x * (1.0 + lax.erf(x * (2.0 ** -0.5)))


def _peer_main_kernel(x_ref, mod_ref, ht_ref, u_ref, vt_ref, n_ref, c_ref, rank_ref, e2_ref,
                      o_ref, acc_ref, act_ref, coef_ref):
    g = pl.program_id(1)
    last = pl.num_programs(1) - 1
    tt = ht_ref.shape[1]
    rows_per_step = u_ref.shape[0] // P_KEYS
    tiles = P_KEYS // BF16_ROWS
    tw = PEER_COEF_TOK

    def consume():
        acc_ref[...] += _mm(vt_ref[...], coef_ref[...] * act_ref[...])

    def prepare():
        zero = jnp.zeros((BF16_ROWS, tw), BF16)
        row0 = pl.multiple_of(g * rows_per_step, rows_per_step)
        n_rows = [n_ref[h, pl.ds(row0, rows_per_step), :] for h in range(P_HEADS)]
        c_rows = [c_ref[h, pl.ds(row0, rows_per_step), :] for h in range(P_HEADS)]
        for ii in range(rows_per_step):
            for t0 in range(0, tt, tw):
                tok = slice(t0, t0 + tw)
                coef = [None] * tiles
                for h in range(P_HEADS):
                    nb = jnp.broadcast_to(n_rows[h][ii:ii + 1, tok], (BF16_ROWS, tw)).astype(BF16)
                    cb = jnp.broadcast_to(c_rows[h][ii:ii + 1, tok], (BF16_ROWS, tw)).astype(BF16)
                    for k in range(tiles):
                        rows = slice(k * BF16_ROWS, (k + 1) * BF16_ROWS)
                        term = jnp.where(rank_ref[h, rows, tok] < nb, e2_ref[h, rows, tok] * cb, zero)
                        coef[k] = term if coef[k] is None else coef[k] + term
                for k in range(tiles):
                    rows = slice(ii * P_KEYS + k * BF16_ROWS, ii * P_KEYS + (k + 1) * BF16_ROWS)
                    coef_ref[rows, tok] = coef[k]
        act_ref[...] = _twice_gelu(_mm(u_ref[...], ht_ref[...])).astype(BF16)

    @pl.when(g == 0)
    def _():
        acc_ref[...] = jnp.zeros_like(acc_ref)
        prepare()

    @pl.when(jnp.logical_and(g > 0, g < last))
    def _():
        consume()
        prepare()

    @pl.when(g == last)
    def _():
        consume()
        o_ref[...] = x_ref[...] + mod_ref[0][5:6] * acc_ref[...].T


def _peer_main(xt, mod, ht, u_all, vt_all, layer, n, c, rank, e2, blocks_per_mod, tt):
    t_all, d = xt.shape
    n_exp = u_all.shape[1]
    eb = PEER_EXP
    nsteps = n_exp // eb
    head = pl.BlockSpec((P_HEADS, P_KEYS, tt), lambda i, s: (0, 0, i))
    return pl.pallas_call(
        _peer_main_kernel,
        grid=(t_all // tt, nsteps + 1),
        in_specs=[pl.BlockSpec((tt, d), lambda i, s: (i, 0)),
                  pl.BlockSpec((1, N_MOD, d), lambda i, s: (i // blocks_per_mod, 0, 0)),
                  pl.BlockSpec((d, tt), lambda i, s: (0, i)),
                  pl.BlockSpec((None, eb, d), lambda i, s: (layer, jnp.minimum(s, nsteps - 1), 0)),
                  pl.BlockSpec((None, d, eb), lambda i, s: (layer, 0, jnp.maximum(s - 1, 0))),
                  head, head, head, head],
        out_specs=pl.BlockSpec((tt, d), lambda i, s: (i, 0)),
        out_shape=jax.ShapeDtypeStruct((t_all, d), F32),
        scratch_shapes=[pltpu.VMEM((d, tt), F32), pltpu.VMEM((eb, tt), BF16), pltpu.VMEM((eb, tt), BF16)],
        compiler_params=_params("arbitrary", "arbitrary"),
        name="peer_main",
    )(xt, mod, ht, u_all, vt_all, n, c, rank, e2)


def _peer_layer(xt, mod, gain, w_q, keys, u_all, vt_all, layer, blocks_per_mod, tt):
    wq_t = w_q.T
    tq = PEER_QUERY_TOK
    ht, n, c, rank, e2 = _peer_query(xt, mod, gain, wq_t, keys, blocks_per_mod * (tt // tq), tq)
    return _peer_main(xt, mod, ht, u_all, vt_all, layer, n, c, rank, e2, blocks_per_mod, tt)


def kernel(x, c, ctx, c_ctx, mod_w, mod_b, norm1_g, norm2_g, ev_w_in, ev_w_out, attn_q_gain, attn_k_gain,
           rw_mu, rw_w0, rw_w2, rw_a0, rw_a2, rw_g2, rw_k_k, rw_k_a, rw_r_k, rw_gn_w, rw_gn_b, ssd_w_in,
           ssd_conv_w, ssd_conv_b, ssd_dt_bias, ssd_a_log, ssd_d, ssd_norm_w, ssd_w_out, peer_w_q,
           peer_keys, peer_u, peer_v):
    bsz, seq, dm = x.shape
    lc = ctx.shape[1]
    depth = mod_w.shape[0]
    assert lc == TOK_TILE and seq % PEER_TOK == 0 and (bsz * lc) % PEER_TOK == 0

    rows = -(-(bsz + 1) // 8) * 8
    cc = jnp.concatenate([c, c_ctx[None], jnp.zeros((rows - bsz - 1, dm), F32)], axis=0)
    mods = _modulation(cc, mod_w, mod_b)
    xc, xl = ctx, x
    u_all = peer_u.astype(BF16)
    vt_all = jnp.swapaxes(peer_v, 1, 2).astype(BF16)

    for i in range(depth):
        last = i == depth - 1
        j = i // 2
        ml = mods[i, :bsz].reshape(bsz, N_MOD, dm)
        mc = mods[i, bsz].reshape(1, N_MOD, dm)
        modx = jnp.stack([jnp.broadcast_to(mc, (bsz, N_MOD, dm)), ml], axis=1)
        if i % 2 == 0:
            za, zr = _in_proj(xc, xl, modx, norm1_g[i], ev_w_in[j].astype(BF16),
                              ((A_IN, F32), (B_IN, F32)))
            q, k, v = _attn_pre(za, attn_q_gain[j], attn_k_gain[j], lc)
            attn = _attention(q, k, v, lc)
            r, vv, kkn, g, bonus, lw, a, kd = _rwkv_pre(
                zr, rw_mu[j], rw_w0[j], rw_w2[j], rw_a0[j], rw_a2[j], rw_g2[j], rw_k_k[j], rw_k_a[j],
                rw_r_k[j], lc)
            wkv = _rwkv_chunks(r, vv, kkn, lw, a, kd, lc)
            xc, xl = _even_out(xc, xl, modx, attn, wkv, bonus, g, rw_gn_w[j], rw_gn_b[j],
                               ev_w_out[j].astype(BF16))
        else:
            n_heads = ssd_a_log.shape[-1]
            inner = n_heads * C_HEAD_DIM
            conv_dim = inner + 2 * C_GROUPS * C_STATE
            w_in = ssd_w_in[j]
            w_dt = jnp.zeros((dm, 2 * LANES), F32)
            w_dt = w_dt.at[:, :n_heads].set(w_in[:, inner + conv_dim:inner + conv_dim + n_heads])
            w_dt = w_dt.at[:, LANES:LANES + n_heads].set(w_in[:, inner + conv_dim + n_heads:])
            dtb = jnp.zeros((1, 2 * LANES), F32)
            dtb = dtb.at[0, :n_heads].set(ssd_dt_bias[j, 0]).at[0, LANES:LANES + n_heads].set(ssd_dt_bias[j, 1])
            w_all = jnp.concatenate([w_in[:, :inner + conv_dim], w_dt], axis=1).astype(BF16)
            zg, xbc, dtraw = _in_proj(xc, xl, modx, norm1_g[i], w_all,
                                      ((inner, BF16), (conv_dim, F32), (2 * LANES, F32)))
            xact, dtact = _ssd_pre(xbc, dtraw, ssd_conv_w[j], ssd_conv_b[j], dtb, lc)
            ys = _ssd_chunks(xact, dtact, ssd_a_log[j], lc)
            assert last, "the SSD layer keeps only the latent stream"
            xl = _odd_out(xl, modx, ys, xact, zg, ssd_d[j], ssd_norm_w[j], ssd_w_out[j].astype(BF16), lc)

        tt = PEER_TOK
        xl = _peer_layer(xl.reshape(bsz * seq, dm), ml, norm2_g[i], peer_w_q[i], peer_keys[i],
                         u_all, vt_all, i, seq // tt, tt).reshape(bsz, seq, dm)
        if not last:
            xc = _peer_layer(xc.reshape(bsz * lc, dm), mc, norm2_g[i], peer_w_q[i], peer_keys[i],
                             u_all, vt_all, i, (bsz * lc) // tt, tt).reshape(bsz, lc, dm)
    return xl
```

```python
import functools
import math

import numpy as np
import jax
import jax.numpy as jnp
from jax import lax
from jax.experimental import pallas as pl
from jax.experimental.pallas import tpu as pltpu

F32 = jnp.float32
BF16 = jnp.bfloat16
HI = lax.Precision.HIGHEST

GRID_W = 64
EPS = 1e-6
N_MOD = 6
A_HEADS, A_KV_HEADS, A_HEAD_DIM = 8, 2, 64
A_Q = A_HEADS * A_HEAD_DIM
A_KV = A_KV_HEADS * A_HEAD_DIM
A_IN = A_Q + 2 * A_KV
ROPE_THETA = 10000.0
B_HEADS, B_HEAD_DIM = 8, 64
B_WIDTH = B_HEADS * B_HEAD_DIM
B_DECAY_LORA, B_AAA_LORA, B_GATE_LORA = 64, 64, 128
B_IN = 3 * B_WIDTH + B_DECAY_LORA + B_AAA_LORA + B_GATE_LORA
B_GN_EPS = 64e-5
C_HEAD_DIM, C_GROUPS, C_STATE, C_CHUNK = 64, 4, 128, 128
P_HEADS, P_KEYS, P_KEY_DIM, P_TOPK = 8, 128, 128, 16

LANES = 128
BF16_ROWS = 16
TOK_TILE = 256
R_CHUNK = 64
RWKV_BATCH = 2
PEER_TOK = 512
PEER_QUERY_TOK = 256
PEER_HEAD_GROUP = 2
PEER_EXP = 1024
PEER_COEF_TOK = 256
VMEM_LIMIT = 56 * 1024 * 1024


def _mm(a, b, prec=None):
    return lax.dot_general(a, b, (((1,), (0,)), ((), ())), precision=prec,
                           preferred_element_type=F32)


def _mm_nt(a, b, prec=None):
    return lax.dot_general(a, b, (((1,), (1,)), ((), ())), precision=prec,
                           preferred_element_type=F32)


def _mm_tn(a, b, prec=None):
    return lax.dot_general(a, b, (((0,), (0,)), ((), ())), precision=prec,
                           preferred_element_type=F32)


def _params(*sem):
    return pltpu.CompilerParams(dimension_semantics=sem, vmem_limit_bytes=VMEM_LIMIT)


def _softplus(x):
    return jnp.maximum(x, 0.0) + jnp.log1p(jnp.exp(-jnp.abs(x)))


def _sigmoid(x):
    return 1.0 / (1.0 + jnp.exp(-x))


def _silu(x):
    return x * _sigmoid(x)


def _mod_kernel(c_ref, w_ref, b_ref, o_ref):
    o_ref[0] = _mm(_silu(c_ref[...]), w_ref[0], HI) + b_ref[0]


def _modulation(cc, mod_w, mod_b):
    depth, d, n6 = mod_w.shape
    rows = cc.shape[0]
    tn = 1024
    return pl.pallas_call(
        _mod_kernel,
        grid=(depth, n6 // tn),
        in_specs=[pl.BlockSpec((rows, d), lambda l, n: (0, 0)),
                  pl.BlockSpec((1, d, tn), lambda l, n: (l, 0, n)),
                  pl.BlockSpec((1, 1, tn), lambda l, n: (l, 0, n))],
        out_specs=pl.BlockSpec((1, rows, tn), lambda l, n: (l, 0, n)),
        out_shape=jax.ShapeDtypeStruct((depth, rows, n6), F32),
        compiler_params=_params("arbitrary", "arbitrary"),
        name="modulation",
    )(cc, mod_w, mod_b.reshape(depth, 1, n6))


def _norm_mod(x, gain, shift, scale):
    ms = jnp.mean(x * x, axis=-1, keepdims=True)
    return x * lax.rsqrt(ms + EPS) * gain * (1.0 + scale) + shift


def _stream_specs(tm, d, lct):
    return [pl.BlockSpec((1, tm, d), lambda b, j: (b, jnp.minimum(j, lct - 1), 0)),
            pl.BlockSpec((1, tm, d), lambda b, j: (b, jnp.maximum(j - lct, 0), 0))]


def _stream_tile(xc_ref, xl_ref, lct):
    return jnp.where(pl.program_id(1) < lct, xc_ref[0], xl_ref[0])


def _in_proj_kernel(xc_ref, xl_ref, mod_ref, g_ref, w_ref, *o_refs, lct):
    mod = mod_ref[0, 0]
    h = _norm_mod(_stream_tile(xc_ref, xl_ref, lct), g_ref[...], mod[0:1], mod[1:2]).astype(BF16)
    off = 0
    for o_ref in o_refs:
        n = o_ref.shape[-1]
        for c0 in range(0, n, 1024):
            c1 = min(n, c0 + 1024)
            o_ref[0, :, c0:c1] = _mm(h, w_ref[:, off + c0:off + c1]).astype(o_ref.dtype)
        off += n


def _in_proj(xc, xl, modx, gain, w, splits):
    bsz, lc, d = xc.shape
    ln = lc + xl.shape[1]
    assert w.shape[1] == sum(n for n, _ in splits)
    tm = TOK_TILE
    lct = lc // tm
    return pl.pallas_call(
        functools.partial(_in_proj_kernel, lct=lct),
        grid=(bsz, ln // tm),
        in_specs=_stream_specs(tm, d, lct) + [
            pl.BlockSpec((1, 1, N_MOD, d), lambda b, j: (b, jnp.where(j < lct, 0, 1), 0, 0)),
            pl.BlockSpec((1, d), lambda b, j: (0, 0)),
            pl.BlockSpec(w.shape, lambda b, j: (0, 0))],
        out_specs=[pl.BlockSpec((1, tm, n), lambda b, j: (b, j, 0)) for n, _ in splits],
        out_shape=[jax.ShapeDtypeStruct((bsz, ln, n), dt) for n, dt in splits],
        compiler_params=_params("arbitrary", "arbitrary"),
        name="in_proj",
    )(xc, xl, modx, gain.reshape(1, d), w)


def _rope_tables(lc, seq):
    t = jnp.arange(seq)
    row = (t // GRID_W).astype(F32)
    col = (t % GRID_W).astype(F32)
    m = A_HEAD_DIM // 4
    inv = ROPE_THETA ** (-jnp.arange(m, dtype=F32) / m)
    ar, ac = row[:, None] * inv, col[:, None] * inv
    cos = jnp.concatenate([jnp.cos(ar), jnp.cos(ar), jnp.cos(ac), jnp.cos(ac)], axis=-1)
    sin = jnp.concatenate([-jnp.sin(ar), jnp.sin(ar), -jnp.sin(ac), jnp.sin(ac)], axis=-1)
    reps = LANES // A_HEAD_DIM
    cos = jnp.concatenate([jnp.ones((lc, A_HEAD_DIM), F32), cos], axis=0)
    sin = jnp.concatenate([jnp.zeros((lc, A_HEAD_DIM), F32), sin], axis=0)
    return jnp.tile(cos, (1, reps)), jnp.tile(sin, (1, reps))


def _head_block_ones(width, scale):
    idx = np.arange(LANES) // width
    return jnp.asarray((idx[:, None] == idx[None, :]).astype(np.float32) * scale)


def _attn_pre_kernel(z_ref, qg_ref, kg_ref, cos_ref, sin_ref, bones_ref, q_ref, k_ref, v_ref):
    tm = z_ref.shape[1]
    cs, sn = cos_ref[...], sin_ref[...]
    bones = bones_ref[...]
    lane = lax.broadcasted_iota(jnp.int32, (tm, LANES), 1)
    first = (lane % 32) < 16
    low = lane < A_HEAD_DIM

    def norm_rope(x, gain):
        ms = _bmm(x * x, bones)
        y = x * lax.rsqrt(ms + EPS) * gain
        partner = jnp.where(first, pltpu.roll(y, LANES - 16, 1), pltpu.roll(y, 16, 1))
        return y * cs + partner * sn

    heads_per_slab = LANES // A_HEAD_DIM
    group = A_HEADS // A_KV_HEADS
    for m in range(A_Q // LANES):
        sl = slice(m * LANES, (m + 1) * LANES)
        y = norm_rope(z_ref[0, :, sl], qg_ref[:, sl]) * (A_HEAD_DIM ** -0.5)
        swapped = pltpu.roll(y, A_HEAD_DIM, 1)
        for hh in range(heads_per_slab):
            h = m * heads_per_slab + hh
            kvh = h // group
            src = y if kvh == hh else swapped
            keep = low if kvh == 0 else jnp.logical_not(low)
            q_ref[0, h] = jnp.where(keep, src, 0.0).astype(BF16)
    k_ref[0] = norm_rope(z_ref[0, :, A_Q:A_Q + A_KV], kg_ref[...]).astype(BF16)
    v_ref[0] = z_ref[0, :, A_Q + A_KV:A_IN].astype(BF16)


def _attn_pre(za, q_gain, k_gain, lc):
    bsz, ln, _ = za.shape
    tm = TOK_TILE
    cos, sin = _rope_tables(lc, ln - lc)
    qg = jnp.tile(q_gain, A_HEADS).reshape(1, A_Q)
    kg = jnp.tile(k_gain, A_KV_HEADS).reshape(1, A_KV)
    bones = _head_block_ones(A_HEAD_DIM, 1.0 / A_HEAD_DIM)
    return pl.pallas_call(
        _attn_pre_kernel,
        grid=(bsz, ln // tm),
        in_specs=[pl.BlockSpec((1, tm, A_IN), lambda b, j: (b, j, 0)),
                  pl.BlockSpec((1, A_Q), lambda b, j: (0, 0)),
                  pl.BlockSpec((1, A_KV), lambda b, j: (0, 0)),
                  pl.BlockSpec((tm, LANES), lambda b, j: (j, 0)),
                  pl.BlockSpec((tm, LANES), lambda b, j: (j, 0)),
                  pl.BlockSpec((LANES, LANES), lambda b, j: (0, 0))],
        out_specs=[pl.BlockSpec((1, A_HEADS, tm, LANES), lambda b, j: (b, 0, j, 0)),
                   pl.BlockSpec((1, tm, A_KV), lambda b, j: (b, j, 0)),
                   pl.BlockSpec((1, tm, A_KV), lambda b, j: (b, j, 0))],
        out_shape=[jax.ShapeDtypeStruct((bsz, A_HEADS, ln, LANES), BF16),
                   jax.ShapeDtypeStruct((bsz, ln, A_KV), BF16),
                   jax.ShapeDtypeStruct((bsz, ln, A_KV), BF16)],
        compiler_params=_params("arbitrary", "arbitrary"),
        name="attn_pre",
    )(za, qg, kg, cos, sin, bones)


def _attn_kernel(q_ref, k_ref, v_ref, o_ref, *, lc):
    tq = q_ref.shape[2]
    j = pl.program_id(1)
    lane = lax.broadcasted_iota(jnp.int32, (tq, LANES), 1)
    low = lane < A_HEAD_DIM
    heads_per_slab = LANES // A_HEAD_DIM
    group = A_HEADS // A_KV_HEADS

    def run(klen):
        k = k_ref[0, :klen, :]
        v = v_ref[0, :klen, :]
        for m in range(A_Q // LANES):
            parts = []
            for hh in range(heads_per_slab):
                h = m * heads_per_slab + hh
                kvh = h // group
                s = _mm_nt(q_ref[0, h], k)
                p = jnp.exp(s - jnp.max(s, axis=-1, keepdims=True))
                l = jnp.sum(p, axis=-1, keepdims=True)
                o = _mm(p.astype(BF16), v) / l
                parts.append(o if kvh == hh else pltpu.roll(o, A_HEAD_DIM, 1))
            o_ref[0, :, m * LANES:(m + 1) * LANES] = jnp.where(low, parts[0], parts[1]).astype(o_ref.dtype)

    @pl.when(j * tq < lc)
    def _():
        run(lc)

    @pl.when(j * tq >= lc)
    def _():
        run(k_ref.shape[1])


def _attention(q, k, v, lc):
    bsz, _, ln, _ = q.shape
    tq = TOK_TILE
    return pl.pallas_call(
        functools.partial(_attn_kernel, lc=lc),
        grid=(bsz, ln // tq),
        in_specs=[pl.BlockSpec((1, A_HEADS, tq, LANES), lambda b, j: (b, 0, j, 0)),
                  pl.BlockSpec((1, ln, A_KV), lambda b, j: (b, 0, 0)),
                  pl.BlockSpec((1, ln, A_KV), lambda b, j: (b, 0, 0))],
        out_specs=pl.BlockSpec((1, tq, A_Q), lambda b, j: (b, j, 0)),
        out_shape=jax.ShapeDtypeStruct((bsz, ln, A_Q), BF16),
        compiler_params=_params("arbitrary", "arbitrary"),
        name="attention",
    )(q, k, v)


def _shifted_rows(z, prev_row, next_row):
    tm = z.shape[0]
    row = lax.broadcasted_iota(jnp.int32, (tm, 1), 0)
    zprev = jnp.where(row == 0, prev_row, pltpu.roll(z, 1, 0))
    znext = jnp.where(row == tm - 1, next_row, pltpu.roll(z, tm - 1, 0))
    return zprev, znext


def _halo_rows(zp_ref, zn_ref, j, lct, nj):
    prev_ok = jnp.logical_and(j != 0, j != lct)
    next_ok = jnp.logical_and(j != lct - 1, j != nj - 1)
    prev_row = jnp.where(prev_ok, zp_ref[0, 7:8, :], 0.0)
    next_row = jnp.where(next_ok, zn_ref[0, 0:1, :], 0.0)
    return prev_row, next_row


def _halo_specs(tm, width, ln):
    nb8 = tm // 8
    last = ln // 8 - 1
    return [pl.BlockSpec((1, tm, width), lambda b, j: (b, j, 0)),
            pl.BlockSpec((1, 8, width), lambda b, j: (b, jnp.maximum(j * nb8 - 1, 0), 0)),
            pl.BlockSpec((1, 8, width), lambda b, j: (b, jnp.minimum((j + 1) * nb8, last), 0))]


def _per_head_sum(x, bones):
    return jnp.concatenate(
        [_bmm(x[:, m * LANES:(m + 1) * LANES], bones) for m in range(x.shape[1] // LANES)], axis=1)


def _rwkv_pre_kernel(z_ref, zp_ref, zn_ref, mu_ref, w0_ref, w2_ref, a0_ref, a2_ref, g2_ref,
                     kk_ref, ka_ref, rk_ref, bones_ref,
                     r_o, v_o, kkn_o, g_o, bonus_o, lw_o, a_o, kd_o, *, lct):
    j = pl.program_id(1)
    z = z_ref[0]
    prev_row, next_row = _halo_rows(zp_ref, zn_ref, j, lct, pl.num_programs(1))
    zprev, znext = _shifted_rows(z, prev_row, next_row)
    zm = z + mu_ref[...] * (0.5 * (zprev + znext) - z)
    w = B_WIDTH
    r, k, v = zm[:, 0:w], zm[:, w:2 * w], zm[:, 2 * w:3 * w]
    wa = zm[:, 3 * w:3 * w + B_DECAY_LORA + B_AAA_LORA]
    gl = zm[:, 3 * w + B_DECAY_LORA + B_AAA_LORA:]
    bones = bones_ref[...]
    kk = k * kk_ref[...]
    kkn = kk / jnp.maximum(jnp.sqrt(_per_head_sum(kk * kk, bones)), 1e-12)
    r_o[0] = r.astype(r_o.dtype)
    v_o[0] = v.astype(v_o.dtype)
    kkn_o[0] = kkn.astype(kkn_o.dtype)
    g_o[0] = _bmm(_sigmoid(gl), g2_ref[...]).astype(g_o.dtype)
    twa = jnp.tanh(wa)
    bonus = jnp.zeros_like(r)
    for d in range(2):
        wraw = w0_ref[d] + _bmm(twa, w2_ref[d])
        lw_o[d, 0] = -jnp.exp(-_softplus(-wraw) - 0.5)
        a = _sigmoid(a0_ref[d] + _bmm(wa, a2_ref[d]))
        kd = k * (1.0 + (a - 1.0) * ka_ref[...])
        a_o[d, 0] = a.astype(a_o.dtype)
        kd_o[d, 0] = kd.astype(kd_o.dtype)
        bonus = bonus + _per_head_sum(r * kd * rk_ref[...], bones) * v
    bonus_o[0] = bonus.astype(bonus_o.dtype)


def _rwkv_pre(zr, mu, w0, w2, a0, a2, g2, k_k, k_a, r_k, lc):
    bsz, ln, width = zr.shape
    tm = TOK_TILE
    w = B_WIDTH
    zeros = jnp.zeros((2, B_DECAY_LORA, w), F32)
    w2p = jnp.concatenate([w2, zeros], axis=1)
    a2p = jnp.concatenate([zeros, a2], axis=1)
    bones = _head_block_ones(B_HEAD_DIM, 1.0)
    full = lambda *shape: pl.BlockSpec(shape, lambda b, j: (0,) * len(shape))
    tok = pl.BlockSpec((1, tm, w), lambda b, j: (b, j, 0))
    tok2 = pl.BlockSpec((2, 1, tm, w), lambda b, j: (0, b, j, 0))
    sd = jax.ShapeDtypeStruct((bsz, ln, w), BF16)
    sd2 = jax.ShapeDtypeStruct((2, bsz, ln, w), BF16)
    lw_sd = jax.ShapeDtypeStruct((2, bsz, ln, w), F32)
    return pl.pallas_call(
        functools.partial(_rwkv_pre_kernel, lct=lc // tm),
        grid=(bsz, ln // tm),
        in_specs=_halo_specs(tm, width, ln) + [
            full(1, width), full(2, 1, w), full(2, 2 * B_DECAY_LORA, w), full(2, 1, w),
            full(2, 2 * B_AAA_LORA, w), full(B_GATE_LORA, w), full(1, w), full(1, w), full(1, w),
            full(LANES, LANES)],
        out_specs=[tok, tok, tok, tok, tok, tok2, tok2, tok2],
        out_shape=[sd, sd, sd, sd, sd, lw_sd, sd2, sd2],
        compiler_params=_params("arbitrary", "arbitrary"),
        name="rwkv_pre",
    )(zr, zr, zr, mu.reshape(1, width), w0.reshape(2, 1, w), w2p, a0.reshape(2, 1, w), a2p, g2,
      k_k.reshape(1, w), k_a.reshape(1, w), r_k.reshape(1, w), bones)


def _scan_chunk(i, d, nctx, ntot):
    rev = jnp.where(i < nctx, nctx - 1 - i, ntot + nctx - 1 - i)
    return jnp.where(d == 0, i, rev)


def _order_masks(n, pair):
    t = np.arange(n * pair)
    same = (t[:, None] // n) == (t[None, :] // n)
    tt, ss = t[:, None] % n, t[None, :] % n
    incl = np.stack([same & (ss <= tt), same & (ss >= tt)]).astype(np.float32)
    strict = np.stack([same & (ss < tt), same & (ss > tt)]).astype(np.float32)
    return jnp.asarray(incl), jnp.asarray(strict)


def _pair_stack(x, low):
    return jnp.concatenate([jnp.where(low, x, 0.0), jnp.where(low, 0.0, x)], axis=0)


def _bmm(a, b):
    return _mm(a.astype(BF16), b.astype(BF16))


def _rwkv_chunk_kernel(rf_ref, vf_ref, kf_ref, lwf_ref, af_ref, kdf_ref,
                       rb_ref, vb_ref, kb_ref, lwb_ref, ab_ref, kdb_ref,
                       tri_ref, incl_ref, strict_ref, of_ref, ob_ref, state_ref):
    c = R_CHUNK
    n2 = 2 * c

    @pl.when(pl.program_id(1) == 0)
    def _():
        state_ref[...] = jnp.zeros_like(state_ref)

    low = lax.broadcasted_iota(jnp.int32, (c, LANES), 1) < B_HEAD_DIM
    eye = (lax.broadcasted_iota(jnp.int32, (LANES, LANES), 0)
           == lax.broadcasted_iota(jnp.int32, (LANES, LANES), 1)).astype(F32)
    stack = lambda x: _pair_stack(x, low)
    refs = ((rf_ref, vf_ref, kf_ref, lwf_ref, af_ref, kdf_ref, of_ref),
            (rb_ref, vb_ref, kb_ref, lwb_ref, ab_ref, kdb_ref, ob_ref))
    dirs, slabs, outs, bats = [], [], [], []
    kts, bts, kds, rts, vs, pcs = [], [], [], [], [], []
    for bb, d in [(bb, d) for bb in range(rf_ref.shape[0]) for d in range(2)]:
        r_ref, v_ref, kkn_ref, lw_ref, a_ref, kd_ref, o_ref = refs[d]
        lw = lw_ref[0, bb]
        cum = _mm(tri_ref[d], lw, HI)
        p_in = jnp.exp(cum)
        p_inv = jnp.exp(-cum)
        p_ex = jnp.exp(cum - lw)
        p_all = jnp.exp(jnp.sum(lw, axis=0, keepdims=True))
        kkn = kkn_ref[bb].astype(F32)
        kt = kkn * p_ex
        bt = kkn * a_ref[0, bb].astype(F32) * p_inv
        kd = kd_ref[0, bb].astype(F32) * p_inv
        rt = r_ref[bb].astype(F32) * p_in
        v = v_ref[bb].astype(F32)
        for m in range(B_WIDTH // LANES):
            sl = slice(m * LANES, (m + 1) * LANES)
            dirs.append(d)
            slabs.append(m)
            bats.append(bb)
            outs.append(o_ref)
            kts.append(stack(kt[:, sl]).astype(BF16))
            bts.append(stack(bt[:, sl]).astype(BF16))
            kds.append(stack(kd[:, sl]).astype(BF16))
            rts.append(stack(rt[:, sl]))
            vs.append(stack(v[:, sl]).astype(BF16))
            pcs.append(p_all[:, sl])
    n = len(dirs)
    incl = [incl_ref[d] > 0.5 for d in range(2)]
    strict = [strict_ref[d] > 0.5 for d in range(2)]
    gram = [_mm_nt(jnp.concatenate([kts[x], rts[x].astype(BF16)], axis=0),
                   jnp.concatenate([bts[x], kds[x]], axis=0)) for x in range(n)]
    a_b = [jnp.where(strict[dirs[x]], gram[x][:n2, :n2], 0.0) for x in range(n)]
    a_k = [jnp.where(strict[dirs[x]], gram[x][:n2, n2:], 0.0) for x in range(n)]
    m_b = [jnp.where(incl[dirs[x]], gram[x][n2:, :n2], 0.0) for x in range(n)]
    m_k = [jnp.where(incl[dirs[x]], gram[x][n2:, n2:], 0.0) for x in range(n)]
    pw = [-a for a in a_b]
    tinv = [eye + p for p in pw]
    for _ in range(int(math.log2(c)) - 1):
        pw = [_bmm(p, p) for p in pw]
        tinv = [t + _bmm(t, p) for t, p in zip(tinv, pw)]
    mv = [_bmm(jnp.concatenate([a_k[x], m_k[x]], axis=0), vs[x]) for x in range(n)]
    wy = [_bmm(tinv[x], jnp.concatenate([kts[x], mv[x][:n2].astype(BF16)], axis=1)) for x in range(n)]
    mbwy = [_bmm(m_b[x], wy[x]) for x in range(n)]
    wyb = [_mm_tn(wy[x].astype(BF16), bts[x]) for x in range(n)]
    vtk = [_mm_tn(vs[x], kds[x]) for x in range(n)]
    s0 = [state_ref[bats[x], dirs[x], slabs[x]] for x in range(n)]
    o_st = [_mm_nt((rts[x] - mbwy[x][:, :LANES]).astype(BF16), s0[x].astype(BF16))
            + (mv[x][n2:] - mbwy[x][:, LANES:]) for x in range(n)]
    for x in range(n):
        g = (eye - wyb[x][:LANES]) * pcs[x]
        h = (vtk[x] - wyb[x][LANES:]) * pcs[x]
        outs[x][bats[x], :, slabs[x] * LANES:(slabs[x] + 1) * LANES] = (o_st[x][:c] + o_st[x][c:]).astype(outs[x].dtype)
        state_ref[bats[x], dirs[x], slabs[x]] = _bmm(s0[x], g) + h


def _rwkv_chunks(r, v, kkn, lw, a, kd, lc):
    bsz, ln, w = r.shape
    c = R_CHUNK
    nctx, ntot = lc // c, ln // c
    nb = RWKV_BATCH if bsz % RWKV_BATCH == 0 else 1
    tri, _ = _order_masks(c, 1)
    incl, strict = _order_masks(c, 2)
    specs = []
    for d in range(2):
        tok = pl.BlockSpec((nb, c, w), lambda b, i, d=d: (b, _scan_chunk(i, d, nctx, ntot), 0))
        tok2 = pl.BlockSpec((1, nb, c, w), lambda b, i, d=d: (d, b, _scan_chunk(i, d, nctx, ntot), 0))
        specs.append((tok, tok2))
    full = lambda *shape: pl.BlockSpec(shape, lambda b, i: (0,) * len(shape))
    sd = jax.ShapeDtypeStruct((bsz, ln, w), BF16)
    return pl.pallas_call(
        _rwkv_chunk_kernel,
        grid=(bsz // nb, ntot),
        in_specs=[specs[0][0]] * 3 + [specs[0][1]] * 3 + [specs[1][0]] * 3 + [specs[1][1]] * 3
        + [full(2, c, c), full(2, 2 * c, 2 * c), full(2, 2 * c, 2 * c)],
        out_specs=[specs[0][0], specs[1][0]],
        out_shape=[sd, sd],
        scratch_shapes=[pltpu.VMEM((nb, 2, w // LANES, LANES, LANES), F32)],
        compiler_params=_params("arbitrary", "arbitrary"),
        name="rwkv_chunks",
    )(r, v, kkn, lw, a, kd, r, v, kkn, lw, a, kd, tri, incl, strict)


def _even_out_kernel(xc_ref, xl_ref, mod_ref, attn_ref, wkvf_ref, wkvb_ref, bonus_ref, g_ref, gw_ref,
                     gb_ref, bones_ref, w_ref, oc_ref, ol_ref, *, lct):
    bones = bones_ref[...]
    y = bonus_ref[0].astype(F32)
    for wkv_ref in (wkvf_ref, wkvb_ref):
        o = wkv_ref[0].astype(F32)
        dlt = o - _per_head_sum(o, bones)
        var = _per_head_sum(dlt * dlt, bones)
        y = y + dlt * lax.rsqrt(var + B_GN_EPS) * gw_ref[...] + gb_ref[...]
    y = y * g_ref[0].astype(F32)
    mix = _mm(attn_ref[0].astype(BF16), w_ref[:A_Q, :]) + _mm(y.astype(BF16), w_ref[A_Q:, :])
    res = _stream_tile(xc_ref, xl_ref, lct) + mod_ref[0, 0][2:3] * mix
    is_ctx = pl.program_id(1) < lct

    @pl.when(is_ctx)
    def _():
        oc_ref[0] = res

    @pl.when(jnp.logical_not(is_ctx))
    def _():
        ol_ref[0] = res


def _even_out(xc, xl, modx, attn, wkv, bonus, g, gn_w, gn_b, w_out):
    bsz, lc, d = xc.shape
    seq = xl.shape[1]
    tm = TOK_TILE
    lct = lc // tm
    w = B_WIDTH
    bones = _head_block_ones(B_HEAD_DIM, 1.0 / B_HEAD_DIM)
    tokw = lambda n: pl.BlockSpec((1, tm, n), lambda b, j: (b, j, 0))
    full = lambda *shape: pl.BlockSpec(shape, lambda b, j: (0,) * len(shape))
    streams = _stream_specs(tm, d, lct)
    return pl.pallas_call(
        functools.partial(_even_out_kernel, lct=lct),
        grid=(bsz, (lc + seq) // tm),
        in_specs=streams + [
            pl.BlockSpec((1, 1, N_MOD, d), lambda b, j: (b, jnp.where(j < lct, 0, 1), 0, 0)),
            tokw(A_Q), tokw(w), tokw(w),
            tokw(w), tokw(w), full(1, w), full(1, w), full(LANES, LANES),
            full(A_Q + w, d)],
        out_specs=streams,
        out_shape=[jax.ShapeDtypeStruct((bsz, lc, d), F32), jax.ShapeDtypeStruct((bsz, seq, d), F32)],
        compiler_params=_params("arbitrary", "arbitrary"),
        name="even_out",
    )(xc, xl, modx, attn, wkv[0], wkv[1], bonus, g, gn_w.reshape(1, w), gn_b.reshape(1, w), bones, w_out)


def _ssd_pre_kernel(z_ref, zp_ref, zn_ref, dt_ref, cw_ref, cb_ref, dtb_ref, x_o, dt_o, *, lct):
    j = pl.program_id(1)
    z = z_ref[0]
    prev_row, next_row = _halo_rows(zp_ref, zn_ref, j, lct, pl.num_programs(1))
    zprev, znext = _shifted_rows(z, prev_row, next_row)
    y = zprev * cw_ref[0:1, :] + z * cw_ref[1:2, :] + znext * cw_ref[2:3, :] + cb_ref[...]
    x_o[0] = _silu(y).astype(x_o.dtype)
    dt_o[0] = _softplus(dt_ref[0] + dtb_ref[...])


def _ssd_pre(xbc, dtraw, conv_w, conv_b, dtb, lc):
    bsz, ln, width = xbc.shape
    tm = TOK_TILE
    nd = dtraw.shape[-1]
    full = lambda *shape: pl.BlockSpec(shape, lambda b, j: (0,) * len(shape))
    return pl.pallas_call(
        functools.partial(_ssd_pre_kernel, lct=lc // tm),
        grid=(bsz, ln // tm),
        in_specs=_halo_specs(tm, width, ln) + [
            pl.BlockSpec((1, tm, nd), lambda b, j: (b, j, 0)),
            full(conv_w.shape[0], width), full(1, width), full(1, nd)],
        out_specs=[pl.BlockSpec((1, tm, width), lambda b, j: (b, j, 0)),
                   pl.BlockSpec((1, tm, nd), lambda b, j: (b, j, 0))],
        out_shape=[jax.ShapeDtypeStruct((bsz, ln, width), BF16),
                   jax.ShapeDtypeStruct((bsz, ln, nd), F32)],
        compiler_params=_params("arbitrary", "arbitrary"),
        name="ssd_pre",
    )(xbc, xbc, xbc, dtraw, conv_w, conv_b.reshape(1, width), dtb)


def _ssd_chunk_kernel(x_ref, b_ref, c_ref, dt_ref, alog_ref, incl_ref, expand_ref, o_ref, state_ref,
                      *, n_heads):
    q = C_CHUNK
    i = pl.program_id(2)

    @pl.when(i == 0)
    def _():
        state_ref[...] = jnp.zeros_like(state_ref)

    lane = lax.broadcasted_iota(jnp.int32, (1, LANES), 1)
    a_neg = jnp.where(lane < n_heads, -jnp.exp(alog_ref[0]), 0.0)
    dt = dt_ref[0]
    dta = dt * a_neg
    incl_f = incl_ref[0]
    cum = _mm(incl_f, dta, HI)
    cum_t = cum.T
    total = jnp.sum(dta, axis=0, keepdims=True)
    expand = expand_ref[...]
    dt_x = _bmm(dt, expand)
    ec_x = _bmm(jnp.exp(cum), expand)
    de_x = _bmm(jnp.exp(total - cum), expand)
    cd_x = _bmm(jnp.broadcast_to(jnp.exp(total), (BF16_ROWS, LANES)), expand)[0:1]
    incl = incl_f > 0.5
    low = lax.broadcasted_iota(jnp.int32, (q, LANES), 1) < C_HEAD_DIM
    heads_per_group = n_heads // C_GROUPS
    pairs_per_group = heads_per_group // 2
    gw = heads_per_group * C_HEAD_DIM
    for g in range(C_GROUPS):
        gsl = slice(g * gw, (g + 1) * gw)
        bg = b_ref[0, :, g * C_STATE:(g + 1) * C_STATE].astype(BF16)
        cg = c_ref[0, :, g * C_STATE:(g + 1) * C_STATE].astype(BF16)
        cb = _mm_nt(cg, bg)
        xd = x_ref[0, :, gsl].astype(F32) * dt_x[:, gsl]
        s_prev = state_ref[g]
        y_off = _mm(cg, s_prev.astype(BF16)) * ec_x[:, gsl]
        state_ref[g] = s_prev * cd_x[:, gsl] + _mm_tn(bg, (xd * de_x[:, gsl]).astype(BF16))
        for pp in range(pairs_per_group):
            pr = g * pairs_per_group + pp
            sl = slice(pr * LANES, (pr + 1) * LANES)
            ms = []
            for hh in range(2):
                h = 2 * pr + hh
                seg = cum[:, h:h + 1] - cum_t[h:h + 1, :]
                ms.append((cb * jnp.exp(jnp.where(incl, seg, -1e30))).astype(BF16))
            xd_pair = _pair_stack(xd[:, pp * LANES:(pp + 1) * LANES], low).astype(BF16)
            y_diag = _mm(jnp.concatenate(ms, axis=1), xd_pair)
            o_ref[0, 0, :, sl] = (y_diag + y_off[:, pp * LANES:(pp + 1) * LANES]).astype(o_ref.dtype)


def _ssd_chunks(xact, dtact, a_log, lc):
    bsz, ln, _ = xact.shape
    q = C_CHUNK
    n_heads = a_log.shape[1]
    inner = n_heads * C_HEAD_DIM
    gn = C_GROUPS * C_STATE
    nctx, ntot = lc // q, ln // q
    incl, _ = _order_masks(q, 1)
    expand = np.zeros((LANES, inner), np.float32)
    expand[np.arange(inner) // C_HEAD_DIM, np.arange(inner)] = 1.0
    alog = jnp.zeros((2, 1, LANES), F32).at[:, 0, :n_heads].set(a_log)
    cmap = lambda off: (lambda b, d, i: (b, _scan_chunk(i, d, nctx, ntot), off))
    return pl.pallas_call(
        functools.partial(_ssd_chunk_kernel, n_heads=n_heads),
        grid=(bsz, 2, ntot),
        in_specs=[pl.BlockSpec((1, q, inner), cmap(0)),
                  pl.BlockSpec((1, q, gn), cmap(inner // gn)),
                  pl.BlockSpec((1, q, gn), cmap(inner // gn + 1)),
                  pl.BlockSpec((1, q, LANES), lambda b, d, i: (b, _scan_chunk(i, d, nctx, ntot), d)),
                  pl.BlockSpec((1, 1, LANES), lambda b, d, i: (d, 0, 0)),
                  pl.BlockSpec((1, q, q), lambda b, d, i: (d, 0, 0)),
                  pl.BlockSpec((LANES, inner), lambda b, d, i: (0, 0))],
        out_specs=pl.BlockSpec((1, 1, q, inner), lambda b, d, i: (d, b, _scan_chunk(i, d, nctx, ntot), 0)),
        out_shape=jax.ShapeDtypeStruct((2, bsz, ln, inner), BF16),
        scratch_shapes=[pltpu.VMEM((C_GROUPS, C_STATE, inner // C_GROUPS), F32)],
        compiler_params=_params("arbitrary", "arbitrary", "arbitrary"),
        name="ssd_chunks",
    )(xact, xact, xact, dtact, alog, incl, jnp.asarray(expand, dtype=BF16))


def _odd_out_kernel(x_ref, mod_ref, ys_ref, xs_ref, zg_ref, dsk_ref, nw_ref, w_ref, o_ref):
    y = (ys_ref[0, 0].astype(F32) + ys_ref[1, 0].astype(F32)) + dsk_ref[...] * xs_ref[0].astype(F32)
    yg = y * _silu(zg_ref[0].astype(F32))
    inner = yg.shape[1]
    gw = inner // C_GROUPS
    parts = []
    for g in range(C_GROUPS):
        t = yg[:, g * gw:(g + 1) * gw]
        ms = jnp.mean(t * t, axis=-1, keepdims=True)
        parts.append((t * lax.rsqrt(ms + EPS) * nw_ref[:, g * gw:(g + 1) * gw]).astype(BF16))
    yn = jnp.concatenate(parts, axis=1)
    o_ref[0] = x_ref[0] + mod_ref[0, 0][2:3] * _mm(yn, w_ref[...])


def _odd_out(x, modx, ys, xact, zg, d_skip, norm_w, w_out, lc):
    bsz, seq, d = x.shape
    ln = lc + seq
    tm = TOK_TILE
    lct = lc // tm
    inner = zg.shape[-1]
    dsk = jnp.repeat(d_skip, C_HEAD_DIM).reshape(1, inner)
    tokw = lambda n: pl.BlockSpec((1, tm, n), lambda b, j: (b, j + lct, 0))
    full = lambda *shape: pl.BlockSpec(shape, lambda b, j: (0,) * len(shape))
    return pl.pallas_call(
        _odd_out_kernel,
        grid=(bsz, (ln - lc) // tm),
        in_specs=[pl.BlockSpec((1, tm, d), lambda b, j: (b, j, 0)),
                  pl.BlockSpec((1, 1, N_MOD, d), lambda b, j: (b, 1, 0, 0)),
                  pl.BlockSpec((2, 1, tm, inner), lambda b, j: (0, b, j + lct, 0)),
                  tokw(inner), tokw(inner), full(1, inner), full(1, inner), full(inner, d)],
        out_specs=pl.BlockSpec((1, tm, d), lambda b, j: (b, j, 0)),
        out_shape=jax.ShapeDtypeStruct((bsz, ln - lc, d), F32),
        compiler_params=_params("arbitrary", "arbitrary"),
        name="odd_out",
    )(x, modx, ys, xact, zg, dsk, norm_w.reshape(1, inner), w_out)


class _Group:
    def __init__(self, parts):
        self.parts = tuple(parts)


def _ew(f, *args):
    groups = [a for a in args if isinstance(a, _Group)]
    if not groups:
        return f(*args)
    return _Group(f(*[a.parts[m] if isinstance(a, _Group) else a for a in args])
                  for m in range(len(groups[0].parts)))


def _exchange(xs, i, l, descending=True):
    hi, lo = _ew(jnp.maximum, xs[i], xs[l]), _ew(jnp.minimum, xs[i], xs[l])
    xs[i], xs[l] = (hi, lo) if descending else (lo, hi)


def _bitonic_merge(xs):
    xs = list(xs)
    j = len(xs) // 2
    while j >= 1:
        for i in range(len(xs)):
            if i ^ j > i:
                _exchange(xs, i, i ^ j)
        j //= 2
    return xs


def _bitonic_sort(xs):
    xs = list(xs)
    k = 2
    while k <= len(xs):
        j = k // 2
        while j >= 1:
            for i in range(len(xs)):
                if i ^ j > i:
                    _exchange(xs, i, i ^ j, descending=(i & k) == 0)
            j //= 2
        k *= 2
    return xs


def _merge_sublanes(xs):
    n = len(xs)
    for shift in (4, 2, 1):
        other = [_ew(lambda v: pltpu.roll(v, shift, 0), x) for x in xs]
        xs = _bitonic_merge([_ew(jnp.maximum, xs[k], other[n - 1 - k]) for k in range(n)])
    return xs


def _row_tile(s, k):
    return _ew(lambda v: v[8 * k:8 * k + 8], s)


def _top_values(s):
    return _merge_sublanes(_bitonic_sort([_row_tile(s, k) for k in range(P_KEYS // 8)]))


def _top_pair_sums(t1, t2, shape):
    sub = lax.broadcasted_iota(jnp.int32, shape, 0)
    lo, hi = t2[0], t2[8]
    for b in range(1, 8):
        lo = _ew(lambda x, y: jnp.where(sub == b, x, y), t2[b], lo)
        hi = _ew(lambda x, y: jnp.where(sub == b, x, y), t2[8 + b], hi)
    xs = [_ew(jnp.add, t1[a], lo) for a in range(P_TOPK)]
    carry = _ew(jnp.add, t1[0], hi)
    for k in range(P_TOPK):
        xs[k], carry = _ew(jnp.maximum, xs[k], carry), _ew(jnp.minimum, xs[k], carry)
    return _merge_sublanes(xs)


def _prefix_count(test, vals):
    sel = functools.partial(_ew, jnp.where)
    b3 = test(vals[7])
    b2 = test(sel(b3, vals[11], vals[3]))
    b1 = test(sel(b3, sel(b2, vals[13], vals[9]), sel(b2, vals[5], vals[1])))
    b0 = test(sel(b3, sel(b2, sel(b1, vals[14], vals[12]), sel(b1, vals[10], vals[8])),
                  sel(b2, sel(b1, vals[6], vals[4]), sel(b1, vals[2], vals[0]))))
    n = _ew(lambda p3, p2, p1, p0: (jnp.where(p3, 8.0, 0.0) + jnp.where(p2, 4.0, 0.0))
            + (jnp.where(p1, 2.0, 0.0) + jnp.where(p0, 1.0, 0.0)), b3, b2, b1, b0)
    return _ew(lambda full, m: jnp.where(full, 16.0, m), test(vals[15]), n)


def _peer_query_kernel(x_ref, mod_ref, g_ref, wq_ref, keys_ref, ht_o, n_o, c_o, rank_o, e2_o):
    assert P_TOPK == 16 and P_KEYS == 128
    mod = mod_ref[0]
    h2 = _norm_mod(x_ref[...], g_ref[...], mod[3:4], mod[4:5])
    ht = h2.T.astype(BF16)
    ht_o[...] = ht
    qt = _mm(wq_ref[...], ht).astype(BF16)
    kd = P_KEY_DIM
    tile_shape = (8, ht.shape[1])
    for h0 in range(0, P_HEADS, PEER_HEAD_GROUP):
        heads = range(h0, h0 + PEER_HEAD_GROUP)
        s1 = _Group(_mm(keys_ref[h, 0], qt[(2 * h) * kd:(2 * h + 1) * kd]) for h in heads)
        s2 = _Group(_mm(keys_ref[h, 1], qt[(2 * h + 1) * kd:(2 * h + 2) * kd]) for h in heads)
        t1 = _top_values(s1)
        t2 = _top_values(s2)
        best = _top_pair_sums(t1, t2, tile_shape)
        z = _ew(jnp.ones_like, best[0])
        for k in range(1, P_TOPK):
            z = _ew(lambda acc, v, top: acc + jnp.exp(v - top), z, best[k], best[0])
        inv_z = _ew(lambda v: 0.5 / v, z)
        tau = best[P_TOPK - 1]
        ranks, e2s = [], []
        for k in range(P_KEYS // 8):
            rows = slice(8 * k, 8 * k + 8)
            s1k, s2k = _row_tile(s1, k), _row_tile(s2, k)
            nk = _prefix_count(lambda v: _ew(lambda s, x, t: (s + x) >= t, s1k, v, tau), t2)
            ck = _ew(lambda s, top, iz: jnp.exp(s - top) * iz, s1k, t1[0], inv_z)
            ranks.append(_prefix_count(lambda v: _ew(lambda s, x: x > s, s2k, v), t2))
            e2s.append(_ew(lambda s, top: jnp.exp(s - top), s2k, t2[0]))
            for m, h in enumerate(heads):
                n_o[h, rows] = nk.parts[m]
                c_o[h, rows] = ck.parts[m]
        for m, h in enumerate(heads):
            rank_o[h] = jnp.concatenate([r.parts[m] for r in ranks], axis=0).astype(BF16)
            e2_o[h] = jnp.concatenate([e.parts[m] for e in e2s], axis=0).astype(BF16)


def _peer_query(xt, mod, gain, wq_t, keys, blocks_per_mod, tt):
    t_all, d = xt.shape
    nq = wq_t.shape[0]
    nblk = t_all // tt
    head = pl.BlockSpec((P_HEADS, P_KEYS, tt), lambda i: (0, 0, i))
    hsd = lambda dt: jax.ShapeDtypeStruct((P_HEADS, P_KEYS, t_all), dt)
    wspec = pl.BlockSpec((nq, d), lambda i: (0, 0))
    kspec = pl.BlockSpec((P_HEADS, 2, P_KEYS, P_KEY_DIM), lambda i: (0, 0, 0, 0))
    return pl.pallas_call(
        _peer_query_kernel,
        grid=(nblk,),
        in_specs=[pl.BlockSpec((tt, d), lambda i: (i, 0)),
                  pl.BlockSpec((1, N_MOD, d), lambda i: (i // blocks_per_mod, 0, 0)),
                  pl.BlockSpec((1, d), lambda i: (0, 0)),
                  wspec, kspec],
        out_specs=[pl.BlockSpec((d, tt), lambda i: (0, i)), head, head, head, head],
        out_shape=[jax.ShapeDtypeStruct((d, t_all), BF16), hsd(F32), hsd(F32), hsd(BF16), hsd(BF16)],
        compiler_params=_params("arbitrary"),
        name="peer_query",
    )(xt, mod, gain.reshape(1, d), wq_t.astype(BF16), keys.astype(BF16))


def _twice_gelu(x):
    return x * (1.0 + lax.erf(x * (2.0 ** -0.5)))


def _peer_main_kernel(x_ref, mod_ref, ht_ref, u_ref, vt_ref, n_ref, c_ref, rank_ref, e2_ref,
                      o_ref, acc_ref, act_ref, coef_ref):
    g = pl.program_id(1)
    last = pl.num_programs(1) - 1
    tt = ht_ref.shape[1]
    rows_per_step = u_ref.shape[0] // P_KEYS
    tiles = P_KEYS // BF16_ROWS
    tw = PEER_COEF_TOK

    def consume():
        acc_ref[...] += _mm(vt_ref[...], coef_ref[...] * act_ref[...])

    def prepare():
        zero = jnp.zeros((BF16_ROWS, tw), BF16)
        row0 = pl.multiple_of(g * rows_per_step, rows_per_step)
        n_rows = [n_ref[h, pl.ds(row0, rows_per_step), :] for h in range(P_HEADS)]
        c_rows = [c_ref[h, pl.ds(row0, rows_per_step), :] for h in range(P_HEADS)]
        for ii in range(rows_per_step):
            for t0 in range(0, tt, tw):
                tok = slice(t0, t0 + tw)
                coef = [None] * tiles
                for h in range(P_HEADS):
                    nb = jnp.broadcast_to(n_rows[h][ii:ii + 1, tok], (BF16_ROWS, tw)).astype(BF16)
                    cb = jnp.broadcast_to(c_rows[h][ii:ii + 1, tok], (BF16_ROWS, tw)).astype(BF16)
                    for k in range(tiles):
                        rows = slice(k * BF16_ROWS, (k + 1) * BF16_ROWS)
                        term = jnp.where(rank_ref[h, rows, tok] < nb, e2_ref[h, rows, tok] * cb, zero)
                        coef[k] = term if coef[k] is None else coef[k] + term
                for k in range(tiles):
                    rows = slice(ii * P_KEYS + k * BF16_ROWS, ii * P_KEYS + (k + 1) * BF16_ROWS)
                    coef_ref[rows, tok] = coef[k]
        act_ref[...] = _twice_gelu(_mm(u_ref[...], ht_ref[...])).astype(BF16)

    @pl.when(g == 0)
    def _():
        acc_ref[...] = jnp.zeros_like(acc_ref)
        prepare()

    @pl.when(jnp.logical_and(g > 0, g < last))
    def _():
        consume()
        prepare()

    @pl.when(g == last)
    def _():
        consume()
        o_ref[...] = x_ref[...] + mod_ref[0][5:6] * acc_ref[...].T


def _peer_main(xt, mod, ht, u_all, vt_all, layer, n, c, rank, e2, blocks_per_mod, tt):
    t_all, d = xt.shape
    n_exp = u_all.shape[1]
    eb = PEER_EXP
    nsteps = n_exp // eb
    head = pl.BlockSpec((P_HEADS, P_KEYS, tt), lambda i, s: (0, 0, i))
    return pl.pallas_call(
        _peer_main_kernel,
        grid=(t_all // tt, nsteps + 1),
        in_specs=[pl.BlockSpec((tt, d), lambda i, s: (i, 0)),
                  pl.BlockSpec((1, N_MOD, d), lambda i, s: (i // blocks_per_mod, 0, 0)),
                  pl.BlockSpec((d, tt), lambda i, s: (0, i)),
                  pl.BlockSpec((None, eb, d), lambda i, s: (layer, jnp.minimum(s, nsteps - 1), 0)),
                  pl.BlockSpec((None, d, eb), lambda i, s: (layer, 0, jnp.maximum(s - 1, 0))),
                  head, head, head, head],
        out_specs=pl.BlockSpec((tt, d), lambda i, s: (i, 0)),
        out_shape=jax.ShapeDtypeStruct((t_all, d), F32),
        scratch_shapes=[pltpu.VMEM((d, tt), F32), pltpu.VMEM((eb, tt), BF16), pltpu.VMEM((eb, tt), BF16)],
        compiler_params=_params("arbitrary", "arbitrary"),
        name="peer_main",
    )(xt, mod, ht, u_all, vt_all, n, c, rank, e2)


def _peer_layer(xt, mod, gain, w_q, keys, u_all, vt_all, layer, blocks_per_mod, tt):
    wq_t = w_q.T
    tq = PEER_QUERY_TOK
    ht, n, c, rank, e2 = _peer_query(xt, mod, gain, wq_t, keys, blocks_per_mod * (tt // tq), tq)
    return _peer_main(xt, mod, ht, u_all, vt_all, layer, n, c, rank, e2, blocks_per_mod, tt)


def kernel(x, c, ctx, c_ctx, mod_w, mod_b, norm1_g, norm2_g, ev_w_in, ev_w_out, attn_q_gain, attn_k_gain,
           rw_mu, rw_w0, rw_w2, rw_a0, rw_a2, rw_g2, rw_k_k, rw_k_a, rw_r_k, rw_gn_w, rw_gn_b, ssd_w_in,
           ssd_conv_w, ssd_conv_b, ssd_dt_bias, ssd_a_log, ssd_d, ssd_norm_w, ssd_w_out, peer_w_q,
           peer_keys, peer_u, peer_v):
    bsz, seq, dm = x.shape
    lc = ctx.shape[1]
    depth = mod_w.shape[0]
    assert lc == TOK_TILE and seq % PEER_TOK == 0 and (bsz * lc) % PEER_TOK == 0

    rows = -(-(bsz + 1) // 8) * 8
    cc = jnp.concatenate([c, c_ctx[None], jnp.zeros((rows - bsz - 1, dm), F32)], axis=0)
    mods = _modulation(cc, mod_w, mod_b)
    xc, xl = ctx, x
    u_all = peer_u.astype(BF16)
    vt_all = jnp.swapaxes(peer_v, 1, 2).astype(BF16)

    for i in range(depth):
        last = i == depth - 1
        j = i // 2
        ml = mods[i, :bsz].reshape(bsz, N_MOD, dm)
        mc = mods[i, bsz].reshape(1, N_MOD, dm)
        modx = jnp.stack([jnp.broadcast_to(mc, (bsz, N_MOD, dm)), ml], axis=1)
        if i % 2 == 0:
            za, zr = _in_proj(xc, xl, modx, norm1_g[i], ev_w_in[j].astype(BF16),
                              ((A_IN, F32), (B_IN, F32)))
            q, k, v = _attn_pre(za, attn_q_gain[j], attn_k_gain[j], lc)
            attn = _attention(q, k, v, lc)
            r, vv, kkn, g, bonus, lw, a, kd = _rwkv_pre(
                zr, rw_mu[j], rw_w0[j], rw_w2[j], rw_a0[j], rw_a2[j], rw_g2[j], rw_k_k[j], rw_k_a[j],
                rw_r_k[j], lc)
            wkv = _rwkv_chunks(r, vv, kkn, lw, a, kd, lc)
            xc, xl = _even_out(xc, xl, modx, attn, wkv, bonus, g, rw_gn_w[j], rw_gn_b[j],
                               ev_w_out[j].astype(BF16))
        else:
            n_heads = ssd_a_log.shape[-1]
            inner = n_heads * C_HEAD_DIM
            conv_dim = inner + 2 * C_GROUPS * C_STATE
            w_in = ssd_w_in[j]
            w_dt = jnp.zeros((dm, 2 * LANES), F32)
            w_dt = w_dt.at[:, :n_heads].set(w_in[:, inner + conv_dim:inner + conv_dim + n_heads])
            w_dt = w_dt.at[:, LANES:LANES + n_heads].set(w_in[:, inner + conv_dim + n_heads:])
            dtb = jnp.zeros((1, 2 * LANES), F32)
            dtb = dtb.at[0, :n_heads].set(ssd_dt_bias[j, 0]).at[0, LANES:LANES + n_heads].set(ssd_dt_bias[j, 1])
            w_all = jnp.concatenate([w_in[:, :inner + conv_dim], w_dt], axis=1).astype(BF16)
            zg, xbc, dtraw = _in_proj(xc, xl, modx, norm1_g[i], w_all,
                                      ((inner, BF16), (conv_dim, F32), (2 * LANES, F32)))
            xact, dtact = _ssd_pre(xbc, dtraw, ssd_conv_w[j], ssd_conv_b[j], dtb, lc)
            ys = _ssd_chunks(xact, dtact, ssd_a_log[j], lc)
            assert last, "the SSD layer keeps only the latent stream"
            xl = _odd_out(xl, modx, ys, xact, zg, ssd_d[j], ssd_norm_w[j], ssd_w_out[j].astype(BF16), lc)

        tt = PEER_TOK
        xl = _peer_layer(xl.reshape(bsz * seq, dm), ml, norm2_g[i], peer_w_q[i], peer_keys[i],
                         u_all, vt_all, i, seq // tt, tt).reshape(bsz, seq, dm)
        if not last:
            xc = _peer_layer(xc.reshape(bsz * lc, dm), mc, norm2_g[i], peer_w_q[i], peer_keys[i],
                             u_all, vt_all, i, (bsz * lc) // tt, tt).reshape(bsz, lc, dm)
    return xl
```

```python
import functools
import math

import numpy as np
import jax
import jax.numpy as jnp
from jax import lax
from jax.experimental import pallas as pl
from jax.experimental.pallas import tpu as pltpu

F32 = jnp.float32
BF16 = jnp.bfloat16
HI = lax.Precision.HIGHEST

GRID_W = 64
EPS = 1e-6
N_MOD = 6
A_HEADS, A_KV_HEADS, A_HEAD_DIM = 8, 2, 64
A_Q = A_HEADS * A_HEAD_DIM
A_KV = A_KV_HEADS * A_HEAD_DIM
A_IN = A_Q + 2 * A_KV
ROPE_THETA = 10000.0
B_HEADS, B_HEAD_DIM = 8, 64
B_WIDTH = B_HEADS * B_HEAD_DIM
B_DECAY_LORA, B_AAA_LORA, B_GATE_LORA = 64, 64, 128
B_IN = 3 * B_WIDTH + B_DECAY_LORA + B_AAA_LORA + B_GATE_LORA
B_GN_EPS = 64e-5
C_HEAD_DIM, C_GROUPS, C_STATE, C_CHUNK = 64, 4, 128, 128
P_HEADS, P_KEYS, P_KEY_DIM, P_TOPK = 8, 128, 128, 16

LANES = 128
BF16_ROWS = 16
TOK_TILE = 256
R_CHUNK = 64
RWKV_BATCH = 2
PEER_TOK = 512
PEER_QUERY_TOK = 256
PEER_HEAD_GROUP = 2
PEER_EXP = 1024
PEER_COEF_TOK = 256
VMEM_LIMIT = 56 * 1024 * 1024


def _mm(a, b, prec=None):
    return lax.dot_general(a, b, (((1,), (0,)), ((), ())), precision=prec,
                           preferred_element_type=F32)


def _mm_nt(a, b, prec=None):
    return lax.dot_general(a, b, (((1,), (1,)), ((), ())), precision=prec,
                           preferred_element_type=F32)


def _mm_tn(a, b, prec=None):
    return lax.dot_general(a, b, (((0,), (0,)), ((), ())), precision=prec,
                           preferred_element_type=F32)


def _params(*sem):
    return pltpu.CompilerParams(dimension_semantics=sem, vmem_limit_bytes=VMEM_LIMIT)


def _softplus(x):
    return jnp.maximum(x, 0.0) + jnp.log1p(jnp.exp(-jnp.abs(x)))


def _sigmoid(x):
    return 1.0 / (1.0 + jnp.exp(-x))


def _silu(x):
    return x * _sigmoid(x)


def _mod_kernel(c_ref, w_ref, b_ref, o_ref):
    o_ref[0] = _mm(_silu(c_ref[...]), w_ref[0], HI) + b_ref[0]


def _modulation(cc, mod_w, mod_b):
    depth, d, n6 = mod_w.shape
    rows = cc.shape[0]
    tn = 1024
    return pl.pallas_call(
        _mod_kernel,
        grid=(depth, n6 // tn),
        in_specs=[pl.BlockSpec((rows, d), lambda l, n: (0, 0)),
                  pl.BlockSpec((1, d, tn), lambda l, n: (l, 0, n)),
                  pl.BlockSpec((1, 1, tn), lambda l, n: (l, 0, n))],
        out_specs=pl.BlockSpec((1, rows, tn), lambda l, n: (l, 0, n)),
        out_shape=jax.ShapeDtypeStruct((depth, rows, n6), F32),
        compiler_params=_params("arbitrary", "arbitrary"),
        name="modulation",
    )(cc, mod_w, mod_b.reshape(depth, 1, n6))


def _norm_mod(x, gain, shift, scale):
    ms = jnp.mean(x * x, axis=-1, keepdims=True)
    return x * lax.rsqrt(ms + EPS) * gain * (1.0 + scale) + shift


def _stream_specs(tm, d, lct):
    return [pl.BlockSpec((1, tm, d), lambda b, j: (b, jnp.minimum(j, lct - 1), 0)),
            pl.BlockSpec((1, tm, d), lambda b, j: (b, jnp.maximum(j - lct, 0), 0))]


def _stream_tile(xc_ref, xl_ref, lct):
    return jnp.where(pl.program_id(1) < lct, xc_ref[0], xl_ref[0])


def _in_proj_kernel(xc_ref, xl_ref, mod_ref, g_ref, w_ref, *o_refs, lct):
    mod = mod_ref[0, 0]
    h = _norm_mod(_stream_tile(xc_ref, xl_ref, lct), g_ref[...], mod[0:1], mod[1:2]).astype(BF16)
    off = 0
    for o_ref in o_refs:
        n = o_ref.shape[-1]
        for c0 in range(0, n, 1024):
            c1 = min(n, c0 + 1024)
            o_ref[0, :, c0:c1] = _mm(h, w_ref[:, off + c0:off + c1]).astype(o_ref.dtype)
        off += n


def _in_proj(xc, xl, modx, gain, w, splits):
    bsz, lc, d = xc.shape
    ln = lc + xl.shape[1]
    assert w.shape[1] == sum(n for n, _ in splits)
    tm = TOK_TILE
    lct = lc // tm
    return pl.pallas_call(
        functools.partial(_in_proj_kernel, lct=lct),
        grid=(bsz, ln // tm),
        in_specs=_stream_specs(tm, d, lct) + [
            pl.BlockSpec((1, 1, N_MOD, d), lambda b, j: (b, jnp.where(j < lct, 0, 1), 0, 0)),
            pl.BlockSpec((1, d), lambda b, j: (0, 0)),
            pl.BlockSpec(w.shape, lambda b, j: (0, 0))],
        out_specs=[pl.BlockSpec((1, tm, n), lambda b, j: (b, j, 0)) for n, _ in splits],
        out_shape=[jax.ShapeDtypeStruct((bsz, ln, n), dt) for n, dt in splits],
        compiler_params=_params("arbitrary", "arbitrary"),
        name="in_proj",
    )(xc, xl, modx, gain.reshape(1, d), w)


def _rope_tables(lc, seq):
    t = jnp.arange(seq)
    row = (t // GRID_W).astype(F32)
    col = (t % GRID_W).astype(F32)
    m = A_HEAD_DIM // 4
    inv = ROPE_THETA ** (-jnp.arange(m, dtype=F32) / m)
    ar, ac = row[:, None] * inv, col[:, None] * inv
    cos = jnp.concatenate([jnp.cos(ar), jnp.cos(ar), jnp.cos(ac), jnp.cos(ac)], axis=-1)
    sin = jnp.concatenate([-jnp.sin(ar), jnp.sin(ar), -jnp.sin(ac), jnp.sin(ac)], axis=-1)
    reps = LANES // A_HEAD_DIM
    cos = jnp.concatenate([jnp.ones((lc, A_HEAD_DIM), F32), cos], axis=0)
    sin = jnp.concatenate([jnp.zeros((lc, A_HEAD_DIM), F32), sin], axis=0)
    return jnp.tile(cos, (1, reps)), jnp.tile(sin, (1, reps))


def _head_block_ones(width, scale):
    idx = np.arange(LANES) // width
    return jnp.asarray((idx[:, None] == idx[None, :]).astype(np.float32) * scale)


def _attn_pre_kernel(z_ref, qg_ref, kg_ref, cos_ref, sin_ref, bones_ref, q_ref, k_ref, v_ref):
    tm = z_ref.shape[1]
    cs, sn = cos_ref[...], sin_ref[...]
    bones = bones_ref[...]
    lane = lax.broadcasted_iota(jnp.int32, (tm, LANES), 1)
    first = (lane % 32) < 16
    low = lane < A_HEAD_DIM

    def norm_rope(x, gain):
        ms = _bmm(x * x, bones)
        y = x * lax.rsqrt(ms + EPS) * gain
        partner = jnp.where(first, pltpu.roll(y, LANES - 16, 1), pltpu.roll(y, 16, 1))
        return y * cs + partner * sn

    heads_per_slab = LANES // A_HEAD_DIM
    group = A_HEADS // A_KV_HEADS
    for m in range(A_Q // LANES):
        sl = slice(m * LANES, (m + 1) * LANES)
        y = norm_rope(z_ref[0, :, sl], qg_ref[:, sl]) * (A_HEAD_DIM ** -0.5)
        swapped = pltpu.roll(y, A_HEAD_DIM, 1)
        for hh in range(heads_per_slab):
            h = m * heads_per_slab + hh
            kvh = h // group
            src = y if kvh == hh else swapped
            keep = low if kvh == 0 else jnp.logical_not(low)
            q_ref[0, h] = jnp.where(keep, src, 0.0).astype(BF16)
    k_ref[0] = norm_rope(z_ref[0, :, A_Q:A_Q + A_KV], kg_ref[...]).astype(BF16)
    v_ref[0] = z_ref[0, :, A_Q + A_KV:A_IN].astype(BF16)


def _attn_pre(za, q_gain, k_gain, lc):
    bsz, ln, _ = za.shape
    tm = TOK_TILE
    cos, sin = _rope_tables(lc, ln - lc)
    qg = jnp.tile(q_gain, A_HEADS).reshape(1, A_Q)
    kg = jnp.tile(k_gain, A_KV_HEADS).reshape(1, A_KV)
    bones = _head_block_ones(A_HEAD_DIM, 1.0 / A_HEAD_DIM)
    return pl.pallas_call(
        _attn_pre_kernel,
        grid=(bsz, ln // tm),
        in_specs=[pl.BlockSpec((1, tm, A_IN), lambda b, j: (b, j, 0)),
                  pl.BlockSpec((1, A_Q), lambda b, j: (0, 0)),
                  pl.BlockSpec((1, A_KV), lambda b, j: (0, 0)),
                  pl.BlockSpec((tm, LANES), lambda b, j: (j, 0)),
                  pl.BlockSpec((tm, LANES), lambda b, j: (j, 0)),
                  pl.BlockSpec((LANES, LANES), lambda b, j: (0, 0))],
        out_specs=[pl.BlockSpec((1, A_HEADS, tm, LANES), lambda b, j: (b, 0, j, 0)),
                   pl.BlockSpec((1, tm, A_KV), lambda b, j: (b, j, 0)),
                   pl.BlockSpec((1, tm, A_KV), lambda b, j: (b, j, 0))],
        out_shape=[jax.ShapeDtypeStruct((bsz, A_HEADS, ln, LANES), BF16),
                   jax.ShapeDtypeStruct((bsz, ln, A_KV), BF16),
                   jax.ShapeDtypeStruct((bsz, ln, A_KV), BF16)],
        compiler_params=_params("arbitrary", "arbitrary"),
        name="attn_pre",
    )(za, qg, kg, cos, sin, bones)


def _attn_kernel(q_ref, k_ref, v_ref, o_ref, *, lc):
    tq = q_ref.shape[2]
    j = pl.program_id(1)
    lane = lax.broadcasted_iota(jnp.int32, (tq, LANES), 1)
    low = lane < A_HEAD_DIM
    heads_per_slab = LANES // A_HEAD_DIM
    group = A_HEADS // A_KV_HEADS

    def run(klen):
        k = k_ref[0, :klen, :]
        v = v_ref[0, :klen, :]
        for m in range(A_Q // LANES):
            parts = []
            for hh in range(heads_per_slab):
                h = m * heads_per_slab + hh
                kvh = h // group
                s = _mm_nt(q_ref[0, h], k)
                p = jnp.exp(s - jnp.max(s, axis=-1, keepdims=True))
                l = jnp.sum(p, axis=-1, keepdims=True)
                o = _mm(p.astype(BF16), v) / l
                parts.append(o if kvh == hh else pltpu.roll(o, A_HEAD_DIM, 1))
            o_ref[0, :, m * LANES:(m + 1) * LANES] = jnp.where(low, parts[0], parts[1]).astype(o_ref.dtype)

    @pl.when(j * tq < lc)
    def _():
        run(lc)

    @pl.when(j * tq >= lc)
    def _():
        run(k_ref.shape[1])


def _attention(q, k, v, lc):
    bsz, _, ln, _ = q.shape
    tq = TOK_TILE
    return pl.pallas_call(
        functools.partial(_attn_kernel, lc=lc),
        grid=(bsz, ln // tq),
        in_specs=[pl.BlockSpec((1, A_HEADS, tq, LANES), lambda b, j: (b, 0, j, 0)),
                  pl.BlockSpec((1, ln, A_KV), lambda b, j: (b, 0, 0)),
                  pl.BlockSpec((1, ln, A_KV), lambda b, j: (b, 0, 0))],
        out_specs=pl.BlockSpec((1, tq, A_Q), lambda b, j: (b, j, 0)),
        out_shape=jax.ShapeDtypeStruct((bsz, ln, A_Q), BF16),
        compiler_params=_params("arbitrary", "arbitrary"),
        name="attention",
    )(q, k, v)


def _shifted_rows(z, prev_row, next_row):
    tm = z.shape[0]
    row = lax.broadcasted_iota(jnp.int32, (tm, 1), 0)
    zprev = jnp.where(row == 0, prev_row, pltpu.roll(z, 1, 0))
    znext = jnp.where(row == tm - 1, next_row, pltpu.roll(z, tm - 1, 0))
    return zprev, znext


def _halo_rows(zp_ref, zn_ref, j, lct, nj):
    prev_ok = jnp.logical_and(j != 0, j != lct)
    next_ok = jnp.logical_and(j != lct - 1, j != nj - 1)
    prev_row = jnp.where(prev_ok, zp_ref[0, 7:8, :], 0.0)
    next_row = jnp.where(next_ok, zn_ref[0, 0:1, :], 0.0)
    return prev_row, next_row


def _halo_specs(tm, width, ln):
    nb8 = tm // 8
    last = ln // 8 - 1
    return [pl.BlockSpec((1, tm, width), lambda b, j: (b, j, 0)),
            pl.BlockSpec((1, 8, width), lambda b, j: (b, jnp.maximum(j * nb8 - 1, 0), 0)),
            pl.BlockSpec((1, 8, width), lambda b, j: (b, jnp.minimum((j + 1) * nb8, last), 0))]


def _per_head_sum(x, bones):
    return jnp.concatenate(
        [_bmm(x[:, m * LANES:(m + 1) * LANES], bones) for m in range(x.shape[1] // LANES)], axis=1)


def _rwkv_pre_kernel(z_ref, zp_ref, zn_ref, mu_ref, w0_ref, w2_ref, a0_ref, a2_ref, g2_ref,
                     kk_ref, ka_ref, rk_ref, bones_ref,
                     r_o, v_o, kkn_o, g_o, bonus_o, lw_o, a_o, kd_o, *, lct):
    j = pl.program_id(1)
    z = z_ref[0]
    prev_row, next_row = _halo_rows(zp_ref, zn_ref, j, lct, pl.num_programs(1))
    zprev, znext = _shifted_rows(z, prev_row, next_row)
    zm = z + mu_ref[...] * (0.5 * (zprev + znext) - z)
    w = B_WIDTH
    r, k, v = zm[:, 0:w], zm[:, w:2 * w], zm[:, 2 * w:3 * w]
    wa = zm[:, 3 * w:3 * w + B_DECAY_LORA + B_AAA_LORA]
    gl = zm[:, 3 * w + B_DECAY_LORA + B_AAA_LORA:]
    bones = bones_ref[...]
    kk = k * kk_ref[...]
    kkn = kk / jnp.maximum(jnp.sqrt(_per_head_sum(kk * kk, bones)), 1e-12)
    r_o[0] = r.astype(r_o.dtype)
    v_o[0] = v.astype(v_o.dtype)
    kkn_o[0] = kkn.astype(kkn_o.dtype)
    g_o[0] = _bmm(_sigmoid(gl), g2_ref[...]).astype(g_o.dtype)
    twa = jnp.tanh(wa)
    bonus = jnp.zeros_like(r)
    for d in range(2):
        wraw = w0_ref[d] + _bmm(twa, w2_ref[d])
        lw_o[d, 0] = -jnp.exp(-_softplus(-wraw) - 0.5)
        a = _sigmoid(a0_ref[d] + _bmm(wa, a2_ref[d]))
        kd = k * (1.0 + (a - 1.0) * ka_ref[...])
        a_o[d, 0] = a.astype(a_o.dtype)
        kd_o[d, 0] = kd.astype(kd_o.dtype)
        bonus = bonus + _per_head_sum(r * kd * rk_ref[...], bones) * v
    bonus_o[0] = bonus.astype(bonus_o.dtype)


def _rwkv_pre(zr, mu, w0, w2, a0, a2, g2, k_k, k_a, r_k, lc):
    bsz, ln, width = zr.shape
    tm = TOK_TILE
    w = B_WIDTH
    zeros = jnp.zeros((2, B_DECAY_LORA, w), F32)
    w2p = jnp.concatenate([w2, zeros], axis=1)
    a2p = jnp.concatenate([zeros, a2], axis=1)
    bones = _head_block_ones(B_HEAD_DIM, 1.0)
    full = lambda *shape: pl.BlockSpec(shape, lambda b, j: (0,) * len(shape))
    tok = pl.BlockSpec((1, tm, w), lambda b, j: (b, j, 0))
    tok2 = pl.BlockSpec((2, 1, tm, w), lambda b, j: (0, b, j, 0))
    sd = jax.ShapeDtypeStruct((bsz, ln, w), BF16)
    sd2 = jax.ShapeDtypeStruct((2, bsz, ln, w), BF16)
    lw_sd = jax.ShapeDtypeStruct((2, bsz, ln, w), F32)
    return pl.pallas_call(
        functools.partial(_rwkv_pre_kernel, lct=lc // tm),
        grid=(bsz, ln // tm),
        in_specs=_halo_specs(tm, width, ln) + [
            full(1, width), full(2, 1, w), full(2, 2 * B_DECAY_LORA, w), full(2, 1, w),
            full(2, 2 * B_AAA_LORA, w), full(B_GATE_LORA, w), full(1, w), full(1, w), full(1, w),
            full(LANES, LANES)],
        out_specs=[tok, tok, tok, tok, tok, tok2, tok2, tok2],
        out_shape=[sd, sd, sd, sd, sd, lw_sd, sd2, sd2],
        compiler_params=_params("arbitrary", "arbitrary"),
        name="rwkv_pre",
    )(zr, zr, zr, mu.reshape(1, width), w0.reshape(2, 1, w), w2p, a0.reshape(2, 1, w), a2p, g2,
      k_k.reshape(1, w), k_a.reshape(1, w), r_k.reshape(1, w), bones)


def _scan_chunk(i, d, nctx, ntot):
    rev = jnp.where(i < nctx, nctx - 1 - i, ntot + nctx - 1 - i)
    return jnp.where(d == 0, i, rev)


def _order_masks(n, pair):
    t = np.arange(n * pair)
    same = (t[:, None] // n) == (t[None, :] // n)
    tt, ss = t[:, None] % n, t[None, :] % n
    incl = np.stack([same & (ss <= tt), same & (ss >= tt)]).astype(np.float32)
    strict = np.stack([same & (ss < tt), same & (ss > tt)]).astype(np.float32)
    return jnp.asarray(incl), jnp.asarray(strict)


def _pair_stack(x, low):
    return jnp.concatenate([jnp.where(low, x, 0.0), jnp.where(low, 0.0, x)], axis=0)


def _bmm(a, b):
    return _mm(a.astype(BF16), b.astype(BF16))


def _rwkv_chunk_kernel(rf_ref, vf_ref, kf_ref, lwf_ref, af_ref, kdf_ref,
                       rb_ref, vb_ref, kb_ref, lwb_ref, ab_ref, kdb_ref,
                       tri_ref, incl_ref, strict_ref, of_ref, ob_ref, state_ref):
    c = R_CHUNK
    n2 = 2 * c

    @pl.when(pl.program_id(1) == 0)
    def _():
        state_ref[...] = jnp.zeros_like(state_ref)

    low = lax.broadcasted_iota(jnp.int32, (c, LANES), 1) < B_HEAD_DIM
    eye = (lax.broadcasted_iota(jnp.int32, (LANES, LANES), 0)
           == lax.broadcasted_iota(jnp.int32, (LANES, LANES), 1)).astype(F32)
    stack = lambda x: _pair_stack(x, low)
    refs = ((rf_ref, vf_ref, kf_ref, lwf_ref, af_ref, kdf_ref, of_ref),
            (rb_ref, vb_ref, kb_ref, lwb_ref, ab_ref, kdb_ref, ob_ref))
    dirs, slabs, outs, bats = [], [], [], []
    kts, bts, kds, rts, vs, pcs = [], [], [], [], [], []
    for bb, d in [(bb, d) for bb in range(rf_ref.shape[0]) for d in range(2)]:
        r_ref, v_ref, kkn_ref, lw_ref, a_ref, kd_ref, o_ref = refs[d]
        lw = lw_ref[0, bb]
        cum = _mm(tri_ref[d], lw, HI)
        p_in = jnp.exp(cum)
        p_inv = jnp.exp(-cum)
        p_ex = jnp.exp(cum - lw)
        p_all = jnp.exp(jnp.sum(lw, axis=0, keepdims=True))
        kkn = kkn_ref[bb].astype(F32)
        kt = kkn * p_ex
        bt = kkn * a_ref[0, bb].astype(F32) * p_inv
        kd = kd_ref[0, bb].astype(F32) * p_inv
        rt = r_ref[bb].astype(F32) * p_in
        v = v_ref[bb].astype(F32)
        for m in range(B_WIDTH // LANES):
            sl = slice(m * LANES, (m + 1) * LANES)
            dirs.append(d)
            slabs.append(m)
            bats.append(bb)
            outs.append(o_ref)
            kts.append(stack(kt[:, sl]).astype(BF16))
            bts.append(stack(bt[:, sl]).astype(BF16))
            kds.append(stack(kd[:, sl]).astype(BF16))
            rts.append(stack(rt[:, sl]))
            vs.append(stack(v[:, sl]).astype(BF16))
            pcs.append(p_all[:, sl])
    n = len(dirs)
    incl = [incl_ref[d] > 0.5 for d in range(2)]
    strict = [strict_ref[d] > 0.5 for d in range(2)]
    gram = [_mm_nt(jnp.concatenate([kts[x], rts[x].astype(BF16)], axis=0),
                   jnp.concatenate([bts[x], kds[x]], axis=0)) for x in range(n)]
    a_b = [jnp.where(strict[dirs[x]], gram[x][:n2, :n2], 0.0) for x in range(n)]
    a_k = [jnp.where(strict[dirs[x]], gram[x][:n2, n2:], 0.0) for x in range(n)]
    m_b = [jnp.where(incl[dirs[x]], gram[x][n2:, :n2], 0.0) for x in range(n)]
    m_k = [jnp.where(incl[dirs[x]], gram[x][n2:, n2:], 0.0) for x in range(n)]
    pw = [-a for a in a_b]
    tinv = [eye + p for p in pw]
    for _ in range(int(math.log2(c)) - 1):
        pw = [_bmm(p, p) for p in pw]
        tinv = [t + _bmm(t, p) for t, p in zip(tinv, pw)]
    mv = [_bmm(jnp.concatenate([a_k[x], m_k[x]], axis=0), vs[x]) for x in range(n)]
    wy = [_bmm(tinv[x], jnp.concatenate([kts[x], mv[x][:n2].astype(BF16)], axis=1)) for x in range(n)]
    mbwy = [_bmm(m_b[x], wy[x]) for x in range(n)]
    wyb = [_mm_tn(wy[x].astype(BF16), bts[x]) for x in range(n)]
    vtk = [_mm_tn(vs[x], kds[x]) for x in range(n)]
    s0 = [state_ref[bats[x], dirs[x], slabs[x]] for x in range(n)]
    o_st = [_mm_nt((rts[x] - mbwy[x][:, :LANES]).astype(BF16), s0[x].astype(BF16))
            + (mv[x][n2:] - mbwy[x][:, LANES:]) for x in range(n)]
    for x in range(n):
        g = (eye - wyb[x][:LANES]) * pcs[x]
        h = (vtk[x] - wyb[x][LANES:]) * pcs[x]
        outs[x][bats[x], :, slabs[x] * LANES:(slabs[x] + 1) * LANES] = (o_st[x][:c] + o_st[x][c:]).astype(outs[x].dtype)
        state_ref[bats[x], dirs[x], slabs[x]] = _bmm(s0[x], g) + h


def _rwkv_chunks(r, v, kkn, lw, a, kd, lc):
    bsz, ln, w = r.shape
    c = R_CHUNK
    nctx, ntot = lc // c, ln // c
    nb = RWKV_BATCH if bsz % RWKV_BATCH == 0 else 1
    tri, _ = _order_masks(c, 1)
    incl, strict = _order_masks(c, 2)
    specs = []
    for d in range(2):
        tok = pl.BlockSpec((nb, c, w), lambda b, i, d=d: (b, _scan_chunk(i, d, nctx, ntot), 0))
        tok2 = pl.BlockSpec((1, nb, c, w), lambda b, i, d=d: (d, b, _scan_chunk(i, d, nctx, ntot), 0))
        specs.append((tok, tok2))
    full = lambda *shape: pl.BlockSpec(shape, lambda b, i: (0,) * len(shape))
    sd = jax.ShapeDtypeStruct((bsz, ln, w), BF16)
    return pl.pallas_call(
        _rwkv_chunk_kernel,
        grid=(bsz // nb, ntot),
        in_specs=[specs[0][0]] * 3 + [specs[0][1]] * 3 + [specs[1][0]] * 3 + [specs[1][1]] * 3
        + [full(2, c, c), full(2, 2 * c, 2 * c), full(2, 2 * c, 2 * c)],
        out_specs=[specs[0][0], specs[1][0]],
        out_shape=[sd, sd],
        scratch_shapes=[pltpu.VMEM((nb, 2, w // LANES, LANES, LANES), F32)],
        compiler_params=_params("arbitrary", "arbitrary"),
        name="rwkv_chunks",
    )(r, v, kkn, lw, a, kd, r, v, kkn, lw, a, kd, tri, incl, strict)


def _even_out_kernel(xc_ref, xl_ref, mod_ref, attn_ref, wkvf_ref, wkvb_ref, bonus_ref, g_ref, gw_ref,
                     gb_ref, bones_ref, w_ref, oc_ref, ol_ref, *, lct):
    bones = bones_ref[...]
    y = bonus_ref[0].astype(F32)
    for wkv_ref in (wkvf_ref, wkvb_ref):
        o = wkv_ref[0].astype(F32)
        dlt = o - _per_head_sum(o, bones)
        var = _per_head_sum(dlt * dlt, bones)
        y = y + dlt * lax.rsqrt(var + B_GN_EPS) * gw_ref[...] + gb_ref[...]
    y = y * g_ref[0].astype(F32)
    mix = _mm(attn_ref[0].astype(BF16), w_ref[:A_Q, :]) + _mm(y.astype(BF16), w_ref[A_Q:, :])
    res = _stream_tile(xc_ref, xl_ref, lct) + mod_ref[0, 0][2:3] * mix
    is_ctx = pl.program_id(1) < lct

    @pl.when(is_ctx)
    def _():
        oc_ref[0] = res

    @pl.when(jnp.logical_not(is_ctx))
    def _():
        ol_ref[0] = res


def _even_out(xc, xl, modx, attn, wkv, bonus, g, gn_w, gn_b, w_out):
    bsz, lc, d = xc.shape
    seq = xl.shape[1]
    tm = TOK_TILE
    lct = lc // tm
    w = B_WIDTH
    bones = _head_block_ones(B_HEAD_DIM, 1.0 / B_HEAD_DIM)
    tokw = lambda n: pl.BlockSpec((1, tm, n), lambda b, j: (b, j, 0))
    full = lambda *shape: pl.BlockSpec(shape, lambda b, j: (0,) * len(shape))
    streams = _stream_specs(tm, d, lct)
    return pl.pallas_call(
        functools.partial(_even_out_kernel, lct=lct),
        grid=(bsz, (lc + seq) // tm),
        in_specs=streams + [
            pl.BlockSpec((1, 1, N_MOD, d), lambda b, j: (b, jnp.where(j < lct, 0, 1), 0, 0)),
            tokw(A_Q), tokw(w), tokw(w),
            tokw(w), tokw(w), full(1, w), full(1, w), full(LANES, LANES),
            full(A_Q + w, d)],
        out_specs=streams,
        out_shape=[jax.ShapeDtypeStruct((bsz, lc, d), F32), jax.ShapeDtypeStruct((bsz, seq, d), F32)],
        compiler_params=_params("arbitrary", "arbitrary"),
        name="even_out",
    )(xc, xl, modx, attn, wkv[0], wkv[1], bonus, g, gn_w.reshape(1, w), gn_b.reshape(1, w), bones, w_out)


def _ssd_pre_kernel(z_ref, zp_ref, zn_ref, dt_ref, cw_ref, cb_ref, dtb_ref, x_o, dt_o, *, lct):
    j = pl.program_id(1)
    z = z_ref[0]
    prev_row, next_row = _halo_rows(zp_ref, zn_ref, j, lct, pl.num_programs(1))
    zprev, znext = _shifted_rows(z, prev_row, next_row)
    y = zprev * cw_ref[0:1, :] + z * cw_ref[1:2, :] + znext * cw_ref[2:3, :] + cb_ref[...]
    x_o[0] = _silu(y).astype(x_o.dtype)
    dt_o[0] = _softplus(dt_ref[0] + dtb_ref[...])


def _ssd_pre(xbc, dtraw, conv_w, conv_b, dtb, lc):
    bsz, ln, width = xbc.shape
    tm = TOK_TILE
    nd = dtraw.shape[-1]
    full = lambda *shape: pl.BlockSpec(shape, lambda b, j: (0,) * len(shape))
    return pl.pallas_call(
        functools.partial(_ssd_pre_kernel, lct=lc // tm),
        grid=(bsz, ln // tm),
        in_specs=_halo_specs(tm, width, ln) + [
            pl.BlockSpec((1, tm, nd), lambda b, j: (b, j, 0)),
            full(conv_w.shape[0], width), full(1, width), full(1, nd)],
        out_specs=[pl.BlockSpec((1, tm, width), lambda b, j: (b, j, 0)),
                   pl.BlockSpec((1, tm, nd), lambda b, j: (b, j, 0))],
        out_shape=[jax.ShapeDtypeStruct((bsz, ln, width), BF16),
                   jax.ShapeDtypeStruct((bsz, ln, nd), F32)],
        compiler_params=_params("arbitrary", "arbitrary"),
        name="ssd_pre",
    )(xbc, xbc, xbc, dtraw, conv_w, conv_b.reshape(1, width), dtb)


def _ssd_chunk_kernel(x_ref, b_ref, c_ref, dt_ref, alog_ref, incl_ref, expand_ref, o_ref, state_ref,
                      *, n_heads):
    q = C_CHUNK
    i = pl.program_id(2)

    @pl.when(i == 0)
    def _():
        state_ref[...] = jnp.zeros_like(state_ref)

    lane = lax.broadcasted_iota(jnp.int32, (1, LANES), 1)
    a_neg = jnp.where(lane < n_heads, -jnp.exp(alog_ref[0]), 0.0)
    incl_f = incl_ref[0]
    expand = expand_ref[...]
    incl = incl_f > 0.5
    low = lax.broadcasted_iota(jnp.int32, (q, LANES), 1) < C_HEAD_DIM
    heads_per_group = n_heads // C_GROUPS
    pairs_per_group = heads_per_group // 2
    gw = heads_per_group * C_HEAD_DIM
    for bb in range(x_ref.shape[0]):
        dt = dt_ref[bb]
        dta = dt * a_neg
        cum = _mm(incl_f, dta, HI)
        cum_t = cum.T
        total = jnp.sum(dta, axis=0, keepdims=True)
        dt_x = _bmm(dt, expand)
        ec_x = _bmm(jnp.exp(cum), expand)
        de_x = _bmm(jnp.exp(total - cum), expand)
        cd_x = _bmm(jnp.broadcast_to(jnp.exp(total), (BF16_ROWS, LANES)), expand)[0:1]
        for g in range(C_GROUPS):
            gsl = slice(g * gw, (g + 1) * gw)
            bg = b_ref[bb, :, g * C_STATE:(g + 1) * C_STATE].astype(BF16)
            cg = c_ref[bb, :, g * C_STATE:(g + 1) * C_STATE].astype(BF16)
            cb = _mm_nt(cg, bg)
            xd = x_ref[bb, :, gsl].astype(F32) * dt_x[:, gsl]
            s_prev = state_ref[bb, g]
            y_off = _mm(cg, s_prev.astype(BF16)) * ec_x[:, gsl]
            state_ref[bb, g] = s_prev * cd_x[:, gsl] + _mm_tn(bg, (xd * de_x[:, gsl]).astype(BF16))
            for pp in range(pairs_per_group):
                pr = g * pairs_per_group + pp
                sl = slice(pr * LANES, (pr + 1) * LANES)
                ms = []
                for hh in range(2):
                    h = 2 * pr + hh
                    seg = cum[:, h:h + 1] - cum_t[h:h + 1, :]
                    ms.append((cb * jnp.exp(jnp.where(incl, seg, -1e30))).astype(BF16))
                xd_pair = _pair_stack(xd[:, pp * LANES:(pp + 1) * LANES], low).astype(BF16)
                y_diag = _mm(jnp.concatenate(ms, axis=1), xd_pair)
                o_ref[0, bb, :, sl] = (y_diag + y_off[:, pp * LANES:(pp + 1) * LANES]).astype(o_ref.dtype)


def _ssd_chunks(xact, dtact, a_log, lc):
    bsz, ln, _ = xact.shape
    q = C_CHUNK
    n_heads = a_log.shape[1]
    inner = n_heads * C_HEAD_DIM
    gn = C_GROUPS * C_STATE
    nctx, ntot = lc // q, ln // q
    nb = RWKV_BATCH if bsz % RWKV_BATCH == 0 else 1
    incl, _ = _order_masks(q, 1)
    expand = np.zeros((LANES, inner), np.float32)
    expand[np.arange(inner) // C_HEAD_DIM, np.arange(inner)] = 1.0
    alog = jnp.zeros((2, 1, LANES), F32).at[:, 0, :n_heads].set(a_log)
    cmap = lambda off: (lambda b, d, i: (b, _scan_chunk(i, d, nctx, ntot), off))
    return pl.pallas_call(
        functools.partial(_ssd_chunk_kernel, n_heads=n_heads),
        grid=(bsz // nb, 2, ntot),
        in_specs=[pl.BlockSpec((nb, q, inner), cmap(0)),
                  pl.BlockSpec((nb, q, gn), cmap(inner // gn)),
                  pl.BlockSpec((nb, q, gn), cmap(inner // gn + 1)),
                  pl.BlockSpec((nb, q, LANES), lambda b, d, i: (b, _scan_chunk(i, d, nctx, ntot), d)),
                  pl.BlockSpec((1, 1, LANES), lambda b, d, i: (d, 0, 0)),
                  pl.BlockSpec((1, q, q), lambda b, d, i: (d, 0, 0)),
                  pl.BlockSpec((LANES, inner), lambda b, d, i: (0, 0))],
        out_specs=pl.BlockSpec((1, nb, q, inner), lambda b, d, i: (d, b, _scan_chunk(i, d, nctx, ntot), 0)),
        out_shape=jax.ShapeDtypeStruct((2, bsz, ln, inner), BF16),
        scratch_shapes=[pltpu.VMEM((nb, C_GROUPS, C_STATE, inner // C_GROUPS), F32)],
        compiler_params=_params("arbitrary", "arbitrary", "arbitrary"),
        name="ssd_chunks",
    )(xact, xact, xact, dtact, alog, incl, jnp.asarray(expand, dtype=BF16))


def _odd_out_kernel(x_ref, mod_ref, ys_ref, xs_ref, zg_ref, dsk_ref, nw_ref, w_ref, o_ref):
    y = (ys_ref[0, 0].astype(F32) + ys_ref[1, 0].astype(F32)) + dsk_ref[...] * xs_ref[0].astype(F32)
    yg = y * _silu(zg_ref[0].astype(F32))
    inner = yg.shape[1]
    gw = inner // C_GROUPS
    parts = []
    for g in range(C_GROUPS):
        t = yg[:, g * gw:(g + 1) * gw]
        ms = jnp.mean(t * t, axis=-1, keepdims=True)
        parts.append((t * lax.rsqrt(ms + EPS) * nw_ref[:, g * gw:(g + 1) * gw]).astype(BF16))
    yn = jnp.concatenate(parts, axis=1)
    o_ref[0] = x_ref[0] + mod_ref[0, 0][2:3] * _mm(yn, w_ref[...])


def _odd_out(x, modx, ys, xact, zg, d_skip, norm_w, w_out, lc):
    bsz, seq, d = x.shape
    ln = lc + seq
    tm = TOK_TILE
    lct = lc // tm
    inner = zg.shape[-1]
    dsk = jnp.repeat(d_skip, C_HEAD_DIM).reshape(1, inner)
    tokw = lambda n: pl.BlockSpec((1, tm, n), lambda b, j: (b, j + lct, 0))
    full = lambda *shape: pl.BlockSpec(shape, lambda b, j: (0,) * len(shape))
    return pl.pallas_call(
        _odd_out_kernel,
        grid=(bsz, (ln - lc) // tm),
        in_specs=[pl.BlockSpec((1, tm, d), lambda b, j: (b, j, 0)),
                  pl.BlockSpec((1, 1, N_MOD, d), lambda b, j: (b, 1, 0, 0)),
                  pl.BlockSpec((2, 1, tm, inner), lambda b, j: (0, b, j + lct, 0)),
                  tokw(inner), tokw(inner), full(1, inner), full(1, inner), full(inner, d)],
        out_specs=pl.BlockSpec((1, tm, d), lambda b, j: (b, j, 0)),
        out_shape=jax.ShapeDtypeStruct((bsz, ln - lc, d), F32),
        compiler_params=_params("arbitrary", "arbitrary"),
        name="odd_out",
    )(x, modx, ys, xact, zg, dsk, norm_w.reshape(1, inner), w_out)


class _Group:
    def __init__(self, parts):
        self.parts = tuple(parts)


def _ew(f, *args):
    groups = [a for a in args if isinstance(a, _Group)]
    if not groups:
        return f(*args)
    return _Group(f(*[a.parts[m] if isinstance(a, _Group) else a for a in args])
                  for m in range(len(groups[0].parts)))


def _exchange(xs, i, l, descending=True):
    hi, lo = _ew(jnp.maximum, xs[i], xs[l]), _ew(jnp.minimum, xs[i], xs[l])
    xs[i], xs[l] = (hi, lo) if descending else (lo, hi)


def _bitonic_merge(xs):
    xs = list(xs)
    j = len(xs) // 2
    while j >= 1:
        for i in range(len(xs)):
            if i ^ j > i:
                _exchange(xs, i, i ^ j)
        j //= 2
    return xs


def _bitonic_sort(xs):
    xs = list(xs)
    k = 2
    while k <= len(xs):
        j = k // 2
        while j >= 1:
            for i in range(len(xs)):
                if i ^ j > i:
                    _exchange(xs, i, i ^ j, descending=(i & k) == 0)
            j //= 2
        k *= 2
    return xs


def _merge_sublanes(xs):
    n = len(xs)
    for shift in (4, 2, 1):
        other = [_ew(lambda v: pltpu.roll(v, shift, 0), x) for x in xs]
        xs = _bitonic_merge([_ew(jnp.maximum, xs[k], other[n - 1 - k]) for k in range(n)])
    return xs


def _row_tile(s, k):
    return _ew(lambda v: v[8 * k:8 * k + 8], s)


def _top_values(s):
    return _merge_sublanes(_bitonic_sort([_row_tile(s, k) for k in range(P_KEYS // 8)]))


def _top_pair_sums(t1, t2, shape):
    sub = lax.broadcasted_iota(jnp.int32, shape, 0)
    lo, hi = t2[0], t2[8]
    for b in range(1, 8):
        lo = _ew(lambda x, y: jnp.where(sub == b, x, y), t2[b], lo)
        hi = _ew(lambda x, y: jnp.where(sub == b, x, y), t2[8 + b], hi)
    xs = [_ew(jnp.add, t1[a], lo) for a in range(P_TOPK)]
    carry = _ew(jnp.add, t1[0], hi)
    for k in range(P_TOPK):
        xs[k], carry = _ew(jnp.maximum, xs[k], carry), _ew(jnp.minimum, xs[k], carry)
    return _merge_sublanes(xs)


def _prefix_count(test, vals):
    sel = functools.partial(_ew, jnp.where)
    b3 = test(vals[7])
    b2 = test(sel(b3, vals[11], vals[3]))
    b1 = test(sel(b3, sel(b2, vals[13], vals[9]), sel(b2, vals[5], vals[1])))
    b0 = test(sel(b3, sel(b2, sel(b1, vals[14], vals[12]), sel(b1, vals[10], vals[8])),
                  sel(b2, sel(b1, vals[6], vals[4]), sel(b1, vals[2], vals[0]))))
    n = _ew(lambda p3, p2, p1, p0: (jnp.where(p3, 8.0, 0.0) + jnp.where(p2, 4.0, 0.0))
            + (jnp.where(p1, 2.0, 0.0) + jnp.where(p0, 1.0, 0.0)), b3, b2, b1, b0)
    return _ew(lambda full, m: jnp.where(full, 16.0, m), test(vals[15]), n)


def _peer_query_kernel(x_ref, mod_ref, g_ref, wq_ref, keys_ref, ht_o, n_o, c_o, rank_o, e2_o):
    assert P_TOPK == 16 and P_KEYS == 128
    mod = mod_ref[0]
    h2 = _norm_mod(x_ref[...], g_ref[...], mod[3:4], mod[4:5])
    ht = h2.T.astype(BF16)
    ht_o[...] = ht
    qt = _mm(wq_ref[...], ht).astype(BF16)
    kd = P_KEY_DIM
    tile_shape = (8, ht.shape[1])
    for h0 in range(0, P_HEADS, PEER_HEAD_GROUP):
        heads = range(h0, h0 + PEER_HEAD_GROUP)
        s1 = _Group(_mm(keys_ref[h, 0], qt[(2 * h) * kd:(2 * h + 1) * kd]) for h in heads)
        s2 = _Group(_mm(keys_ref[h, 1], qt[(2 * h + 1) * kd:(2 * h + 2) * kd]) for h in heads)
        t1 = _top_values(s1)
        t2 = _top_values(s2)
        best = _top_pair_sums(t1, t2, tile_shape)
        z = _ew(jnp.ones_like, best[0])
        for k in range(1, P_TOPK):
            z = _ew(lambda acc, v, top: acc + jnp.exp(v - top), z, best[k], best[0])
        inv_z = _ew(lambda v: 0.5 / v, z)
        tau = best[P_TOPK - 1]
        ranks, e2s = [], []
        for k in range(P_KEYS // 8):
            rows = slice(8 * k, 8 * k + 8)
            s1k, s2k = _row_tile(s1, k), _row_tile(s2, k)
            nk = _prefix_count(lambda v: _ew(lambda s, x, t: (s + x) >= t, s1k, v, tau), t2)
            ck = _ew(lambda s, top, iz: jnp.exp(s - top) * iz, s1k, t1[0], inv_z)
            ranks.append(_prefix_count(lambda v: _ew(lambda s, x: x > s, s2k, v), t2))
            e2s.append(_ew(lambda s, top: jnp.exp(s - top), s2k, t2[0]))
            for m, h in enumerate(heads):
                n_o[h, rows] = nk.parts[m]
                c_o[h, rows] = ck.parts[m]
        for m, h in enumerate(heads):
            rank_o[h] = jnp.concatenate([r.parts[m] for r in ranks], axis=0).astype(BF16)
            e2_o[h] = jnp.concatenate([e.parts[m] for e in e2s], axis=0).astype(BF16)


def _peer_query(xt, mod, gain, wq_t, keys, blocks_per_mod, tt):
    t_all, d = xt.shape
    nq = wq_t.shape[0]
    nblk = t_all // tt
    head = pl.BlockSpec((P_HEADS, P_KEYS, tt), lambda i: (0, 0, i))
    hsd = lambda dt: jax.ShapeDtypeStruct((P_HEADS, P_KEYS, t_all), dt)
    wspec = pl.BlockSpec((nq, d), lambda i: (0, 0))
    kspec = pl.BlockSpec((P_HEADS, 2, P_KEYS, P_KEY_DIM), lambda i: (0, 0, 0, 0))
    return pl.pallas_call(
        _peer_query_kernel,
        grid=(nblk,),
        in_specs=[pl.BlockSpec((tt, d), lambda i: (i, 0)),
                  pl.BlockSpec((1, N_MOD, d), lambda i: (i // blocks_per_mod, 0, 0)),
                  pl.BlockSpec((1, d), lambda i: (0, 0)),
                  wspec, kspec],
        out_specs=[pl.BlockSpec((d, tt), lambda i: (0, i)), head, head, head, head],
        out_shape=[jax.ShapeDtypeStruct((d, t_all), BF16), hsd(F32), hsd(F32), hsd(BF16), hsd(BF16)],
        compiler_params=_params("arbitrary"),
        name="peer_query",
    )(xt, mod, gain.reshape(1, d), wq_t.astype(BF16), keys.astype(BF16))


def _twice_gelu(x):
    return x * (1.0 + lax.erf(x * (2.0 ** -0.5)))


def _peer_main_kernel(x_ref, mod_ref, ht_ref, u_ref, vt_ref, n_ref, c_ref, rank_ref, e2_ref,
                      o_ref, acc_ref, act_ref, coef_ref):
    g = pl.program_id(1)
    last = pl.num_programs(1) - 1
    tt = ht_ref.shape[1]
    rows_per_step = u_ref.shape[0] // P_KEYS
    tiles = P_KEYS // BF16_ROWS
    tw = PEER_COEF_TOK

    def consume():
        acc_ref[...] += _mm(vt_ref[...], coef_ref[...] * act_ref[...])

    def prepare():
        zero = jnp.zeros((BF16_ROWS, tw), BF16)
        row0 = pl.multiple_of(g * rows_per_step, rows_per_step)
        n_rows = [n_ref[h, pl.ds(row0, rows_per_step), :] for h in range(P_HEADS)]
        c_rows = [c_ref[h, pl.ds(row0, rows_per_step), :] for h in range(P_HEADS)]
        for ii in range(rows_per_step):
            for t0 in range(0, tt, tw):
                tok = slice(t0, t0 + tw)
                coef = [None] * tiles
                for h in range(P_HEADS):
                    nb = jnp.broadcast_to(n_rows[h][ii:ii + 1, tok], (BF16_ROWS, tw)).astype(BF16)
                    cb = jnp.broadcast_to(c_rows[h][ii:ii + 1, tok], (BF16_ROWS, tw)).astype(BF16)
                    for k in range(tiles):
                        rows = slice(k * BF16_ROWS, (k + 1) * BF16_ROWS)
                        term = jnp.where(rank_ref[h, rows, tok] < nb, e2_ref[h, rows, tok] * cb, zero)
                        coef[k] = term if coef[k] is None else coef[k] + term
                for k in range(tiles):
                    rows = slice(ii * P_KEYS + k * BF16_ROWS, ii * P_KEYS + (k + 1) * BF16_ROWS)
                    coef_ref[rows, tok] = coef[k]
        act_ref[...] = _twice_gelu(_mm(u_ref[...], ht_ref[...])).astype(BF16)

    @pl.when(g == 0)
    def _():
        acc_ref[...] = jnp.zeros_like(acc_ref)
        prepare()

    @pl.when(jnp.logical_and(g > 0, g < last))
    def _():
        consume()
        prepare()

    @pl.when(g == last)
    def _():
        consume()
        o_ref[...] = x_ref[...] + mod_ref[0][5:6] * acc_ref[...].T


def _peer_main(xt, mod, ht, u_all, vt_all, layer, n, c, rank, e2, blocks_per_mod, tt):
    t_all, d = xt.shape
    n_exp = u_all.shape[1]
    eb = PEER_EXP
    nsteps = n_exp // eb
    head = pl.BlockSpec((P_HEADS, P_KEYS, tt), lambda i, s: (0, 0, i))
    return pl.pallas_call(
        _peer_main_kernel,
        grid=(t_all // tt, nsteps + 1),
        in_specs=[pl.BlockSpec((tt, d), lambda i, s: (i, 0)),
                  pl.BlockSpec((1, N_MOD, d), lambda i, s: (i // blocks_per_mod, 0, 0)),
                  pl.BlockSpec((d, tt), lambda i, s: (0, i)),
                  pl.BlockSpec((None, eb, d), lambda i, s: (layer, jnp.minimum(s, nsteps - 1), 0)),
                  pl.BlockSpec((None, d, eb), lambda i, s: (layer, 0, jnp.maximum(s - 1, 0))),
                  head, head, head, head],
        out_specs=pl.BlockSpec((tt, d), lambda i, s: (i, 0)),
        out_shape=jax.ShapeDtypeStruct((t_all, d), F32),
        scratch_shapes=[pltpu.VMEM((d, tt), F32), pltpu.VMEM((eb, tt), BF16), pltpu.VMEM((eb, tt), BF16)],
        compiler_params=_params("arbitrary", "arbitrary"),
        name="peer_main",
    )(xt, mod, ht, u_all, vt_all, n, c, rank, e2)


def _peer_layer(xt, mod, gain, w_q, keys, u_all, vt_all, layer, blocks_per_mod, tt):
    wq_t = w_q.T
    tq = PEER_QUERY_TOK
    ht, n, c, rank, e2 = _peer_query(xt, mod, gain, wq_t, keys, blocks_per_mod * (tt // tq), tq)
    return _peer_main(xt, mod, ht, u_all, vt_all, layer, n, c, rank, e2, blocks_per_mod, tt)


def kernel(x, c, ctx, c_ctx, mod_w, mod_b, norm1_g, norm2_g, ev_w_in, ev_w_out, attn_q_gain, attn_k_gain,
           rw_mu, rw_w0, rw_w2, rw_a0, rw_a2, rw_g2, rw_k_k, rw_k_a, rw_r_k, rw_gn_w, rw_gn_b, ssd_w_in,
           ssd_conv_w, ssd_conv_b, ssd_dt_bias, ssd_a_log, ssd_d, ssd_norm_w, ssd_w_out, peer_w_q,
           peer_keys, peer_u, peer_v):
    bsz, seq, dm = x.shape
    lc = ctx.shape[1]
    depth = mod_w.shape[0]
    assert lc == TOK_TILE and seq % PEER_TOK == 0 and (bsz * lc) % PEER_TOK == 0

    rows = -(-(bsz + 1) // 8) * 8
    cc = jnp.concatenate([c, c_ctx[None], jnp.zeros((rows - bsz - 1, dm), F32)], axis=0)
    mods = _modulation(cc, mod_w, mod_b)
    xc, xl = ctx, x
    u_all = peer_u.astype(BF16)
    vt_all = jnp.swapaxes(peer_v, 1, 2).astype(BF16)

    for i in range(depth):
        last = i == depth - 1
        j = i // 2
        ml = mods[i, :bsz].reshape(bsz, N_MOD, dm)
        mc = mods[i, bsz].reshape(1, N_MOD, dm)
        modx = jnp.stack([jnp.broadcast_to(mc, (bsz, N_MOD, dm)), ml], axis=1)
        if i % 2 == 0:
            za, zr = _in_proj(xc, xl, modx, norm1_g[i], ev_w_in[j].astype(BF16),
                              ((A_IN, F32), (B_IN, F32)))
            q, k, v = _attn_pre(za, attn_q_gain[j], attn_k_gain[j], lc)
            attn = _attention(q, k, v, lc)
            r, vv, kkn, g, bonus, lw, a, kd = _rwkv_pre(
                zr, rw_mu[j], rw_w0[j], rw_w2[j], rw_a0[j], rw_a2[j], rw_g2[j], rw_k_k[j], rw_k_a[j],
                rw_r_k[j], lc)
            wkv = _rwkv_chunks(r, vv, kkn, lw, a, kd, lc)
            xc, xl = _even_out(xc, xl, modx, attn, wkv, bonus, g, rw_gn_w[j], rw_gn_b[j],
                               ev_w_out[j].astype(BF16))
        else:
            n_heads = ssd_a_log.shape[-1]
            inner = n_heads * C_HEAD_DIM
            conv_dim = inner + 2 * C_GROUPS * C_STATE
            w_in = ssd_w_in[j]
            w_dt = jnp.zeros((dm, 2 * LANES), F32)
            w_dt = w_dt.at[:, :n_heads].set(w_in[:, inner + conv_dim:inner + conv_dim + n_heads])
            w_dt = w_dt.at[:, LANES:LANES + n_heads].set(w_in[:, inner + conv_dim + n_heads:])
            dtb = jnp.zeros((1, 2 * LANES), F32)
            dtb = dtb.at[0, :n_heads].set(ssd_dt_bias[j, 0]).at[0, LANES:LANES + n_heads].set(ssd_dt_bias[j, 1])
            w_all = jnp.concatenate([w_in[:, :inner + conv_dim], w_dt], axis=1).astype(BF16)
            zg, xbc, dtraw = _in_proj(xc, xl, modx, norm1_g[i], w_all,
                                      ((inner, BF16), (conv_dim, F32), (2 * LANES, F32)))
            xact, dtact = _ssd_pre(xbc, dtraw, ssd_conv_w[j], ssd_conv_b[j], dtb, lc)
            ys = _ssd_chunks(xact, dtact, ssd_a_log[j], lc)
            assert last, "the SSD layer keeps only the latent stream"
            xl = _odd_out(xl, modx, ys, xact, zg, ssd_d[j], ssd_norm_w[j], ssd_w_out[j].astype(BF16), lc)

        tt = PEER_TOK
        xl = _peer_layer(xl.reshape(bsz * seq, dm), ml, norm2_g[i], peer_w_q[i], peer_keys[i],
                         u_all, vt_all, i, seq // tt, tt).reshape(bsz, seq, dm)
        if not last:
            xc = _peer_layer(xc.reshape(bsz * lc, dm), mc, norm2_g[i], peer_w_q[i], peer_keys[i],
                             u_all, vt_all, i, (bsz * lc) // tt, tt).reshape(bsz, lc, dm)
    return xl
```
